```python
import jax
import jax.numpy as jnp
from jax import lax
import numpy as np

D_MODEL = 1024
BATCH = 8
SEQ = 4096
DEPTH = 4

GRID_W = 64
CTX_LEN = 256
N_EVEN = (DEPTH + 1) // 2
N_ODD = DEPTH // 2
D_FF = 4 * D_MODEL
N_MOD = 6
EPS = 1e-6

ML_DK = 128
ML_DV = 128
ML_HEADS = D_MODEL // (2 * ML_DV)
ML_QK = ML_HEADS * ML_DK
ML_W = ML_HEADS * ML_DV
ML_CHUNK = 128
GATE_CAP = 15.0

AT_DH = 64
AT_HEADS = D_MODEL // (2 * AT_DH)
AT_KV_HEADS = AT_HEADS // 4
AT_GROUP = AT_HEADS // AT_KV_HEADS
AT_W = AT_HEADS * AT_DH
AT_KV_W = AT_KV_HEADS * AT_DH
WINDOW = 128
AT_BLOCK = 128
ROPE_BASE = 10000.0

EV_SIZES = (ML_QK, ML_QK, ML_W, ML_W, 4 * ML_HEADS, AT_W, AT_KV_W, AT_KV_W)
P_EVEN = 2 * ML_QK + 2 * ML_W + 4 * ML_HEADS + AT_W + 2 * AT_KV_W
MIX_W = ML_W + AT_W

RW_HEAD = 64
RW_HEADS = D_MODEL // RW_HEAD
RW_DECAY_LORA = 64
RW_AAA_LORA = 64
RW_GATE_LORA = 160
RW_LN_EPS = 64e-5

kernel_name = 'hybrid_mlstm_swa_rwkv7_dit_trunk'


def split_points():
    pts, acc = [], 0
    for s in EV_SIZES[:-1]:
        acc += s
        pts.append(acc)
    return pts


def rmsnorm(u, g):
    uf = u.astype(jnp.float32)
    y = uf * lax.rsqrt(jnp.mean(uf * uf, axis=-1, keepdims=True) + EPS)
    return (y * g.astype(jnp.float32)).astype(u.dtype)


def softcap(u):
    return GATE_CAP * jnp.tanh(u / GATE_CAP)


def modulate(u, lc, shift_c, scale_c, shift_l, scale_l):
    uc = u[:, :lc] * (1 + scale_c) + shift_c
    ul = u[:, lc:] * (1 + scale_l[:, None]) + shift_l[:, None]
    return jnp.concatenate([uc, ul], axis=1)


def apply_gate(y, lc, gate_c, gate_l):
    return jnp.concatenate([y[:, :lc] * gate_c, y[:, lc:] * gate_l[:, None]], axis=1)


def dir_order(u, lc):
    return jnp.concatenate([jnp.flip(u[:, :lc], axis=1), jnp.flip(u[:, lc:], axis=1)], axis=1)


def seq_order(u, lc, d):
    return u if d == 0 else dir_order(u, lc)


def seg_shift(u, lc, step):
    t = u.shape[1]
    pos = jnp.arange(t)
    if step == 1:
        s = jnp.pad(u, ((0, 0), (1, 0), (0, 0)))[:, :-1]
        border = (pos == 0) | (pos == lc)
    else:
        s = jnp.pad(u, ((0, 0), (0, 1), (0, 0)))[:, 1:]
        border = (pos == lc - 1) | (pos == t - 1)
    return jnp.where(border[None, :, None], jnp.zeros((), u.dtype), s)


def axial_rope(u, rows):
    f32 = jnp.float32
    half = AT_DH // 2
    quarter = half // 2
    inv = ROPE_BASE ** (-jnp.arange(quarter, dtype=f32) / quarter)
    rpos = jnp.repeat(jnp.arange(rows, dtype=f32), GRID_W)
    cpos = jnp.tile(jnp.arange(GRID_W, dtype=f32), rows)

    def rot(xh, pos):
        ang = pos[:, None] * inv[None, :]
        cos = jnp.cos(ang)[None, :, None, :].astype(xh.dtype)
        sin = jnp.sin(ang)[None, :, None, :].astype(xh.dtype)
        x1, x2 = xh[..., :quarter], xh[..., quarter:]
        return jnp.concatenate([x1 * cos - x2 * sin, x1 * sin + x2 * cos], axis=-1)

    return jnp.concatenate([rot(u[..., :half], rpos), rot(u[..., half:], cpos)], axis=-1)


def band_blocks(u):
    b, s = u.shape[0], u.shape[1]
    nb = s // AT_BLOCK
    up = jnp.pad(u, ((0, 0), (AT_BLOCK, AT_BLOCK), (0, 0), (0, 0)))
    up = up.reshape(b, nb + 2, AT_BLOCK, AT_KV_HEADS, AT_DH)
    return jnp.concatenate([up[:, :-2], up[:, 1:-1], up[:, 2:]], axis=2)


def window_attention(q, k, v, lc, rows, q_g, k_g, sink):
    f32 = jnp.float32
    q = rmsnorm(q, q_g)
    k = rmsnorm(k, k_g)
    qc, kc, vc = q[:, :lc], k[:, :lc], v[:, :lc]
    ql = axial_rope(q[:, lc:], rows)
    kl = axial_rope(k[:, lc:], rows)
    vl = v[:, lc:]
    b, s = ql.shape[0], ql.shape[1]
    nb = s // AT_BLOCK
    nk = 3 * AT_BLOCK
    scale = AT_DH ** -0.5
    sink_hg = sink.astype(f32).reshape(1, 1, AT_KV_HEADS, AT_GROUP, 1, 1)
    qlb = ql.reshape(b, nb, AT_BLOCK, AT_KV_HEADS, AT_GROUP, AT_DH)
    kb, vb = band_blocks(kl), band_blocks(vl)
    s_band = jnp.einsum('bnqhgd,bnkhd->bnhgqk', qlb, kb).astype(f32) * scale
    s_ctx = jnp.einsum('bnqhgd,bkhd->bnhgqk', qlb, kc).astype(f32) * scale
    qi = jnp.arange(AT_BLOCK)[:, None]
    ki = jnp.arange(nk)[None, :]
    in_window = jnp.abs(ki - AT_BLOCK - qi) <= WINDOW
    kpos = (jnp.arange(nb)[:, None] - 1) * AT_BLOCK + jnp.arange(nk)[None, :]
    in_range = (kpos >= 0) & (kpos < s)
    valid = in_window[None] & in_range[:, None, :]
    s_band = jnp.where(valid[None, :, None, None], s_band, -jnp.inf)
    sink_l = jnp.broadcast_to(sink_hg, s_band.shape[:-1] + (1,))
    p = jax.nn.softmax(jnp.concatenate([s_band, s_ctx, sink_l], axis=-1), axis=-1)
    ol = (jnp.einsum('bnhgqk,bnkhd->bnqhgd', p[..., :nk].astype(v.dtype), vb)
          + jnp.einsum('bnhgqk,bkhd->bnqhgd', p[..., nk:nk + lc].astype(v.dtype), vc))
    ol = ol.reshape(b, s, AT_W)
    qcg = qc.reshape(b, lc, AT_KV_HEADS, AT_GROUP, AT_DH)
    s_cc = jnp.einsum('bqhgd,bkhd->bhgqk', qcg, kc).astype(f32) * scale
    sink_c = jnp.broadcast_to(sink_hg[0], s_cc.shape[:-1] + (1,))
    pc = jax.nn.softmax(jnp.concatenate([s_cc, sink_c], axis=-1), axis=-1)[..., :lc]
    oc = jnp.einsum('bhgqk,bkhd->bqhgd', pc.astype(v.dtype), vc).reshape(b, lc, AT_W)
    return jnp.concatenate([oc, ol], axis=1)


def mlstm_chunk_scan(q, k, v, ig, lf):
    b, h, t, dk = q.shape
    dv = v.shape[-1]
    L = ML_CHUNK
    nc = t // L

    def chunks(u):
        return jnp.moveaxis(u.reshape(u.shape[:2] + (nc, L) + u.shape[3:]), 2, 0)

    lower = jnp.tril(jnp.ones((L, L), dtype=bool))

    def step(carry, xs):
        C, n, m = carry
        qc, kc, vc, ic, fc = xs
        bcum = jnp.cumsum(fc, axis=-1)
        dlog = bcum[..., :, None] - bcum[..., None, :] + ic[..., None, :]
        dlog = jnp.where(lower, dlog, -jnp.inf)
        inter = bcum + m[..., None]
        mt = jnp.maximum(jnp.max(dlog, axis=-1), inter)
        dw = jnp.exp(dlog - mt[..., None])
        iw = jnp.exp(inter - mt)
        sc = jnp.einsum('bhtk,bhsk->bhts', qc, kc) * dw
        num = (jnp.einsum('bhts,bhsv->bhtv', sc, vc)
               + iw[..., None] * jnp.einsum('bhvk,bhtk->bhtv', C, qc))
        den = jnp.maximum(jnp.abs(jnp.sum(sc, axis=-1) + iw * jnp.einsum('bhk,bhtk->bht', n, qc)),
                          jnp.exp(-mt))
        hout = num / den[..., None]
        bl = bcum[..., -1]
        ws_log = bl[..., None] - bcum + ic
        m_new = jnp.maximum(bl + m, jnp.max(ws_log, axis=-1))
        ws = jnp.exp(ws_log - m_new[..., None])
        carry_decay = jnp.exp(bl + m - m_new)
        C = carry_decay[..., None, None] * C + jnp.einsum('bhsv,bhsk->bhvk', vc * ws[..., None], kc)
        n = carry_decay[..., None] * n + jnp.einsum('bhs,bhsk->bhk', ws, kc)
        return (C, n, m_new), hout

    f32 = jnp.float32
    init = (jnp.zeros((b, h, dv, dk), f32), jnp.zeros((b, h, dk), f32), jnp.zeros((b, h), f32))
    _, hs = lax.scan(step, init, (chunks(q), chunks(k), chunks(v), chunks(ig), chunks(lf)))
    return jnp.moveaxis(hs, 0, 2).reshape(b, h, t, dv)


def mlstm_mixer(q, k, v, o, gates, lc, f_bias, out_g):
    f32 = jnp.float32
    b, t, _ = q.shape
    qh = q.reshape(b, t, ML_HEADS, ML_DK).astype(f32)
    kh = k.reshape(b, t, ML_HEADS, ML_DK).astype(f32) * (ML_DK ** -0.5)
    vh = v.reshape(b, t, ML_HEADS, ML_DV).astype(f32)
    g = gates.reshape(b, t, 2, 2, ML_HEADS).astype(f32)
    ig = softcap(g[:, :, :, 0])
    lf = jax.nn.log_sigmoid(softcap(g[:, :, :, 1] + f_bias.astype(f32)))
    outs = []
    for d in range(2):
        def bhtd(u):
            return jnp.moveaxis(seq_order(u, lc, d), 2, 1)
        hd = mlstm_chunk_scan(bhtd(qh), bhtd(kh), bhtd(vh), bhtd(ig[:, :, d]), bhtd(lf[:, :, d]))
        outs.append(seq_order(jnp.moveaxis(hd, 1, 2), lc, d))
    hsum = rmsnorm(outs[0] + outs[1], out_g.reshape(ML_HEADS, ML_DV))
    return (hsum.reshape(b, t, ML_W) * jax.nn.sigmoid(o.astype(f32))).astype(q.dtype)


def even_mixer(h, lc, rows, w_in, b_in, w_out, f_bias, out_g, q_g, k_g, sink):
    b, t, _ = h.shape
    p = h @ w_in + b_in
    mq, mk, mv, mo, mg, aq, ak, av = jnp.split(p, split_points(), axis=-1)
    ym = mlstm_mixer(mq, mk, mv, mo, mg, lc, f_bias, out_g)
    ya = window_attention(aq.reshape(b, t, AT_HEADS, AT_DH), ak.reshape(b, t, AT_KV_HEADS, AT_DH),
                          av.reshape(b, t, AT_KV_HEADS, AT_DH), lc, rows, q_g, k_g, sink)
    return jnp.concatenate([ym, ya], axis=-1) @ w_out


def rwkv7_scan(r, w, k, v, a, bb):
    b, t, h, n = r.shape

    def step(S, xs):
        rt, wt, kt, vt, at, bt = xs
        S = (S * wt[:, :, None, :]
             + jnp.einsum('bhvk,bhk->bhv', S, at)[..., None] * bt[:, :, None, :]
             + vt[..., None] * kt[:, :, None, :])
        return S, jnp.einsum('bhvk,bhk->bhv', S, rt)

    xs = tuple(jnp.moveaxis(u, 1, 0) for u in (r, w, k, v, a, bb))
    _, y = lax.scan(step, jnp.zeros((b, h, n, n), jnp.float32), xs)
    return jnp.moveaxis(y, 0, 1)


def rwkv7_mixer(u, lc, mu, w_rkv, w0, w1, w2, a0, a1, a2, g1, g2, k_k, k_a, r_k, ln_g, ln_b, w_out):
    f32 = jnp.float32
    b, t, d_model = u.shape
    d_prev = seg_shift(u, lc, 1) - u
    d_next = seg_shift(u, lc, -1) - u
    xr, xw, xk, xv, xa, xg = [u + mu[i, 0] * d_prev + mu[i, 1] * d_next for i in range(6)]
    r = xr @ w_rkv[0]
    k = xk @ w_rkv[1]
    v = xv @ w_rkv[2]
    gate = jax.nn.sigmoid(xg @ g1) @ g2

    def heads(z):
        return z.reshape(b, t, RW_HEADS, RW_HEAD).astype(f32)

    kk = heads(k * k_k)
    kk = kk * lax.rsqrt(jnp.maximum(jnp.sum(kk * kk, axis=-1, keepdims=True), 1e-24))
    rh, vh = heads(r), heads(v)
    outs, bonus = [], []
    for d in range(2):
        w_log = -jax.nn.softplus(-(w0[d] + jnp.tanh(xw @ w1[d]) @ w2[d])) - 0.5
        decay = jnp.exp(-jnp.exp(heads(w_log)))
        a = jax.nn.sigmoid(a0[d] + (xa @ a1[d]) @ a2[d])
        kd = heads(k * (1 + (a - 1) * k_a))
        args = [seq_order(z, lc, d) for z in (rh, decay, kd, vh, -kk, kk * heads(a))]
        outs.append(seq_order(rwkv7_scan(*args), lc, d))
        bonus.append(jnp.sum(rh * kd * r_k.astype(f32), axis=-1, keepdims=True) * vh)
    y = outs[0] + outs[1]
    mean = jnp.mean(y, axis=-1, keepdims=True)
    var = jnp.mean(jnp.square(y - mean), axis=-1, keepdims=True)
    y = ((y - mean) * lax.rsqrt(var + RW_LN_EPS)).reshape(b, t, d_model)
    y = y * ln_g.astype(f32) + ln_b.astype(f32) + (bonus[0] + bonus[1]).reshape(b, t, d_model)
    return (y * gate.astype(f32)).astype(u.dtype) @ w_out


def channel_mixer(h, w1, w2):
    return jnp.square(jax.nn.relu(h @ w1)) @ w2


def setup_inputs(seed: int = 0) -> dict:
    key = jax.random.key(seed)
    ks = iter(jax.random.split(key, 48))
    f32 = jnp.float32
    D = D_MODEL

    def nrm(shape, scale):
        return jax.random.normal(next(ks), shape, f32) * scale

    return {
        'x': nrm((BATCH, SEQ, D), 1.0),
        'c': nrm((BATCH, D), 1.0),
        'ctx': nrm((BATCH, CTX_LEN, D), 1.0),
        'c_ctx': nrm((D,), 1.0),
        'ada_w': nrm((DEPTH, D, N_MOD * D), 0.5 * D ** -0.5),
        'ada_b': nrm((DEPTH, N_MOD * D), 0.02),
        'norm1_g': 1.0 + nrm((DEPTH, D), 0.02),
        'norm2_g': 1.0 + nrm((DEPTH, D), 0.02),
        'mlp_w1': nrm((DEPTH, D, D_FF), D ** -0.5),
        'mlp_w2': nrm((DEPTH, D_FF, D), D_FF ** -0.5),
        'ev_w_in': nrm((N_EVEN, D, P_EVEN), D ** -0.5),
        'ev_b_in': nrm((N_EVEN, P_EVEN), 0.02),
        'ev_w_out': nrm((N_EVEN, MIX_W, D), MIX_W ** -0.5),
        'ml_f_bias': jnp.linspace(3.0, 6.0, ML_HEADS, dtype=f32)[None, None] + nrm((N_EVEN, 2, ML_HEADS), 0.1),
        'ml_out_g': 1.0 + nrm((N_EVEN, ML_W), 0.02),
        'at_q_g': 1.0 + nrm((N_EVEN, AT_DH), 0.02),
        'at_k_g': 1.0 + nrm((N_EVEN, AT_DH), 0.02),
        'at_sink': nrm((N_EVEN, AT_HEADS), 0.5),
        'rw_mu': jax.random.uniform(next(ks), (N_ODD, 6, 2, D), f32, 0.0, 0.5),
        'rw_w_rkv': nrm((N_ODD, 3, D, D), D ** -0.5),
        'rw_w0': jnp.linspace(-6.0, -1.0, D, dtype=f32) + nrm((N_ODD, 2, D), 0.1),
        'rw_w1': nrm((N_ODD, 2, D, RW_DECAY_LORA), D ** -0.5),
        'rw_w2': nrm((N_ODD, 2, RW_DECAY_LORA, D), 0.1 * RW_DECAY_LORA ** -0.5),
        'rw_a0': nrm((N_ODD, 2, D), 0.1),
        'rw_a1': nrm((N_ODD, 2, D, RW_AAA_LORA), D ** -0.5),
        'rw_a2': nrm((N_ODD, 2, RW_AAA_LORA, D), 0.5 * RW_AAA_LORA ** -0.5),
        'rw_g1': nrm((N_ODD, D, RW_GATE_LORA), D ** -0.5),
        'rw_g2': nrm((N_ODD, RW_GATE_LORA, D), RW_GATE_LORA ** -0.5),
        'rw_k_k': 0.85 + nrm((N_ODD, D), 0.02),
        'rw_k_a': 1.0 + nrm((N_ODD, D), 0.02),
        'rw_r_k': nrm((N_ODD, RW_HEADS, RW_HEAD), 0.1),
        'rw_ln_g': 1.0 + nrm((N_ODD, D), 0.02),
        'rw_ln_b': nrm((N_ODD, D), 0.02),
        'rw_w_out': nrm((N_ODD, D, D), D ** -0.5),
    }


def reference(x, c, ctx, c_ctx, ada_w, ada_b, norm1_g, norm2_g, mlp_w1, mlp_w2,
              ev_w_in, ev_b_in, ev_w_out, ml_f_bias, ml_out_g, at_q_g, at_k_g, at_sink,
              rw_mu, rw_w_rkv, rw_w0, rw_w1, rw_w2, rw_a0, rw_a1, rw_a2, rw_g1, rw_g2,
              rw_k_k, rw_k_a, rw_r_k, rw_ln_g, rw_ln_b, rw_w_out):
    b, s, d_model = x.shape
    lc = ctx.shape[1]
    rows = s // GRID_W
    z = jnp.concatenate([ctx, x], axis=1)
    s_lat = jax.nn.silu(c)
    s_ctx = jax.nn.silu(c_ctx)
    for layer in range(DEPTH):
        mod_l = (s_lat @ ada_w[layer] + ada_b[layer]).reshape(b, N_MOD, d_model)
        mod_c = (s_ctx @ ada_w[layer] + ada_b[layer]).reshape(N_MOD, d_model)
        h = modulate(rmsnorm(z, norm1_g[layer]), lc, mod_c[0], mod_c[1], mod_l[:, 0], mod_l[:, 1])
        j = layer // 2
        if layer % 2 == 0:
            y = even_mixer(h, lc, rows, ev_w_in[j], ev_b_in[j], ev_w_out[j], ml_f_bias[j], ml_out_g[j],
                           at_q_g[j], at_k_g[j], at_sink[j])
        else:
            y = rwkv7_mixer(h, lc, rw_mu[j], rw_w_rkv[j], rw_w0[j], rw_w1[j], rw_w2[j], rw_a0[j],
                            rw_a1[j], rw_a2[j], rw_g1[j], rw_g2[j], rw_k_k[j], rw_k_a[j], rw_r_k[j],
                            rw_ln_g[j], rw_ln_b[j], rw_w_out[j])
        z = z + apply_gate(y, lc, mod_c[2], mod_l[:, 2])
        if layer == DEPTH - 1:
            z = z[:, lc:]
            lc = 0
        h = modulate(rmsnorm(z, norm2_g[layer]), lc, mod_c[3], mod_c[4], mod_l[:, 3], mod_l[:, 4])
        z = z + apply_gate(channel_mixer(h, mlp_w1[layer], mlp_w2[layer]), lc, mod_c[5], mod_l[:, 5])
    return z
```

```python
import functools

import jax
import jax.numpy as jnp
from jax import lax
from jax.experimental import pallas as pl
from jax.experimental.pallas import tpu as pltpu

F32 = jnp.float32
BF16 = jnp.bfloat16

D_MODEL = 1024
N_MOD = 6
D_FF = 4 * D_MODEL
EPS = 1e-6
GRID_W = 64

ML_HEADS = 4
ML_D = 128
ML_W = ML_HEADS * ML_D
ML_CHUNK = 128
GATE_CAP = 15.0

AT_DH = 64
AT_HEADS = 8
AT_KV_HEADS = 2
AT_W = AT_HEADS * AT_DH
AT_BLOCK = 128
ROPE_BASE = 10000.0

RW_HEAD = 64
RW_PAIRS = D_MODEL // 128
RW_CHUNK = 64
RW_LN_EPS = 64e-5
RW_DECAY_LORA = 64
RW_AAA_LORA = 64
RW_GATE_LORA = 160
RW_GATE_PAD = 256

LANES = 128
TOKEN_TILE = 256
FF_TILE = 1024
MOD_ROWS = 16
VMEM_LIMIT = 56 * 1024 * 1024
NEG = -1e30
RW_MM = BF16

EV_MQKV = 0
EV_MO = 3 * ML_W
EV_AQ = EV_MO + ML_W
EV_AK = EV_AQ + AT_W
EV_AV = EV_AK + 256
EV_GATE = EV_AV + 256
EV_COLS = EV_GATE + LANES


def _dot(a, b):
    return jnp.dot(a, b, preferred_element_type=F32)


def _dot_tb(a, b):
    return lax.dot_general(a, b, (((1,), (1,)), ((), ())), preferred_element_type=F32)


def _dot_ta(a, b):
    return lax.dot_general(a, b, (((0,), (0,)), ((), ())), preferred_element_type=F32)


def _norm_mod(z, g, shift, scale):
    ms = jnp.mean(z * z, axis=-1, keepdims=True)
    return (z * lax.rsqrt(ms + EPS) * g) * (1.0 + scale) + shift


def _sigmoid(x):
    return 1.0 / (1.0 + jnp.exp(-x))


def _softplus(x):
    return jnp.maximum(x, 0.0) + jnp.log1p(jnp.exp(-jnp.abs(x)))


def _const_spec(shape):
    nd = len(shape)
    return pl.BlockSpec(shape, lambda *_: (0,) * nd)


def _params(sem):
    return pltpu.CompilerParams(dimension_semantics=sem, vmem_limit_bytes=VMEM_LIMIT)


def _ada_kernel(c_ref, w_ref, b_ref, o_ref):
    cv = c_ref[...]
    s = cv * _sigmoid(cv)
    o_ref[...] = jnp.dot(s, w_ref[...], preferred_element_type=F32,
                         precision=lax.Precision.HIGHEST) + b_ref[...]


def _ada_call(cstack, ada_w, ada_b):
    depth = ada_w.shape[0]
    n = N_MOD * D_MODEL
    tn = 1024
    return pl.pallas_call(
        _ada_kernel,
        grid=(depth, n // tn),
        in_specs=[pl.BlockSpec((MOD_ROWS, D_MODEL), lambda l, j: (0, 0)),
                  pl.BlockSpec((None, D_MODEL, tn), lambda l, j: (l, 0, j)),
                  pl.BlockSpec((None, 1, tn), lambda l, j: (l, 0, j))],
        out_specs=pl.BlockSpec((None, MOD_ROWS, tn), lambda l, j: (l, 0, j)),
        out_shape=jax.ShapeDtypeStruct((depth, MOD_ROWS, n), F32),
        compiler_params=_params(("parallel", "parallel")),
        name="ada_ln",
    )(cstack, ada_w, ada_b.reshape(depth, 1, n))


def _rope(x, cos, sin, lane_lo):
    n = x.shape[1]
    up = pltpu.roll(x, n - 16, 1)
    dn = pltpu.roll(x, 16, 1)
    reps = n // LANES
    c = jnp.concatenate([cos] * reps, axis=1) if reps > 1 else cos
    s = jnp.concatenate([sin] * reps, axis=1) if reps > 1 else sin
    lo = jnp.concatenate([lane_lo] * reps, axis=1) if reps > 1 else lane_lo
    return x * c + jnp.where(lo, up, dn) * s


def _even_in_kernel(z_ref, g_ref, mod_ref, w_ref, b_ref, gq_ref, gk_ref, grp_ref, cos_ref, sin_ref,
                    fb_ref, isf_ref, mqkv_ref, mo_ref, aq_ref, akv_ref, gate_ref):
    h = _norm_mod(z_ref[...], g_ref[...], mod_ref[0:1, :], mod_ref[1:2, :])
    p = _dot(h.astype(BF16), w_ref[...]) + b_ref[...]
    mqkv_ref[:, 0:ML_W] = p[:, 0:ML_W].astype(BF16)
    mqkv_ref[:, ML_W:2 * ML_W] = (p[:, ML_W:2 * ML_W] * (ML_D ** -0.5)).astype(BF16)
    mqkv_ref[:, 2 * ML_W:3 * ML_W] = p[:, 2 * ML_W:3 * ML_W].astype(BF16)
    mo_ref[...] = p[:, EV_MO:EV_MO + ML_W]
    cos = cos_ref[...]
    sin = sin_ref[...]
    lane_lo = (lax.broadcasted_iota(jnp.int32, (1, LANES), 1) % 32) < 16
    q = p[:, EV_AQ:EV_AQ + AT_W]
    qms = _dot((q * q).astype(BF16), grp_ref[...])
    q = q * lax.rsqrt(qms + EPS) * gq_ref[...]
    aq_ref[...] = (_rope(q, cos, sin, lane_lo) * (AT_DH ** -0.5)).astype(BF16)
    k = p[:, EV_AK:EV_AK + 256]
    kms = _dot((k * k).astype(BF16), grp_ref[0:256, 0:256])
    k = k * lax.rsqrt(kms + EPS) * gk_ref[...]
    akv_ref[:, 0:256] = _rope(k, cos, sin, lane_lo).astype(BF16)
    akv_ref[:, 256:512] = p[:, EV_AV:EV_AV + 256].astype(BF16)
    gt = p[:, EV_GATE:EV_GATE + LANES] + fb_ref[...]
    sc = GATE_CAP * jnp.tanh(gt * (1.0 / GATE_CAP))
    logsig = jnp.minimum(sc, 0.0) - jnp.log1p(jnp.exp(-jnp.abs(sc)))
    gate_ref[...] = jnp.where(isf_ref[...] > 0.5, logsig, sc)


def _even_in_call(z, g, mod, w, b, gq, gk, grp, cos, sin, fb, isf, nb):
    bsz, t, d = z.shape
    tm = TOKEN_TILE
    tok = lambda width: pl.BlockSpec((None, tm, width), lambda bi, i: (bi, i, 0))
    return pl.pallas_call(
        _even_in_kernel,
        grid=(bsz, t // tm),
        in_specs=[tok(d),
                  _const_spec((1, d)),
                  pl.BlockSpec((None, N_MOD, d), lambda bi, i: (jnp.where(i == 0, nb, bi), 0, 0)),
                  _const_spec((d, EV_COLS)),
                  _const_spec((1, EV_COLS)),
                  _const_spec((1, AT_W)),
                  _const_spec((1, 256)),
                  _const_spec((AT_W, AT_W)),
                  pl.BlockSpec((tm, LANES), lambda bi, i: (i, 0)),
                  pl.BlockSpec((tm, LANES), lambda bi, i: (i, 0)),
                  _const_spec((1, LANES)),
                  _const_spec((1, LANES))],
        out_specs=[tok(3 * ML_W), tok(ML_W), tok(AT_W), tok(512), tok(LANES)],
        out_shape=[jax.ShapeDtypeStruct((bsz, t, 3 * ML_W), BF16),
                   jax.ShapeDtypeStruct((bsz, t, ML_W), F32),
                   jax.ShapeDtypeStruct((bsz, t, AT_W), BF16),
                   jax.ShapeDtypeStruct((bsz, t, 512), BF16),
                   jax.ShapeDtypeStruct((bsz, t, LANES), F32)],
        compiler_params=_params(("parallel", "parallel")),
        name="even_in",
    )(z, g, mod, w, b, gq, gk, grp, cos, sin, fb, isf)


def _mlstm_kernel(qf_ref, kf_ref, vf_ref, qb_ref, kb_ref, vb_ref, gcf_ref, gcb_ref, grf_ref, grb_ref,
                  hf_ref, hb_ref, c_s, n_s, m_s):
    ln = ML_CHUNK

    @pl.when(pl.program_id(1) == 0)
    def _():
        c_s[...] = jnp.zeros_like(c_s)
        n_s[...] = jnp.zeros_like(n_s)
        m_s[...] = jnp.zeros_like(m_s)

    ri = lax.broadcasted_iota(jnp.int32, (ln, ln), 0)
    ci = lax.broadcasted_iota(jnp.int32, (ln, ln), 1)
    for d in range(2):
        q_ref, k_ref, v_ref, gc_ref, gr_ref, o_ref = (
            (qf_ref, kf_ref, vf_ref, gcf_ref, grf_ref, hf_ref) if d == 0 else
            (qb_ref, kb_ref, vb_ref, gcb_ref, grb_ref, hb_ref))
        before = (ci <= ri) if d == 0 else (ci >= ri)
        before_t = (ri <= ci) if d == 0 else (ri >= ci)
        for hd in range(ML_HEADS):
            st = d * ML_HEADS + hd
            ic_col, fc_col = 8 * d + hd, 8 * d + 4 + hd
            lanes = slice(hd * ML_D, (hd + 1) * ML_D)
            q = q_ref[:, lanes]
            k = k_ref[:, lanes]
            v = v_ref[:, lanes]
            i_row = gr_ref[ic_col:ic_col + 1, :]
            f_row = gr_ref[fc_col:fc_col + 1, :]
            i_col = gc_ref[:, ic_col:ic_col + 1]
            f_col = gc_ref[:, fc_col:fc_col + 1]
            bcum_col = jnp.sum(jnp.where(before, f_row, 0.0), axis=1, keepdims=True)
            bcum_row = jnp.sum(jnp.where(before_t, f_col, 0.0), axis=0, keepdims=True)
            m_old = m_s[st:st + 1, 0:1]
            dlog = jnp.where(before, bcum_col - bcum_row + i_row, NEG)
            inter = bcum_col + m_old
            mt = jnp.maximum(jnp.max(dlog, axis=1, keepdims=True), inter)
            dw = jnp.exp(dlog - mt)
            iw = jnp.exp(inter - mt)
            sc = _dot_tb(q, k) * dw
            ct = c_s[st]
            num = _dot(sc.astype(BF16), v) + iw * _dot(q, ct.astype(BF16))
            nvec = n_s[st:st + 1, :]
            qn = jnp.sum(q.astype(F32) * nvec, axis=1, keepdims=True)
            den = jnp.maximum(jnp.abs(jnp.sum(sc, axis=1, keepdims=True) + iw * qn), jnp.exp(-mt))
            o_ref[:, lanes] = num / den
            bl = jnp.sum(f_row, axis=1, keepdims=True)
            ws_log = bl - bcum_col + i_col
            m_new = jnp.maximum(bl + m_old, jnp.max(ws_log, axis=0, keepdims=True))
            ws = jnp.exp(ws_log - m_new)
            decay = jnp.exp(bl + m_old - m_new)
            kf32 = k.astype(F32)
            c_s[st] = decay * ct + _dot_ta(k, (v.astype(F32) * ws).astype(BF16))
            n_s[st:st + 1, :] = decay * nvec + jnp.sum(kf32 * ws, axis=0, keepdims=True)
            m_s[st:st + 1, :] = jnp.broadcast_to(m_new, (1, LANES))


def _rev_chunk(i, nctx, n):
    return jnp.where(i < nctx, nctx - 1 - i, n + nctx - 1 - i)


def _mlstm_call(mqkv, gates, gates_t, lc):
    bsz, t, _ = mqkv.shape
    ln = ML_CHUNK
    n = t // ln
    nctx = lc // ln
    fwd = lambda col: pl.BlockSpec((None, ln, ML_W), lambda bi, i: (bi, i, col))
    bwd = lambda col: pl.BlockSpec((None, ln, ML_W), lambda bi, i: (bi, _rev_chunk(i, nctx, n), col))
    return pl.pallas_call(
        _mlstm_kernel,
        grid=(bsz, n),
        in_specs=[fwd(0), fwd(1), fwd(2), bwd(0), bwd(1), bwd(2),
                  pl.BlockSpec((None, ln, LANES), lambda bi, i: (bi, i, 0)),
                  pl.BlockSpec((None, ln, LANES), lambda bi, i: (bi, _rev_chunk(i, nctx, n), 0)),
                  pl.BlockSpec((None, 16, ln), lambda bi, i: (bi, 0, i)),
                  pl.BlockSpec((None, 16, ln), lambda bi, i: (bi, 0, _rev_chunk(i, nctx, n)))],
        out_specs=[pl.BlockSpec((None, ln, ML_W), lambda bi, i: (bi, i, 0)),
                   pl.BlockSpec((None, ln, ML_W), lambda bi, i: (bi, _rev_chunk(i, nctx, n), 0))],
        out_shape=[jax.ShapeDtypeStruct((bsz, t, ML_W), F32)] * 2,
        scratch_shapes=[pltpu.VMEM((2 * ML_HEADS, ML_D, ML_D), F32),
                        pltpu.VMEM((2 * ML_HEADS, LANES), F32),
                        pltpu.VMEM((2 * ML_HEADS, LANES), F32)],
        compiler_params=_params(("parallel", "arbitrary")),
        name="mlstm_scan",
    )(mqkv, mqkv, mqkv, mqkv, mqkv, mqkv, gates, gates, gates_t, gates_t)


def _attn_kernel(q_ref, kvp_ref, kvo_ref, kvn_ref, kvc_ref, sink_ref, o_ref, *, nctx, n):
    j = pl.program_id(1)
    blk = AT_BLOCK
    latent = j >= nctx
    qi = lax.broadcasted_iota(jnp.int32, (blk, blk), 0)
    ki = lax.broadcasted_iota(jnp.int32, (blk, blk), 1)
    ok_prev = jnp.logical_and(jnp.logical_and(latent, j > nctx), ki >= qi)
    ok_own = jnp.logical_and(latent, ki >= 0)
    ok_next = jnp.logical_and(jnp.logical_and(latent, j < n - 1), ki <= qi)
    lane = lax.broadcasted_iota(jnp.int32, (1, LANES), 1)
    half = (lane < AT_DH, lane >= AT_DH)
    pieces = ((kvp_ref, ok_prev), (kvo_ref, ok_own), (kvn_ref, ok_next), (kvc_ref, None))
    for c in range(AT_HEADS // 2):
        g = c // 2
        qc = q_ref[:, c * LANES:(c + 1) * LANES]
        acc = None
        for e in range(2):
            hd = 2 * c + e
            sink = sink_ref[hd:hd + 1, 0:1]
            ss = []
            m = sink
            for ref, ok in pieces:
                kx = jnp.where(half[e], ref[:, g * LANES:(g + 1) * LANES], jnp.zeros((), BF16))
                s = _dot_tb(qc, kx)
                if ok is not None:
                    s = jnp.where(ok, s, NEG)
                ss.append(s)
                m = jnp.maximum(m, jnp.max(s, axis=1, keepdims=True))
            den = jnp.exp(sink - m)
            o = None
            for (ref, ok), s in zip(pieces, ss):
                pr = jnp.exp(s - m)
                den = den + jnp.sum(pr, axis=1, keepdims=True)
                vx = jnp.where(half[e], ref[:, 256 + g * LANES:256 + (g + 1) * LANES], jnp.zeros((), BF16))
                pv = _dot(pr.astype(BF16), vx)
                o = pv if o is None else o + pv
            o = o / den
            acc = o if acc is None else acc + o
        o_ref[:, c * LANES:(c + 1) * LANES] = acc.astype(o_ref.dtype)


def _attn_call(aq, akv, sink, lc):
    bsz, t, _ = aq.shape
    blk = AT_BLOCK
    n = t // blk
    nctx = lc // blk
    kv = lambda f: pl.BlockSpec((None, blk, 512), lambda bi, j: (bi, f(j), 0))
    return pl.pallas_call(
        functools.partial(_attn_kernel, nctx=nctx, n=n),
        grid=(bsz, n),
        in_specs=[pl.BlockSpec((None, blk, AT_W), lambda bi, j: (bi, j, 0)),
                  kv(lambda j: jnp.clip(j - 1, nctx, n - 1)),
                  kv(lambda j: jnp.clip(j, nctx, n - 1)),
                  kv(lambda j: jnp.clip(j + 1, nctx, n - 1)),
                  pl.BlockSpec((None, lc, 512), lambda bi, j: (bi, 0, 0)),
                  _const_spec((AT_HEADS, LANES))],
        out_specs=pl.BlockSpec((None, blk, AT_W), lambda bi, j: (bi, j, 0)),
        out_shape=jax.ShapeDtypeStruct((bsz, t, AT_W), BF16),
        compiler_params=_params(("parallel", "parallel")),
        name="window_attn",
    )(aq, akv, akv, akv, akv, sink)


def _even_out_kernel(z_ref, hf_ref, hb_ref, mo_ref, ya_ref, og_ref, w_ref, mod_ref, o_ref):
    hs = hf_ref[...] + hb_ref[...]
    parts = []
    for hd in range(ML_HEADS):
        x = hs[:, hd * ML_D:(hd + 1) * ML_D]
        parts.append(x * lax.rsqrt(jnp.mean(x * x, axis=1, keepdims=True) + EPS))
    ym = jnp.concatenate(parts, axis=1) * og_ref[...] * _sigmoid(mo_ref[...])
    y = _dot(ym.astype(BF16), w_ref[0:ML_W, :]) + _dot(ya_ref[...], w_ref[ML_W:ML_W + AT_W, :])
    o_ref[...] = z_ref[...] + mod_ref[2:3, :] * y


def _even_out_call(z, hf, hb, mo, ya, og, w, mod, nb):
    bsz, t, d = z.shape
    tm = TOKEN_TILE
    tok = lambda width: pl.BlockSpec((None, tm, width), lambda bi, i: (bi, i, 0))
    return pl.pallas_call(
        _even_out_kernel,
        grid=(bsz, t // tm),
        in_specs=[tok(d), tok(ML_W), tok(ML_W), tok(ML_W), tok(AT_W),
                  _const_spec((1, ML_W)),
                  _const_spec((ML_W + AT_W, d)),
                  pl.BlockSpec((None, N_MOD, d), lambda bi, i: (jnp.where(i == 0, nb, bi), 0, 0))],
        out_specs=tok(d),
        out_shape=jax.ShapeDtypeStruct((bsz, t, d), F32),
        compiler_params=_params(("parallel", "parallel")),
        name="even_out",
    )(z, hf, hb, mo, ya, og, w, mod)


def _mlp_kernel(z_ref, g_ref, mod_ref, w1_ref, w2_ref, o_ref):
    z = z_ref[...]
    h = _norm_mod(z, g_ref[...], mod_ref[3:4, :], mod_ref[4:5, :]).astype(BF16)
    acc = jnp.zeros(z.shape, F32)
    for f in range(D_FF // FF_TILE):
        a = jnp.maximum(_dot(h, w1_ref[:, f * FF_TILE:(f + 1) * FF_TILE]), 0.0)
        acc = acc + _dot((a * a).astype(BF16), w2_ref[f * FF_TILE:(f + 1) * FF_TILE, :])
    o_ref[...] = z + mod_ref[5:6, :] * acc


def _mlp_call(z, g, mod, w1, w2, nb, skip):
    bsz, t, d = z.shape
    tm = TOKEN_TILE
    nt = t // tm - skip
    mod_idx = (lambda bi, i: (bi, 0, 0)) if skip else (lambda bi, i: (jnp.where(i == 0, nb, bi), 0, 0))
    return pl.pallas_call(
        _mlp_kernel,
        grid=(bsz, nt),
        in_specs=[pl.BlockSpec((None, tm, d), lambda bi, i: (bi, i + skip, 0)),
                  _const_spec((1, d)),
                  pl.BlockSpec((None, N_MOD, d), mod_idx),
                  _const_spec((d, D_FF)),
                  _const_spec((D_FF, d))],
        out_specs=pl.BlockSpec((None, tm, d), lambda bi, i: (bi, i, 0)),
        out_shape=jax.ShapeDtypeStruct((bsz, nt * tm, d), F32),
        compiler_params=_params(("parallel", "parallel")),
        name="mlp",
    )(z, g, mod, w1, w2)


def _rwkv_in_kernel(z_ref, zp_ref, zn_ref, g_ref, mod_ref, mu_ref, wrkv_ref, w1_ref, a1_ref, g1_ref,
                    w2_ref, a2_ref, g2_ref, w0_ref, a0_ref, kk_ref, ka_ref, rk_ref, grp_ref,
                    sh_ref, dr_ref, lw_ref, gate_ref, bonus_ref, *, nctx, ntile):
    i = pl.program_id(1)
    tm, d = z_ref.shape
    g = g_ref[...]
    shift = mod_ref[0:1, :]
    scale = mod_ref[1:2, :]
    h = _norm_mod(z_ref[...], g, shift, scale)
    no_prev = jnp.logical_or(i == 0, i == nctx)
    no_next = jnp.logical_or(i == nctx - 1, i == ntile - 1)
    hp = jnp.where(no_prev, 0.0, _norm_mod(zp_ref[7:8, :], g, shift, scale))
    hn = jnp.where(no_next, 0.0, _norm_mod(zn_ref[0:1, :], g, shift, scale))
    row = lax.broadcasted_iota(jnp.int32, (tm, 1), 0)
    dp = jnp.where(row == 0, hp, pltpu.roll(h, 1, 0)) - h
    dn = jnp.where(row == tm - 1, hn, pltpu.roll(h, tm - 1, 0)) - h

    def mix(n):
        return (h + mu_ref[2 * n:2 * n + 1, :] * dp + mu_ref[2 * n + 1:2 * n + 2, :] * dn).astype(BF16)

    r = _dot(mix(0), wrkv_ref[0])
    k = _dot(mix(2), wrkv_ref[1])
    v = _dot(mix(3), wrkv_ref[2])
    gate_ref[...] = _dot(_sigmoid(_dot(mix(5), g1_ref[...])).astype(BF16), g2_ref[...])
    lora_w = _dot(jnp.tanh(_dot(mix(1), w1_ref[...])).astype(BF16), w2_ref[...])
    lora_a = _dot(_dot(mix(4), a1_ref[...]).astype(BF16), a2_ref[...])
    kkr = k * kk_ref[...]
    ssq = _dot((kkr * kkr).astype(BF16), grp_ref[...]) * float(RW_HEAD)
    kk = kkr * lax.rsqrt(jnp.maximum(ssq, 1e-24))
    kd_sum = None
    for dr in range(2):
        cols = slice(dr * d, (dr + 1) * d)
        w_log = -_softplus(-(w0_ref[dr:dr + 1, :] + lora_w[:, cols])) - 0.5
        lw = -jnp.exp(w_log)
        a = _sigmoid(a0_ref[dr:dr + 1, :] + lora_a[:, cols])
        kd = k * (1.0 + (a - 1.0) * ka_ref[...])
        bvec = kk * a
        kd_sum = kd if kd_sum is None else kd_sum + kd
        for p in range(RW_PAIRS):
            ls = slice(p * LANES, (p + 1) * LANES)
            lw_ref[dr, p] = lw[:, ls]
            dr_ref[dr, 0, p] = kd[:, ls].astype(BF16)
            dr_ref[dr, 1, p] = bvec[:, ls].astype(BF16)
    for p in range(RW_PAIRS):
        ls = slice(p * LANES, (p + 1) * LANES)
        sh_ref[0, p] = r[:, ls].astype(BF16)
        sh_ref[1, p] = v[:, ls].astype(BF16)
        sh_ref[2, p] = kk[:, ls].astype(BF16)
    bsum = _dot((r * kd_sum * rk_ref[...]).astype(BF16), grp_ref[...]) * float(RW_HEAD)
    bonus_ref[...] = bsum * v


def _rwkv_in_call(z, g, mod, mu, wrkv, w1, a1, g1, w2, a2, g2, w0, a0, k_k, k_a, r_k, grp, nb, lc):
    bsz, t, d = z.shape
    tm = TOKEN_TILE
    ntile = t // tm
    nctx = lc // tm
    r8 = tm // 8
    tok = pl.BlockSpec((None, tm, d), lambda bi, i: (bi, i, 0))
    return pl.pallas_call(
        functools.partial(_rwkv_in_kernel, nctx=nctx, ntile=ntile),
        grid=(bsz, ntile),
        in_specs=[tok,
                  pl.BlockSpec((None, 8, d), lambda bi, i: (bi, jnp.maximum(i * r8 - 1, 0), 0)),
                  pl.BlockSpec((None, 8, d), lambda bi, i: (bi, jnp.minimum((i + 1) * r8, t // 8 - 1), 0)),
                  _const_spec((1, d)),
                  pl.BlockSpec((None, N_MOD, d), lambda bi, i: (jnp.where(i < nctx, nb, bi), 0, 0)),
                  _const_spec((12, d)),
                  _const_spec((3, d, d)),
                  _const_spec((d, LANES)),
                  _const_spec((d, LANES)),
                  _const_spec((d, RW_GATE_PAD)),
                  _const_spec((LANES, 2 * d)),
                  _const_spec((LANES, 2 * d)),
                  _const_spec((RW_GATE_PAD, d)),
                  _const_spec((2, d)),
                  _const_spec((2, d)),
                  _const_spec((1, d)),
                  _const_spec((1, d)),
                  _const_spec((1, d)),
                  _const_spec((d, d))],
        out_specs=[pl.BlockSpec((None, 3, RW_PAIRS, tm, LANES), lambda bi, i: (bi, 0, 0, i, 0)),
                   pl.BlockSpec((None, 2, 2, RW_PAIRS, tm, LANES), lambda bi, i: (bi, 0, 0, 0, i, 0)),
                   pl.BlockSpec((None, 2, RW_PAIRS, tm, LANES), lambda bi, i: (bi, 0, 0, i, 0)),
                   tok, tok],
        out_shape=[jax.ShapeDtypeStruct((bsz, 3, RW_PAIRS, t, LANES), BF16),
                   jax.ShapeDtypeStruct((bsz, 2, 2, RW_PAIRS, t, LANES), BF16),
                   jax.ShapeDtypeStruct((bsz, 2, RW_PAIRS, t, LANES), F32),
                   jax.ShapeDtypeStruct((bsz, t, d), F32),
                   jax.ShapeDtypeStruct((bsz, t, d), F32)],
        compiler_params=_params(("parallel", "parallel")),
        name="rwkv_in",
    )(z, z, z, g, mod, mu, wrkv, w1, a1, g1, w2, a2, g2, w0, a0, k_k, k_a, r_k, grp)


def _rwkv_pair_chunk(r, v, kk, kd, bv, lw, s_old, rev):
    ln = RW_CHUNK
    row = lax.broadcasted_iota(jnp.int32, (ln, 2 * ln), 0)
    col = lax.broadcasted_iota(jnp.int32, (ln, 2 * ln), 1)
    sidx = jnp.where(col >= ln, col - ln, col)
    lane_a = lax.broadcasted_iota(jnp.int32, (1, LANES), 1) < RW_HEAD
    tr = lax.broadcasted_iota(jnp.int32, (ln, ln), 0)
    tc = lax.broadcasted_iota(jnp.int32, (ln, ln), 1)
    if rev:
        tri = (tc >= tr).astype(F32)
        strict = sidx > row
        incl = sidx >= row
    else:
        tri = (tc <= tr).astype(F32)
        strict = sidx < row
        incl = sidx <= row
    gcum = jnp.dot(tri, lw, preferred_element_type=F32, precision=lax.Precision.HIGHEST)
    gtot = jnp.sum(lw, axis=0, keepdims=True)
    e_pos = jnp.exp(gcum)
    e_neg = jnp.exp(-gcum)
    e_end = jnp.exp(gtot - gcum)
    rf, vf, kkf, kdf, bf = (x.astype(F32) for x in (r, v, kk, kd, bv))
    a_t = (-kkf * jnp.exp(gcum - lw)).astype(RW_MM)
    r_t = (rf * e_pos).astype(RW_MM)
    b_t = (bf * e_neg).astype(RW_MM)
    k_t = (kdf * e_neg).astype(RW_MM)
    b_h = (bf * e_end).astype(RW_MM)
    k_h = (kdf * e_end).astype(RW_MM)

    def bd(x):
        zero = jnp.zeros((), x.dtype)
        return jnp.concatenate([jnp.where(lane_a, x, zero), jnp.where(lane_a, zero, x)], axis=0)

    ar = jnp.concatenate([a_t, r_t], axis=0)
    x_b = _dot_tb(ar, bd(b_t))
    x_k = _dot_tb(ar, bd(k_t))
    n_w = jnp.where(strict, x_b[:ln], 0.0)
    m_rb = jnp.where(incl, x_b[ln:], 0.0)
    m_ak = jnp.where(strict, x_k[:ln], 0.0)
    m_rk = jnp.where(incl, x_k[ln:], 0.0)
    x_w = jnp.where(sidx == row, 1.0, 0.0)
    diff = row ^ sidx
    h = 1
    while h < ln:
        n_l = jnp.where(jnp.logical_and(diff >= h, diff < 2 * h), n_w, 0.0)
        tmp = _dot(n_l.astype(RW_MM), bd(x_w.astype(RW_MM)))
        x_w = x_w + _dot(x_w.astype(RW_MM), bd(tmp.astype(RW_MM)))
        h *= 2
    ars = _dot_tb(ar, s_old.astype(RW_MM))
    mv = _dot(jnp.concatenate([m_ak, m_rk], axis=0).astype(RW_MM), bd(v))
    gmat = ars[:ln] + mv[:ln]
    u = _dot(x_w.astype(RW_MM), bd(gmat.astype(RW_MM)))
    y = ars[ln:] + mv[ln:] + _dot(m_rb.astype(RW_MM), bd(u.astype(RW_MM)))
    upd = _dot_ta(jnp.concatenate([u.astype(RW_MM), v], axis=0), jnp.concatenate([b_h, k_h], axis=0))
    vr = lax.broadcasted_iota(jnp.int32, (LANES, LANES), 0) < RW_HEAD
    kc = lax.broadcasted_iota(jnp.int32, (LANES, LANES), 1) < RW_HEAD
    s_new = s_old * jnp.exp(gtot) + jnp.where(vr == kc, upd, 0.0)
    return y, s_new


def _rwkv_scan_kernel(shf_ref, shb_ref, drf_ref, drb_ref, lwf_ref, lwb_ref, yf_ref, yb_ref, s_s):
    @pl.when(pl.program_id(1) == 0)
    def _():
        s_s[...] = jnp.zeros_like(s_s)

    def body(p, carry):
        yf, sf = _rwkv_pair_chunk(shf_ref[0, p], shf_ref[1, p], shf_ref[2, p], drf_ref[0, p], drf_ref[1, p],
                                  lwf_ref[p], s_s[0, p], False)
        yf_ref[p] = yf
        s_s[0, p] = sf
        yb, sb = _rwkv_pair_chunk(shb_ref[0, p], shb_ref[1, p], shb_ref[2, p], drb_ref[0, p], drb_ref[1, p],
                                  lwb_ref[p], s_s[1, p], True)
        yb_ref[p] = yb
        s_s[1, p] = sb
        return carry

    lax.fori_loop(0, RW_PAIRS, body, 0)


def _rwkv_scan_call(sh, dr, lw, lc):
    bsz, _, _, t, _ = sh.shape
    ln = RW_CHUNK
    n = t // ln
    nctx = lc // ln
    rv = lambda i: _rev_chunk(i, nctx, n)
    return pl.pallas_call(
        _rwkv_scan_kernel,
        grid=(bsz, n),
        in_specs=[pl.BlockSpec((None, 3, RW_PAIRS, ln, LANES), lambda bi, i: (bi, 0, 0, i, 0)),
                  pl.BlockSpec((None, 3, RW_PAIRS, ln, LANES), lambda bi, i: (bi, 0, 0, rv(i), 0)),
                  pl.BlockSpec((None, None, 2, RW_PAIRS, ln, LANES), lambda bi, i: (bi, 0, 0, 0, i, 0)),
                  pl.BlockSpec((None, None, 2, RW_PAIRS, ln, LANES), lambda bi, i: (bi, 1, 0, 0, rv(i), 0)),
                  pl.BlockSpec((None, None, RW_PAIRS, ln, LANES), lambda bi, i: (bi, 0, 0, i, 0)),
                  pl.BlockSpec((None, None, RW_PAIRS, ln, LANES), lambda bi, i: (bi, 1, 0, rv(i), 0))],
        out_specs=[pl.BlockSpec((None, RW_PAIRS, ln, LANES), lambda bi, i: (bi, 0, i, 0)),
                   pl.BlockSpec((None, RW_PAIRS, ln, LANES), lambda bi, i: (bi, 0, rv(i), 0))],
        out_shape=[jax.ShapeDtypeStruct((bsz, RW_PAIRS, t, LANES), F32)] * 2,
        scratch_shapes=[pltpu.VMEM((2, RW_PAIRS, LANES, LANES), F32)],
        compiler_params=_params(("parallel", "arbitrary")),
        name="rwkv_scan",
    )(sh, sh, dr, dr, lw, lw)


def _rwkv_out_kernel(z_ref, yf_ref, yb_ref, gate_ref, bonus_ref, lng_ref, lnb_ref, grp_ref, w_ref, mod_ref, o_ref):
    y = jnp.concatenate([yf_ref[p] + yb_ref[p] for p in range(RW_PAIRS)], axis=1)
    mean = _dot(y.astype(BF16), grp_ref[...])
    yc = y - mean
    var = _dot((yc * yc).astype(BF16), grp_ref[...])
    yn = yc * lax.rsqrt(var + RW_LN_EPS) * lng_ref[...] + lnb_ref[...] + bonus_ref[...]
    out = _dot((yn * gate_ref[...]).astype(BF16), w_ref[...])
    o_ref[...] = z_ref[...] + mod_ref[2:3, :] * out


def _rwkv_out_call(z, yf, yb, gate, bonus, ln_g, ln_b, grp, w, mod, nb, lc):
    bsz, t, d = z.shape
    tm = TOKEN_TILE
    nctx = lc // tm
    tok = pl.BlockSpec((None, tm, d), lambda bi, i: (bi, i, 0))
    pair = pl.BlockSpec((None, RW_PAIRS, tm, LANES), lambda bi, i: (bi, 0, i, 0))
    return pl.pallas_call(
        _rwkv_out_kernel,
        grid=(bsz, t // tm),
        in_specs=[tok, pair, pair, tok, tok,
                  _const_spec((1, d)), _const_spec((1, d)), _const_spec((d, d)), _const_spec((d, d)),
                  pl.BlockSpec((None, N_MOD, d), lambda bi, i: (jnp.where(i < nctx, nb, bi), 0, 0))],
        out_specs=tok,
        out_shape=jax.ShapeDtypeStruct((bsz, t, d), F32),
        compiler_params=_params(("parallel", "parallel")),
        name="rwkv_out",
    )(z, yf, yb, gate, bonus, ln_g, ln_b, grp, w, mod)


def _group_mean_matrix(n, width):
    idx = jnp.arange(n) // width
    return ((idx[:, None] == idx[None, :]).astype(F32) / width).astype(BF16)


def _rope_tables(lc, s):
    quarter = AT_DH // 4
    inv = ROPE_BASE ** (-jnp.arange(quarter, dtype=F32) / quarter)
    pos = jnp.arange(s)
    rpos = (pos // GRID_W).astype(F32)
    cpos = (pos % GRID_W).astype(F32)
    ang_r = rpos[:, None] * inv[None, :]
    ang_c = cpos[:, None] * inv[None, :]
    cos64 = jnp.concatenate([jnp.cos(ang_r), jnp.cos(ang_r), jnp.cos(ang_c), jnp.cos(ang_c)], axis=1)
    sin64 = jnp.concatenate([-jnp.sin(ang_r), jnp.sin(ang_r), -jnp.sin(ang_c), jnp.sin(ang_c)], axis=1)
    cos = jnp.concatenate([jnp.ones((lc, AT_DH), F32), cos64], axis=0)
    sin = jnp.concatenate([jnp.zeros((lc, AT_DH), F32), sin64], axis=0)
    return jnp.tile(cos, (1, 2)), jnp.tile(sin, (1, 2))


def _even_weights(w_in, b_in):
    def cols(m):
        mq, mk, mv, mo, mg, aq, ak, av = jnp.split(m, [512, 1024, 1536, 2048, 2064, 2576, 2704], axis=-1)
        dup = lambda u: jnp.concatenate([u[..., :64], u[..., :64], u[..., 64:], u[..., 64:]], axis=-1)
        mgp = jnp.pad(mg, [(0, 0)] * (m.ndim - 1) + [(0, LANES - 16)])
        return jnp.concatenate([mq, mk, mv, mo, aq, dup(ak), dup(av), mgp], axis=-1)
    return cols(w_in).astype(BF16), cols(b_in[None, :])


def kernel(x, c, ctx, c_ctx, ada_w, ada_b, norm1_g, norm2_g, mlp_w1, mlp_w2, ev_w_in, ev_b_in, ev_w_out, ml_f_bias, ml_out_g, at_q_g, at_k_g, at_sink, rw_mu, rw_w_rkv, rw_w0, rw_w1, rw_w2, rw_a0, rw_a1, rw_a2, rw_g1, rw_g2, rw_k_k, rw_k_a, rw_r_k, rw_ln_g, rw_ln_b, rw_w_out):
    bsz, s, d = x.shape
    lc = ctx.shape[1]
    depth = ada_w.shape[0]
    assert d == D_MODEL and lc == TOKEN_TILE and s % TOKEN_TILE == 0 and bsz < MOD_ROWS
    z = jnp.concatenate([ctx, x], axis=1)

    cstack = jnp.zeros((MOD_ROWS, d), F32).at[:bsz].set(c).at[bsz].set(c_ctx)
    mod_all = _ada_call(cstack, ada_w, ada_b).reshape(depth, MOD_ROWS, N_MOD, d)

    grp512 = _group_mean_matrix(AT_W, AT_DH)
    grp_d = _group_mean_matrix(d, RW_HEAD)
    cos_t, sin_t = _rope_tables(lc, s)
    gate_is_f = ((jnp.arange(LANES) % 8 >= 4) & (jnp.arange(LANES) < 16)).astype(F32)[None, :]

    for layer in range(depth):
        mod = mod_all[layer]
        j = layer // 2
        n1 = norm1_g[layer][None, :]
        if layer % 2 == 0:
            w_in, b_in = _even_weights(ev_w_in[j], ev_b_in[j])
            fb = jnp.zeros((LANES,), F32)
            for dr in range(2):
                fb = fb.at[8 * dr + 4:8 * dr + 8].set(ml_f_bias[j, dr])
            mqkv, mo, aq, akv, gates = _even_in_call(
                z, n1, mod, w_in, b_in, jnp.tile(at_q_g[j], AT_HEADS)[None, :], jnp.tile(at_k_g[j], 4)[None, :],
                grp512, cos_t, sin_t, fb[None, :], gate_is_f, bsz)
            gates_t = jnp.transpose(gates[:, :, :16], (0, 2, 1))
            hf, hb = _mlstm_call(mqkv, gates, gates_t, lc)
            sink = jnp.broadcast_to(at_sink[j][:, None], (AT_HEADS, LANES))
            ya = _attn_call(aq, akv, sink, lc)
            z = _even_out_call(z, hf, hb, mo, ya, ml_out_g[j][None, :], ev_w_out[j].astype(BF16), mod, bsz)
        else:
            pad_g = RW_GATE_PAD - RW_GATE_LORA
            zeros = jnp.zeros((RW_DECAY_LORA, d), F32)
            w2 = jnp.concatenate([jnp.concatenate([rw_w2[j, 0], zeros], axis=1),
                                  jnp.concatenate([zeros, rw_w2[j, 1]], axis=1)], axis=0)
            a2 = jnp.concatenate([jnp.concatenate([rw_a2[j, 0], zeros], axis=1),
                                  jnp.concatenate([zeros, rw_a2[j, 1]], axis=1)], axis=0)
            sh, dr_, lw, gate, bonus = _rwkv_in_call(
                z, n1, mod, rw_mu[j].reshape(12, d), rw_w_rkv[j].astype(BF16),
                jnp.concatenate([rw_w1[j, 0], rw_w1[j, 1]], axis=1).astype(BF16),
                jnp.concatenate([rw_a1[j, 0], rw_a1[j, 1]], axis=1).astype(BF16),
                jnp.pad(rw_g1[j], ((0, 0), (0, pad_g))).astype(BF16),
                w2.astype(BF16), a2.astype(BF16),
                jnp.pad(rw_g2[j], ((0, pad_g), (0, 0))).astype(BF16),
                rw_w0[j], rw_a0[j], rw_k_k[j][None, :], rw_k_a[j][None, :], rw_r_k[j].reshape(1, d),
                grp_d, bsz, lc)
            yf, yb = _rwkv_scan_call(sh, dr_, lw, lc)
            z = _rwkv_out_call(z, yf, yb, gate, bonus, rw_ln_g[j][None, :], rw_ln_b[j][None, :], grp_d,
                               rw_w_out[j].astype(BF16), mod, bsz, lc)
        last = layer == depth - 1
        z = _mlp_call(z, norm2_g[layer][None, :], mod, mlp_w1[layer].astype(BF16), mlp_w2[layer].astype(BF16),
                      bsz, lc // TOKEN_TILE if last else 0)
    return z
```

```python
import functools

import jax
import jax.numpy as jnp
from jax import lax
from jax.experimental import pallas as pl
from jax.experimental.pallas import tpu as pltpu

F32 = jnp.float32
BF16 = jnp.bfloat16

D_MODEL = 1024
N_MOD = 6
D_FF = 4 * D_MODEL
EPS = 1e-6
GRID_W = 64

ML_HEADS = 4
ML_D = 128
ML_W = ML_HEADS * ML_D
ML_CHUNK = 128
GATE_CAP = 15.0

AT_DH = 64
AT_HEADS = 8
AT_KV_HEADS = 2
AT_W = AT_HEADS * AT_DH
AT_BLOCK = 128
ROPE_BASE = 10000.0

RW_HEAD = 64
RW_PAIRS = D_MODEL // 128
RW_CHUNK = 64
RW_LN_EPS = 64e-5
RW_DECAY_LORA = 64
RW_AAA_LORA = 64
RW_GATE_LORA = 160
RW_GATE_PAD = 256

LANES = 128
TOKEN_TILE = 256
FF_TILE = 1024
MOD_ROWS = 16
VMEM_LIMIT = 56 * 1024 * 1024
NEG = -1e30
RW_MM = BF16
RW_LOCKSTEP = 8

EV_MQKV = 0
EV_MO = 3 * ML_W
EV_AQ = EV_MO + ML_W
EV_AK = EV_AQ + AT_W
EV_AV = EV_AK + 256
EV_GATE = EV_AV + 256
EV_COLS = EV_GATE + LANES


def _dot(a, b):
    return jnp.dot(a, b, preferred_element_type=F32)


def _dot_tb(a, b):
    return lax.dot_general(a, b, (((1,), (1,)), ((), ())), preferred_element_type=F32)


def _dot_ta(a, b):
    return lax.dot_general(a, b, (((0,), (0,)), ((), ())), preferred_element_type=F32)


def _norm_mod(z, g, shift, scale):
    ms = jnp.mean(z * z, axis=-1, keepdims=True)
    return (z * lax.rsqrt(ms + EPS) * g) * (1.0 + scale) + shift


def _sigmoid(x):
    return 1.0 / (1.0 + jnp.exp(-x))


def _softplus(x):
    return jnp.maximum(x, 0.0) + jnp.log1p(jnp.exp(-jnp.abs(x)))


def _const_spec(shape):
    nd = len(shape)
    return pl.BlockSpec(shape, lambda *_: (0,) * nd)


def _params(sem):
    return pltpu.CompilerParams(dimension_semantics=sem, vmem_limit_bytes=VMEM_LIMIT)


def _ada_kernel(c_ref, w_ref, b_ref, o_ref):
    cv = c_ref[...]
    s = cv * _sigmoid(cv)
    o_ref[...] = jnp.dot(s, w_ref[...], preferred_element_type=F32,
                         precision=lax.Precision.HIGHEST) + b_ref[...]


def _ada_call(cstack, ada_w, ada_b):
    depth = ada_w.shape[0]
    n = N_MOD * D_MODEL
    tn = 1024
    return pl.pallas_call(
        _ada_kernel,
        grid=(depth, n // tn),
        in_specs=[pl.BlockSpec((MOD_ROWS, D_MODEL), lambda l, j: (0, 0)),
                  pl.BlockSpec((None, D_MODEL, tn), lambda l, j: (l, 0, j)),
                  pl.BlockSpec((None, 1, tn), lambda l, j: (l, 0, j))],
        out_specs=pl.BlockSpec((None, MOD_ROWS, tn), lambda l, j: (l, 0, j)),
        out_shape=jax.ShapeDtypeStruct((depth, MOD_ROWS, n), F32),
        compiler_params=_params(("parallel", "parallel")),
        name="ada_ln",
    )(cstack, ada_w, ada_b.reshape(depth, 1, n))


def _rope(x, cos, sin, lane_lo):
    n = x.shape[1]
    up = pltpu.roll(x, n - 16, 1)
    dn = pltpu.roll(x, 16, 1)
    reps = n // LANES
    c = jnp.concatenate([cos] * reps, axis=1) if reps > 1 else cos
    s = jnp.concatenate([sin] * reps, axis=1) if reps > 1 else sin
    lo = jnp.concatenate([lane_lo] * reps, axis=1) if reps > 1 else lane_lo
    return x * c + jnp.where(lo, up, dn) * s


def _even_in_kernel(z_ref, g_ref, mod_ref, w_ref, b_ref, gq_ref, gk_ref, grp_ref, cos_ref, sin_ref,
                    fb_ref, isf_ref, mqkv_ref, mo_ref, aq_ref, akv_ref, gate_ref):
    h = _norm_mod(z_ref[...], g_ref[...], mod_ref[0:1, :], mod_ref[1:2, :])
    p = _dot(h.astype(BF16), w_ref[...]) + b_ref[...]
    mqkv_ref[:, 0:ML_W] = p[:, 0:ML_W].astype(BF16)
    mqkv_ref[:, ML_W:2 * ML_W] = (p[:, ML_W:2 * ML_W] * (ML_D ** -0.5)).astype(BF16)
    mqkv_ref[:, 2 * ML_W:3 * ML_W] = p[:, 2 * ML_W:3 * ML_W].astype(BF16)
    mo_ref[...] = p[:, EV_MO:EV_MO + ML_W]
    cos = cos_ref[...]
    sin = sin_ref[...]
    lane_lo = (lax.broadcasted_iota(jnp.int32, (1, LANES), 1) % 32) < 16
    q = p[:, EV_AQ:EV_AQ + AT_W]
    qms = _dot((q * q).astype(BF16), grp_ref[...])
    q = q * lax.rsqrt(qms + EPS) * gq_ref[...]
    aq_ref[...] = (_rope(q, cos, sin, lane_lo) * (AT_DH ** -0.5)).astype(BF16)
    k = p[:, EV_AK:EV_AK + 256]
    kms = _dot((k * k).astype(BF16), grp_ref[0:256, 0:256])
    k = k * lax.rsqrt(kms + EPS) * gk_ref[...]
    akv_ref[:, 0:256] = _rope(k, cos, sin, lane_lo).astype(BF16)
    akv_ref[:, 256:512] = p[:, EV_AV:EV_AV + 256].astype(BF16)
    gt = p[:, EV_GATE:EV_GATE + LANES] + fb_ref[...]
    sc = GATE_CAP * jnp.tanh(gt * (1.0 / GATE_CAP))
    logsig = jnp.minimum(sc, 0.0) - jnp.log1p(jnp.exp(-jnp.abs(sc)))
    gate_ref[...] = jnp.where(isf_ref[...] > 0.5, logsig, sc)


def _even_in_call(z, g, mod, w, b, gq, gk, grp, cos, sin, fb, isf, nb):
    bsz, t, d = z.shape
    tm = TOKEN_TILE
    tok = lambda width: pl.BlockSpec((None, tm, width), lambda bi, i: (bi, i, 0))
    return pl.pallas_call(
        _even_in_kernel,
        grid=(bsz, t // tm),
        in_specs=[tok(d),
                  _const_spec((1, d)),
                  pl.BlockSpec((None, N_MOD, d), lambda bi, i: (jnp.where(i == 0, nb, bi), 0, 0)),
                  _const_spec((d, EV_COLS)),
                  _const_spec((1, EV_COLS)),
                  _const_spec((1, AT_W)),
                  _const_spec((1, 256)),
                  _const_spec((AT_W, AT_W)),
                  pl.BlockSpec((tm, LANES), lambda bi, i: (i, 0)),
                  pl.BlockSpec((tm, LANES), lambda bi, i: (i, 0)),
                  _const_spec((1, LANES)),
                  _const_spec((1, LANES))],
        out_specs=[tok(3 * ML_W), tok(ML_W), tok(AT_W), tok(512), tok(LANES)],
        out_shape=[jax.ShapeDtypeStruct((bsz, t, 3 * ML_W), BF16),
                   jax.ShapeDtypeStruct((bsz, t, ML_W), F32),
                   jax.ShapeDtypeStruct((bsz, t, AT_W), BF16),
                   jax.ShapeDtypeStruct((bsz, t, 512), BF16),
                   jax.ShapeDtypeStruct((bsz, t, LANES), F32)],
        compiler_params=_params(("parallel", "parallel")),
        name="even_in",
    )(z, g, mod, w, b, gq, gk, grp, cos, sin, fb, isf)


def _mlstm_kernel(qf_ref, kf_ref, vf_ref, qb_ref, kb_ref, vb_ref, gcf_ref, gcb_ref, grf_ref, grb_ref,
                  hf_ref, hb_ref, c_s, n_s, m_s):
    ln = ML_CHUNK

    @pl.when(pl.program_id(1) == 0)
    def _():
        c_s[...] = jnp.zeros_like(c_s)
        n_s[...] = jnp.zeros_like(n_s)
        m_s[...] = jnp.zeros_like(m_s)

    ri = lax.broadcasted_iota(jnp.int32, (ln, ln), 0)
    ci = lax.broadcasted_iota(jnp.int32, (ln, ln), 1)
    for d in range(2):
        q_ref, k_ref, v_ref, gc_ref, gr_ref, o_ref = (
            (qf_ref, kf_ref, vf_ref, gcf_ref, grf_ref, hf_ref) if d == 0 else
            (qb_ref, kb_ref, vb_ref, gcb_ref, grb_ref, hb_ref))
        before = (ci <= ri) if d == 0 else (ci >= ri)
        before_t = (ri <= ci) if d == 0 else (ri >= ci)
        for hd in range(ML_HEADS):
            st = d * ML_HEADS + hd
            ic_col, fc_col = 8 * d + hd, 8 * d + 4 + hd
            lanes = slice(hd * ML_D, (hd + 1) * ML_D)
            q = q_ref[:, lanes]
            k = k_ref[:, lanes]
            v = v_ref[:, lanes]
            i_row = gr_ref[ic_col:ic_col + 1, :]
            f_row = gr_ref[fc_col:fc_col + 1, :]
            i_col = gc_ref[:, ic_col:ic_col + 1]
            f_col = gc_ref[:, fc_col:fc_col + 1]
            bcum_col = jnp.sum(jnp.where(before, f_row, 0.0), axis=1, keepdims=True)
            bcum_row = jnp.sum(jnp.where(before_t, f_col, 0.0), axis=0, keepdims=True)
            m_old = m_s[st:st + 1, 0:1]
            dlog = jnp.where(before, bcum_col - bcum_row + i_row, NEG)
            inter = bcum_col + m_old
            mt = jnp.maximum(jnp.max(dlog, axis=1, keepdims=True), inter)
            dw = jnp.exp(dlog - mt)
            iw = jnp.exp(inter - mt)
            sc = _dot_tb(q, k) * dw
            ct = c_s[st]
            num = _dot(sc.astype(BF16), v) + iw * _dot(q, ct.astype(BF16))
            nvec = n_s[st:st + 1, :]
            qn = jnp.sum(q.astype(F32) * nvec, axis=1, keepdims=True)
            den = jnp.maximum(jnp.abs(jnp.sum(sc, axis=1, keepdims=True) + iw * qn), jnp.exp(-mt))
            o_ref[:, lanes] = num / den
            bl = jnp.sum(f_row, axis=1, keepdims=True)
            ws_log = bl - bcum_col + i_col
            m_new = jnp.maximum(bl + m_old, jnp.max(ws_log, axis=0, keepdims=True))
            ws = jnp.exp(ws_log - m_new)
            decay = jnp.exp(bl + m_old - m_new)
            kf32 = k.astype(F32)
            c_s[st] = decay * ct + _dot_ta(k, (v.astype(F32) * ws).astype(BF16))
            n_s[st:st + 1, :] = decay * nvec + jnp.sum(kf32 * ws, axis=0, keepdims=True)
            m_s[st:st + 1, :] = jnp.broadcast_to(m_new, (1, LANES))


def _rev_chunk(i, nctx, n):
    return jnp.where(i < nctx, nctx - 1 - i, n + nctx - 1 - i)


def _mlstm_call(mqkv, gates, gates_t, lc):
    bsz, t, _ = mqkv.shape
    ln = ML_CHUNK
    n = t // ln
    nctx = lc // ln
    fwd = lambda col: pl.BlockSpec((None, ln, ML_W), lambda bi, i: (bi, i, col))
    bwd = lambda col: pl.BlockSpec((None, ln, ML_W), lambda bi, i: (bi, _rev_chunk(i, nctx, n), col))
    return pl.pallas_call(
        _mlstm_kernel,
        grid=(bsz, n),
        in_specs=[fwd(0), fwd(1), fwd(2), bwd(0), bwd(1), bwd(2),
                  pl.BlockSpec((None, ln, LANES), lambda bi, i: (bi, i, 0)),
                  pl.BlockSpec((None, ln, LANES), lambda bi, i: (bi, _rev_chunk(i, nctx, n), 0)),
                  pl.BlockSpec((None, 16, ln), lambda bi, i: (bi, 0, i)),
                  pl.BlockSpec((None, 16, ln), lambda bi, i: (bi, 0, _rev_chunk(i, nctx, n)))],
        out_specs=[pl.BlockSpec((None, ln, ML_W), lambda bi, i: (bi, i, 0)),
                   pl.BlockSpec((None, ln, ML_W), lambda bi, i: (bi, _rev_chunk(i, nctx, n), 0))],
        out_shape=[jax.ShapeDtypeStruct((bsz, t, ML_W), F32)] * 2,
        scratch_shapes=[pltpu.VMEM((2 * ML_HEADS, ML_D, ML_D), F32),
                        pltpu.VMEM((2 * ML_HEADS, LANES), F32),
                        pltpu.VMEM((2 * ML_HEADS, LANES), F32)],
        compiler_params=_params(("parallel", "arbitrary")),
        name="mlstm_scan",
    )(mqkv, mqkv, mqkv, mqkv, mqkv, mqkv, gates, gates, gates_t, gates_t)


def _attn_kernel(q_ref, kvp_ref, kvo_ref, kvn_ref, kvc_ref, sink_ref, o_ref, *, nctx, n):
    j = pl.program_id(1)
    blk = AT_BLOCK
    latent = j >= nctx
    qi = lax.broadcasted_iota(jnp.int32, (blk, blk), 0)
    ki = lax.broadcasted_iota(jnp.int32, (blk, blk), 1)
    ok_prev = jnp.logical_and(jnp.logical_and(latent, j > nctx), ki >= qi)
    ok_own = jnp.logical_and(latent, ki >= 0)
    ok_next = jnp.logical_and(jnp.logical_and(latent, j < n - 1), ki <= qi)
    lane = lax.broadcasted_iota(jnp.int32, (1, LANES), 1)
    half = (lane < AT_DH, lane >= AT_DH)
    pieces = ((kvp_ref, ok_prev), (kvo_ref, ok_own), (kvn_ref, ok_next), (kvc_ref, None))
    for c in range(AT_HEADS // 2):
        g = c // 2
        qc = q_ref[:, c * LANES:(c + 1) * LANES]
        acc = None
        for e in range(2):
            hd = 2 * c + e
            sink = sink_ref[hd:hd + 1, 0:1]
            ss = []
            m = sink
            for ref, ok in pieces:
                kx = jnp.where(half[e], ref[:, g * LANES:(g + 1) * LANES], jnp.zeros((), BF16))
                s = _dot_tb(qc, kx)
                if ok is not None:
                    s = jnp.where(ok, s, NEG)
                ss.append(s)
                m = jnp.maximum(m, jnp.max(s, axis=1, keepdims=True))
            den = jnp.exp(sink - m)
            o = None
            for (ref, ok), s in zip(pieces, ss):
                pr = jnp.exp(s - m)
                den = den + jnp.sum(pr, axis=1, keepdims=True)
                vx = jnp.where(half[e], ref[:, 256 + g * LANES:256 + (g + 1) * LANES], jnp.zeros((), BF16))
                pv = _dot(pr.astype(BF16), vx)
                o = pv if o is None else o + pv
            o = o / den
            acc = o if acc is None else acc + o
        o_ref[:, c * LANES:(c + 1) * LANES] = acc.astype(o_ref.dtype)


def _attn_call(aq, akv, sink, lc):
    bsz, t, _ = aq.shape
    blk = AT_BLOCK
    n = t // blk
    nctx = lc // blk
    kv = lambda f: pl.BlockSpec((None, blk, 512), lambda bi, j: (bi, f(j), 0))
    return pl.pallas_call(
        functools.partial(_attn_kernel, nctx=nctx, n=n),
        grid=(bsz, n),
        in_specs=[pl.BlockSpec((None, blk, AT_W), lambda bi, j: (bi, j, 0)),
                  kv(lambda j: jnp.clip(j - 1, nctx, n - 1)),
                  kv(lambda j: jnp.clip(j, nctx, n - 1)),
                  kv(lambda j: jnp.clip(j + 1, nctx, n - 1)),
                  pl.BlockSpec((None, lc, 512), lambda bi, j: (bi, 0, 0)),
                  _const_spec((AT_HEADS, LANES))],
        out_specs=pl.BlockSpec((None, blk, AT_W), lambda bi, j: (bi, j, 0)),
        out_shape=jax.ShapeDtypeStruct((bsz, t, AT_W), BF16),
        compiler_params=_params(("parallel", "parallel")),
        name="window_attn",
    )(aq, akv, akv, akv, akv, sink)


def _even_out_kernel(z_ref, hf_ref, hb_ref, mo_ref, ya_ref, og_ref, w_ref, mod_ref, o_ref):
    hs = hf_ref[...] + hb_ref[...]
    parts = []
    for hd in range(ML_HEADS):
        x = hs[:, hd * ML_D:(hd + 1) * ML_D]
        parts.append(x * lax.rsqrt(jnp.mean(x * x, axis=1, keepdims=True) + EPS))
    ym = jnp.concatenate(parts, axis=1) * og_ref[...] * _sigmoid(mo_ref[...])
    y = _dot(ym.astype(BF16), w_ref[0:ML_W, :]) + _dot(ya_ref[...], w_ref[ML_W:ML_W + AT_W, :])
    o_ref[...] = z_ref[...] + mod_ref[2:3, :] * y


def _even_out_call(z, hf, hb, mo, ya, og, w, mod, nb):
    bsz, t, d = z.shape
    tm = TOKEN_TILE
    tok = lambda width: pl.BlockSpec((None, tm, width), lambda bi, i: (bi, i, 0))
    return pl.pallas_call(
        _even_out_kernel,
        grid=(bsz, t // tm),
        in_specs=[tok(d), tok(ML_W), tok(ML_W), tok(ML_W), tok(AT_W),
                  _const_spec((1, ML_W)),
                  _const_spec((ML_W + AT_W, d)),
                  pl.BlockSpec((None, N_MOD, d), lambda bi, i: (jnp.where(i == 0, nb, bi), 0, 0))],
        out_specs=tok(d),
        out_shape=jax.ShapeDtypeStruct((bsz, t, d), F32),
        compiler_params=_params(("parallel", "parallel")),
        name="even_out",
    )(z, hf, hb, mo, ya, og, w, mod)


def _mlp_kernel(z_ref, g_ref, mod_ref, w1_ref, w2_ref, o_ref):
    z = z_ref[...]
    h = _norm_mod(z, g_ref[...], mod_ref[3:4, :], mod_ref[4:5, :]).astype(BF16)
    acc = jnp.zeros(z.shape, F32)
    for f in range(D_FF // FF_TILE):
        a = jnp.maximum(_dot(h, w1_ref[:, f * FF_TILE:(f + 1) * FF_TILE]), 0.0)
        acc = acc + _dot((a * a).astype(BF16), w2_ref[f * FF_TILE:(f + 1) * FF_TILE, :])
    o_ref[...] = z + mod_ref[5:6, :] * acc


def _mlp_call(z, g, mod, w1, w2, nb, skip):
    bsz, t, d = z.shape
    tm = TOKEN_TILE
    nt = t // tm - skip
    mod_idx = (lambda bi, i: (bi, 0, 0)) if skip else (lambda bi, i: (jnp.where(i == 0, nb, bi), 0, 0))
    return pl.pallas_call(
        _mlp_kernel,
        grid=(bsz, nt),
        in_specs=[pl.BlockSpec((None, tm, d), lambda bi, i: (bi, i + skip, 0)),
                  _const_spec((1, d)),
                  pl.BlockSpec((None, N_MOD, d), mod_idx),
                  _const_spec((d, D_FF)),
                  _const_spec((D_FF, d))],
        out_specs=pl.BlockSpec((None, tm, d), lambda bi, i: (bi, i, 0)),
        out_shape=jax.ShapeDtypeStruct((bsz, nt * tm, d), F32),
        compiler_params=_params(("parallel", "parallel")),
        name="mlp",
    )(z, g, mod, w1, w2)


def _rwkv_in_kernel(z_ref, zp_ref, zn_ref, g_ref, mod_ref, mu_ref, wrkv_ref, w1_ref, a1_ref, g1_ref,
                    w2_ref, a2_ref, g2_ref, w0_ref, a0_ref, kk_ref, ka_ref, rk_ref, grp_ref,
                    sh_ref, dr_ref, lw_ref, gate_ref, bonus_ref, *, nctx, ntile):
    i = pl.program_id(1)
    tm, d = z_ref.shape
    g = g_ref[...]
    shift = mod_ref[0:1, :]
    scale = mod_ref[1:2, :]
    h = _norm_mod(z_ref[...], g, shift, scale)
    no_prev = jnp.logical_or(i == 0, i == nctx)
    no_next = jnp.logical_or(i == nctx - 1, i == ntile - 1)
    hp = jnp.where(no_prev, 0.0, _norm_mod(zp_ref[7:8, :], g, shift, scale))
    hn = jnp.where(no_next, 0.0, _norm_mod(zn_ref[0:1, :], g, shift, scale))
    row = lax.broadcasted_iota(jnp.int32, (tm, 1), 0)
    dp = jnp.where(row == 0, hp, pltpu.roll(h, 1, 0)) - h
    dn = jnp.where(row == tm - 1, hn, pltpu.roll(h, tm - 1, 0)) - h

    def mix(n):
        return (h + mu_ref[2 * n:2 * n + 1, :] * dp + mu_ref[2 * n + 1:2 * n + 2, :] * dn).astype(BF16)

    r = _dot(mix(0), wrkv_ref[0])
    k = _dot(mix(2), wrkv_ref[1])
    v = _dot(mix(3), wrkv_ref[2])
    gate_ref[...] = _dot(_sigmoid(_dot(mix(5), g1_ref[...])).astype(BF16), g2_ref[...])
    lora_w = _dot(jnp.tanh(_dot(mix(1), w1_ref[...])).astype(BF16), w2_ref[...])
    lora_a = _dot(_dot(mix(4), a1_ref[...]).astype(BF16), a2_ref[...])
    kkr = k * kk_ref[...]
    ssq = _dot((kkr * kkr).astype(BF16), grp_ref[...]) * float(RW_HEAD)
    kk = kkr * lax.rsqrt(jnp.maximum(ssq, 1e-24))
    kd_sum = None
    for dr in range(2):
        cols = slice(dr * d, (dr + 1) * d)
        w_log = -_softplus(-(w0_ref[dr:dr + 1, :] + lora_w[:, cols])) - 0.5
        lw = -jnp.exp(w_log)
        a = _sigmoid(a0_ref[dr:dr + 1, :] + lora_a[:, cols])
        kd = k * (1.0 + (a - 1.0) * ka_ref[...])
        bvec = kk * a
        kd_sum = kd if kd_sum is None else kd_sum + kd
        for p in range(RW_PAIRS):
            ls = slice(p * LANES, (p + 1) * LANES)
            lw_ref[dr, p] = lw[:, ls]
            dr_ref[dr, 0, p] = kd[:, ls].astype(BF16)
            dr_ref[dr, 1, p] = bvec[:, ls].astype(BF16)
    for p in range(RW_PAIRS):
        ls = slice(p * LANES, (p + 1) * LANES)
        sh_ref[0, p] = r[:, ls].astype(BF16)
        sh_ref[1, p] = v[:, ls].astype(BF16)
        sh_ref[2, p] = kk[:, ls].astype(BF16)
    bsum = _dot((r * kd_sum * rk_ref[...]).astype(BF16), grp_ref[...]) * float(RW_HEAD)
    bonus_ref[...] = bsum * v


def _rwkv_in_call(z, g, mod, mu, wrkv, w1, a1, g1, w2, a2, g2, w0, a0, k_k, k_a, r_k, grp, nb, lc):
    bsz, t, d = z.shape
    tm = TOKEN_TILE
    ntile = t // tm
    nctx = lc // tm
    r8 = tm // 8
    tok = pl.BlockSpec((None, tm, d), lambda bi, i: (bi, i, 0))
    return pl.pallas_call(
        functools.partial(_rwkv_in_kernel, nctx=nctx, ntile=ntile),
        grid=(bsz, ntile),
        in_specs=[tok,
                  pl.BlockSpec((None, 8, d), lambda bi, i: (bi, jnp.maximum(i * r8 - 1, 0), 0)),
                  pl.BlockSpec((None, 8, d), lambda bi, i: (bi, jnp.minimum((i + 1) * r8, t // 8 - 1), 0)),
                  _const_spec((1, d)),
                  pl.BlockSpec((None, N_MOD, d), lambda bi, i: (jnp.where(i < nctx, nb, bi), 0, 0)),
                  _const_spec((12, d)),
                  _const_spec((3, d, d)),
                  _const_spec((d, LANES)),
                  _const_spec((d, LANES)),
                  _const_spec((d, RW_GATE_PAD)),
                  _const_spec((LANES, 2 * d)),
                  _const_spec((LANES, 2 * d)),
                  _const_spec((RW_GATE_PAD, d)),
                  _const_spec((2, d)),
                  _const_spec((2, d)),
                  _const_spec((1, d)),
                  _const_spec((1, d)),
                  _const_spec((1, d)),
                  _const_spec((d, d))],
        out_specs=[pl.BlockSpec((None, 3, RW_PAIRS, tm, LANES), lambda bi, i: (bi, 0, 0, i, 0)),
                   pl.BlockSpec((None, 2, 2, RW_PAIRS, tm, LANES), lambda bi, i: (bi, 0, 0, 0, i, 0)),
                   pl.BlockSpec((None, 2, RW_PAIRS, tm, LANES), lambda bi, i: (bi, 0, 0, i, 0)),
                   tok, tok],
        out_shape=[jax.ShapeDtypeStruct((bsz, 3, RW_PAIRS, t, LANES), BF16),
                   jax.ShapeDtypeStruct((bsz, 2, 2, RW_PAIRS, t, LANES), BF16),
                   jax.ShapeDtypeStruct((bsz, 2, RW_PAIRS, t, LANES), F32),
                   jax.ShapeDtypeStruct((bsz, t, d), F32),
                   jax.ShapeDtypeStruct((bsz, t, d), F32)],
        compiler_params=_params(("parallel", "parallel")),
        name="rwkv_in",
    )(z, z, z, g, mod, mu, wrkv, w1, a1, g1, w2, a2, g2, w0, a0, k_k, k_a, r_k, grp)


def _rwkv_chunk_group(chains):
    ln = RW_CHUNK
    row = lax.broadcasted_iota(jnp.int32, (ln, 2 * ln), 0)
    col = lax.broadcasted_iota(jnp.int32, (ln, 2 * ln), 1)
    sidx = jnp.where(col >= ln, col - ln, col)
    lane_a = lax.broadcasted_iota(jnp.int32, (1, LANES), 1) < RW_HEAD
    tr = lax.broadcasted_iota(jnp.int32, (ln, ln), 0)
    tc = lax.broadcasted_iota(jnp.int32, (ln, ln), 1)
    tri = {False: (tc <= tr).astype(F32), True: (tc >= tr).astype(F32)}
    strict = {False: sidx < row, True: sidx > row}
    incl = {False: sidx <= row, True: sidx >= row}
    eye_w = jnp.where(sidx == row, 1.0, 0.0)
    diff = row ^ sidx
    vr = lax.broadcasted_iota(jnp.int32, (LANES, LANES), 0) < RW_HEAD
    kc = lax.broadcasted_iota(jnp.int32, (LANES, LANES), 1) < RW_HEAD
    same_head = vr == kc

    def bd(x):
        zero = jnp.zeros((), x.dtype)
        return jnp.concatenate([jnp.where(lane_a, x, zero), jnp.where(lane_a, zero, x)], axis=0)

    revs = [c[7] for c in chains]
    vs = [c[1] for c in chains]
    s_olds = [c[6] for c in chains]
    gcum = [jnp.dot(tri[c[7]], c[5], preferred_element_type=F32, precision=lax.Precision.HIGHEST) for c in chains]
    gtot = [jnp.sum(c[5], axis=0, keepdims=True) for c in chains]

    def scaled(c, g, gt):
        r, v, kk, kd, bv, lw = (x.astype(F32) for x in c[:6])
        e_pos = jnp.exp(g)
        e_neg = jnp.exp(-g)
        e_end = jnp.exp(gt - g)
        ar = jnp.concatenate([(-kk * jnp.exp(g - lw)).astype(RW_MM), (r * e_pos).astype(RW_MM)], axis=0)
        bk_end = jnp.concatenate([(bv * e_end).astype(RW_MM), (kd * e_end).astype(RW_MM)], axis=0)
        return ar, (bv * e_neg).astype(RW_MM), (kd * e_neg).astype(RW_MM), bk_end

    sc = [scaled(c, g, gt) for c, g, gt in zip(chains, gcum, gtot)]
    ars_ = [x[0] for x in sc]
    x_b = [_dot_tb(x[0], bd(x[1])) for x in sc]
    x_k = [_dot_tb(x[0], bd(x[2])) for x in sc]
    n_w = [jnp.where(strict[rv], x[:ln], 0.0) for x, rv in zip(x_b, revs)]
    m_rb = [jnp.where(incl[rv], x[ln:], 0.0).astype(RW_MM) for x, rv in zip(x_b, revs)]
    m_k = [jnp.concatenate([jnp.where(strict[rv], x[:ln], 0.0), jnp.where(incl[rv], x[ln:], 0.0)],
                           axis=0).astype(RW_MM) for x, rv in zip(x_k, revs)]
    x_w = [eye_w + jnp.where(diff == 1, n, 0.0) for n in n_w]
    h = 2
    while h < ln:
        lvl = jnp.logical_and(diff >= h, diff < 2 * h)
        tmp = [_dot(jnp.where(lvl, n, 0.0).astype(RW_MM), bd(x.astype(RW_MM))) for n, x in zip(n_w, x_w)]
        x_w = [x + _dot(x.astype(RW_MM), bd(t.astype(RW_MM))) for x, t in zip(x_w, tmp)]
        h *= 2
    ars = [_dot_tb(a, s.astype(RW_MM)) for a, s in zip(ars_, s_olds)]
    mv = [_dot(m, bd(v)) for m, v in zip(m_k, vs)]
    u = [_dot(x.astype(RW_MM), bd((a[:ln] + m[:ln]).astype(RW_MM))) for x, a, m in zip(x_w, ars, mv)]
    y = [a[ln:] + m[ln:] + _dot(rb, bd(uu.astype(RW_MM))) for a, m, rb, uu in zip(ars, mv, m_rb, u)]
    upd = [_dot_ta(jnp.concatenate([uu.astype(RW_MM), v], axis=0), x[3]) for uu, v, x in zip(u, vs, sc)]
    s_new = [s * jnp.exp(gt) + jnp.where(same_head, up, 0.0) for s, gt, up in zip(s_olds, gtot, upd)]
    return list(zip(y, s_new))


def _rwkv_scan_kernel(shf_ref, shb_ref, drf_ref, drb_ref, lwf_ref, lwb_ref, yf_ref, yb_ref, s_s):
    @pl.when(pl.program_id(1) == 0)
    def _():
        s_s[...] = jnp.zeros_like(s_s)

    for p0 in range(0, RW_PAIRS, RW_LOCKSTEP):
        chains = []
        for p in range(p0, p0 + RW_LOCKSTEP):
            chains.append((shf_ref[0, p], shf_ref[1, p], shf_ref[2, p], drf_ref[0, p], drf_ref[1, p],
                           lwf_ref[p], s_s[0, p], False))
            chains.append((shb_ref[0, p], shb_ref[1, p], shb_ref[2, p], drb_ref[0, p], drb_ref[1, p],
                           lwb_ref[p], s_s[1, p], True))
        outs = _rwkv_chunk_group(chains)
        for i, (y, s_new) in enumerate(outs):
            p, d = p0 + i // 2, i % 2
            (yf_ref if d == 0 else yb_ref)[p] = y
            s_s[d, p] = s_new


def _rwkv_scan_call(sh, dr, lw, lc):
    bsz, _, _, t, _ = sh.shape
    ln = RW_CHUNK
    n = t // ln
    nctx = lc // ln
    rv = lambda i: _rev_chunk(i, nctx, n)
    return pl.pallas_call(
        _rwkv_scan_kernel,
        grid=(bsz, n),
        in_specs=[pl.BlockSpec((None, 3, RW_PAIRS, ln, LANES), lambda bi, i: (bi, 0, 0, i, 0)),
                  pl.BlockSpec((None, 3, RW_PAIRS, ln, LANES), lambda bi, i: (bi, 0, 0, rv(i), 0)),
                  pl.BlockSpec((None, None, 2, RW_PAIRS, ln, LANES), lambda bi, i: (bi, 0, 0, 0, i, 0)),
                  pl.BlockSpec((None, None, 2, RW_PAIRS, ln, LANES), lambda bi, i: (bi, 1, 0, 0, rv(i), 0)),
                  pl.BlockSpec((None, None, RW_PAIRS, ln, LANES), lambda bi, i: (bi, 0, 0, i, 0)),
                  pl.BlockSpec((None, None, RW_PAIRS, ln, LANES), lambda bi, i: (bi, 1, 0, rv(i), 0))],
        out_specs=[pl.BlockSpec((None, RW_PAIRS, ln, LANES), lambda bi, i: (bi, 0, i, 0)),
                   pl.BlockSpec((None, RW_PAIRS, ln, LANES), lambda bi, i: (bi, 0, rv(i), 0))],
        out_shape=[jax.ShapeDtypeStruct((bsz, RW_PAIRS, t, LANES), F32)] * 2,
        scratch_shapes=[pltpu.VMEM((2, RW_PAIRS, LANES, LANES), F32)],
        compiler_params=_params(("parallel", "arbitrary")),
        name="rwkv_scan",
    )(sh, sh, dr, dr, lw, lw)


def _rwkv_out_kernel(z_ref, yf_ref, yb_ref, gate_ref, bonus_ref, lng_ref, lnb_ref, grp_ref, w_ref, mod_ref, o_ref):
    y = jnp.concatenate([yf_ref[p] + yb_ref[p] for p in range(RW_PAIRS)], axis=1)
    mean = _dot(y.astype(BF16), grp_ref[...])
    yc = y - mean
    var = _dot((yc * yc).astype(BF16), grp_ref[...])
    yn = yc * lax.rsqrt(var + RW_LN_EPS) * lng_ref[...] + lnb_ref[...] + bonus_ref[...]
    out = _dot((yn * gate_ref[...]).astype(BF16), w_ref[...])
    o_ref[...] = z_ref[...] + mod_ref[2:3, :] * out


def _rwkv_out_call(z, yf, yb, gate, bonus, ln_g, ln_b, grp, w, mod, nb, lc):
    bsz, t, d = z.shape
    tm = TOKEN_TILE
    nctx = lc // tm
    tok = pl.BlockSpec((None, tm, d), lambda bi, i: (bi, i, 0))
    pair = pl.BlockSpec((None, RW_PAIRS, tm, LANES), lambda bi, i: (bi, 0, i, 0))
    return pl.pallas_call(
        _rwkv_out_kernel,
        grid=(bsz, t // tm),
        in_specs=[tok, pair, pair, tok, tok,
                  _const_spec((1, d)), _const_spec((1, d)), _const_spec((d, d)), _const_spec((d, d)),
                  pl.BlockSpec((None, N_MOD, d), lambda bi, i: (jnp.where(i < nctx, nb, bi), 0, 0))],
        out_specs=tok,
        out_shape=jax.ShapeDtypeStruct((bsz, t, d), F32),
        compiler_params=_params(("parallel", "parallel")),
        name="rwkv_out",
    )(z, yf, yb, gate, bonus, ln_g, ln_b, grp, w, mod)


def _group_mean_matrix(n, width):
    idx = jnp.arange(n) // width
    return ((idx[:, None] == idx[None, :]).astype(F32) / width).astype(BF16)


def _rope_tables(lc, s):
    quarter = AT_DH // 4
    inv = ROPE_BASE ** (-jnp.arange(quarter, dtype=F32) / quarter)
    pos = jnp.arange(s)
    rpos = (pos // GRID_W).astype(F32)
    cpos = (pos % GRID_W).astype(F32)
    ang_r = rpos[:, None] * inv[None, :]
    ang_c = cpos[:, None] * inv[None, :]
    cos64 = jnp.concatenate([jnp.cos(ang_r), jnp.cos(ang_r), jnp.cos(ang_c), jnp.cos(ang_c)], axis=1)
    sin64 = jnp.concatenate([-jnp.sin(ang_r), jnp.sin(ang_r), -jnp.sin(ang_c), jnp.sin(ang_c)], axis=1)
    cos = jnp.concatenate([jnp.ones((lc, AT_DH), F32), cos64], axis=0)
    sin = jnp.concatenate([jnp.zeros((lc, AT_DH), F32), sin64], axis=0)
    return jnp.tile(cos, (1, 2)), jnp.tile(sin, (1, 2))


def _even_weights(w_in, b_in):
    def cols(m):
        mq, mk, mv, mo, mg, aq, ak, av = jnp.split(m, [512, 1024, 1536, 2048, 2064, 2576, 2704], axis=-1)
        dup = lambda u: jnp.concatenate([u[..., :64], u[..., :64], u[..., 64:], u[..., 64:]], axis=-1)
        mgp = jnp.pad(mg, [(0, 0)] * (m.ndim - 1) + [(0, LANES - 16)])
        return jnp.concatenate([mq, mk, mv, mo, aq, dup(ak), dup(av), mgp], axis=-1)
    return cols(w_in).astype(BF16), cols(b_in[None, :])


def kernel(x, c, ctx, c_ctx, ada_w, ada_b, norm1_g, norm2_g, mlp_w1, mlp_w2, ev_w_in, ev_b_in, ev_w_out, ml_f_bias, ml_out_g, at_q_g, at_k_g, at_sink, rw_mu, rw_w_rkv, rw_w0, rw_w1, rw_w2, rw_a0, rw_a1, rw_a2, rw_g1, rw_g2, rw_k_k, rw_k_a, rw_r_k, rw_ln_g, rw_ln_b, rw_w_out):
    bsz, s, d = x.shape
    lc = ctx.shape[1]
    depth = ada_w.shape[0]
    assert d == D_MODEL and lc == TOKEN_TILE and s % TOKEN_TILE == 0 and bsz < MOD_ROWS
    z = jnp.concatenate([ctx, x], axis=1)

    cstack = jnp.zeros((MOD_ROWS, d), F32).at[:bsz].set(c).at[bsz].set(c_ctx)
    mod_all = _ada_call(cstack, ada_w, ada_b).reshape(depth, MOD_ROWS, N_MOD, d)

    grp512 = _group_mean_matrix(AT_W, AT_DH)
    grp_d = _group_mean_matrix(d, RW_HEAD)
    cos_t, sin_t = _rope_tables(lc, s)
    gate_is_f = ((jnp.arange(LANES) % 8 >= 4) & (jnp.arange(LANES) < 16)).astype(F32)[None, :]

    for layer in range(depth):
        mod = mod_all[layer]
        j = layer // 2
        n1 = norm1_g[layer][None, :]
        if layer % 2 == 0:
            w_in, b_in = _even_weights(ev_w_in[j], ev_b_in[j])
            fb = jnp.zeros((LANES,), F32)
            for dr in range(2):
                fb = fb.at[8 * dr + 4:8 * dr + 8].set(ml_f_bias[j, dr])
            mqkv, mo, aq, akv, gates = _even_in_call(
                z, n1, mod, w_in, b_in, jnp.tile(at_q_g[j], AT_HEADS)[None, :], jnp.tile(at_k_g[j], 4)[None, :],
                grp512, cos_t, sin_t, fb[None, :], gate_is_f, bsz)
            gates_t = jnp.transpose(gates[:, :, :16], (0, 2, 1))
            hf, hb = _mlstm_call(mqkv, gates, gates_t, lc)
            sink = jnp.broadcast_to(at_sink[j][:, None], (AT_HEADS, LANES))
            ya = _attn_call(aq, akv, sink, lc)
            z = _even_out_call(z, hf, hb, mo, ya, ml_out_g[j][None, :], ev_w_out[j].astype(BF16), mod, bsz)
        else:
            pad_g = RW_GATE_PAD - RW_GATE_LORA
            zeros = jnp.zeros((RW_DECAY_LORA, d), F32)
            w2 = jnp.concatenate([jnp.concatenate([rw_w2[j, 0], zeros], axis=1),
                                  jnp.concatenate([zeros, rw_w2[j, 1]], axis=1)], axis=0)
            a2 = jnp.concatenate([jnp.concatenate([rw_a2[j, 0], zeros], axis=1),
                                  jnp.concatenate([zeros, rw_a2[j, 1]], axis=1)], axis=0)
            sh, dr_, lw, gate, bonus = _rwkv_in_call(
                z, n1, mod, rw_mu[j].reshape(12, d), rw_w_rkv[j].astype(BF16),
                jnp.concatenate([rw_w1[j, 0], rw_w1[j, 1]], axis=1).astype(BF16),
                jnp.concatenate([rw_a1[j, 0], rw_a1[j, 1]], axis=1).astype(BF16),
                jnp.pad(rw_g1[j], ((0, 0), (0, pad_g))).astype(BF16),
                w2.astype(BF16), a2.astype(BF16),
                jnp.pad(rw_g2[j], ((0, pad_g), (0, 0))).astype(BF16),
                rw_w0[j], rw_a0[j], rw_k_k[j][None, :], rw_k_a[j][None, :], rw_r_k[j].reshape(1, d),
                grp_d, bsz, lc)
            yf, yb = _rwkv_scan_call(sh, dr_, lw, lc)
            z = _rwkv_out_call(z, yf, yb, gate, bonus, rw_ln_g[j][None, :], rw_ln_b[j][None, :], grp_d,
                               rw_w_out[j].astype(BF16), mod, bsz, lc)
        last = layer == depth - 1
        z = _mlp_call(z, norm2_g[layer][None, :], mod, mlp_w1[layer].astype(BF16), mlp_w2[layer].astype(BF16),
                      bsz, lc // TOKEN_TILE if last else 0)
    return z
```

```python
import functools

import jax
import jax.numpy as jnp
from jax import lax
from jax.experimental import pallas as pl
from jax.experimental.pallas import tpu as pltpu

F32 = jnp.float32
BF16 = jnp.bfloat16

D_MODEL = 1024
N_MOD = 6
D_FF = 4 * D_MODEL
EPS = 1e-6
GRID_W = 64

ML_HEADS = 4
ML_D = 128
ML_W = ML_HEADS * ML_D
ML_CHUNK = 128
GATE_CAP = 15.0

AT_DH = 64
AT_HEADS = 8
AT_KV_HEADS = 2
AT_W = AT_HEADS * AT_DH
AT_BLOCK = 128
ROPE_BASE = 10000.0

RW_HEAD = 64
RW_PAIRS = D_MODEL // 128
RW_CHUNK = 64
RW_LN_EPS = 64e-5
RW_DECAY_LORA = 64
RW_AAA_LORA = 64
RW_GATE_LORA = 160
RW_GATE_PAD = 256

LANES = 128
TOKEN_TILE = 256
FF_TILE = 1024
MOD_ROWS = 16
VMEM_LIMIT = 56 * 1024 * 1024
NEG = -1e30
RW_MM = BF16
RW_LOCKSTEP = 8

EV_MQKV = 0
EV_MO = 3 * ML_W
EV_AQ = EV_MO + ML_W
EV_AK = EV_AQ + AT_W
EV_AV = EV_AK + 256
EV_GATE = EV_AV + 256
EV_COLS = EV_GATE + LANES


def _dot(a, b):
    return jnp.dot(a, b, preferred_element_type=F32)


def _dot_tb(a, b):
    return lax.dot_general(a, b, (((1,), (1,)), ((), ())), preferred_element_type=F32)


def _dot_ta(a, b):
    return lax.dot_general(a, b, (((0,), (0,)), ((), ())), preferred_element_type=F32)


def _norm_mod(z, g, shift, scale):
    ms = jnp.mean(z * z, axis=-1, keepdims=True)
    return (z * lax.rsqrt(ms + EPS) * g) * (1.0 + scale) + shift


def _sigmoid(x):
    return 1.0 / (1.0 + jnp.exp(-x))


def _softplus(x):
    return jnp.maximum(x, 0.0) + jnp.log1p(jnp.exp(-jnp.abs(x)))


def _const_spec(shape):
    nd = len(shape)
    return pl.BlockSpec(shape, lambda *_: (0,) * nd)


def _params(sem):
    return pltpu.CompilerParams(dimension_semantics=sem, vmem_limit_bytes=VMEM_LIMIT)


def _ada_kernel(c_ref, w_ref, b_ref, o_ref):
    cv = c_ref[...]
    s = cv * _sigmoid(cv)
    o_ref[...] = jnp.dot(s, w_ref[...], preferred_element_type=F32,
                         precision=lax.Precision.HIGHEST) + b_ref[...]


def _ada_call(cstack, ada_w, ada_b):
    depth = ada_w.shape[0]
    n = N_MOD * D_MODEL
    tn = 1024
    return pl.pallas_call(
        _ada_kernel,
        grid=(depth, n // tn),
        in_specs=[pl.BlockSpec((MOD_ROWS, D_MODEL), lambda l, j: (0, 0)),
                  pl.BlockSpec((None, D_MODEL, tn), lambda l, j: (l, 0, j)),
                  pl.BlockSpec((None, 1, tn), lambda l, j: (l, 0, j))],
        out_specs=pl.BlockSpec((None, MOD_ROWS, tn), lambda l, j: (l, 0, j)),
        out_shape=jax.ShapeDtypeStruct((depth, MOD_ROWS, n), F32),
        compiler_params=_params(("parallel", "parallel")),
        name="ada_ln",
    )(cstack, ada_w, ada_b.reshape(depth, 1, n))


def _rope(x, cos, sin, lane_lo):
    n = x.shape[1]
    up = pltpu.roll(x, n - 16, 1)
    dn = pltpu.roll(x, 16, 1)
    reps = n // LANES
    c = jnp.concatenate([cos] * reps, axis=1) if reps > 1 else cos
    s = jnp.concatenate([sin] * reps, axis=1) if reps > 1 else sin
    lo = jnp.concatenate([lane_lo] * reps, axis=1) if reps > 1 else lane_lo
    return x * c + jnp.where(lo, up, dn) * s


def _even_in_kernel(z_ref, g_ref, mod_ref, w_ref, b_ref, gq_ref, gk_ref, grp_ref, cos_ref, sin_ref,
                    fb_ref, isf_ref, mqkv_ref, mo_ref, aq_ref, akv_ref, gate_ref):
    h = _norm_mod(z_ref[...], g_ref[...], mod_ref[0:1, :], mod_ref[1:2, :])
    p = _dot(h.astype(BF16), w_ref[...]) + b_ref[...]
    mqkv_ref[:, 0:ML_W] = p[:, 0:ML_W].astype(BF16)
    mqkv_ref[:, ML_W:2 * ML_W] = (p[:, ML_W:2 * ML_W] * (ML_D ** -0.5)).astype(BF16)
    mqkv_ref[:, 2 * ML_W:3 * ML_W] = p[:, 2 * ML_W:3 * ML_W].astype(BF16)
    mo_ref[...] = p[:, EV_MO:EV_MO + ML_W]
    cos = cos_ref[...]
    sin = sin_ref[...]
    lane_lo = (lax.broadcasted_iota(jnp.int32, (1, LANES), 1) % 32) < 16
    q = p[:, EV_AQ:EV_AQ + AT_W]
    qms = _dot((q * q).astype(BF16), grp_ref[...])
    q = q * lax.rsqrt(qms + EPS) * gq_ref[...]
    aq_ref[...] = (_rope(q, cos, sin, lane_lo) * (AT_DH ** -0.5)).astype(BF16)
    k = p[:, EV_AK:EV_AK + 256]
    kms = _dot((k * k).astype(BF16), grp_ref[0:256, 0:256])
    k = k * lax.rsqrt(kms + EPS) * gk_ref[...]
    akv_ref[:, 0:256] = _rope(k, cos, sin, lane_lo).astype(BF16)
    akv_ref[:, 256:512] = p[:, EV_AV:EV_AV + 256].astype(BF16)
    gt = p[:, EV_GATE:EV_GATE + LANES] + fb_ref[...]
    sc = GATE_CAP * jnp.tanh(gt * (1.0 / GATE_CAP))
    logsig = jnp.minimum(sc, 0.0) - jnp.log1p(jnp.exp(-jnp.abs(sc)))
    gate_ref[...] = jnp.where(isf_ref[...] > 0.5, logsig, sc)


def _even_in_call(z, g, mod, w, b, gq, gk, grp, cos, sin, fb, isf, nb):
    bsz, t, d = z.shape
    tm = TOKEN_TILE
    tok = lambda width: pl.BlockSpec((None, tm, width), lambda bi, i: (bi, i, 0))
    return pl.pallas_call(
        _even_in_kernel,
        grid=(bsz, t // tm),
        in_specs=[tok(d),
                  _const_spec((1, d)),
                  pl.BlockSpec((None, N_MOD, d), lambda bi, i: (jnp.where(i == 0, nb, bi), 0, 0)),
                  _const_spec((d, EV_COLS)),
                  _const_spec((1, EV_COLS)),
                  _const_spec((1, AT_W)),
                  _const_spec((1, 256)),
                  _const_spec((AT_W, AT_W)),
                  pl.BlockSpec((tm, LANES), lambda bi, i: (i, 0)),
                  pl.BlockSpec((tm, LANES), lambda bi, i: (i, 0)),
                  _const_spec((1, LANES)),
                  _const_spec((1, LANES))],
        out_specs=[tok(3 * ML_W), tok(ML_W), tok(AT_W), tok(512), tok(LANES)],
        out_shape=[jax.ShapeDtypeStruct((bsz, t, 3 * ML_W), BF16),
                   jax.ShapeDtypeStruct((bsz, t, ML_W), F32),
                   jax.ShapeDtypeStruct((bsz, t, AT_W), BF16),
                   jax.ShapeDtypeStruct((bsz, t, 512), BF16),
                   jax.ShapeDtypeStruct((bsz, t, LANES), F32)],
        compiler_params=_params(("parallel", "parallel")),
        name="even_in",
    )(z, g, mod, w, b, gq, gk, grp, cos, sin, fb, isf)


def _split3(x):
    hi = x.astype(BF16)
    r1 = x - hi.astype(F32)
    mid = r1.astype(BF16)
    return hi, mid, (r1 - mid.astype(F32)).astype(BF16)


def _mlstm_kernel(qf_ref, ktf_ref, vf_ref, qb_ref, ktb_ref, vb_ref, gcf_ref, gcb_ref, grf_ref, grb_ref, sel_ref,
                  hf_ref, hb_ref, c_s, m_s):
    ln = ML_CHUNK

    @pl.when(pl.program_id(1) == 0)
    def _():
        c_s[...] = jnp.zeros_like(c_s)
        m_s[...] = jnp.zeros_like(m_s)

    ri = lax.broadcasted_iota(jnp.int32, (ln, ln), 0)
    ci = lax.broadcasted_iota(jnp.int32, (ln, ln), 1)
    hi = lax.Precision.HIGHEST
    before = (ci <= ri, ci >= ri)
    refs = ((qf_ref, ktf_ref, vf_ref, gcf_ref, grf_ref, hf_ref), (qb_ref, ktb_ref, vb_ref, gcb_ref, grb_ref, hb_ref))
    cum_r = [jnp.dot(refs[d][4][...], before[1 - d].astype(F32), preferred_element_type=F32, precision=hi)
             for d in range(2)]
    bc_all = []
    for d in range(2):
        cum_c = jnp.dot(before[d].astype(F32), refs[d][3][...], preferred_element_type=F32, precision=hi)
        sel = sel_ref[d]
        bc_all.append(sum(_dot(part, sel) for part in _split3(cum_c)))
    chains = [(d, hd) for d in range(2) for hd in range(ML_HEADS)]
    lanes = [slice(hd * ML_D, (hd + 1) * ML_D) for _, hd in chains]
    st = [d * ML_HEADS + hd for d, hd in chains]
    ones = jnp.ones((ln, ML_D), BF16)
    q = [refs[d][0][:, ls] for (d, _), ls in zip(chains, lanes)]
    kt = [refs[d][1][ls, :] for (d, _), ls in zip(chains, lanes)]
    v1 = [jnp.concatenate([refs[d][2][:, ls], ones], axis=1) for (d, _), ls in zip(chains, lanes)]
    i_row = [refs[d][4][8 * d + hd:8 * d + hd + 1, :] for d, hd in chains]
    f_row = [refs[d][4][8 * d + 4 + hd:8 * d + 5 + hd, :] for d, hd in chains]
    bcum_row = [cum_r[d][8 * d + 4 + hd:8 * d + 5 + hd, :] for d, hd in chains]
    bcum = [bc_all[d][:, ls] for (d, _), ls in zip(chains, lanes)]
    m_old = [m_s[s:s + 1, 0:1] for s in st]
    c_old = [c_s[s] for s in st]
    qk = [_dot(a, b) for a, b in zip(q, kt)]
    r2 = [_dot(a, c.astype(BF16)) for a, c in zip(q, c_old)]
    dlog = [jnp.where(before[d], bc - br + ir, NEG) for (d, _), bc, br, ir in zip(chains, bcum, bcum_row, i_row)]
    inter = [bc + m for bc, m in zip(bcum, m_old)]
    mt = [jnp.maximum(jnp.max(dl, axis=1, keepdims=True), it) for dl, it in zip(dlog, inter)]
    sc = [x * jnp.exp(dl - m) for x, dl, m in zip(qk, dlog, mt)]
    iw = [jnp.exp(it - m) for it, m in zip(inter, mt)]
    r1 = [_dot(s.astype(BF16), b) for s, b in zip(sc, v1)]
    for j, (d, _) in enumerate(chains):
        tot = r1[j] + jnp.concatenate([iw[j], iw[j]], axis=1) * r2[j]
        den = jnp.maximum(jnp.abs(tot[:, ML_D:]), jnp.exp(-mt[j]))
        refs[d][5][:, lanes[j]] = tot[:, :ML_D] / den
    bl = [jnp.sum(fr, axis=1, keepdims=True) for fr in f_row]
    ws_log = [b - br + ir for b, br, ir in zip(bl, bcum_row, i_row)]
    m_new = [jnp.maximum(b + m, jnp.max(w, axis=1, keepdims=True)) for b, m, w in zip(bl, m_old, ws_log)]
    ws = [jnp.exp(w - m) for w, m in zip(ws_log, m_new)]
    upd = [_dot((a.astype(F32) * w).astype(BF16), b) for a, w, b in zip(kt, ws, v1)]
    for j, s in enumerate(st):
        c_s[s] = jnp.exp(bl[j] + m_old[j] - m_new[j]) * c_old[j] + upd[j]
        m_s[s:s + 1, :] = jnp.broadcast_to(m_new[j], (1, LANES))


def _rev_chunk(i, nctx, n):
    return jnp.where(i < nctx, nctx - 1 - i, n + nctx - 1 - i)


def _mlstm_call(mqkv, kt, gates, gates_t, lc):
    bsz, t, _ = mqkv.shape
    ln = ML_CHUNK
    assert ln == ML_D == LANES
    n = t // ln
    nctx = lc // ln
    rv = lambda i: _rev_chunk(i, nctx, n)
    col = jnp.arange(LANES)[None, :, None]
    want = (8 * jnp.arange(2)[:, None, None] + 4 + jnp.arange(ML_HEADS * ln)[None, None, :] // ln)
    sel = (col == want).astype(BF16)
    return pl.pallas_call(
        _mlstm_kernel,
        grid=(bsz, n),
        in_specs=[pl.BlockSpec((None, ln, ML_W), lambda bi, i: (bi, i, 0)),
                  pl.BlockSpec((None, ML_W, ln), lambda bi, i: (bi, 0, i)),
                  pl.BlockSpec((None, ln, ML_W), lambda bi, i: (bi, i, 2)),
                  pl.BlockSpec((None, ln, ML_W), lambda bi, i: (bi, rv(i), 0)),
                  pl.BlockSpec((None, ML_W, ln), lambda bi, i: (bi, 0, rv(i))),
                  pl.BlockSpec((None, ln, ML_W), lambda bi, i: (bi, rv(i), 2)),
                  pl.BlockSpec((None, ln, LANES), lambda bi, i: (bi, i, 0)),
                  pl.BlockSpec((None, ln, LANES), lambda bi, i: (bi, rv(i), 0)),
                  pl.BlockSpec((None, 16, ln), lambda bi, i: (bi, 0, i)),
                  pl.BlockSpec((None, 16, ln), lambda bi, i: (bi, 0, rv(i))),
                  _const_spec((2, LANES, ML_HEADS * ln))],
        out_specs=[pl.BlockSpec((None, ln, ML_W), lambda bi, i: (bi, i, 0)),
                   pl.BlockSpec((None, ln, ML_W), lambda bi, i: (bi, rv(i), 0))],
        out_shape=[jax.ShapeDtypeStruct((bsz, t, ML_W), F32)] * 2,
        scratch_shapes=[pltpu.VMEM((2 * ML_HEADS, ML_D, 2 * ML_D), F32),
                        pltpu.VMEM((2 * ML_HEADS, LANES), F32)],
        compiler_params=_params(("parallel", "arbitrary")),
        name="mlstm_scan",
    )(mqkv, kt, mqkv, mqkv, kt, mqkv, gates, gates, gates_t, gates_t, sel)


def _attn_kernel(q_ref, ktp_ref, kto_ref, ktn_ref, ktc_ref, vp_ref, vo_ref, vn_ref, vc_ref, sink_ref, o_ref,
                 *, nctx, n):
    j = pl.program_id(1)
    blk = AT_BLOCK
    latent = j >= nctx
    qi = lax.broadcasted_iota(jnp.int32, (blk, blk), 0)
    ki = lax.broadcasted_iota(jnp.int32, (blk, blk), 1)
    ok_prev = jnp.logical_and(jnp.logical_and(latent, j > nctx), ki >= qi)
    ok_own = jnp.logical_and(latent, ki >= 0)
    ok_next = jnp.logical_and(jnp.logical_and(latent, j < n - 1), ki <= qi)
    lane = lax.broadcasted_iota(jnp.int32, (1, LANES), 1)
    sub = lax.broadcasted_iota(jnp.int32, (LANES, 1), 0)
    lane_half = (lane < AT_DH, lane >= AT_DH)
    sub_half = (sub < AT_DH, sub >= AT_DH)
    kts = (ktp_ref, kto_ref, ktn_ref, ktc_ref)
    vs = (vp_ref, vo_ref, vn_ref, vc_ref)
    oks = (ok_prev, ok_own, ok_next, None)
    zero = jnp.zeros((), BF16)
    kx, vx = {}, {}
    for g in range(AT_KV_HEADS):
        rows = slice(g * LANES, (g + 1) * LANES)
        for e in range(2):
            kx[g, e] = [jnp.where(sub_half[e], r[rows, :], zero) for r in kts]
            vx[g, e] = [jnp.concatenate([jnp.where(lane_half[e], r[:, rows], zero),
                                         jnp.ones((r.shape[0], LANES), BF16)], axis=1) for r in vs]
    heads = [(hd // 2, hd % 2, hd // 4) for hd in range(AT_HEADS)]
    qc = [q_ref[:, c * LANES:(c + 1) * LANES] for c in range(AT_HEADS // 2)]
    sink = [sink_ref[hd:hd + 1, 0:1] for hd in range(AT_HEADS)]
    ss = [[_dot(qc[c], kx[g, e][p]) if oks[p] is None else jnp.where(oks[p], _dot(qc[c], kx[g, e][p]), NEG)
           for p in range(4)] for c, e, g in heads]
    m = []
    for hd in range(AT_HEADS):
        band = jnp.maximum(jnp.maximum(ss[hd][0], ss[hd][1]), ss[hd][2])
        m.append(jnp.maximum(jnp.maximum(jnp.max(band, axis=1, keepdims=True),
                                         jnp.max(ss[hd][3], axis=1, keepdims=True)), sink[hd]))
    res = []
    for hd, (c, e, g) in enumerate(heads):
        acc = None
        for p in range(4):
            pv = _dot(jnp.exp(ss[hd][p] - m[hd]).astype(BF16), vx[g, e][p])
            acc = pv if acc is None else acc + pv
        res.append(acc)
    outs = [r[:, :LANES] / (r[:, LANES:] + jnp.exp(s - mm)) for r, s, mm in zip(res, sink, m)]
    for c in range(AT_HEADS // 2):
        o_ref[:, c * LANES:(c + 1) * LANES] = (outs[2 * c] + outs[2 * c + 1]).astype(o_ref.dtype)


def _attn_call(aq, akt, akv, sink, lc):
    bsz, t, _ = aq.shape
    blk = AT_BLOCK
    n = t // blk
    nctx = lc // blk
    prev = lambda j: jnp.clip(j - 1, nctx, n - 1)
    own = lambda j: jnp.clip(j, nctx, n - 1)
    nxt = lambda j: jnp.clip(j + 1, nctx, n - 1)
    kt = lambda f: pl.BlockSpec((None, 256, blk), lambda bi, j: (bi, 0, f(j)))
    vv = lambda f: pl.BlockSpec((None, blk, 256), lambda bi, j: (bi, f(j), 1))
    return pl.pallas_call(
        functools.partial(_attn_kernel, nctx=nctx, n=n),
        grid=(bsz, n),
        in_specs=[pl.BlockSpec((None, blk, AT_W), lambda bi, j: (bi, j, 0)),
                  kt(prev), kt(own), kt(nxt),
                  pl.BlockSpec((None, 256, lc), lambda bi, j: (bi, 0, 0)),
                  vv(prev), vv(own), vv(nxt),
                  pl.BlockSpec((None, lc, 256), lambda bi, j: (bi, 0, 1)),
                  _const_spec((AT_HEADS, LANES))],
        out_specs=pl.BlockSpec((None, blk, AT_W), lambda bi, j: (bi, j, 0)),
        out_shape=jax.ShapeDtypeStruct((bsz, t, AT_W), BF16),
        compiler_params=_params(("parallel", "parallel")),
        name="window_attn",
    )(aq, akt, akt, akt, akt, akv, akv, akv, akv, sink)


def _even_out_kernel(z_ref, hf_ref, hb_ref, mo_ref, ya_ref, og_ref, w_ref, mod_ref, o_ref):
    hs = hf_ref[...] + hb_ref[...]
    parts = []
    for hd in range(ML_HEADS):
        x = hs[:, hd * ML_D:(hd + 1) * ML_D]
        parts.append(x * lax.rsqrt(jnp.mean(x * x, axis=1, keepdims=True) + EPS))
    ym = jnp.concatenate(parts, axis=1) * og_ref[...] * _sigmoid(mo_ref[...])
    y = _dot(ym.astype(BF16), w_ref[0:ML_W, :]) + _dot(ya_ref[...], w_ref[ML_W:ML_W + AT_W, :])
    o_ref[...] = z_ref[...] + mod_ref[2:3, :] * y


def _even_out_call(z, hf, hb, mo, ya, og, w, mod, nb):
    bsz, t, d = z.shape
    tm = TOKEN_TILE
    tok = lambda width: pl.BlockSpec((None, tm, width), lambda bi, i: (bi, i, 0))
    return pl.pallas_call(
        _even_out_kernel,
        grid=(bsz, t // tm),
        in_specs=[tok(d), tok(ML_W), tok(ML_W), tok(ML_W), tok(AT_W),
                  _const_spec((1, ML_W)),
                  _const_spec((ML_W + AT_W, d)),
                  pl.BlockSpec((None, N_MOD, d), lambda bi, i: (jnp.where(i == 0, nb, bi), 0, 0))],
        out_specs=tok(d),
        out_shape=jax.ShapeDtypeStruct((bsz, t, d), F32),
        compiler_params=_params(("parallel", "parallel")),
        name="even_out",
    )(z, hf, hb, mo, ya, og, w, mod)


def _mlp_kernel(z_ref, g_ref, mod_ref, w1_ref, w2_ref, o_ref):
    z = z_ref[...]
    h = _norm_mod(z, g_ref[...], mod_ref[3:4, :], mod_ref[4:5, :]).astype(BF16)
    acc = jnp.zeros(z.shape, F32)
    for f in range(D_FF // FF_TILE):
        a = jnp.maximum(_dot(h, w1_ref[:, f * FF_TILE:(f + 1) * FF_TILE]), 0.0)
        acc = acc + _dot((a * a).astype(BF16), w2_ref[f * FF_TILE:(f + 1) * FF_TILE, :])
    o_ref[...] = z + mod_ref[5:6, :] * acc


def _mlp_call(z, g, mod, w1, w2, nb, skip):
    bsz, t, d = z.shape
    tm = TOKEN_TILE
    nt = t // tm - skip
    mod_idx = (lambda bi, i: (bi, 0, 0)) if skip else (lambda bi, i: (jnp.where(i == 0, nb, bi), 0, 0))
    return pl.pallas_call(
        _mlp_kernel,
        grid=(bsz, nt),
        in_specs=[pl.BlockSpec((None, tm, d), lambda bi, i: (bi, i + skip, 0)),
                  _const_spec((1, d)),
                  pl.BlockSpec((None, N_MOD, d), mod_idx),
                  _const_spec((d, D_FF)),
                  _const_spec((D_FF, d))],
        out_specs=pl.BlockSpec((None, tm, d), lambda bi, i: (bi, i, 0)),
        out_shape=jax.ShapeDtypeStruct((bsz, nt * tm, d), F32),
        compiler_params=_params(("parallel", "parallel")),
        name="mlp",
    )(z, g, mod, w1, w2)


def _rwkv_in_kernel(z_ref, zp_ref, zn_ref, g_ref, mod_ref, mu_ref, wrkv_ref, w1_ref, a1_ref, g1_ref,
                    w2_ref, a2_ref, g2_ref, w0_ref, a0_ref, kk_ref, ka_ref, rk_ref, grp_ref,
                    sh_ref, dr_ref, lw_ref, gate_ref, bonus_ref, *, nctx, ntile):
    i = pl.program_id(1)
    tm, d = z_ref.shape
    g = g_ref[...]
    shift = mod_ref[0:1, :]
    scale = mod_ref[1:2, :]
    h = _norm_mod(z_ref[...], g, shift, scale)
    no_prev = jnp.logical_or(i == 0, i == nctx)
    no_next = jnp.logical_or(i == nctx - 1, i == ntile - 1)
    hp = jnp.where(no_prev, 0.0, _norm_mod(zp_ref[7:8, :], g, shift, scale))
    hn = jnp.where(no_next, 0.0, _norm_mod(zn_ref[0:1, :], g, shift, scale))
    row = lax.broadcasted_iota(jnp.int32, (tm, 1), 0)
    dp = jnp.where(row == 0, hp, pltpu.roll(h, 1, 0)) - h
    dn = jnp.where(row == tm - 1, hn, pltpu.roll(h, tm - 1, 0)) - h

    def mix(n):
        return (h + mu_ref[2 * n:2 * n + 1, :] * dp + mu_ref[2 * n + 1:2 * n + 2, :] * dn).astype(BF16)

    r = _dot(mix(0), wrkv_ref[0])
    k = _dot(mix(2), wrkv_ref[1])
    v = _dot(mix(3), wrkv_ref[2])
    gate_ref[...] = _dot(_sigmoid(_dot(mix(5), g1_ref[...])).astype(BF16), g2_ref[...])
    lora_w = _dot(jnp.tanh(_dot(mix(1), w1_ref[...])).astype(BF16), w2_ref[...])
    lora_a = _dot(_dot(mix(4), a1_ref[...]).astype(BF16), a2_ref[...])
    kkr = k * kk_ref[...]
    ssq = _dot((kkr * kkr).astype(BF16), grp_ref[...]) * float(RW_HEAD)
    kk = kkr * lax.rsqrt(jnp.maximum(ssq, 1e-24))
    kd_sum = None
    for dr in range(2):
        cols = slice(dr * d, (dr + 1) * d)
        w_log = -_softplus(-(w0_ref[dr:dr + 1, :] + lora_w[:, cols])) - 0.5
        lw = -jnp.exp(w_log)
        a = _sigmoid(a0_ref[dr:dr + 1, :] + lora_a[:, cols])
        kd = k * (1.0 + (a - 1.0) * ka_ref[...])
        bvec = kk * a
        kd_sum = kd if kd_sum is None else kd_sum + kd
        for p in range(RW_PAIRS):
            ls = slice(p * LANES, (p + 1) * LANES)
            lw_ref[dr, p] = lw[:, ls]
            dr_ref[dr, 0, p] = kd[:, ls].astype(BF16)
            dr_ref[dr, 1, p] = bvec[:, ls].astype(BF16)
    for p in range(RW_PAIRS):
        ls = slice(p * LANES, (p + 1) * LANES)
        sh_ref[0, p] = r[:, ls].astype(BF16)
        sh_ref[1, p] = v[:, ls].astype(BF16)
        sh_ref[2, p] = kk[:, ls].astype(BF16)
    bsum = _dot((r * kd_sum * rk_ref[...]).astype(BF16), grp_ref[...]) * float(RW_HEAD)
    bonus_ref[...] = bsum * v


def _rwkv_in_call(z, g, mod, mu, wrkv, w1, a1, g1, w2, a2, g2, w0, a0, k_k, k_a, r_k, grp, nb, lc):
    bsz, t, d = z.shape
    tm = TOKEN_TILE
    ntile = t // tm
    nctx = lc // tm
    r8 = tm // 8
    tok = pl.BlockSpec((None, tm, d), lambda bi, i: (bi, i, 0))
    return pl.pallas_call(
        functools.partial(_rwkv_in_kernel, nctx=nctx, ntile=ntile),
        grid=(bsz, ntile),
        in_specs=[tok,
                  pl.BlockSpec((None, 8, d), lambda bi, i: (bi, jnp.maximum(i * r8 - 1, 0), 0)),
                  pl.BlockSpec((None, 8, d), lambda bi, i: (bi, jnp.minimum((i + 1) * r8, t // 8 - 1), 0)),
                  _const_spec((1, d)),
                  pl.BlockSpec((None, N_MOD, d), lambda bi, i: (jnp.where(i < nctx, nb, bi), 0, 0)),
                  _const_spec((12, d)),
                  _const_spec((3, d, d)),
                  _const_spec((d, LANES)),
                  _const_spec((d, LANES)),
                  _const_spec((d, RW_GATE_PAD)),
                  _const_spec((LANES, 2 * d)),
                  _const_spec((LANES, 2 * d)),
                  _const_spec((RW_GATE_PAD, d)),
                  _const_spec((2, d)),
                  _const_spec((2, d)),
                  _const_spec((1, d)),
                  _const_spec((1, d)),
                  _const_spec((1, d)),
                  _const_spec((d, d))],
        out_specs=[pl.BlockSpec((None, 3, RW_PAIRS, tm, LANES), lambda bi, i: (bi, 0, 0, i, 0)),
                   pl.BlockSpec((None, 2, 2, RW_PAIRS, tm, LANES), lambda bi, i: (bi, 0, 0, 0, i, 0)),
                   pl.BlockSpec((None, 2, RW_PAIRS, tm, LANES), lambda bi, i: (bi, 0, 0, i, 0)),
                   tok, tok],
        out_shape=[jax.ShapeDtypeStruct((bsz, 3, RW_PAIRS, t, LANES), BF16),
                   jax.ShapeDtypeStruct((bsz, 2, 2, RW_PAIRS, t, LANES), BF16),
                   jax.ShapeDtypeStruct((bsz, 2, RW_PAIRS, t, LANES), F32),
                   jax.ShapeDtypeStruct((bsz, t, d), F32),
                   jax.ShapeDtypeStruct((bsz, t, d), F32)],
        compiler_params=_params(("parallel", "parallel")),
        name="rwkv_in",
    )(z, z, z, g, mod, mu, wrkv, w1, a1, g1, w2, a2, g2, w0, a0, k_k, k_a, r_k, grp)


def _rwkv_chunk_group(chains):
    ln = RW_CHUNK
    row = lax.broadcasted_iota(jnp.int32, (ln, 2 * ln), 0)
    col = lax.broadcasted_iota(jnp.int32, (ln, 2 * ln), 1)
    sidx = jnp.where(col >= ln, col - ln, col)
    lane_a = lax.broadcasted_iota(jnp.int32, (1, LANES), 1) < RW_HEAD
    tr = lax.broadcasted_iota(jnp.int32, (ln, ln), 0)
    tc = lax.broadcasted_iota(jnp.int32, (ln, ln), 1)
    tri = {False: (tc <= tr).astype(F32), True: (tc >= tr).astype(F32)}
    strict = {False: sidx < row, True: sidx > row}
    incl = {False: sidx <= row, True: sidx >= row}
    eye_w = jnp.where(sidx == row, 1.0, 0.0)
    diff = row ^ sidx
    vr = lax.broadcasted_iota(jnp.int32, (LANES, LANES), 0) < RW_HEAD
    kc = lax.broadcasted_iota(jnp.int32, (LANES, LANES), 1) < RW_HEAD
    same_head = vr == kc

    def bd(x):
        zero = jnp.zeros((), x.dtype)
        return jnp.concatenate([jnp.where(lane_a, x, zero), jnp.where(lane_a, zero, x)], axis=0)

    revs = [c[7] for c in chains]
    vs = [c[1] for c in chains]
    s_olds = [c[6] for c in chains]
    gcum = [jnp.dot(tri[c[7]], c[5], preferred_element_type=F32, precision=lax.Precision.HIGHEST) for c in chains]
    gtot = [jnp.sum(c[5], axis=0, keepdims=True) for c in chains]

    def scaled(c, g, gt):
        r, v, kk, kd, bv, lw = (x.astype(F32) for x in c[:6])
        e_pos = jnp.exp(g)
        e_neg = jnp.exp(-g)
        e_end = jnp.exp(gt - g)
        ar = jnp.concatenate([(-kk * jnp.exp(g - lw)).astype(RW_MM), (r * e_pos).astype(RW_MM)], axis=0)
        bk_end = jnp.concatenate([(bv * e_end).astype(RW_MM), (kd * e_end).astype(RW_MM)], axis=0)
        return ar, (bv * e_neg).astype(RW_MM), (kd * e_neg).astype(RW_MM), bk_end

    sc = [scaled(c, g, gt) for c, g, gt in zip(chains, gcum, gtot)]
    ars_ = [x[0] for x in sc]
    x_b = [_dot_tb(x[0], bd(x[1])) for x in sc]
    x_k = [_dot_tb(x[0], bd(x[2])) for x in sc]
    n_w = [jnp.where(strict[rv], x[:ln], 0.0) for x, rv in zip(x_b, revs)]
    m_rb = [jnp.where(incl[rv], x[ln:], 0.0).astype(RW_MM) for x, rv in zip(x_b, revs)]
    m_k = [jnp.concatenate([jnp.where(strict[rv], x[:ln], 0.0), jnp.where(incl[rv], x[ln:], 0.0)],
                           axis=0).astype(RW_MM) for x, rv in zip(x_k, revs)]
    x_w = [eye_w + jnp.where(diff == 1, n, 0.0) for n in n_w]
    h = 2
    while h < ln:
        lvl = jnp.logical_and(diff >= h, diff < 2 * h)
        tmp = [_dot(jnp.where(lvl, n, 0.0).astype(RW_MM), bd(x.astype(RW_MM))) for n, x in zip(n_w, x_w)]
        x_w = [x + _dot(x.astype(RW_MM), bd(t.astype(RW_MM))) for x, t in zip(x_w, tmp)]
        h *= 2
    ars = [_dot_tb(a, s.astype(RW_MM)) for a, s in zip(ars_, s_olds)]
    mv = [_dot(m, bd(v)) for m, v in zip(m_k, vs)]
    u = [_dot(x.astype(RW_MM), bd((a[:ln] + m[:ln]).astype(RW_MM))) for x, a, m in zip(x_w, ars, mv)]
    y = [a[ln:] + m[ln:] + _dot(rb, bd(uu.astype(RW_MM))) for a, m, rb, uu in zip(ars, mv, m_rb, u)]
    upd = [_dot_ta(jnp.concatenate([uu.astype(RW_MM), v], axis=0), x[3]) for uu, v, x in zip(u, vs, sc)]
    s_new = [s * jnp.exp(gt) + jnp.where(same_head, up, 0.0) for s, gt, up in zip(s_olds, gtot, upd)]
    return list(zip(y, s_new))


def _rwkv_scan_kernel(shf_ref, shb_ref, drf_ref, drb_ref, lwf_ref, lwb_ref, yf_ref, yb_ref, s_s):
    @pl.when(pl.program_id(1) == 0)
    def _():
        s_s[...] = jnp.zeros_like(s_s)

    for p0 in range(0, RW_PAIRS, RW_LOCKSTEP):
        chains = []
        for p in range(p0, p0 + RW_LOCKSTEP):
            chains.append((shf_ref[0, p], shf_ref[1, p], shf_ref[2, p], drf_ref[0, p], drf_ref[1, p],
                           lwf_ref[p], s_s[0, p], False))
            chains.append((shb_ref[0, p], shb_ref[1, p], shb_ref[2, p], drb_ref[0, p], drb_ref[1, p],
                           lwb_ref[p], s_s[1, p], True))
        outs = _rwkv_chunk_group(chains)
        for i, (y, s_new) in enumerate(outs):
            p, d = p0 + i // 2, i % 2
            (yf_ref if d == 0 else yb_ref)[p] = y
            s_s[d, p] = s_new


def _rwkv_scan_call(sh, dr, lw, lc):
    bsz, _, _, t, _ = sh.shape
    ln = RW_CHUNK
    n = t // ln
    nctx = lc // ln
    rv = lambda i: _rev_chunk(i, nctx, n)
    return pl.pallas_call(
        _rwkv_scan_kernel,
        grid=(bsz, n),
        in_specs=[pl.BlockSpec((None, 3, RW_PAIRS, ln, LANES), lambda bi, i: (bi, 0, 0, i, 0)),
                  pl.BlockSpec((None, 3, RW_PAIRS, ln, LANES), lambda bi, i: (bi, 0, 0, rv(i), 0)),
                  pl.BlockSpec((None, None, 2, RW_PAIRS, ln, LANES), lambda bi, i: (bi, 0, 0, 0, i, 0)),
                  pl.BlockSpec((None, None, 2, RW_PAIRS, ln, LANES), lambda bi, i: (bi, 1, 0, 0, rv(i), 0)),
                  pl.BlockSpec((None, None, RW_PAIRS, ln, LANES), lambda bi, i: (bi, 0, 0, i, 0)),
                  pl.BlockSpec((None, None, RW_PAIRS, ln, LANES), lambda bi, i: (bi, 1, 0, rv(i), 0))],
        out_specs=[pl.BlockSpec((None, RW_PAIRS, ln, LANES), lambda bi, i: (bi, 0, i, 0)),
                   pl.BlockSpec((None, RW_PAIRS, ln, LANES), lambda bi, i: (bi, 0, rv(i), 0))],
        out_shape=[jax.ShapeDtypeStruct((bsz, RW_PAIRS, t, LANES), F32)] * 2,
        scratch_shapes=[pltpu.VMEM((2, RW_PAIRS, LANES, LANES), F32)],
        compiler_params=_params(("parallel", "arbitrary")),
        name="rwkv_scan",
    )(sh, sh, dr, dr, lw, lw)


def _rwkv_out_kernel(z_ref, yf_ref, yb_ref, gate_ref, bonus_ref, lng_ref, lnb_ref, grp_ref, w_ref, mod_ref, o_ref):
    y = jnp.concatenate([yf_ref[p] + yb_ref[p] for p in range(RW_PAIRS)], axis=1)
    mean = _dot(y.astype(BF16), grp_ref[...])
    yc = y - mean
    var = _dot((yc * yc).astype(BF16), grp_ref[...])
    yn = yc * lax.rsqrt(var + RW_LN_EPS) * lng_ref[...] + lnb_ref[...] + bonus_ref[...]
    out = _dot((yn * gate_ref[...]).astype(BF16), w_ref[...])
    o_ref[...] = z_ref[...] + mod_ref[2:3, :] * out


def _rwkv_out_call(z, yf, yb, gate, bonus, ln_g, ln_b, grp, w, mod, nb, lc):
    bsz, t, d = z.shape
    tm = TOKEN_TILE
    nctx = lc // tm
    tok = pl.BlockSpec((None, tm, d), lambda bi, i: (bi, i, 0))
    pair = pl.BlockSpec((None, RW_PAIRS, tm, LANES), lambda bi, i: (bi, 0, i, 0))
    return pl.pallas_call(
        _rwkv_out_kernel,
        grid=(bsz, t // tm),
        in_specs=[tok, pair, pair, tok, tok,
                  _const_spec((1, d)), _const_spec((1, d)), _const_spec((d, d)), _const_spec((d, d)),
                  pl.BlockSpec((None, N_MOD, d), lambda bi, i: (jnp.where(i < nctx, nb, bi), 0, 0))],
        out_specs=tok,
        out_shape=jax.ShapeDtypeStruct((bsz, t, d), F32),
        compiler_params=_params(("parallel", "parallel")),
        name="rwkv_out",
    )(z, yf, yb, gate, bonus, ln_g, ln_b, grp, w, mod)


def _group_mean_matrix(n, width):
    idx = jnp.arange(n) // width
    return ((idx[:, None] == idx[None, :]).astype(F32) / width).astype(BF16)


def _rope_tables(lc, s):
    quarter = AT_DH // 4
    inv = ROPE_BASE ** (-jnp.arange(quarter, dtype=F32) / quarter)
    pos = jnp.arange(s)
    rpos = (pos // GRID_W).astype(F32)
    cpos = (pos % GRID_W).astype(F32)
    ang_r = rpos[:, None] * inv[None, :]
    ang_c = cpos[:, None] * inv[None, :]
    cos64 = jnp.concatenate([jnp.cos(ang_r), jnp.cos(ang_r), jnp.cos(ang_c), jnp.cos(ang_c)], axis=1)
    sin64 = jnp.concatenate([-jnp.sin(ang_r), jnp.sin(ang_r), -jnp.sin(ang_c), jnp.sin(ang_c)], axis=1)
    cos = jnp.concatenate([jnp.ones((lc, AT_DH), F32), cos64], axis=0)
    sin = jnp.concatenate([jnp.zeros((lc, AT_DH), F32), sin64], axis=0)
    return jnp.tile(cos, (1, 2)), jnp.tile(sin, (1, 2))


def _even_weights(w_in, b_in):
    def cols(m):
        mq, mk, mv, mo, mg, aq, ak, av = jnp.split(m, [512, 1024, 1536, 2048, 2064, 2576, 2704], axis=-1)
        dup = lambda u: jnp.concatenate([u[..., :64], u[..., :64], u[..., 64:], u[..., 64:]], axis=-1)
        mgp = jnp.pad(mg, [(0, 0)] * (m.ndim - 1) + [(0, LANES - 16)])
        return jnp.concatenate([mq, mk, mv, mo, aq, dup(ak), dup(av), mgp], axis=-1)
    return cols(w_in).astype(BF16), cols(b_in[None, :])


def kernel(x, c, ctx, c_ctx, ada_w, ada_b, norm1_g, norm2_g, mlp_w1, mlp_w2, ev_w_in, ev_b_in, ev_w_out, ml_f_bias, ml_out_g, at_q_g, at_k_g, at_sink, rw_mu, rw_w_rkv, rw_w0, rw_w1, rw_w2, rw_a0, rw_a1, rw_a2, rw_g1, rw_g2, rw_k_k, rw_k_a, rw_r_k, rw_ln_g, rw_ln_b, rw_w_out):
    bsz, s, d = x.shape
    lc = ctx.shape[1]
    depth = ada_w.shape[0]
    assert d == D_MODEL and lc == TOKEN_TILE and s % TOKEN_TILE == 0 and bsz < MOD_ROWS
    z = jnp.concatenate([ctx, x], axis=1)

    cstack = jnp.zeros((MOD_ROWS, d), F32).at[:bsz].set(c).at[bsz].set(c_ctx)
    mod_all = _ada_call(cstack, ada_w, ada_b).reshape(depth, MOD_ROWS, N_MOD, d)

    grp512 = _group_mean_matrix(AT_W, AT_DH)
    grp_d = _group_mean_matrix(d, RW_HEAD)
    cos_t, sin_t = _rope_tables(lc, s)
    gate_is_f = ((jnp.arange(LANES) % 8 >= 4) & (jnp.arange(LANES) < 16)).astype(F32)[None, :]

    for layer in range(depth):
        mod = mod_all[layer]
        j = layer // 2
        n1 = norm1_g[layer][None, :]
        if layer % 2 == 0:
            w_in, b_in = _even_weights(ev_w_in[j], ev_b_in[j])
            fb = jnp.zeros((LANES,), F32)
            for dr in range(2):
                fb = fb.at[8 * dr + 4:8 * dr + 8].set(ml_f_bias[j, dr])
            mqkv, mo, aq, akv, gates = _even_in_call(
                z, n1, mod, w_in, b_in, jnp.tile(at_q_g[j], AT_HEADS)[None, :], jnp.tile(at_k_g[j], 4)[None, :],
                grp512, cos_t, sin_t, fb[None, :], gate_is_f, bsz)
            gates_t = jnp.transpose(gates[:, :, :16], (0, 2, 1))
            kt = jnp.transpose(mqkv[:, :, ML_W:2 * ML_W], (0, 2, 1))
            hf, hb = _mlstm_call(mqkv, kt, gates, gates_t, lc)
            sink = jnp.broadcast_to(at_sink[j][:, None], (AT_HEADS, LANES))
            akt = jnp.transpose(akv[:, :, :256], (0, 2, 1))
            ya = _attn_call(aq, akt, akv, sink, lc)
            z = _even_out_call(z, hf, hb, mo, ya, ml_out_g[j][None, :], ev_w_out[j].astype(BF16), mod, bsz)
        else:
            pad_g = RW_GATE_PAD - RW_GATE_LORA
            zeros = jnp.zeros((RW_DECAY_LORA, d), F32)
            w2 = jnp.concatenate([jnp.concatenate([rw_w2[j, 0], zeros], axis=1),
                                  jnp.concatenate([zeros, rw_w2[j, 1]], axis=1)], axis=0)
            a2 = jnp.concatenate([jnp.concatenate([rw_a2[j, 0], zeros], axis=1),
                                  jnp.concatenate([zeros, rw_a2[j, 1]], axis=1)], axis=0)
            sh, dr_, lw, gate, bonus = _rwkv_in_call(
                z, n1, mod, rw_mu[j].reshape(12, d), rw_w_rkv[j].astype(BF16),
                jnp.concatenate([rw_w1[j, 0], rw_w1[j, 1]], axis=1).astype(BF16),
                jnp.concatenate([rw_a1[j, 0], rw_a1[j, 1]], axis=1).astype(BF16),
                jnp.pad(rw_g1[j], ((0, 0), (0, pad_g))).astype(BF16),
                w2.astype(BF16), a2.astype(BF16),
                jnp.pad(rw_g2[j], ((0, pad_g), (0, 0))).astype(BF16),
                rw_w0[j], rw_a0[j], rw_k_k[j][None, :], rw_k_a[j][None, :], rw_r_k[j].reshape(1, d),
                grp_d, bsz, lc)
            yf, yb = _rwkv_scan_call(sh, dr_, lw, lc)
            z = _rwkv_out_call(z, yf, yb, gate, bonus, rw_ln_g[j][None, :], rw_ln_b[j][None, :], grp_d,
                               rw_w_out[j].astype(BF16), mod, bsz, lc)
        last = layer == depth - 1
        z = _mlp_call(z, norm2_g[layer][None, :], mod, mlp_w1[layer].astype(BF16), mlp_w2[layer].astype(BF16),
                      bsz, lc // TOKEN_TILE if last else 0)
    return z
```

```python
import functools

import jax
import jax.numpy as jnp
from jax import lax
from jax.experimental import pallas as pl
from jax.experimental.pallas import tpu as pltpu

F32 = jnp.float32
BF16 = jnp.bfloat16

D_MODEL = 1024
N_MOD = 6
D_FF = 4 * D_MODEL
EPS = 1e-6
GRID_W = 64

ML_HEADS = 4
ML_D = 128
ML_W = ML_HEADS * ML_D
ML_CHUNK = 128
GATE_CAP = 15.0

AT_DH = 64
AT_HEADS = 8
AT_KV_HEADS = 2
AT_W = AT_HEADS * AT_DH
AT_BLOCK = 128
ROPE_BASE = 10000.0

RW_HEAD = 64
RW_PAIRS = D_MODEL // 128
RW_CHUNK = 64
RW_LN_EPS = 64e-5
RW_DECAY_LORA = 64
RW_AAA_LORA = 64
RW_GATE_LORA = 160
RW_GATE_PAD = 256

LANES = 128
TOKEN_TILE = 256
FF_TILE = 1024
MOD_ROWS = 16
VMEM_LIMIT = 56 * 1024 * 1024
NEG = -1e30
RW_MM = BF16
RW_BATCH = 2
RW_ROW_TILE = 8

EV_MQKV = 0
EV_MO = 3 * ML_W
EV_AQ = EV_MO + ML_W
EV_AK = EV_AQ + AT_W
EV_AV = EV_AK + 256
EV_GATE = EV_AV + 256
EV_COLS = EV_GATE + LANES


def _dot(a, b):
    return jnp.dot(a, b, preferred_element_type=F32)


def _dot_tb(a, b):
    return lax.dot_general(a, b, (((1,), (1,)), ((), ())), preferred_element_type=F32)


def _dot_ta(a, b):
    return lax.dot_general(a, b, (((0,), (0,)), ((), ())), preferred_element_type=F32)


def _norm_mod(z, g, shift, scale):
    ms = jnp.mean(z * z, axis=-1, keepdims=True)
    return (z * lax.rsqrt(ms + EPS) * g) * (1.0 + scale) + shift


def _sigmoid(x):
    return 1.0 / (1.0 + jnp.exp(-x))


def _softplus(x):
    return jnp.maximum(x, 0.0) + jnp.log(1.0 + jnp.exp(-jnp.abs(x)))


def _head_sum(x, e_ref, et_ref, terms):
    s = _dot(x.astype(BF16), e_ref[...])
    hi = s.astype(BF16)
    out = _dot(hi, et_ref[...])
    if terms > 1:
        out = out + _dot((s - hi.astype(F32)).astype(BF16), et_ref[...])
    return out


def _const_spec(shape):
    nd = len(shape)
    return pl.BlockSpec(shape, lambda *_: (0,) * nd)


def _params(sem):
    return pltpu.CompilerParams(dimension_semantics=sem, vmem_limit_bytes=VMEM_LIMIT)


def _ada_kernel(c_ref, w_ref, b_ref, o_ref):
    cv = c_ref[...]
    s = cv * _sigmoid(cv)
    o_ref[...] = jnp.dot(s, w_ref[...], preferred_element_type=F32,
                         precision=lax.Precision.HIGHEST) + b_ref[...]


def _ada_call(cstack, ada_w, ada_b):
    depth = ada_w.shape[0]
    n = N_MOD * D_MODEL
    tn = 1024
    return pl.pallas_call(
        _ada_kernel,
        grid=(depth, n // tn),
        in_specs=[pl.BlockSpec((MOD_ROWS, D_MODEL), lambda l, j: (0, 0)),
                  pl.BlockSpec((None, D_MODEL, tn), lambda l, j: (l, 0, j)),
                  pl.BlockSpec((None, 1, tn), lambda l, j: (l, 0, j))],
        out_specs=pl.BlockSpec((None, MOD_ROWS, tn), lambda l, j: (l, 0, j)),
        out_shape=jax.ShapeDtypeStruct((depth, MOD_ROWS, n), F32),
        compiler_params=_params(("parallel", "parallel")),
        name="ada_ln",
    )(cstack, ada_w, ada_b.reshape(depth, 1, n))


def _rope(x, cos, sin, lane_lo):
    n = x.shape[1]
    up = pltpu.roll(x, n - 16, 1)
    dn = pltpu.roll(x, 16, 1)
    reps = n // LANES
    c = jnp.concatenate([cos] * reps, axis=1) if reps > 1 else cos
    s = jnp.concatenate([sin] * reps, axis=1) if reps > 1 else sin
    lo = jnp.concatenate([lane_lo] * reps, axis=1) if reps > 1 else lane_lo
    return x * c + jnp.where(lo, up, dn) * s


def _even_in_kernel(z_ref, g_ref, mod_ref, w_ref, b_ref, gq_ref, gk_ref, grp_ref, cos_ref, sin_ref,
                    fb_ref, isf_ref, mqkv_ref, mo_ref, aq_ref, akv_ref, gate_ref):
    h = _norm_mod(z_ref[...], g_ref[...], mod_ref[0:1, :], mod_ref[1:2, :])
    p = _dot(h.astype(BF16), w_ref[...]) + b_ref[...]
    mqkv_ref[:, 0:ML_W] = p[:, 0:ML_W].astype(BF16)
    mqkv_ref[:, ML_W:2 * ML_W] = (p[:, ML_W:2 * ML_W] * (ML_D ** -0.5)).astype(BF16)
    mqkv_ref[:, 2 * ML_W:3 * ML_W] = p[:, 2 * ML_W:3 * ML_W].astype(BF16)
    mo_ref[...] = p[:, EV_MO:EV_MO + ML_W]
    cos = cos_ref[...]
    sin = sin_ref[...]
    lane_lo = (lax.broadcasted_iota(jnp.int32, (1, LANES), 1) % 32) < 16
    q = p[:, EV_AQ:EV_AQ + AT_W]
    qms = _dot((q * q).astype(BF16), grp_ref[...])
    q = q * lax.rsqrt(qms + EPS) * gq_ref[...]
    aq_ref[...] = (_rope(q, cos, sin, lane_lo) * (AT_DH ** -0.5)).astype(BF16)
    k = p[:, EV_AK:EV_AK + 256]
    kms = _dot((k * k).astype(BF16), grp_ref[0:256, 0:256])
    k = k * lax.rsqrt(kms + EPS) * gk_ref[...]
    akv_ref[:, 0:256] = _rope(k, cos, sin, lane_lo).astype(BF16)
    akv_ref[:, 256:512] = p[:, EV_AV:EV_AV + 256].astype(BF16)
    gt = p[:, EV_GATE:EV_GATE + LANES] + fb_ref[...]
    sc = GATE_CAP * jnp.tanh(gt * (1.0 / GATE_CAP))
    logsig = jnp.minimum(sc, 0.0) - jnp.log(1.0 + jnp.exp(-jnp.abs(sc)))
    gate_ref[...] = jnp.where(isf_ref[...] > 0.5, logsig, sc)


def _even_in_call(z, g, mod, w, b, gq, gk, grp, cos, sin, fb, isf, nb):
    bsz, t, d = z.shape
    tm = TOKEN_TILE
    tok = lambda width: pl.BlockSpec((None, tm, width), lambda bi, i: (bi, i, 0))
    return pl.pallas_call(
        _even_in_kernel,
        grid=(bsz, t // tm),
        in_specs=[tok(d),
                  _const_spec((1, d)),
                  pl.BlockSpec((None, N_MOD, d), lambda bi, i: (jnp.where(i == 0, nb, bi), 0, 0)),
                  _const_spec((d, EV_COLS)),
                  _const_spec((1, EV_COLS)),
                  _const_spec((1, AT_W)),
                  _const_spec((1, 256)),
                  _const_spec((AT_W, AT_W)),
                  pl.BlockSpec((tm, LANES), lambda bi, i: (i, 0)),
                  pl.BlockSpec((tm, LANES), lambda bi, i: (i, 0)),
                  _const_spec((1, LANES)),
                  _const_spec((1, LANES))],
        out_specs=[tok(3 * ML_W), tok(ML_W), tok(AT_W), tok(512), tok(LANES)],
        out_shape=[jax.ShapeDtypeStruct((bsz, t, 3 * ML_W), BF16),
                   jax.ShapeDtypeStruct((bsz, t, ML_W), F32),
                   jax.ShapeDtypeStruct((bsz, t, AT_W), BF16),
                   jax.ShapeDtypeStruct((bsz, t, 512), BF16),
                   jax.ShapeDtypeStruct((bsz, t, LANES), F32)],
        compiler_params=_params(("parallel", "parallel")),
        name="even_in",
    )(z, g, mod, w, b, gq, gk, grp, cos, sin, fb, isf)


def _split3(x):
    hi = x.astype(BF16)
    r1 = x - hi.astype(F32)
    mid = r1.astype(BF16)
    return hi, mid, (r1 - mid.astype(F32)).astype(BF16)


def _mlstm_kernel(qf_ref, ktf_ref, vf_ref, qb_ref, ktb_ref, vb_ref, gcf_ref, gcb_ref, grf_ref, grb_ref, sel_ref,
                  hf_ref, hb_ref, c_s, m_s):
    ln = ML_CHUNK

    @pl.when(pl.program_id(1) == 0)
    def _():
        c_s[...] = jnp.zeros_like(c_s)
        m_s[...] = jnp.zeros_like(m_s)

    ri = lax.broadcasted_iota(jnp.int32, (ln, ln), 0)
    ci = lax.broadcasted_iota(jnp.int32, (ln, ln), 1)
    hi = lax.Precision.HIGHEST
    before = (ci <= ri, ci >= ri)
    refs = ((qf_ref, ktf_ref, vf_ref, gcf_ref, grf_ref, hf_ref), (qb_ref, ktb_ref, vb_ref, gcb_ref, grb_ref, hb_ref))
    cum_r = [jnp.dot(refs[d][4][...], before[1 - d].astype(F32), preferred_element_type=F32, precision=hi)
             for d in range(2)]
    bc_all = []
    for d in range(2):
        cum_c = jnp.dot(before[d].astype(F32), refs[d][3][...], preferred_element_type=F32, precision=hi)
        sel = sel_ref[d]
        bc_all.append(sum(_dot(part, sel) for part in _split3(cum_c)))
    chains = [(d, hd) for d in range(2) for hd in range(ML_HEADS)]
    lanes = [slice(hd * ML_D, (hd + 1) * ML_D) for _, hd in chains]
    st = [d * ML_HEADS + hd for d, hd in chains]
    ones = jnp.ones((ln, ML_D), BF16)
    q = [refs[d][0][:, ls] for (d, _), ls in zip(chains, lanes)]
    kt = [refs[d][1][ls, :] for (d, _), ls in zip(chains, lanes)]
    v1 = [jnp.concatenate([refs[d][2][:, ls], ones], axis=1) for (d, _), ls in zip(chains, lanes)]
    i_row = [refs[d][4][8 * d + hd:8 * d + hd + 1, :] for d, hd in chains]
    f_row = [refs[d][4][8 * d + 4 + hd:8 * d + 5 + hd, :] for d, hd in chains]
    bcum_row = [cum_r[d][8 * d + 4 + hd:8 * d + 5 + hd, :] for d, hd in chains]
    bcum = [bc_all[d][:, ls] for (d, _), ls in zip(chains, lanes)]
    m_old = [m_s[s:s + 1, 0:1] for s in st]
    c_old = [c_s[s] for s in st]
    qk = [_dot(a, b) for a, b in zip(q, kt)]
    r2 = [_dot(a, c.astype(BF16)) for a, c in zip(q, c_old)]
    dlog = [jnp.where(before[d], bc - br + ir, NEG) for (d, _), bc, br, ir in zip(chains, bcum, bcum_row, i_row)]
    inter = [bc + m for bc, m in zip(bcum, m_old)]
    mt = [jnp.maximum(jnp.max(dl, axis=1, keepdims=True), it) for dl, it in zip(dlog, inter)]
    sc = [x * jnp.exp(dl - m) for x, dl, m in zip(qk, dlog, mt)]
    iw = [jnp.exp(it - m) for it, m in zip(inter, mt)]
    r1 = [_dot(s.astype(BF16), b) for s, b in zip(sc, v1)]
    for j, (d, _) in enumerate(chains):
        tot = r1[j] + jnp.concatenate([iw[j], iw[j]], axis=1) * r2[j]
        den = jnp.maximum(jnp.abs(tot[:, ML_D:]), jnp.exp(-mt[j]))
        refs[d][5][:, lanes[j]] = tot[:, :ML_D] / den
    bl = [jnp.sum(fr, axis=1, keepdims=True) for fr in f_row]
    ws_log = [b - br + ir for b, br, ir in zip(bl, bcum_row, i_row)]
    m_new = [jnp.maximum(b + m, jnp.max(w, axis=1, keepdims=True)) for b, m, w in zip(bl, m_old, ws_log)]
    ws = [jnp.exp(w - m) for w, m in zip(ws_log, m_new)]
    upd = [_dot((a.astype(F32) * w).astype(BF16), b) for a, w, b in zip(kt, ws, v1)]
    for j, s in enumerate(st):
        c_s[s] = jnp.exp(bl[j] + m_old[j] - m_new[j]) * c_old[j] + upd[j]
        m_s[s:s + 1, :] = jnp.broadcast_to(m_new[j], (1, LANES))


def _rev_chunk(i, nctx, n):
    return jnp.where(i < nctx, nctx - 1 - i, n + nctx - 1 - i)


def _mlstm_call(mqkv, kt, gates, gates_t, lc):
    bsz, t, _ = mqkv.shape
    ln = ML_CHUNK
    assert ln == ML_D == LANES
    n = t // ln
    nctx = lc // ln
    rv = lambda i: _rev_chunk(i, nctx, n)
    col = jnp.arange(LANES)[None, :, None]
    want = (8 * jnp.arange(2)[:, None, None] + 4 + jnp.arange(ML_HEADS * ln)[None, None, :] // ln)
    sel = (col == want).astype(BF16)
    return pl.pallas_call(
        _mlstm_kernel,
        grid=(bsz, n),
        in_specs=[pl.BlockSpec((None, ln, ML_W), lambda bi, i: (bi, i, 0)),
                  pl.BlockSpec((None, ML_W, ln), lambda bi, i: (bi, 0, i)),
                  pl.BlockSpec((None, ln, ML_W), lambda bi, i: (bi, i, 2)),
                  pl.BlockSpec((None, ln, ML_W), lambda bi, i: (bi, rv(i), 0)),
                  pl.BlockSpec((None, ML_W, ln), lambda bi, i: (bi, 0, rv(i))),
                  pl.BlockSpec((None, ln, ML_W), lambda bi, i: (bi, rv(i), 2)),
                  pl.BlockSpec((None, ln, LANES), lambda bi, i: (bi, i, 0)),
                  pl.BlockSpec((None, ln, LANES), lambda bi, i: (bi, rv(i), 0)),
                  pl.BlockSpec((None, 16, ln), lambda bi, i: (bi, 0, i)),
                  pl.BlockSpec((None, 16, ln), lambda bi, i: (bi, 0, rv(i))),
                  _const_spec((2, LANES, ML_HEADS * ln))],
        out_specs=[pl.BlockSpec((None, ln, ML_W), lambda bi, i: (bi, i, 0)),
                   pl.BlockSpec((None, ln, ML_W), lambda bi, i: (bi, rv(i), 0))],
        out_shape=[jax.ShapeDtypeStruct((bsz, t, ML_W), F32)] * 2,
        scratch_shapes=[pltpu.VMEM((2 * ML_HEADS, ML_D, 2 * ML_D), F32),
                        pltpu.VMEM((2 * ML_HEADS, LANES), F32)],
        compiler_params=_params(("parallel", "arbitrary")),
        name="mlstm_scan",
    )(mqkv, kt, mqkv, mqkv, kt, mqkv, gates, gates, gates_t, gates_t, sel)


def _attn_kernel(q_ref, ktp_ref, kto_ref, ktn_ref, ktc_ref, vp_ref, vo_ref, vn_ref, vc_ref, sink_ref, o_ref,
                 *, nctx, n):
    j = pl.program_id(1)
    blk = AT_BLOCK
    latent = j >= nctx
    qi = lax.broadcasted_iota(jnp.int32, (blk, blk), 0)
    ki = lax.broadcasted_iota(jnp.int32, (blk, blk), 1)
    ok_prev = jnp.logical_and(jnp.logical_and(latent, j > nctx), ki >= qi)
    ok_own = jnp.logical_and(latent, ki >= 0)
    ok_next = jnp.logical_and(jnp.logical_and(latent, j < n - 1), ki <= qi)
    lane = lax.broadcasted_iota(jnp.int32, (1, LANES), 1)
    sub = lax.broadcasted_iota(jnp.int32, (LANES, 1), 0)
    lane_half = (lane < AT_DH, lane >= AT_DH)
    sub_half = (sub < AT_DH, sub >= AT_DH)
    kts = (ktp_ref, kto_ref, ktn_ref, ktc_ref)
    vs = (vp_ref, vo_ref, vn_ref, vc_ref)
    oks = (ok_prev, ok_own, ok_next, None)
    zero = jnp.zeros((), BF16)
    kx, vx = {}, {}
    for g in range(AT_KV_HEADS):
        rows = slice(g * LANES, (g + 1) * LANES)
        for e in range(2):
            kx[g, e] = [jnp.where(sub_half[e], r[rows, :], zero) for r in kts]
            vx[g, e] = [jnp.concatenate([jnp.where(lane_half[e], r[:, rows], zero),
                                         jnp.ones((r.shape[0], LANES), BF16)], axis=1) for r in vs]
    heads = [(hd // 2, hd % 2, hd // 4) for hd in range(AT_HEADS)]
    qc = [q_ref[:, c * LANES:(c + 1) * LANES] for c in range(AT_HEADS // 2)]
    sink = [sink_ref[hd:hd + 1, 0:1] for hd in range(AT_HEADS)]
    ss = [[_dot(qc[c], kx[g, e][p]) if oks[p] is None else jnp.where(oks[p], _dot(qc[c], kx[g, e][p]), NEG)
           for p in range(4)] for c, e, g in heads]
    m = []
    for hd in range(AT_HEADS):
        band = jnp.maximum(jnp.maximum(ss[hd][0], ss[hd][1]), ss[hd][2])
        m.append(jnp.maximum(jnp.maximum(jnp.max(band, axis=1, keepdims=True),
                                         jnp.max(ss[hd][3], axis=1, keepdims=True)), sink[hd]))
    res = []
    for hd, (c, e, g) in enumerate(heads):
        acc = None
        for p in range(4):
            pv = _dot(jnp.exp(ss[hd][p] - m[hd]).astype(BF16), vx[g, e][p])
            acc = pv if acc is None else acc + pv
        res.append(acc)
    outs = [r[:, :LANES] / (r[:, LANES:] + jnp.exp(s - mm)) for r, s, mm in zip(res, sink, m)]
    for c in range(AT_HEADS // 2):
        o_ref[:, c * LANES:(c + 1) * LANES] = (outs[2 * c] + outs[2 * c + 1]).astype(o_ref.dtype)


def _attn_call(aq, akt, akv, sink, lc):
    bsz, t, _ = aq.shape
    blk = AT_BLOCK
    n = t // blk
    nctx = lc // blk
    prev = lambda j: jnp.clip(j - 1, nctx, n - 1)
    own = lambda j: jnp.clip(j, nctx, n - 1)
    nxt = lambda j: jnp.clip(j + 1, nctx, n - 1)
    kt = lambda f: pl.BlockSpec((None, 256, blk), lambda bi, j: (bi, 0, f(j)))
    vv = lambda f: pl.BlockSpec((None, blk, 256), lambda bi, j: (bi, f(j), 1))
    return pl.pallas_call(
        functools.partial(_attn_kernel, nctx=nctx, n=n),
        grid=(bsz, n),
        in_specs=[pl.BlockSpec((None, blk, AT_W), lambda bi, j: (bi, j, 0)),
                  kt(prev), kt(own), kt(nxt),
                  pl.BlockSpec((None, 256, lc), lambda bi, j: (bi, 0, 0)),
                  vv(prev), vv(own), vv(nxt),
                  pl.BlockSpec((None, lc, 256), lambda bi, j: (bi, 0, 1)),
                  _const_spec((AT_HEADS, LANES))],
        out_specs=pl.BlockSpec((None, blk, AT_W), lambda bi, j: (bi, j, 0)),
        out_shape=jax.ShapeDtypeStruct((bsz, t, AT_W), BF16),
        compiler_params=_params(("parallel", "parallel")),
        name="window_attn",
    )(aq, akt, akt, akt, akt, akv, akv, akv, akv, sink)


def _even_out_kernel(z_ref, hf_ref, hb_ref, mo_ref, ya_ref, og_ref, w_ref, mod_ref, o_ref):
    hs = hf_ref[...] + hb_ref[...]
    parts = []
    for hd in range(ML_HEADS):
        x = hs[:, hd * ML_D:(hd + 1) * ML_D]
        parts.append(x * lax.rsqrt(jnp.mean(x * x, axis=1, keepdims=True) + EPS))
    ym = jnp.concatenate(parts, axis=1) * og_ref[...] * _sigmoid(mo_ref[...])
    y = _dot(ym.astype(BF16), w_ref[0:ML_W, :]) + _dot(ya_ref[...], w_ref[ML_W:ML_W + AT_W, :])
    o_ref[...] = z_ref[...] + mod_ref[2:3, :] * y


def _even_out_call(z, hf, hb, mo, ya, og, w, mod, nb):
    bsz, t, d = z.shape
    tm = TOKEN_TILE
    tok = lambda width: pl.BlockSpec((None, tm, width), lambda bi, i: (bi, i, 0))
    return pl.pallas_call(
        _even_out_kernel,
        grid=(bsz, t // tm),
        in_specs=[tok(d), tok(ML_W), tok(ML_W), tok(ML_W), tok(AT_W),
                  _const_spec((1, ML_W)),
                  _const_spec((ML_W + AT_W, d)),
                  pl.BlockSpec((None, N_MOD, d), lambda bi, i: (jnp.where(i == 0, nb, bi), 0, 0))],
        out_specs=tok(d),
        out_shape=jax.ShapeDtypeStruct((bsz, t, d), F32),
        compiler_params=_params(("parallel", "parallel")),
        name="even_out",
    )(z, hf, hb, mo, ya, og, w, mod)


def _mlp_kernel(z_ref, g_ref, mod_ref, w1_ref, w2_ref, o_ref):
    z = z_ref[...]
    h = _norm_mod(z, g_ref[...], mod_ref[3:4, :], mod_ref[4:5, :]).astype(BF16)
    acc = jnp.zeros(z.shape, F32)
    for f in range(D_FF // FF_TILE):
        a = jnp.maximum(_dot(h, w1_ref[:, f * FF_TILE:(f + 1) * FF_TILE]), 0.0)
        acc = acc + _dot((a * a).astype(BF16), w2_ref[f * FF_TILE:(f + 1) * FF_TILE, :])
    o_ref[...] = z + mod_ref[5:6, :] * acc


def _mlp_call(z, g, mod, w1, w2, nb, skip):
    bsz, t, d = z.shape
    tm = TOKEN_TILE
    nt = t // tm - skip
    mod_idx = (lambda bi, i: (bi, 0, 0)) if skip else (lambda bi, i: (jnp.where(i == 0, nb, bi), 0, 0))
    return pl.pallas_call(
        _mlp_kernel,
        grid=(bsz, nt),
        in_specs=[pl.BlockSpec((None, tm, d), lambda bi, i: (bi, i + skip, 0)),
                  _const_spec((1, d)),
                  pl.BlockSpec((None, N_MOD, d), mod_idx),
                  _const_spec((d, D_FF)),
                  _const_spec((D_FF, d))],
        out_specs=pl.BlockSpec((None, tm, d), lambda bi, i: (bi, i, 0)),
        out_shape=jax.ShapeDtypeStruct((bsz, nt * tm, d), F32),
        compiler_params=_params(("parallel", "parallel")),
        name="mlp",
    )(z, g, mod, w1, w2)


def _rwkv_in_kernel(z_ref, zp_ref, zn_ref, g_ref, mod_ref, mu_ref, wrkv_ref, w1_ref, a1_ref, g1_ref,
                    w2_ref, a2_ref, g2_ref, w0_ref, a0_ref, kk_ref, ka_ref, rk_ref, e_ref, et_ref,
                    sh_ref, dr_ref, lw_ref, gate_ref, bonus_ref, *, nctx, ntile):
    i = pl.program_id(1)
    tm, d = z_ref.shape
    g = g_ref[...]
    shift = mod_ref[0:1, :]
    scale = mod_ref[1:2, :]
    h = _norm_mod(z_ref[...], g, shift, scale)
    no_prev = jnp.logical_or(i == 0, i == nctx)
    no_next = jnp.logical_or(i == nctx - 1, i == ntile - 1)
    hp = jnp.where(no_prev, 0.0, _norm_mod(zp_ref[7:8, :], g, shift, scale))
    hn = jnp.where(no_next, 0.0, _norm_mod(zn_ref[0:1, :], g, shift, scale))
    row = lax.broadcasted_iota(jnp.int32, (tm, 1), 0)
    dp = jnp.where(row == 0, hp, pltpu.roll(h, 1, 0)) - h
    dn = jnp.where(row == tm - 1, hn, pltpu.roll(h, tm - 1, 0)) - h

    def mix(n):
        return (h + mu_ref[2 * n:2 * n + 1, :] * dp + mu_ref[2 * n + 1:2 * n + 2, :] * dn).astype(BF16)

    r = _dot(mix(0), wrkv_ref[0])
    k = _dot(mix(2), wrkv_ref[1])
    v = _dot(mix(3), wrkv_ref[2])
    gate_ref[...] = _dot(_sigmoid(_dot(mix(5), g1_ref[...])).astype(BF16), g2_ref[...])
    lora_w = _dot(jnp.tanh(_dot(mix(1), w1_ref[...])).astype(BF16), w2_ref[...])
    lora_a = _dot(_dot(mix(4), a1_ref[...]).astype(BF16), a2_ref[...])
    kkr = k * kk_ref[...]
    ssq = _head_sum(kkr * kkr, e_ref, et_ref, 1)
    kk = kkr * lax.rsqrt(jnp.maximum(ssq, 1e-24))
    kd_sum = None
    for dr in range(2):
        cols = slice(dr * d, (dr + 1) * d)
        w_log = -_softplus(-(w0_ref[dr:dr + 1, :] + lora_w[:, cols])) - 0.5
        lw = -jnp.exp(w_log)
        a = _sigmoid(a0_ref[dr:dr + 1, :] + lora_a[:, cols])
        kd = k * (1.0 + (a - 1.0) * ka_ref[...])
        bvec = kk * a
        kd_sum = kd if kd_sum is None else kd_sum + kd
        for p in range(RW_PAIRS):
            ls = slice(p * LANES, (p + 1) * LANES)
            lw_ref[dr, p] = lw[:, ls]
            dr_ref[dr, 0, p] = kd[:, ls].astype(BF16)
            dr_ref[dr, 1, p] = bvec[:, ls].astype(BF16)
    for p in range(RW_PAIRS):
        ls = slice(p * LANES, (p + 1) * LANES)
        sh_ref[0, p] = r[:, ls].astype(BF16)
        sh_ref[1, p] = v[:, ls].astype(BF16)
        sh_ref[2, p] = kk[:, ls].astype(BF16)
    bsum = _head_sum(r * kd_sum * rk_ref[...], e_ref, et_ref, 1)
    bonus_ref[...] = bsum * v


def _rwkv_in_call(z, g, mod, mu, wrkv, w1, a1, g1, w2, a2, g2, w0, a0, k_k, k_a, r_k, e, et, nb, lc):
    bsz, t, d = z.shape
    tm = TOKEN_TILE
    ntile = t // tm
    nctx = lc // tm
    r8 = tm // 8
    tok = pl.BlockSpec((None, tm, d), lambda bi, i: (bi, i, 0))
    return pl.pallas_call(
        functools.partial(_rwkv_in_kernel, nctx=nctx, ntile=ntile),
        grid=(bsz, ntile),
        in_specs=[tok,
                  pl.BlockSpec((None, 8, d), lambda bi, i: (bi, jnp.maximum(i * r8 - 1, 0), 0)),
                  pl.BlockSpec((None, 8, d), lambda bi, i: (bi, jnp.minimum((i + 1) * r8, t // 8 - 1), 0)),
                  _const_spec((1, d)),
                  pl.BlockSpec((None, N_MOD, d), lambda bi, i: (jnp.where(i < nctx, nb, bi), 0, 0)),
                  _const_spec((12, d)),
                  _const_spec((3, d, d)),
                  _const_spec((d, LANES)),
                  _const_spec((d, LANES)),
                  _const_spec((d, RW_GATE_PAD)),
                  _const_spec((LANES, 2 * d)),
                  _const_spec((LANES, 2 * d)),
                  _const_spec((RW_GATE_PAD, d)),
                  _const_spec((2, d)),
                  _const_spec((2, d)),
                  _const_spec((1, d)),
                  _const_spec((1, d)),
                  _const_spec((1, d)),
                  _const_spec((d, LANES)),
                  _const_spec((LANES, d))],
        out_specs=[pl.BlockSpec((None, 3, RW_PAIRS, tm, LANES), lambda bi, i: (bi, 0, 0, i, 0)),
                   pl.BlockSpec((None, 2, 2, RW_PAIRS, tm, LANES), lambda bi, i: (bi, 0, 0, 0, i, 0)),
                   pl.BlockSpec((None, 2, RW_PAIRS, tm, LANES), lambda bi, i: (bi, 0, 0, i, 0)),
                   tok, tok],
        out_shape=[jax.ShapeDtypeStruct((bsz, 3, RW_PAIRS, t, LANES), BF16),
                   jax.ShapeDtypeStruct((bsz, 2, 2, RW_PAIRS, t, LANES), BF16),
                   jax.ShapeDtypeStruct((bsz, 2, RW_PAIRS, t, LANES), F32),
                   jax.ShapeDtypeStruct((bsz, t, d), F32),
                   jax.ShapeDtypeStruct((bsz, t, d), F32)],
        compiler_params=_params(("parallel", "parallel")),
        name="rwkv_in",
    )(z, z, z, g, mod, mu, wrkv, w1, a1, g1, w2, a2, g2, w0, a0, k_k, k_a, r_k, e, et)


def _rwkv_chunk_group(chains):
    ln = RW_CHUNK
    row = lax.broadcasted_iota(jnp.int32, (ln, 2 * ln), 0)
    col = lax.broadcasted_iota(jnp.int32, (ln, 2 * ln), 1)
    sidx = jnp.where(col >= ln, col - ln, col)
    lane_a = lax.broadcasted_iota(jnp.int32, (1, LANES), 1) < RW_HEAD
    tr = lax.broadcasted_iota(jnp.int32, (ln, ln), 0)
    tc = lax.broadcasted_iota(jnp.int32, (ln, ln), 1)
    tri = {False: (tc <= tr).astype(F32), True: (tc >= tr).astype(F32)}
    strict = {False: sidx < row, True: sidx > row}
    incl = {False: sidx <= row, True: sidx >= row}
    eye_w = jnp.where(sidx == row, 1.0, 0.0)
    diff = row ^ sidx
    vr = lax.broadcasted_iota(jnp.int32, (LANES, LANES), 0) < RW_HEAD
    kc = lax.broadcasted_iota(jnp.int32, (LANES, LANES), 1) < RW_HEAD
    same_head = vr == kc

    def bd(x):
        zero = jnp.zeros((), x.dtype)
        return jnp.concatenate([jnp.where(lane_a, x, zero), jnp.where(lane_a, zero, x)], axis=0)

    revs = [c[7] for c in chains]
    vs = [c[1] for c in chains]
    s_olds = [c[6] for c in chains]
    gcum = [jnp.dot(tri[c[7]], c[5], preferred_element_type=F32, precision=lax.Precision.HIGHEST) for c in chains]
    gtot = [jnp.sum(c[5], axis=0, keepdims=True) for c in chains]

    def scaled(c, g, gt):
        r, v, kk, kd, bv, lw = (x.astype(F32) for x in c[:6])
        e_pos = jnp.exp(g)
        e_neg = jnp.exp(-g)
        e_end = jnp.exp(gt - g)
        ar = jnp.concatenate([(-kk * jnp.exp(g - lw)).astype(RW_MM), (r * e_pos).astype(RW_MM)], axis=0)
        bk_end = jnp.concatenate([(bv * e_end).astype(RW_MM), (kd * e_end).astype(RW_MM)], axis=0)
        return ar, (bv * e_neg).astype(RW_MM), (kd * e_neg).astype(RW_MM), bk_end

    sc = [scaled(c, g, gt) for c, g, gt in zip(chains, gcum, gtot)]
    ars_ = [x[0] for x in sc]
    x_bk = [_dot_tb(x[0], jnp.concatenate([bd(x[1]), bd(x[2])], axis=0)) for x in sc]
    x_b = [x[:, :2 * ln] for x in x_bk]
    x_k = [x[:, 2 * ln:] for x in x_bk]
    n_w =[jnp.where(strict[rv], x[:ln], 0.0) for x, rv in zip(x_b, revs)]
    m_rb = [jnp.where(incl[rv], x[ln:], 0.0).astype(RW_MM) for x, rv in zip(x_b, revs)]
    m_k = [jnp.concatenate([jnp.where(strict[rv], x[:ln], 0.0), jnp.where(incl[rv], x[ln:], 0.0)],
                           axis=0).astype(RW_MM) for x, rv in zip(x_k, revs)]
    x_w = [eye_w + jnp.where(diff == 1, n, 0.0) for n in n_w]

    def take_rows(x, h, odd):
        return jnp.concatenate([x[b * h:(b + 1) * h] for b in range(ln // h) if (b % 2 == 1) == odd], axis=0)

    def put_rows(base, upd, h, odd):
        parts, j = [], 0
        for b in range(ln // h):
            if (b % 2 == 1) == odd:
                blk = upd[j * h:(j + 1) * h]
                parts.append(blk if base is None else base[b * h:(b + 1) * h] + blk)
                j += 1
            else:
                parts.append(jnp.zeros((h, upd.shape[1]), upd.dtype) if base is None else base[b * h:(b + 1) * h])
        return jnp.concatenate(parts, axis=0)

    h = 2
    while h < ln:
        lvl = jnp.logical_and(diff >= h, diff < 2 * h)
        n_l = [jnp.where(lvl, n, 0.0) for n in n_w]
        if h < RW_ROW_TILE:
            tmp = [_dot(n.astype(RW_MM), bd(x.astype(RW_MM))) for n, x in zip(n_l, x_w)]
            x_w = [x + _dot(x.astype(RW_MM), bd(t.astype(RW_MM))) for x, t in zip(x_w, tmp)]
        else:
            odd = [not rv for rv in revs]
            tmp = [_dot(take_rows(n, h, o).astype(RW_MM), bd(x.astype(RW_MM))) for n, x, o in zip(n_l, x_w, odd)]
            tmp = [put_rows(None, t, h, o) for t, o in zip(tmp, odd)]
            cor = [_dot(take_rows(x, h, o).astype(RW_MM), bd(t.astype(RW_MM))) for x, t, o in zip(x_w, tmp, odd)]
            x_w = [put_rows(x, c, h, o) for x, c, o in zip(x_w, cor, odd)]
        h *= 2
    ars = [_dot_tb(a, s.astype(RW_MM)) for a, s in zip(ars_, s_olds)]
    mv = [_dot(m, bd(v)) for m, v in zip(m_k, vs)]
    u = [_dot(x.astype(RW_MM), bd((a[:ln] + m[:ln]).astype(RW_MM))) for x, a, m in zip(x_w, ars, mv)]
    y = [a[ln:] + m[ln:] + _dot(rb, bd(uu.astype(RW_MM))) for a, m, rb, uu in zip(ars, mv, m_rb, u)]
    upd = [_dot_ta(jnp.concatenate([uu.astype(RW_MM), v], axis=0), x[3]) for uu, v, x in zip(u, vs, sc)]
    s_new = [s * jnp.exp(gt) + jnp.where(same_head, up, 0.0) for s, gt, up in zip(s_olds, gtot, upd)]
    return list(zip(y, s_new))


def _rwkv_scan_kernel(shf_ref, shb_ref, drf_ref, drb_ref, lwf_ref, lwb_ref, yf_ref, yb_ref, s_s):
    @pl.when(pl.program_id(1) == 0)
    def _():
        s_s[...] = jnp.zeros_like(s_s)

    chains, where = [], []
    for bb in range(RW_BATCH):
        for p in range(RW_PAIRS):
            chains.append((shf_ref[bb, 0, p], shf_ref[bb, 1, p], shf_ref[bb, 2, p], drf_ref[bb, 0, p],
                           drf_ref[bb, 1, p], lwf_ref[bb, p], s_s[bb, 0, p], False))
            where.append((yf_ref, bb, 0, p))
            chains.append((shb_ref[bb, 0, p], shb_ref[bb, 1, p], shb_ref[bb, 2, p], drb_ref[bb, 0, p],
                           drb_ref[bb, 1, p], lwb_ref[bb, p], s_s[bb, 1, p], True))
            where.append((yb_ref, bb, 1, p))
    for (y, s_new), (y_ref, bb, d, p) in zip(_rwkv_chunk_group(chains), where):
        y_ref[bb, p] = y
        s_s[bb, d, p] = s_new


def _rwkv_scan_call(sh, dr, lw, lc):
    bsz, _, _, t, _ = sh.shape
    ln = RW_CHUNK
    nb = RW_BATCH
    assert bsz % nb == 0
    n = t // ln
    nctx = lc // ln
    rv = lambda i: _rev_chunk(i, nctx, n)
    return pl.pallas_call(
        _rwkv_scan_kernel,
        grid=(bsz // nb, n),
        in_specs=[pl.BlockSpec((nb, 3, RW_PAIRS, ln, LANES), lambda bi, i: (bi, 0, 0, i, 0)),
                  pl.BlockSpec((nb, 3, RW_PAIRS, ln, LANES), lambda bi, i: (bi, 0, 0, rv(i), 0)),
                  pl.BlockSpec((nb, None, 2, RW_PAIRS, ln, LANES), lambda bi, i: (bi, 0, 0, 0, i, 0)),
                  pl.BlockSpec((nb, None, 2, RW_PAIRS, ln, LANES), lambda bi, i: (bi, 1, 0, 0, rv(i), 0)),
                  pl.BlockSpec((nb, None, RW_PAIRS, ln, LANES), lambda bi, i: (bi, 0, 0, i, 0)),
                  pl.BlockSpec((nb, None, RW_PAIRS, ln, LANES), lambda bi, i: (bi, 1, 0, rv(i), 0))],
        out_specs=[pl.BlockSpec((nb, RW_PAIRS, ln, LANES), lambda bi, i: (bi, 0, i, 0)),
                   pl.BlockSpec((nb, RW_PAIRS, ln, LANES), lambda bi, i: (bi, 0, rv(i), 0))],
        out_shape=[jax.ShapeDtypeStruct((bsz, RW_PAIRS, t, LANES), F32)] * 2,
        scratch_shapes=[pltpu.VMEM((nb, 2, RW_PAIRS, LANES, LANES), F32)],
        compiler_params=_params(("parallel", "arbitrary")),
        name="rwkv_scan",
    )(sh, sh, dr, dr, lw, lw)


def _rwkv_out_kernel(z_ref, yf_ref, yb_ref, gate_ref, bonus_ref, lng_ref, lnb_ref, e_ref, et_ref, w_ref, mod_ref,
                     o_ref):
    y = jnp.concatenate([yf_ref[p] + yb_ref[p] for p in range(RW_PAIRS)], axis=1)
    yc = y - _head_sum(y, e_ref, et_ref, 2) * (1.0 / RW_HEAD)
    var = _head_sum(yc * yc, e_ref, et_ref, 1) * (1.0 / RW_HEAD)
    yn = yc * lax.rsqrt(var + RW_LN_EPS) * lng_ref[...] + lnb_ref[...] + bonus_ref[...]
    out = _dot((yn * gate_ref[...]).astype(BF16), w_ref[...])
    o_ref[...] = z_ref[...] + mod_ref[2:3, :] * out


def _rwkv_out_call(z, yf, yb, gate, bonus, ln_g, ln_b, e, et, w, mod, nb, lc):
    bsz, t, d = z.shape
    tm = TOKEN_TILE
    nctx = lc // tm
    tok = pl.BlockSpec((None, tm, d), lambda bi, i: (bi, i, 0))
    pair = pl.BlockSpec((None, RW_PAIRS, tm, LANES), lambda bi, i: (bi, 0, i, 0))
    return pl.pallas_call(
        _rwkv_out_kernel,
        grid=(bsz, t // tm),
        in_specs=[tok, pair, pair, tok, tok,
                  _const_spec((1, d)), _const_spec((1, d)), _const_spec((d, LANES)), _const_spec((LANES, d)),
                  _const_spec((d, d)),
                  pl.BlockSpec((None, N_MOD, d), lambda bi, i: (jnp.where(i < nctx, nb, bi), 0, 0))],
        out_specs=tok,
        out_shape=jax.ShapeDtypeStruct((bsz, t, d), F32),
        compiler_params=_params(("parallel", "parallel")),
        name="rwkv_out",
    )(z, yf, yb, gate, bonus, ln_g, ln_b, e, et, w, mod)


def _group_mean_matrix(n, width):
    idx = jnp.arange(n) // width
    return ((idx[:, None] == idx[None, :]).astype(F32) / width).astype(BF16)


def _rope_tables(lc, s):
    quarter = AT_DH // 4
    inv = ROPE_BASE ** (-jnp.arange(quarter, dtype=F32) / quarter)
    pos = jnp.arange(s)
    rpos = (pos // GRID_W).astype(F32)
    cpos = (pos % GRID_W).astype(F32)
    ang_r = rpos[:, None] * inv[None, :]
    ang_c = cpos[:, None] * inv[None, :]
    cos64 = jnp.concatenate([jnp.cos(ang_r), jnp.cos(ang_r), jnp.cos(ang_c), jnp.cos(ang_c)], axis=1)
    sin64 = jnp.concatenate([-jnp.sin(ang_r), jnp.sin(ang_r), -jnp.sin(ang_c), jnp.sin(ang_c)], axis=1)
    cos = jnp.concatenate([jnp.ones((lc, AT_DH), F32), cos64], axis=0)
    sin = jnp.concatenate([jnp.zeros((lc, AT_DH), F32), sin64], axis=0)
    return jnp.tile(cos, (1, 2)), jnp.tile(sin, (1, 2))


def _even_weights(w_in, b_in):
    def cols(m):
        mq, mk, mv, mo, mg, aq, ak, av = jnp.split(m, [512, 1024, 1536, 2048, 2064, 2576, 2704], axis=-1)
        dup = lambda u: jnp.concatenate([u[..., :64], u[..., :64], u[..., 64:], u[..., 64:]], axis=-1)
        mgp = jnp.pad(mg, [(0, 0)] * (m.ndim - 1) + [(0, LANES - 16)])
        return jnp.concatenate([mq, mk, mv, mo, aq, dup(ak), dup(av), mgp], axis=-1)
    return cols(w_in).astype(BF16), cols(b_in[None, :])


def kernel(x, c, ctx, c_ctx, ada_w, ada_b, norm1_g, norm2_g, mlp_w1, mlp_w2, ev_w_in, ev_b_in, ev_w_out, ml_f_bias, ml_out_g, at_q_g, at_k_g, at_sink, rw_mu, rw_w_rkv, rw_w0, rw_w1, rw_w2, rw_a0, rw_a1, rw_a2, rw_g1, rw_g2, rw_k_k, rw_k_a, rw_r_k, rw_ln_g, rw_ln_b, rw_w_out):
    bsz, s, d = x.shape
    lc = ctx.shape[1]
    depth = ada_w.shape[0]
    assert d == D_MODEL and lc == TOKEN_TILE and s % TOKEN_TILE == 0 and bsz < MOD_ROWS
    z = jnp.concatenate([ctx, x], axis=1)

    cstack = jnp.zeros((MOD_ROWS, d), F32).at[:bsz].set(c).at[bsz].set(c_ctx)
    mod_all = _ada_call(cstack, ada_w, ada_b).reshape(depth, MOD_ROWS, N_MOD, d)

    grp512 = _group_mean_matrix(AT_W, AT_DH)
    head_e = (jnp.arange(d)[:, None] // RW_HEAD == jnp.arange(LANES)[None, :]).astype(BF16)
    head_et = head_e.T
    cos_t, sin_t = _rope_tables(lc, s)
    gate_is_f = ((jnp.arange(LANES) % 8 >= 4) & (jnp.arange(LANES) < 16)).astype(F32)[None, :]

    for layer in range(depth):
        mod = mod_all[layer]
        j = layer // 2
        n1 = norm1_g[layer][None, :]
        if layer % 2 == 0:
            w_in, b_in = _even_weights(ev_w_in[j], ev_b_in[j])
            fb = jnp.zeros((LANES,), F32)
            for dr in range(2):
                fb = fb.at[8 * dr + 4:8 * dr + 8].set(ml_f_bias[j, dr])
            mqkv, mo, aq, akv, gates = _even_in_call(
                z, n1, mod, w_in, b_in, jnp.tile(at_q_g[j], AT_HEADS)[None, :], jnp.tile(at_k_g[j], 4)[None, :],
                grp512, cos_t, sin_t, fb[None, :], gate_is_f, bsz)
            gates_t = jnp.transpose(gates[:, :, :16], (0, 2, 1))
            kt = jnp.transpose(mqkv[:, :, ML_W:2 * ML_W], (0, 2, 1))
            hf, hb = _mlstm_call(mqkv, kt, gates, gates_t, lc)
            sink = jnp.broadcast_to(at_sink[j][:, None], (AT_HEADS, LANES))
            akt = jnp.transpose(akv[:, :, :256], (0, 2, 1))
            ya = _attn_call(aq, akt, akv, sink, lc)
            z = _even_out_call(z, hf, hb, mo, ya, ml_out_g[j][None, :], ev_w_out[j].astype(BF16), mod, bsz)
        else:
            pad_g = RW_GATE_PAD - RW_GATE_LORA
            zeros = jnp.zeros((RW_DECAY_LORA, d), F32)
            w2 = jnp.concatenate([jnp.concatenate([rw_w2[j, 0], zeros], axis=1),
                                  jnp.concatenate([zeros, rw_w2[j, 1]], axis=1)], axis=0)
            a2 = jnp.concatenate([jnp.concatenate([rw_a2[j, 0], zeros], axis=1),
                                  jnp.concatenate([zeros, rw_a2[j, 1]], axis=1)], axis=0)
            sh, dr_, lw, gate, bonus = _rwkv_in_call(
                z, n1, mod, rw_mu[j].reshape(12, d), rw_w_rkv[j].astype(BF16),
                jnp.concatenate([rw_w1[j, 0], rw_w1[j, 1]], axis=1).astype(BF16),
                jnp.concatenate([rw_a1[j, 0], rw_a1[j, 1]], axis=1).astype(BF16),
                jnp.pad(rw_g1[j], ((0, 0), (0, pad_g))).astype(BF16),
                w2.astype(BF16), a2.astype(BF16),
                jnp.pad(rw_g2[j], ((0, pad_g), (0, 0))).astype(BF16),
                rw_w0[j], rw_a0[j], rw_k_k[j][None, :], rw_k_a[j][None, :], rw_r_k[j].reshape(1, d),
                head_e, head_et, bsz, lc)
            yf, yb = _rwkv_scan_call(sh, dr_, lw, lc)
            z = _rwkv_out_call(z, yf, yb, gate, bonus, rw_ln_g[j][None, :], rw_ln_b[j][None, :], head_e, head_et,
                               rw_w_out[j].astype(BF16), mod, bsz, lc)
        last = layer == depth - 1
        z = _mlp_call(z, norm2_g[layer][None, :], mod, mlp_w1[layer].astype(BF16), mlp_w2[layer].astype(BF16),
                      bsz, lc // TOKEN_TILE if last else 0)
    return z
```

```python
import functools

import jax
import jax.numpy as jnp
from jax import lax
from jax.experimental import pallas as pl
from jax.experimental.pallas import tpu as pltpu

F32 = jnp.float32
BF16 = jnp.bfloat16

D_MODEL = 1024
N_MOD = 6
D_FF = 4 * D_MODEL
EPS = 1e-6
GRID_W = 64

ML_HEADS = 4
ML_D = 128
ML_W = ML_HEADS * ML_D
ML_CHUNK = 128
GATE_CAP = 15.0

AT_DH = 64
AT_HEADS = 8
AT_KV_HEADS = 2
AT_W = AT_HEADS * AT_DH
AT_BLOCK = 128
ROPE_BASE = 10000.0

RW_HEAD = 64
RW_PAIRS = D_MODEL // 128
RW_CHUNK = 64
RW_LN_EPS = 64e-5
RW_DECAY_LORA = 64
RW_AAA_LORA = 64
RW_GATE_LORA = 160
RW_GATE_PAD = 256
RW_DECAY_SCALE = 0.6065306597126334

LANES = 128
TOKEN_TILE = 256
FF_TILE = 1024
MOD_ROWS = 16
VMEM_LIMIT = 56 * 1024 * 1024
NEG = -1e30
RW_MM = BF16
RW_BATCH = 2
RW_ROW_TILE = 8

EV_MQKV = 0
EV_MO = 3 * ML_W
EV_AQ = EV_MO + ML_W
EV_AK = EV_AQ + AT_W
EV_AV = EV_AK + 256
EV_GATE = EV_AV + 256
EV_COLS = EV_GATE + LANES


def _dot(a, b):
    return jnp.dot(a, b, preferred_element_type=F32)


def _dot_tb(a, b):
    return lax.dot_general(a, b, (((1,), (1,)), ((), ())), preferred_element_type=F32)


def _dot_ta(a, b):
    return lax.dot_general(a, b, (((0,), (0,)), ((), ())), preferred_element_type=F32)


def _norm_mod(z, g, shift, scale):
    ms = jnp.mean(z * z, axis=-1, keepdims=True)
    return (z * lax.rsqrt(ms + EPS) * g) * (1.0 + scale) + shift


def _sigmoid(x):
    return 1.0 / (1.0 + jnp.exp(-x))


def _head_sum(x, e_ref, et_ref, terms):
    s = _dot(x.astype(BF16), e_ref[...])
    hi = s.astype(BF16)
    out = _dot(hi, et_ref[...])
    if terms > 1:
        out = out + _dot((s - hi.astype(F32)).astype(BF16), et_ref[...])
    return out


def _const_spec(shape):
    nd = len(shape)
    return pl.BlockSpec(shape, lambda *_: (0,) * nd, pipeline_mode=pl.Buffered(1))


def _params(sem):
    return pltpu.CompilerParams(dimension_semantics=sem, vmem_limit_bytes=VMEM_LIMIT)


def _ada_kernel(c_ref, w_ref, b_ref, o_ref):
    cv = c_ref[...]
    s = cv * _sigmoid(cv)
    o_ref[...] = jnp.dot(s, w_ref[...], preferred_element_type=F32,
                         precision=lax.Precision.HIGHEST) + b_ref[...]


def _ada_call(cstack, ada_w, ada_b):
    depth = ada_w.shape[0]
    n = N_MOD * D_MODEL
    tn = 1024
    return pl.pallas_call(
        _ada_kernel,
        grid=(depth, n // tn),
        in_specs=[pl.BlockSpec((MOD_ROWS, D_MODEL), lambda l, j: (0, 0)),
                  pl.BlockSpec((None, D_MODEL, tn), lambda l, j: (l, 0, j)),
                  pl.BlockSpec((None, 1, tn), lambda l, j: (l, 0, j))],
        out_specs=pl.BlockSpec((None, MOD_ROWS, tn), lambda l, j: (l, 0, j)),
        out_shape=jax.ShapeDtypeStruct((depth, MOD_ROWS, n), F32),
        compiler_params=_params(("parallel", "parallel")),
        name="ada_ln",
    )(cstack, ada_w, ada_b.reshape(depth, 1, n))


def _rope(x, cos, sin, lane_lo):
    n = x.shape[1]
    up = pltpu.roll(x, n - 16, 1)
    dn = pltpu.roll(x, 16, 1)
    reps = n // LANES
    c = jnp.concatenate([cos] * reps, axis=1) if reps > 1 else cos
    s = jnp.concatenate([sin] * reps, axis=1) if reps > 1 else sin
    lo = jnp.concatenate([lane_lo] * reps, axis=1) if reps > 1 else lane_lo
    return x * c + jnp.where(lo, up, dn) * s


def _even_in_kernel(z_ref, g_ref, mod_ref, w_ref, b_ref, gq_ref, gk_ref, grp_ref, cos_ref, sin_ref,
                    fb_ref, isf_ref, mqkv_ref, mo_ref, aq_ref, akv_ref, gate_ref):
    h = _norm_mod(z_ref[...], g_ref[...], mod_ref[0:1, :], mod_ref[1:2, :])
    p = _dot(h.astype(BF16), w_ref[...]) + b_ref[...]
    mqkv_ref[:, 0:ML_W] = p[:, 0:ML_W].astype(BF16)
    mqkv_ref[:, ML_W:2 * ML_W] = (p[:, ML_W:2 * ML_W] * (ML_D ** -0.5)).astype(BF16)
    mqkv_ref[:, 2 * ML_W:3 * ML_W] = p[:, 2 * ML_W:3 * ML_W].astype(BF16)
    mo_ref[...] = p[:, EV_MO:EV_MO + ML_W].astype(mo_ref.dtype)
    cos = cos_ref[...]
    sin = sin_ref[...]
    lane_lo = (lax.broadcasted_iota(jnp.int32, (1, LANES), 1) % 32) < 16
    q = p[:, EV_AQ:EV_AQ + AT_W]
    qms = _dot((q * q).astype(BF16), grp_ref[...])
    q = q * lax.rsqrt(qms + EPS) * gq_ref[...]
    aq_ref[...] = (_rope(q, cos, sin, lane_lo) * (AT_DH ** -0.5)).astype(BF16)
    k = p[:, EV_AK:EV_AK + 256]
    kms = _dot((k * k).astype(BF16), grp_ref[0:256, 0:256])
    k = k * lax.rsqrt(kms + EPS) * gk_ref[...]
    akv_ref[:, 0:256] = _rope(k, cos, sin, lane_lo).astype(BF16)
    akv_ref[:, 256:512] = p[:, EV_AV:EV_AV + 256].astype(BF16)
    gt = p[:, EV_GATE:EV_GATE + LANES] + fb_ref[...]
    sc = GATE_CAP * jnp.tanh(gt * (1.0 / GATE_CAP))
    logsig = jnp.minimum(sc, 0.0) - jnp.log(1.0 + jnp.exp(-jnp.abs(sc)))
    gate_ref[...] = jnp.where(isf_ref[...] > 0.5, logsig, sc)


def _even_in_call(z, g, mod, w, b, gq, gk, grp, cos, sin, fb, isf, nb):
    bsz, t, d = z.shape
    tm = TOKEN_TILE
    tok = lambda width: pl.BlockSpec((None, tm, width), lambda bi, i: (bi, i, 0))
    return pl.pallas_call(
        _even_in_kernel,
        grid=(bsz, t // tm),
        in_specs=[tok(d),
                  _const_spec((1, d)),
                  pl.BlockSpec((None, N_MOD, d), lambda bi, i: (jnp.where(i == 0, nb, bi), 0, 0)),
                  _const_spec((d, EV_COLS)),
                  _const_spec((1, EV_COLS)),
                  _const_spec((1, AT_W)),
                  _const_spec((1, 256)),
                  _const_spec((AT_W, AT_W)),
                  pl.BlockSpec((tm, LANES), lambda bi, i: (i, 0)),
                  pl.BlockSpec((tm, LANES), lambda bi, i: (i, 0)),
                  _const_spec((1, LANES)),
                  _const_spec((1, LANES))],
        out_specs=[tok(3 * ML_W), tok(ML_W), tok(AT_W), tok(512), tok(LANES)],
        out_shape=[jax.ShapeDtypeStruct((bsz, t, 3 * ML_W), BF16),
                   jax.ShapeDtypeStruct((bsz, t, ML_W), BF16),
                   jax.ShapeDtypeStruct((bsz, t, AT_W), BF16),
                   jax.ShapeDtypeStruct((bsz, t, 512), BF16),
                   jax.ShapeDtypeStruct((bsz, t, LANES), F32)],
        compiler_params=_params(("parallel", "parallel")),
        name="even_in",
    )(z, g, mod, w, b, gq, gk, grp, cos, sin, fb, isf)


def _split3(x):
    hi = x.astype(BF16)
    r1 = x - hi.astype(F32)
    mid = r1.astype(BF16)
    return hi, mid, (r1 - mid.astype(F32)).astype(BF16)


def _mlstm_kernel(qf_ref, ktf_ref, vf_ref, qb_ref, ktb_ref, vb_ref, gcf_ref, gcb_ref, grf_ref, grb_ref, sel_ref,
                  hf_ref, hb_ref, c_s, m_s):
    ln = ML_CHUNK

    @pl.when(pl.program_id(1) == 0)
    def _():
        c_s[...] = jnp.zeros_like(c_s)
        m_s[...] = jnp.zeros_like(m_s)

    ri = lax.broadcasted_iota(jnp.int32, (ln, ln), 0)
    ci = lax.broadcasted_iota(jnp.int32, (ln, ln), 1)
    hi = lax.Precision.HIGHEST
    before = (ci <= ri, ci >= ri)
    refs = ((qf_ref, ktf_ref, vf_ref, gcf_ref, grf_ref, hf_ref), (qb_ref, ktb_ref, vb_ref, gcb_ref, grb_ref, hb_ref))
    cum_r = [jnp.dot(refs[d][4][...], before[1 - d].astype(F32), preferred_element_type=F32, precision=hi)
             for d in range(2)]
    bc_all = []
    for d in range(2):
        cum_c = jnp.dot(before[d].astype(F32), refs[d][3][...], preferred_element_type=F32, precision=hi)
        sel = sel_ref[d]
        bc_all.append(sum(_dot(part, sel) for part in _split3(cum_c)))
    chains = [(d, hd) for d in range(2) for hd in range(ML_HEADS)]
    lanes = [slice(hd * ML_D, (hd + 1) * ML_D) for _, hd in chains]
    st = [d * ML_HEADS + hd for d, hd in chains]
    ones = jnp.ones((ln, ML_D), BF16)
    q = [refs[d][0][:, ls] for (d, _), ls in zip(chains, lanes)]
    kt = [refs[d][1][ls, :] for (d, _), ls in zip(chains, lanes)]
    v1 = [jnp.concatenate([refs[d][2][:, ls], ones], axis=1) for (d, _), ls in zip(chains, lanes)]
    i_row = [refs[d][4][8 * d + hd:8 * d + hd + 1, :] for d, hd in chains]
    f_row = [refs[d][4][8 * d + 4 + hd:8 * d + 5 + hd, :] for d, hd in chains]
    bcum_row = [cum_r[d][8 * d + 4 + hd:8 * d + 5 + hd, :] for d, hd in chains]
    bcum = [bc_all[d][:, ls] for (d, _), ls in zip(chains, lanes)]
    m_old = [m_s[s:s + 1, 0:1] for s in st]
    c_old = [c_s[s] for s in st]
    qk = [_dot(a, b) for a, b in zip(q, kt)]
    r2 = [_dot(a, c.astype(BF16)) for a, c in zip(q, c_old)]
    dlog = [jnp.where(before[d], bc - br + ir, NEG) for (d, _), bc, br, ir in zip(chains, bcum, bcum_row, i_row)]
    inter = [bc + m for bc, m in zip(bcum, m_old)]
    mt = [jnp.maximum(jnp.max(dl, axis=1, keepdims=True), it) for dl, it in zip(dlog, inter)]
    sc = [x * jnp.exp(dl - m) for x, dl, m in zip(qk, dlog, mt)]
    iw = [jnp.exp(it - m) for it, m in zip(inter, mt)]
    r1 = [_dot(s.astype(BF16), b) for s, b in zip(sc, v1)]
    for j, (d, _) in enumerate(chains):
        tot = r1[j] + jnp.concatenate([iw[j], iw[j]], axis=1) * r2[j]
        den = jnp.maximum(jnp.abs(tot[:, ML_D:]), jnp.exp(-mt[j]))
        refs[d][5][:, lanes[j]] = (tot[:, :ML_D] / den).astype(refs[d][5].dtype)
    bl = [jnp.sum(fr, axis=1, keepdims=True) for fr in f_row]
    ws_log = [b - br + ir for b, br, ir in zip(bl, bcum_row, i_row)]
    m_new = [jnp.maximum(b + m, jnp.max(w, axis=1, keepdims=True)) for b, m, w in zip(bl, m_old, ws_log)]
    ws = [jnp.exp(w - m) for w, m in zip(ws_log, m_new)]
    upd = [_dot((a.astype(F32) * w).astype(BF16), b) for a, w, b in zip(kt, ws, v1)]
    for j, s in enumerate(st):
        c_s[s] = jnp.exp(bl[j] + m_old[j] - m_new[j]) * c_old[j] + upd[j]
        m_s[s:s + 1, :] = jnp.broadcast_to(m_new[j], (1, LANES))


def _rev_chunk(i, nctx, n):
    return jnp.where(i < nctx, nctx - 1 - i, n + nctx - 1 - i)


def _mlstm_call(mqkv, kt, gates, gates_t, lc):
    bsz, t, _ = mqkv.shape
    ln = ML_CHUNK
    assert ln == ML_D == LANES
    n = t // ln
    nctx = lc // ln
    rv = lambda i: _rev_chunk(i, nctx, n)
    col = jnp.arange(LANES)[None, :, None]
    want = (8 * jnp.arange(2)[:, None, None] + 4 + jnp.arange(ML_HEADS * ln)[None, None, :] // ln)
    sel = (col == want).astype(BF16)
    return pl.pallas_call(
        _mlstm_kernel,
        grid=(bsz, n),
        in_specs=[pl.BlockSpec((None, ln, ML_W), lambda bi, i: (bi, i, 0)),
                  pl.BlockSpec((None, ML_W, ln), lambda bi, i: (bi, 0, i)),
                  pl.BlockSpec((None, ln, ML_W), lambda bi, i: (bi, i, 2)),
                  pl.BlockSpec((None, ln, ML_W), lambda bi, i: (bi, rv(i), 0)),
                  pl.BlockSpec((None, ML_W, ln), lambda bi, i: (bi, 0, rv(i))),
                  pl.BlockSpec((None, ln, ML_W), lambda bi, i: (bi, rv(i), 2)),
                  pl.BlockSpec((None, ln, LANES), lambda bi, i: (bi, i, 0)),
                  pl.BlockSpec((None, ln, LANES), lambda bi, i: (bi, rv(i), 0)),
                  pl.BlockSpec((None, 16, ln), lambda bi, i: (bi, 0, i)),
                  pl.BlockSpec((None, 16, ln), lambda bi, i: (bi, 0, rv(i))),
                  _const_spec((2, LANES, ML_HEADS * ln))],
        out_specs=[pl.BlockSpec((None, ln, ML_W), lambda bi, i: (bi, i, 0)),
                   pl.BlockSpec((None, ln, ML_W), lambda bi, i: (bi, rv(i), 0))],
        out_shape=[jax.ShapeDtypeStruct((bsz, t, ML_W), BF16)] * 2,
        scratch_shapes=[pltpu.VMEM((2 * ML_HEADS, ML_D, 2 * ML_D), F32),
                        pltpu.VMEM((2 * ML_HEADS, LANES), F32)],
        compiler_params=_params(("parallel", "arbitrary")),
        name="mlstm_scan",
    )(mqkv, kt, mqkv, mqkv, kt, mqkv, gates, gates, gates_t, gates_t, sel)


def _attn_kernel(q_ref, ktp_ref, kto_ref, ktn_ref, ktc_ref, vp_ref, vo_ref, vn_ref, vc_ref, sink_ref, o_ref,
                 *, nctx, n):
    j = pl.program_id(1)
    blk = AT_BLOCK
    latent = j >= nctx
    qi = lax.broadcasted_iota(jnp.int32, (blk, blk), 0)
    ki = lax.broadcasted_iota(jnp.int32, (blk, blk), 1)
    ok_prev = jnp.logical_and(jnp.logical_and(latent, j > nctx), ki >= qi)
    ok_own = jnp.logical_and(latent, ki >= 0)
    ok_next = jnp.logical_and(jnp.logical_and(latent, j < n - 1), ki <= qi)
    lane = lax.broadcasted_iota(jnp.int32, (1, LANES), 1)
    sub = lax.broadcasted_iota(jnp.int32, (LANES, 1), 0)
    lane_half = (lane < AT_DH, lane >= AT_DH)
    sub_half = (sub < AT_DH, sub >= AT_DH)
    kts = (ktp_ref, kto_ref, ktn_ref, ktc_ref)
    vs = (vp_ref, vo_ref, vn_ref, vc_ref)
    oks = (ok_prev, ok_own, ok_next, None)
    zero = jnp.zeros((), BF16)
    kx, vx = {}, {}
    for g in range(AT_KV_HEADS):
        rows = slice(g * LANES, (g + 1) * LANES)
        for e in range(2):
            kx[g, e] = [jnp.where(sub_half[e], r[rows, :], zero) for r in kts]
            vx[g, e] = [jnp.concatenate([jnp.where(lane_half[e], r[:, rows], zero),
                                         jnp.ones((r.shape[0], LANES), BF16)], axis=1) for r in vs]
    heads = [(hd // 2, hd % 2, hd // 4) for hd in range(AT_HEADS)]
    qc = [q_ref[:, c * LANES:(c + 1) * LANES] for c in range(AT_HEADS // 2)]
    sink = [sink_ref[hd:hd + 1, 0:1] for hd in range(AT_HEADS)]
    ss = [[_dot(qc[c], kx[g, e][p]) if oks[p] is None else jnp.where(oks[p], _dot(qc[c], kx[g, e][p]), NEG)
           for p in range(4)] for c, e, g in heads]
    m = []
    for hd in range(AT_HEADS):
        band = jnp.maximum(jnp.maximum(ss[hd][0], ss[hd][1]), ss[hd][2])
        m.append(jnp.maximum(jnp.maximum(jnp.max(band, axis=1, keepdims=True),
                                         jnp.max(ss[hd][3], axis=1, keepdims=True)), sink[hd]))
    res = []
    for hd, (c, e, g) in enumerate(heads):
        acc = None
        for p in range(4):
            pv = _dot(jnp.exp(ss[hd][p] - m[hd]).astype(BF16), vx[g, e][p])
            acc = pv if acc is None else acc + pv
        res.append(acc)
    outs = [r[:, :LANES] / (r[:, LANES:] + jnp.exp(s - mm)) for r, s, mm in zip(res, sink, m)]
    for c in range(AT_HEADS // 2):
        o_ref[:, c * LANES:(c + 1) * LANES] = (outs[2 * c] + outs[2 * c + 1]).astype(o_ref.dtype)


def _attn_call(aq, akt, akv, sink, lc):
    bsz, t, _ = aq.shape
    blk = AT_BLOCK
    n = t // blk
    nctx = lc // blk
    prev = lambda j: jnp.clip(j - 1, nctx, n - 1)
    own = lambda j: jnp.clip(j, nctx, n - 1)
    nxt = lambda j: jnp.clip(j + 1, nctx, n - 1)
    kt = lambda f: pl.BlockSpec((None, 256, blk), lambda bi, j: (bi, 0, f(j)))
    vv = lambda f: pl.BlockSpec((None, blk, 256), lambda bi, j: (bi, f(j), 1))
    return pl.pallas_call(
        functools.partial(_attn_kernel, nctx=nctx, n=n),
        grid=(bsz, n),
        in_specs=[pl.BlockSpec((None, blk, AT_W), lambda bi, j: (bi, j, 0)),
                  kt(prev), kt(own), kt(nxt),
                  pl.BlockSpec((None, 256, lc), lambda bi, j: (bi, 0, 0)),
                  vv(prev), vv(own), vv(nxt),
                  pl.BlockSpec((None, lc, 256), lambda bi, j: (bi, 0, 1)),
                  _const_spec((AT_HEADS, LANES))],
        out_specs=pl.BlockSpec((None, blk, AT_W), lambda bi, j: (bi, j, 0)),
        out_shape=jax.ShapeDtypeStruct((bsz, t, AT_W), BF16),
        compiler_params=_params(("parallel", "parallel")),
        name="window_attn",
    )(aq, akt, akt, akt, akt, akv, akv, akv, akv, sink)


def _mlp_tail(z, g, mod_ref, w1_ref, w2_ref):
    h = _norm_mod(z, g, mod_ref[3:4, :], mod_ref[4:5, :]).astype(BF16)
    acc = jnp.zeros(z.shape, F32)
    for f in range(D_FF // FF_TILE):
        a = jnp.maximum(_dot(h, w1_ref[:, f * FF_TILE:(f + 1) * FF_TILE]), 0.0)
        acc = acc + _dot((a * a).astype(BF16), w2_ref[f * FF_TILE:(f + 1) * FF_TILE, :])
    return z + mod_ref[5:6, :] * acc


def _even_out_kernel(z_ref, hf_ref, hb_ref, mo_ref, ya_ref, og_ref, w_ref, mod_ref, g2_ref, w1_ref, w2_ref, o_ref):
    hs = hf_ref[...].astype(F32) + hb_ref[...].astype(F32)
    parts = []
    for hd in range(ML_HEADS):
        x = hs[:, hd * ML_D:(hd + 1) * ML_D]
        parts.append(x * lax.rsqrt(jnp.mean(x * x, axis=1, keepdims=True) + EPS))
    ym = jnp.concatenate(parts, axis=1) * og_ref[...] * _sigmoid(mo_ref[...].astype(F32))
    y = _dot(ym.astype(BF16), w_ref[0:ML_W, :]) + _dot(ya_ref[...], w_ref[ML_W:ML_W + AT_W, :])
    o_ref[...] = _mlp_tail(z_ref[...] + mod_ref[2:3, :] * y, g2_ref[...], mod_ref, w1_ref, w2_ref)


def _even_out_call(z, hf, hb, mo, ya, og, w, mod, g2, w1, w2, nb, skip):
    bsz, t, d = z.shape
    tm = TOKEN_TILE
    nt = t // tm - skip
    tok = lambda width: pl.BlockSpec((None, tm, width), lambda bi, i: (bi, i + skip, 0))
    return pl.pallas_call(
        _even_out_kernel,
        grid=(bsz, nt),
        in_specs=[tok(d), tok(ML_W), tok(ML_W), tok(ML_W), tok(AT_W),
                  _const_spec((1, ML_W)),
                  _const_spec((ML_W + AT_W, d)),
                  pl.BlockSpec((None, N_MOD, d), lambda bi, i: (jnp.where(i + skip == 0, nb, bi), 0, 0)),
                  _const_spec((1, d)),
                  _const_spec((d, D_FF)),
                  _const_spec((D_FF, d))],
        out_specs=pl.BlockSpec((None, tm, d), lambda bi, i: (bi, i, 0)),
        out_shape=jax.ShapeDtypeStruct((bsz, nt * tm, d), F32),
        compiler_params=_params(("parallel", "parallel")),
        name="even_out_mlp",
    )(z, hf, hb, mo, ya, og, w, mod, g2, w1, w2)


def _rwkv_in_kernel(z_ref, zp_ref, zn_ref, g_ref, mod_ref, mu_ref, wrkv_ref, w1_ref, a1_ref, g1_ref,
                    w2_ref, a2_ref, g2_ref, w0_ref, a0_ref, kk_ref, ka_ref, rk_ref, e_ref, et_ref,
                    sh_ref, dr_ref, lw_ref, gate_ref, bonus_ref, *, nctx, ntile):
    i = pl.program_id(1)
    tm, d = z_ref.shape
    g = g_ref[...]
    shift = mod_ref[0:1, :]
    scale = mod_ref[1:2, :]
    h = _norm_mod(z_ref[...], g, shift, scale)
    no_prev = jnp.logical_or(i == 0, i == nctx)
    no_next = jnp.logical_or(i == nctx - 1, i == ntile - 1)
    hp = jnp.where(no_prev, 0.0, _norm_mod(zp_ref[7:8, :], g, shift, scale))
    hn = jnp.where(no_next, 0.0, _norm_mod(zn_ref[0:1, :], g, shift, scale))
    row = lax.broadcasted_iota(jnp.int32, (tm, 1), 0)
    dp = jnp.where(row == 0, hp, pltpu.roll(h, 1, 0)) - h
    dn = jnp.where(row == tm - 1, hn, pltpu.roll(h, tm - 1, 0)) - h

    hb, dpb, dnb = h.astype(BF16), dp.astype(BF16), dn.astype(BF16)
    mub = mu_ref[...].astype(BF16)

    def mix(n):
        return hb + mub[2 * n:2 * n + 1, :] * dpb + mub[2 * n + 1:2 * n + 2, :] * dnb

    r = _dot(mix(0), wrkv_ref[0])
    k = _dot(mix(2), wrkv_ref[1])
    v = _dot(mix(3), wrkv_ref[2])
    gate_ref[...] = _dot(_sigmoid(_dot(mix(5), g1_ref[...])).astype(BF16), g2_ref[...]).astype(gate_ref.dtype)
    lora_w = _dot(jnp.tanh(_dot(mix(1), w1_ref[...])).astype(BF16), w2_ref[...])
    lora_a = _dot(_dot(mix(4), a1_ref[...]).astype(BF16), a2_ref[...])
    kkr = k * kk_ref[...]
    ssq = _head_sum(kkr * kkr, e_ref, et_ref, 1)
    kk = kkr * lax.rsqrt(jnp.maximum(ssq, 1e-24))
    kd_sum = None
    for dr in range(2):
        cols = slice(dr * d, (dr + 1) * d)
        lw = -RW_DECAY_SCALE * _sigmoid(w0_ref[dr:dr + 1, :] + lora_w[:, cols])
        a = _sigmoid(a0_ref[dr:dr + 1, :] + lora_a[:, cols])
        kd = k * (1.0 + (a - 1.0) * ka_ref[...])
        bvec = kk * a
        kd_sum = kd if kd_sum is None else kd_sum + kd
        for p in range(RW_PAIRS):
            ls = slice(p * LANES, (p + 1) * LANES)
            lw_ref[dr, p] = lw[:, ls]
            dr_ref[dr, 0, p] = kd[:, ls].astype(BF16)
            dr_ref[dr, 1, p] = bvec[:, ls].astype(BF16)
    for p in range(RW_PAIRS):
        ls = slice(p * LANES, (p + 1) * LANES)
        sh_ref[0, p] = r[:, ls].astype(BF16)
        sh_ref[1, p] = v[:, ls].astype(BF16)
        sh_ref[2, p] = kk[:, ls].astype(BF16)
    bsum = _head_sum(r * kd_sum * rk_ref[...], e_ref, et_ref, 1)
    bonus_ref[...] = (bsum * v).astype(bonus_ref.dtype)


def _rwkv_in_call(z, g, mod, mu, wrkv, w1, a1, g1, w2, a2, g2, w0, a0, k_k, k_a, r_k, e, et, nb, lc):
    bsz, t, d = z.shape
    tm = TOKEN_TILE
    ntile = t // tm
    nctx = lc // tm
    r8 = tm // 8
    tok = pl.BlockSpec((None, tm, d), lambda bi, i: (bi, i, 0))
    return pl.pallas_call(
        functools.partial(_rwkv_in_kernel, nctx=nctx, ntile=ntile),
        grid=(bsz, ntile),
        in_specs=[tok,
                  pl.BlockSpec((None, 8, d), lambda bi, i: (bi, jnp.maximum(i * r8 - 1, 0), 0)),
                  pl.BlockSpec((None, 8, d), lambda bi, i: (bi, jnp.minimum((i + 1) * r8, t // 8 - 1), 0)),
                  _const_spec((1, d)),
                  pl.BlockSpec((None, N_MOD, d), lambda bi, i: (jnp.where(i < nctx, nb, bi), 0, 0)),
                  _const_spec((12, d)),
                  _const_spec((3, d, d)),
                  _const_spec((d, LANES)),
                  _const_spec((d, LANES)),
                  _const_spec((d, RW_GATE_PAD)),
                  _const_spec((LANES, 2 * d)),
                  _const_spec((LANES, 2 * d)),
                  _const_spec((RW_GATE_PAD, d)),
                  _const_spec((2, d)),
                  _const_spec((2, d)),
                  _const_spec((1, d)),
                  _const_spec((1, d)),
                  _const_spec((1, d)),
                  _const_spec((d, LANES)),
                  _const_spec((LANES, d))],
        out_specs=[pl.BlockSpec((None, 3, RW_PAIRS, tm, LANES), lambda bi, i: (bi, 0, 0, i, 0)),
                   pl.BlockSpec((None, 2, 2, RW_PAIRS, tm, LANES), lambda bi, i: (bi, 0, 0, 0, i, 0)),
                   pl.BlockSpec((None, 2, RW_PAIRS, tm, LANES), lambda bi, i: (bi, 0, 0, i, 0)),
                   tok, tok],
        out_shape=[jax.ShapeDtypeStruct((bsz, 3, RW_PAIRS, t, LANES), BF16),
                   jax.ShapeDtypeStruct((bsz, 2, 2, RW_PAIRS, t, LANES), BF16),
                   jax.ShapeDtypeStruct((bsz, 2, RW_PAIRS, t, LANES), F32),
                   jax.ShapeDtypeStruct((bsz, t, d), BF16),
                   jax.ShapeDtypeStruct((bsz, t, d), BF16)],
        compiler_params=_params(("parallel", "parallel")),
        name="rwkv_in",
    )(z, z, z, g, mod, mu, wrkv, w1, a1, g1, w2, a2, g2, w0, a0, k_k, k_a, r_k, e, et)


def _rwkv_chunk_group(chains):
    ln = RW_CHUNK
    row = lax.broadcasted_iota(jnp.int32, (ln, 2 * ln), 0)
    col = lax.broadcasted_iota(jnp.int32, (ln, 2 * ln), 1)
    sidx = jnp.where(col >= ln, col - ln, col)
    lane_a = lax.broadcasted_iota(jnp.int32, (1, LANES), 1) < RW_HEAD
    tr = lax.broadcasted_iota(jnp.int32, (ln, ln), 0)
    tc = lax.broadcasted_iota(jnp.int32, (ln, ln), 1)
    tri = {False: (tc <= tr).astype(F32), True: (tc >= tr).astype(F32)}
    strict = {False: sidx < row, True: sidx > row}
    incl = {False: sidx <= row, True: sidx >= row}
    eye_w = jnp.where(sidx == row, 1.0, 0.0)
    diff = row ^ sidx
    vr = lax.broadcasted_iota(jnp.int32, (LANES, LANES), 0) < RW_HEAD
    kc = lax.broadcasted_iota(jnp.int32, (LANES, LANES), 1) < RW_HEAD
    same_head = vr == kc

    def bd(x):
        zero = jnp.zeros((), x.dtype)
        return jnp.concatenate([jnp.where(lane_a, x, zero), jnp.where(lane_a, zero, x)], axis=0)

    revs = [c[7] for c in chains]
    vs = [c[1] for c in chains]
    s_olds = [c[6] for c in chains]
    gcum = [jnp.dot(tri[c[7]], c[5], preferred_element_type=F32, precision=lax.Precision.HIGHEST) for c in chains]
    gtot = [jnp.sum(c[5], axis=0, keepdims=True) for c in chains]

    def scaled(c, g, gt):
        r, v, kk, kd, bv, lw = (x.astype(F32) for x in c[:6])
        e_pos = jnp.exp(g)
        e_neg = jnp.exp(-g)
        e_end = jnp.exp(gt - g)
        ar = jnp.concatenate([(-kk * jnp.exp(g - lw)).astype(RW_MM), (r * e_pos).astype(RW_MM)], axis=0)
        bk_end = jnp.concatenate([(bv * e_end).astype(RW_MM), (kd * e_end).astype(RW_MM)], axis=0)
        return ar, (bv * e_neg).astype(RW_MM), (kd * e_neg).astype(RW_MM), bk_end

    sc = [scaled(c, g, gt) for c, g, gt in zip(chains, gcum, gtot)]
    ars_ = [x[0] for x in sc]
    x_bk = [_dot_tb(x[0], jnp.concatenate([bd(x[1]), bd(x[2])], axis=0)) for x in sc]
    x_b = [x[:, :2 * ln] for x in x_bk]
    x_k = [x[:, 2 * ln:] for x in x_bk]
    n_w =[jnp.where(strict[rv], x[:ln], 0.0) for x, rv in zip(x_b, revs)]
    m_rb = [jnp.where(incl[rv], x[ln:], 0.0).astype(RW_MM) for x, rv in zip(x_b, revs)]
    m_k = [jnp.concatenate([jnp.where(strict[rv], x[:ln], 0.0), jnp.where(incl[rv], x[ln:], 0.0)],
                           axis=0).astype(RW_MM) for x, rv in zip(x_k, revs)]
    x_w = [eye_w + jnp.where(diff == 1, n, 0.0) for n in n_w]

    def take_rows(x, h, odd):
        return jnp.concatenate([x[b * h:(b + 1) * h] for b in range(ln // h) if (b % 2 == 1) == odd], axis=0)

    def put_rows(base, upd, h, odd):
        parts, j = [], 0
        for b in range(ln // h):
            if (b % 2 == 1) == odd:
                blk = upd[j * h:(j + 1) * h]
                parts.append(blk if base is None else base[b * h:(b + 1) * h] + blk)
                j += 1
            else:
                parts.append(jnp.zeros((h, upd.shape[1]), upd.dtype) if base is None else base[b * h:(b + 1) * h])
        return jnp.concatenate(parts, axis=0)

    h = 2
    while h < ln:
        lvl = jnp.logical_and(diff >= h, diff < 2 * h)
        n_l = [jnp.where(lvl, n, 0.0) for n in n_w]
        if h < RW_ROW_TILE:
            tmp = [_dot(n.astype(RW_MM), bd(x.astype(RW_MM))) for n, x in zip(n_l, x_w)]
            x_w = [x + _dot(x.astype(RW_MM), bd(t.astype(RW_MM))) for x, t in zip(x_w, tmp)]
        else:
            odd = [not rv for rv in revs]
            tmp = [_dot(take_rows(n, h, o).astype(RW_MM), bd(x.astype(RW_MM))) for n, x, o in zip(n_l, x_w, odd)]
            tmp = [put_rows(None, t, h, o) for t, o in zip(tmp, odd)]
            cor = [_dot(take_rows(x, h, o).astype(RW_MM), bd(t.astype(RW_MM))) for x, t, o in zip(x_w, tmp, odd)]
            x_w = [put_rows(x, c, h, o) for x, c, o in zip(x_w, cor, odd)]
        h *= 2
    ars = [_dot_tb(a, s.astype(RW_MM)) for a, s in zip(ars_, s_olds)]
    mv = [_dot(m, bd(v)) for m, v in zip(m_k, vs)]
    u = [_dot(x.astype(RW_MM), bd((a[:ln] + m[:ln]).astype(RW_MM))) for x, a, m in zip(x_w, ars, mv)]
    y = [a[ln:] + m[ln:] + _dot(rb, bd(uu.astype(RW_MM))) for a, m, rb, uu in zip(ars, mv, m_rb, u)]
    upd = [_dot_ta(jnp.concatenate([uu.astype(RW_MM), v], axis=0), x[3]) for uu, v, x in zip(u, vs, sc)]
    s_new = [s * jnp.exp(gt) + jnp.where(same_head, up, 0.0) for s, gt, up in zip(s_olds, gtot, upd)]
    return list(zip(y, s_new))


def _rwkv_scan_kernel(shf_ref, shb_ref, drf_ref, drb_ref, lwf_ref, lwb_ref, yf_ref, yb_ref, s_s):
    @pl.when(pl.program_id(1) == 0)
    def _():
        s_s[...] = jnp.zeros_like(s_s)

    chains, where = [], []
    for bb in range(RW_BATCH):
        for p in range(RW_PAIRS):
            chains.append((shf_ref[bb, 0, p], shf_ref[bb, 1, p], shf_ref[bb, 2, p], drf_ref[bb, 0, p],
                           drf_ref[bb, 1, p], lwf_ref[bb, p], s_s[bb, 0, p], False))
            where.append((yf_ref, bb, 0, p))
            chains.append((shb_ref[bb, 0, p], shb_ref[bb, 1, p], shb_ref[bb, 2, p], drb_ref[bb, 0, p],
                           drb_ref[bb, 1, p], lwb_ref[bb, p], s_s[bb, 1, p], True))
            where.append((yb_ref, bb, 1, p))
    for (y, s_new), (y_ref, bb, d, p) in zip(_rwkv_chunk_group(chains), where):
        y_ref[bb, p] = y.astype(y_ref.dtype)
        s_s[bb, d, p] = s_new


def _rwkv_scan_call(sh, dr, lw, lc):
    bsz, _, _, t, _ = sh.shape
    ln = RW_CHUNK
    nb = RW_BATCH
    assert bsz % nb == 0
    n = t // ln
    nctx = lc // ln
    rv = lambda i: _rev_chunk(i, nctx, n)
    return pl.pallas_call(
        _rwkv_scan_kernel,
        grid=(bsz // nb, n),
        in_specs=[pl.BlockSpec((nb, 3, RW_PAIRS, ln, LANES), lambda bi, i: (bi, 0, 0, i, 0)),
                  pl.BlockSpec((nb, 3, RW_PAIRS, ln, LANES), lambda bi, i: (bi, 0, 0, rv(i), 0)),
                  pl.BlockSpec((nb, None, 2, RW_PAIRS, ln, LANES), lambda bi, i: (bi, 0, 0, 0, i, 0)),
                  pl.BlockSpec((nb, None, 2, RW_PAIRS, ln, LANES), lambda bi, i: (bi, 1, 0, 0, rv(i), 0)),
                  pl.BlockSpec((nb, None, RW_PAIRS, ln, LANES), lambda bi, i: (bi, 0, 0, i, 0)),
                  pl.BlockSpec((nb, None, RW_PAIRS, ln, LANES), lambda bi, i: (bi, 1, 0, rv(i), 0))],
        out_specs=[pl.BlockSpec((nb, RW_PAIRS, ln, LANES), lambda bi, i: (bi, 0, i, 0)),
                   pl.BlockSpec((nb, RW_PAIRS, ln, LANES), lambda bi, i: (bi, 0, rv(i), 0))],
        out_shape=[jax.ShapeDtypeStruct((bsz, RW_PAIRS, t, LANES), BF16)] * 2,
        scratch_shapes=[pltpu.VMEM((nb, 2, RW_PAIRS, LANES, LANES), F32)],
        compiler_params=_params(("parallel", "arbitrary")),
        name="rwkv_scan",
    )(sh, sh, dr, dr, lw, lw)


def _rwkv_out_kernel(z_ref, yf_ref, yb_ref, gate_ref, bonus_ref, lng_ref, lnb_ref, e_ref, et_ref, w_ref, mod_ref,
                     g2_ref, w1_ref, w2_ref, o_ref):
    y = jnp.concatenate([yf_ref[p].astype(F32) + yb_ref[p].astype(F32) for p in range(RW_PAIRS)], axis=1)
    yc = y - _head_sum(y, e_ref, et_ref, 2) * (1.0 / RW_HEAD)
    var = _head_sum(yc * yc, e_ref, et_ref, 1) * (1.0 / RW_HEAD)
    yn = yc * lax.rsqrt(var + RW_LN_EPS) * lng_ref[...] + lnb_ref[...] + bonus_ref[...]
    out = _dot((yn * gate_ref[...]).astype(BF16), w_ref[...])
    o_ref[...] = _mlp_tail(z_ref[...] + mod_ref[2:3, :] * out, g2_ref[...], mod_ref, w1_ref, w2_ref)


def _rwkv_out_call(z, yf, yb, gate, bonus, ln_g, ln_b, e, et, w, mod, g2, w1, w2, nb, lc, skip):
    bsz, t, d = z.shape
    tm = TOKEN_TILE
    nctx = lc // tm
    nt = t // tm - skip
    tok = pl.BlockSpec((None, tm, d), lambda bi, i: (bi, i + skip, 0))
    pair = pl.BlockSpec((None, RW_PAIRS, tm, LANES), lambda bi, i: (bi, 0, i + skip, 0))
    return pl.pallas_call(
        _rwkv_out_kernel,
        grid=(bsz, nt),
        in_specs=[tok, pair, pair, tok, tok,
                  _const_spec((1, d)), _const_spec((1, d)), _const_spec((d, LANES)), _const_spec((LANES, d)),
                  _const_spec((d, d)),
                  pl.BlockSpec((None, N_MOD, d), lambda bi, i: (jnp.where(i + skip < nctx, nb, bi), 0, 0)),
                  _const_spec((1, d)),
                  _const_spec((d, D_FF)),
                  _const_spec((D_FF, d))],
        out_specs=pl.BlockSpec((None, tm, d), lambda bi, i: (bi, i, 0)),
        out_shape=jax.ShapeDtypeStruct((bsz, nt * tm, d), F32),
        compiler_params=_params(("parallel", "parallel")),
        name="rwkv_out_mlp",
    )(z, yf, yb, gate, bonus, ln_g, ln_b, e, et, w, mod, g2, w1, w2)


def _group_mean_matrix(n, width):
    idx = jnp.arange(n) // width
    return ((idx[:, None] == idx[None, :]).astype(F32) / width).astype(BF16)


def _rope_tables(lc, s):
    quarter = AT_DH // 4
    inv = ROPE_BASE ** (-jnp.arange(quarter, dtype=F32) / quarter)
    pos = jnp.arange(s)
    rpos = (pos // GRID_W).astype(F32)
    cpos = (pos % GRID_W).astype(F32)
    ang_r = rpos[:, None] * inv[None, :]
    ang_c = cpos[:, None] * inv[None, :]
    cos64 = jnp.concatenate([jnp.cos(ang_r), jnp.cos(ang_r), jnp.cos(ang_c), jnp.cos(ang_c)], axis=1)
    sin64 = jnp.concatenate([-jnp.sin(ang_r), jnp.sin(ang_r), -jnp.sin(ang_c), jnp.sin(ang_c)], axis=1)
    cos = jnp.concatenate([jnp.ones((lc, AT_DH), F32), cos64], axis=0)
    sin = jnp.concatenate([jnp.zeros((lc, AT_DH), F32), sin64], axis=0)
    return jnp.tile(cos, (1, 2)), jnp.tile(sin, (1, 2))


def _even_weights(w_in, b_in):
    def cols(m):
        mq, mk, mv, mo, mg, aq, ak, av = jnp.split(m, [512, 1024, 1536, 2048, 2064, 2576, 2704], axis=-1)
        dup = lambda u: jnp.concatenate([u[..., :64], u[..., :64], u[..., 64:], u[..., 64:]], axis=-1)
        mgp = jnp.pad(mg, [(0, 0)] * (m.ndim - 1) + [(0, LANES - 16)])
        return jnp.concatenate([mq, mk, mv, mo, aq, dup(ak), dup(av), mgp], axis=-1)
    return cols(w_in).astype(BF16), cols(b_in[None, :])


def kernel(x, c, ctx, c_ctx, ada_w, ada_b, norm1_g, norm2_g, mlp_w1, mlp_w2, ev_w_in, ev_b_in, ev_w_out, ml_f_bias, ml_out_g, at_q_g, at_k_g, at_sink, rw_mu, rw_w_rkv, rw_w0, rw_w1, rw_w2, rw_a0, rw_a1, rw_a2, rw_g1, rw_g2, rw_k_k, rw_k_a, rw_r_k, rw_ln_g, rw_ln_b, rw_w_out):
    bsz, s, d = x.shape
    lc = ctx.shape[1]
    depth = ada_w.shape[0]
    assert d == D_MODEL and lc == TOKEN_TILE and s % TOKEN_TILE == 0 and bsz < MOD_ROWS
    z = jnp.concatenate([ctx, x], axis=1)

    cstack = jnp.zeros((MOD_ROWS, d), F32).at[:bsz].set(c).at[bsz].set(c_ctx)
    mod_all = _ada_call(cstack, ada_w, ada_b).reshape(depth, MOD_ROWS, N_MOD, d)

    grp512 = _group_mean_matrix(AT_W, AT_DH)
    head_e = (jnp.arange(d)[:, None] // RW_HEAD == jnp.arange(LANES)[None, :]).astype(BF16)
    head_et = head_e.T
    cos_t, sin_t = _rope_tables(lc, s)
    gate_is_f = ((jnp.arange(LANES) % 8 >= 4) & (jnp.arange(LANES) < 16)).astype(F32)[None, :]

    for layer in range(depth):
        mod = mod_all[layer]
        j = layer // 2
        n1 = norm1_g[layer][None, :]
        mlp_args = (norm2_g[layer][None, :], mlp_w1[layer].astype(BF16), mlp_w2[layer].astype(BF16))
        skip = lc // TOKEN_TILE if layer == depth - 1 else 0
        if layer % 2 == 0:
            w_in, b_in = _even_weights(ev_w_in[j], ev_b_in[j])
            fb = jnp.zeros((LANES,), F32)
            for dr in range(2):
                fb = fb.at[8 * dr + 4:8 * dr + 8].set(ml_f_bias[j, dr])
            mqkv, mo, aq, akv, gates = _even_in_call(
                z, n1, mod, w_in, b_in, jnp.tile(at_q_g[j], AT_HEADS)[None, :], jnp.tile(at_k_g[j], 4)[None, :],
                grp512, cos_t, sin_t, fb[None, :], gate_is_f, bsz)
            gates_t = jnp.transpose(gates[:, :, :16], (0, 2, 1))
            kt = jnp.transpose(mqkv[:, :, ML_W:2 * ML_W], (0, 2, 1))
            hf, hb = _mlstm_call(mqkv, kt, gates, gates_t, lc)
            sink = jnp.broadcast_to(at_sink[j][:, None], (AT_HEADS, LANES))
            akt = jnp.transpose(akv[:, :, :256], (0, 2, 1))
            ya = _attn_call(aq, akt, akv, sink, lc)
            z = _even_out_call(z, hf, hb, mo, ya, ml_out_g[j][None, :], ev_w_out[j].astype(BF16), mod, *mlp_args,
                               bsz, skip)
        else:
            pad_g = RW_GATE_PAD - RW_GATE_LORA
            zeros = jnp.zeros((RW_DECAY_LORA, d), F32)
            w2 = jnp.concatenate([jnp.concatenate([rw_w2[j, 0], zeros], axis=1),
                                  jnp.concatenate([zeros, rw_w2[j, 1]], axis=1)], axis=0)
            a2 = jnp.concatenate([jnp.concatenate([rw_a2[j, 0], zeros], axis=1),
                                  jnp.concatenate([zeros, rw_a2[j, 1]], axis=1)], axis=0)
            sh, dr_, lw, gate, bonus = _rwkv_in_call(
                z, n1, mod, rw_mu[j].reshape(12, d), rw_w_rkv[j].astype(BF16),
                jnp.concatenate([rw_w1[j, 0], rw_w1[j, 1]], axis=1).astype(BF16),
                jnp.concatenate([rw_a1[j, 0], rw_a1[j, 1]], axis=1).astype(BF16),
                jnp.pad(rw_g1[j], ((0, 0), (0, pad_g))).astype(BF16),
                w2.astype(BF16), a2.astype(BF16),
                jnp.pad(rw_g2[j], ((0, pad_g), (0, 0))).astype(BF16),
                rw_w0[j], rw_a0[j], rw_k_k[j][None, :], rw_k_a[j][None, :], rw_r_k[j].reshape(1, d),
                head_e, head_et, bsz, lc)
            yf, yb = _rwkv_scan_call(sh, dr_, lw, lc)
            z = _rwkv_out_call(z, yf, yb, gate, bonus, rw_ln_g[j][None, :], rw_ln_b[j][None, :], head_e, head_et,
                               rw_w_out[j].astype(BF16), mod, *mlp_args, bsz, lc, skip)
    return z
```

```python
import functools

import jax
import jax.numpy as jnp
from jax import lax
from jax.experimental import pallas as pl
from jax.experimental.pallas import tpu as pltpu

F32 = jnp.float32
BF16 = jnp.bfloat16

D_MODEL = 1024
N_MOD = 6
D_FF = 4 * D_MODEL
EPS = 1e-6
GRID_W = 64

ML_HEADS = 4
ML_D = 128
ML_W = ML_HEADS * ML_D
ML_CHUNK = 128
GATE_CAP = 15.0

AT_DH = 64
AT_HEADS = 8
AT_KV_HEADS = 2
AT_W = AT_HEADS * AT_DH
AT_BLOCK = 128
ROPE_BASE = 10000.0

RW_HEAD = 64
RW_PAIRS = D_MODEL // 128
RW_CHUNK = 64
RW_LN_EPS = 64e-5
RW_DECAY_LORA = 64
RW_AAA_LORA = 64
RW_GATE_LORA = 160
RW_GATE_PAD = 256
RW_DECAY_SCALE = 0.6065306597126334

LANES = 128
TOKEN_TILE = 256
FF_TILE = 1024
MOD_ROWS = 16
VMEM_LIMIT = 56 * 1024 * 1024
NEG = -1e30
RW_MM = BF16
RW_BATCH = 2
RW_ROW_TILE = 8

EV_MQKV = 0
EV_MO = 3 * ML_W
EV_AQ = EV_MO + ML_W
EV_AK = EV_AQ + AT_W
EV_AV = EV_AK + 256
EV_GATE = EV_AV + 256
EV_COLS = EV_GATE + LANES


def _dot(a, b):
    return jnp.dot(a, b, preferred_element_type=F32)


def _dot_tb(a, b):
    return lax.dot_general(a, b, (((1,), (1,)), ((), ())), preferred_element_type=F32)


def _dot_ta(a, b):
    return lax.dot_general(a, b, (((0,), (0,)), ((), ())), preferred_element_type=F32)


def _norm_mod(z, g, shift, scale):
    ms = jnp.mean(z * z, axis=-1, keepdims=True)
    return (z * lax.rsqrt(ms + EPS) * g) * (1.0 + scale) + shift


def _sigmoid(x):
    return 1.0 / (1.0 + jnp.exp(-x))


def _head_sum(x, e_ref, et_ref, terms):
    s = _dot(x.astype(BF16), e_ref[...])
    hi = s.astype(BF16)
    out = _dot(hi, et_ref[...])
    if terms > 1:
        out = out + _dot((s - hi.astype(F32)).astype(BF16), et_ref[...])
    return out


def _const_spec(shape):
    nd = len(shape)
    return pl.BlockSpec(shape, lambda *_: (0,) * nd, pipeline_mode=pl.Buffered(1))


def _params(sem):
    return pltpu.CompilerParams(dimension_semantics=sem, vmem_limit_bytes=VMEM_LIMIT)


def _ada_kernel(c_ref, w_ref, b_ref, o_ref):
    cv = c_ref[...]
    s = cv * _sigmoid(cv)
    o_ref[...] = jnp.dot(s, w_ref[...], preferred_element_type=F32,
                         precision=lax.Precision.HIGHEST) + b_ref[...]


def _ada_call(cstack, ada_w, ada_b):
    depth = ada_w.shape[0]
    n = N_MOD * D_MODEL
    tn = 1024
    return pl.pallas_call(
        _ada_kernel,
        grid=(depth, n // tn),
        in_specs=[pl.BlockSpec((MOD_ROWS, D_MODEL), lambda l, j: (0, 0)),
                  pl.BlockSpec((None, D_MODEL, tn), lambda l, j: (l, 0, j)),
                  pl.BlockSpec((None, 1, tn), lambda l, j: (l, 0, j))],
        out_specs=pl.BlockSpec((None, MOD_ROWS, tn), lambda l, j: (l, 0, j)),
        out_shape=jax.ShapeDtypeStruct((depth, MOD_ROWS, n), F32),
        compiler_params=_params(("parallel", "parallel")),
        name="ada_ln",
    )(cstack, ada_w, ada_b.reshape(depth, 1, n))


def _rope(x, cos, sin, lane_lo):
    n = x.shape[1]
    up = pltpu.roll(x, n - 16, 1)
    dn = pltpu.roll(x, 16, 1)
    reps = n // LANES
    c = jnp.concatenate([cos] * reps, axis=1) if reps > 1 else cos
    s = jnp.concatenate([sin] * reps, axis=1) if reps > 1 else sin
    lo = jnp.concatenate([lane_lo] * reps, axis=1) if reps > 1 else lane_lo
    return x * c + jnp.where(lo, up, dn) * s


def _even_in_kernel(z_ref, g_ref, mod_ref, w_ref, b_ref, gq_ref, gk_ref, grp_ref, cos_ref, sin_ref,
                    fb_ref, isf_ref, mqv_ref, kt_ref, mo_ref, aq_ref, akt_ref, av_ref, gate_ref, gatet_ref):
    h = _norm_mod(z_ref[...], g_ref[...], mod_ref[0:1, :], mod_ref[1:2, :])
    p = _dot(h.astype(BF16), w_ref[...]) + b_ref[...]
    mqv_ref[:, 0:ML_W] = p[:, 0:ML_W].astype(BF16)
    mqv_ref[:, ML_W:2 * ML_W] = p[:, 2 * ML_W:3 * ML_W].astype(BF16)
    kt_ref[...] = (p[:, ML_W:2 * ML_W] * (ML_D ** -0.5)).T.astype(BF16)
    mo_ref[...] = p[:, EV_MO:EV_MO + ML_W].astype(mo_ref.dtype)
    cos = cos_ref[...]
    sin = sin_ref[...]
    lane_lo = (lax.broadcasted_iota(jnp.int32, (1, LANES), 1) % 32) < 16
    q = p[:, EV_AQ:EV_AQ + AT_W]
    qms = _dot((q * q).astype(BF16), grp_ref[...])
    q = q * lax.rsqrt(qms + EPS) * gq_ref[...]
    aq_ref[...] = (_rope(q, cos, sin, lane_lo) * (AT_DH ** -0.5)).astype(BF16)
    k = p[:, EV_AK:EV_AK + 256]
    kms = _dot((k * k).astype(BF16), grp_ref[0:256, 0:256])
    k = k * lax.rsqrt(kms + EPS) * gk_ref[...]
    akt_ref[...] = _rope(k, cos, sin, lane_lo).T.astype(BF16)
    av_ref[...] = p[:, EV_AV:EV_AV + 256].astype(BF16)
    gt = p[:, EV_GATE:EV_GATE + LANES] + fb_ref[...]
    sc = GATE_CAP * jnp.tanh(gt * (1.0 / GATE_CAP))
    logsig = jnp.minimum(sc, 0.0) - jnp.log(1.0 + jnp.exp(-jnp.abs(sc)))
    gates = jnp.where(isf_ref[...] > 0.5, logsig, sc)
    gate_ref[...] = gates
    gatet_ref[...] = gates.T[0:16, :]


def _even_in_call(z, g, mod, w, b, gq, gk, grp, cos, sin, fb, isf, nb):
    bsz, t, d = z.shape
    tm = TOKEN_TILE
    tok = lambda width: pl.BlockSpec((None, tm, width), lambda bi, i: (bi, i, 0))
    tpose = lambda rows: pl.BlockSpec((None, rows, tm), lambda bi, i: (bi, 0, i))
    return pl.pallas_call(
        _even_in_kernel,
        grid=(bsz, t // tm),
        in_specs=[tok(d),
                  _const_spec((1, d)),
                  pl.BlockSpec((None, N_MOD, d), lambda bi, i: (jnp.where(i == 0, nb, bi), 0, 0)),
                  _const_spec((d, EV_COLS)),
                  _const_spec((1, EV_COLS)),
                  _const_spec((1, AT_W)),
                  _const_spec((1, 256)),
                  _const_spec((AT_W, AT_W)),
                  pl.BlockSpec((tm, LANES), lambda bi, i: (i, 0)),
                  pl.BlockSpec((tm, LANES), lambda bi, i: (i, 0)),
                  _const_spec((1, LANES)),
                  _const_spec((1, LANES))],
        out_specs=[tok(2 * ML_W), tpose(ML_W), tok(ML_W), tok(AT_W), tpose(256), tok(256), tok(LANES), tpose(16)],
        out_shape=[jax.ShapeDtypeStruct((bsz, t, 2 * ML_W), BF16),
                   jax.ShapeDtypeStruct((bsz, ML_W, t), BF16),
                   jax.ShapeDtypeStruct((bsz, t, ML_W), BF16),
                   jax.ShapeDtypeStruct((bsz, t, AT_W), BF16),
                   jax.ShapeDtypeStruct((bsz, 256, t), BF16),
                   jax.ShapeDtypeStruct((bsz, t, 256), BF16),
                   jax.ShapeDtypeStruct((bsz, t, LANES), F32),
                   jax.ShapeDtypeStruct((bsz, 16, t), F32)],
        compiler_params=_params(("parallel", "parallel")),
        name="even_in",
    )(z, g, mod, w, b, gq, gk, grp, cos, sin, fb, isf)


def _split3(x):
    hi = x.astype(BF16)
    r1 = x - hi.astype(F32)
    mid = r1.astype(BF16)
    return hi, mid, (r1 - mid.astype(F32)).astype(BF16)


def _mlstm_kernel(qf_ref, ktf_ref, vf_ref, qb_ref, ktb_ref, vb_ref, gcf_ref, gcb_ref, grf_ref, grb_ref, sel_ref,
                  hf_ref, hb_ref, c_s, m_s):
    ln = ML_CHUNK

    @pl.when(pl.program_id(1) == 0)
    def _():
        c_s[...] = jnp.zeros_like(c_s)
        m_s[...] = jnp.zeros_like(m_s)

    ri = lax.broadcasted_iota(jnp.int32, (ln, ln), 0)
    ci = lax.broadcasted_iota(jnp.int32, (ln, ln), 1)
    hi = lax.Precision.HIGHEST
    before = (ci <= ri, ci >= ri)
    refs = ((qf_ref, ktf_ref, vf_ref, gcf_ref, grf_ref, hf_ref), (qb_ref, ktb_ref, vb_ref, gcb_ref, grb_ref, hb_ref))
    cum_r = [jnp.dot(refs[d][4][...], before[1 - d].astype(F32), preferred_element_type=F32, precision=hi)
             for d in range(2)]
    bc_all = []
    for d in range(2):
        cum_c = jnp.dot(before[d].astype(F32), refs[d][3][...], preferred_element_type=F32, precision=hi)
        sel = sel_ref[d]
        bc_all.append(sum(_dot(part, sel) for part in _split3(cum_c)))
    chains = [(d, hd) for d in range(2) for hd in range(ML_HEADS)]
    lanes = [slice(hd * ML_D, (hd + 1) * ML_D) for _, hd in chains]
    st = [d * ML_HEADS + hd for d, hd in chains]
    ones = jnp.ones((ln, ML_D), BF16)
    q = [refs[d][0][:, ls] for (d, _), ls in zip(chains, lanes)]
    kt = [refs[d][1][ls, :] for (d, _), ls in zip(chains, lanes)]
    v1 = [jnp.concatenate([refs[d][2][:, ls], ones], axis=1) for (d, _), ls in zip(chains, lanes)]
    i_row = [refs[d][4][8 * d + hd:8 * d + hd + 1, :] for d, hd in chains]
    f_row = [refs[d][4][8 * d + 4 + hd:8 * d + 5 + hd, :] for d, hd in chains]
    bcum_row = [cum_r[d][8 * d + 4 + hd:8 * d + 5 + hd, :] for d, hd in chains]
    bcum = [bc_all[d][:, ls] for (d, _), ls in zip(chains, lanes)]
    m_old = [m_s[s:s + 1, 0:1] for s in st]
    c_old = [c_s[s] for s in st]
    qk = [_dot(a, b) for a, b in zip(q, kt)]
    r2 = [_dot(a, c.astype(BF16)) for a, c in zip(q, c_old)]
    dlog = [jnp.where(before[d], bc - br + ir, NEG) for (d, _), bc, br, ir in zip(chains, bcum, bcum_row, i_row)]
    inter = [bc + m for bc, m in zip(bcum, m_old)]
    mt = [jnp.maximum(jnp.max(dl, axis=1, keepdims=True), it) for dl, it in zip(dlog, inter)]
    sc = [x * jnp.exp(dl - m) for x, dl, m in zip(qk, dlog, mt)]
    iw = [jnp.exp(it - m) for it, m in zip(inter, mt)]
    r1 = [_dot(s.astype(BF16), b) for s, b in zip(sc, v1)]
    for j, (d, _) in enumerate(chains):
        tot = r1[j] + jnp.concatenate([iw[j], iw[j]], axis=1) * r2[j]
        den = jnp.maximum(jnp.abs(tot[:, ML_D:]), jnp.exp(-mt[j]))
        refs[d][5][:, lanes[j]] = (tot[:, :ML_D] / den).astype(refs[d][5].dtype)
    bl = [jnp.sum(fr, axis=1, keepdims=True) for fr in f_row]
    ws_log = [b - br + ir for b, br, ir in zip(bl, bcum_row, i_row)]
    m_new = [jnp.maximum(b + m, jnp.max(w, axis=1, keepdims=True)) for b, m, w in zip(bl, m_old, ws_log)]
    ws = [jnp.exp(w - m) for w, m in zip(ws_log, m_new)]
    upd = [_dot((a.astype(F32) * w).astype(BF16), b) for a, w, b in zip(kt, ws, v1)]
    for j, s in enumerate(st):
        c_s[s] = jnp.exp(bl[j] + m_old[j] - m_new[j]) * c_old[j] + upd[j]
        m_s[s:s + 1, :] = jnp.broadcast_to(m_new[j], (1, LANES))


def _rev_chunk(i, nctx, n):
    return jnp.where(i < nctx, nctx - 1 - i, n + nctx - 1 - i)


def _mlstm_call(mqv, kt, gates, gates_t, lc):
    bsz, t, _ = mqv.shape
    ln = ML_CHUNK
    assert ln == ML_D == LANES
    n = t // ln
    nctx = lc // ln
    rv = lambda i: _rev_chunk(i, nctx, n)
    col = jnp.arange(LANES)[None, :, None]
    want = (8 * jnp.arange(2)[:, None, None] + 4 + jnp.arange(ML_HEADS * ln)[None, None, :] // ln)
    sel = (col == want).astype(BF16)
    return pl.pallas_call(
        _mlstm_kernel,
        grid=(bsz, n),
        in_specs=[pl.BlockSpec((None, ln, ML_W), lambda bi, i: (bi, i, 0)),
                  pl.BlockSpec((None, ML_W, ln), lambda bi, i: (bi, 0, i)),
                  pl.BlockSpec((None, ln, ML_W), lambda bi, i: (bi, i, 1)),
                  pl.BlockSpec((None, ln, ML_W), lambda bi, i: (bi, rv(i), 0)),
                  pl.BlockSpec((None, ML_W, ln), lambda bi, i: (bi, 0, rv(i))),
                  pl.BlockSpec((None, ln, ML_W), lambda bi, i: (bi, rv(i), 1)),
                  pl.BlockSpec((None, ln, LANES), lambda bi, i: (bi, i, 0)),
                  pl.BlockSpec((None, ln, LANES), lambda bi, i: (bi, rv(i), 0)),
                  pl.BlockSpec((None, 16, ln), lambda bi, i: (bi, 0, i)),
                  pl.BlockSpec((None, 16, ln), lambda bi, i: (bi, 0, rv(i))),
                  _const_spec((2, LANES, ML_HEADS * ln))],
        out_specs=[pl.BlockSpec((None, ln, ML_W), lambda bi, i: (bi, i, 0)),
                   pl.BlockSpec((None, ln, ML_W), lambda bi, i: (bi, rv(i), 0))],
        out_shape=[jax.ShapeDtypeStruct((bsz, t, ML_W), BF16)] * 2,
        scratch_shapes=[pltpu.VMEM((2 * ML_HEADS, ML_D, 2 * ML_D), F32),
                        pltpu.VMEM((2 * ML_HEADS, LANES), F32)],
        compiler_params=_params(("parallel", "arbitrary")),
        name="mlstm_scan",
    )(mqv, kt, mqv, mqv, kt, mqv, gates, gates, gates_t, gates_t, sel)


def _attn_kernel(q_ref, ktp_ref, kto_ref, ktn_ref, ktc_ref, vp_ref, vo_ref, vn_ref, vc_ref, sink_ref, o_ref,
                 *, nctx, n):
    j = pl.program_id(1)
    blk = AT_BLOCK
    latent = j >= nctx
    qi = lax.broadcasted_iota(jnp.int32, (blk, blk), 0)
    ki = lax.broadcasted_iota(jnp.int32, (blk, blk), 1)
    ok_prev = jnp.logical_and(jnp.logical_and(latent, j > nctx), ki >= qi)
    ok_own = jnp.logical_and(latent, ki >= 0)
    ok_next = jnp.logical_and(jnp.logical_and(latent, j < n - 1), ki <= qi)
    lane = lax.broadcasted_iota(jnp.int32, (1, LANES), 1)
    sub = lax.broadcasted_iota(jnp.int32, (LANES, 1), 0)
    lane_half = (lane < AT_DH, lane >= AT_DH)
    sub_half = (sub < AT_DH, sub >= AT_DH)
    kts = (ktp_ref, kto_ref, ktn_ref, ktc_ref)
    vs = (vp_ref, vo_ref, vn_ref, vc_ref)
    oks = (ok_prev, ok_own, ok_next, None)
    zero = jnp.zeros((), BF16)
    kx, vx = {}, {}
    for g in range(AT_KV_HEADS):
        rows = slice(g * LANES, (g + 1) * LANES)
        for e in range(2):
            kx[g, e] = [jnp.where(sub_half[e], r[rows, :], zero) for r in kts]
            vx[g, e] = [jnp.concatenate([jnp.where(lane_half[e], r[:, rows], zero),
                                         jnp.ones((r.shape[0], LANES), BF16)], axis=1) for r in vs]
    heads = [(hd // 2, hd % 2, hd // 4) for hd in range(AT_HEADS)]
    qc = [q_ref[:, c * LANES:(c + 1) * LANES] for c in range(AT_HEADS // 2)]
    sink = [sink_ref[hd:hd + 1, 0:1] for hd in range(AT_HEADS)]
    ss = [[_dot(qc[c], kx[g, e][p]) if oks[p] is None else jnp.where(oks[p], _dot(qc[c], kx[g, e][p]), NEG)
           for p in range(4)] for c, e, g in heads]
    m = []
    for hd in range(AT_HEADS):
        band = jnp.maximum(jnp.maximum(ss[hd][0], ss[hd][1]), ss[hd][2])
        m.append(jnp.maximum(jnp.maximum(jnp.max(band, axis=1, keepdims=True),
                                         jnp.max(ss[hd][3], axis=1, keepdims=True)), sink[hd]))
    res = []
    for hd, (c, e, g) in enumerate(heads):
        acc = None
        for p in range(4):
            pv = _dot(jnp.exp(ss[hd][p] - m[hd]).astype(BF16), vx[g, e][p])
            acc = pv if acc is None else acc + pv
        res.append(acc)
    outs = [r[:, :LANES] / (r[:, LANES:] + jnp.exp(s - mm)) for r, s, mm in zip(res, sink, m)]
    for c in range(AT_HEADS // 2):
        o_ref[:, c * LANES:(c + 1) * LANES] = (outs[2 * c] + outs[2 * c + 1]).astype(o_ref.dtype)


def _attn_call(aq, akt, av, sink, lc):
    bsz, t, _ = aq.shape
    blk = AT_BLOCK
    n = t // blk
    nctx = lc // blk
    prev = lambda j: jnp.clip(j - 1, nctx, n - 1)
    own = lambda j: jnp.clip(j, nctx, n - 1)
    nxt = lambda j: jnp.clip(j + 1, nctx, n - 1)
    kt = lambda f: pl.BlockSpec((None, 256, blk), lambda bi, j: (bi, 0, f(j)))
    vv = lambda f: pl.BlockSpec((None, blk, 256), lambda bi, j: (bi, f(j), 0))
    return pl.pallas_call(
        functools.partial(_attn_kernel, nctx=nctx, n=n),
        grid=(bsz, n),
        in_specs=[pl.BlockSpec((None, blk, AT_W), lambda bi, j: (bi, j, 0)),
                  kt(prev), kt(own), kt(nxt),
                  pl.BlockSpec((None, 256, lc), lambda bi, j: (bi, 0, 0)),
                  vv(prev), vv(own), vv(nxt),
                  pl.BlockSpec((None, lc, 256), lambda bi, j: (bi, 0, 0)),
                  _const_spec((AT_HEADS, LANES))],
        out_specs=pl.BlockSpec((None, blk, AT_W), lambda bi, j: (bi, j, 0)),
        out_shape=jax.ShapeDtypeStruct((bsz, t, AT_W), BF16),
        compiler_params=_params(("parallel", "parallel")),
        name="window_attn",
    )(aq, akt, akt, akt, akt, av, av, av, av, sink)


def _mlp_tail(z, g, mod_ref, w1_ref, w2_ref):
    h = _norm_mod(z, g, mod_ref[3:4, :], mod_ref[4:5, :]).astype(BF16)
    acc = jnp.zeros(z.shape, F32)
    for f in range(D_FF // FF_TILE):
        a = jnp.maximum(_dot(h, w1_ref[:, f * FF_TILE:(f + 1) * FF_TILE]), 0.0)
        acc = acc + _dot((a * a).astype(BF16), w2_ref[f * FF_TILE:(f + 1) * FF_TILE, :])
    return z + mod_ref[5:6, :] * acc


def _even_out_kernel(z_ref, hf_ref, hb_ref, mo_ref, ya_ref, og_ref, w_ref, mod_ref, g2_ref, w1_ref, w2_ref, o_ref):
    hs = hf_ref[...].astype(F32) + hb_ref[...].astype(F32)
    parts = []
    for hd in range(ML_HEADS):
        x = hs[:, hd * ML_D:(hd + 1) * ML_D]
        parts.append(x * lax.rsqrt(jnp.mean(x * x, axis=1, keepdims=True) + EPS))
    ym = jnp.concatenate(parts, axis=1) * og_ref[...] * _sigmoid(mo_ref[...].astype(F32))
    y = _dot(ym.astype(BF16), w_ref[0:ML_W, :]) + _dot(ya_ref[...], w_ref[ML_W:ML_W + AT_W, :])
    o_ref[...] = _mlp_tail(z_ref[...] + mod_ref[2:3, :] * y, g2_ref[...], mod_ref, w1_ref, w2_ref)


def _even_out_call(z, hf, hb, mo, ya, og, w, mod, g2, w1, w2, nb, skip):
    bsz, t, d = z.shape
    tm = TOKEN_TILE
    nt = t // tm - skip
    tok = lambda width: pl.BlockSpec((None, tm, width), lambda bi, i: (bi, i + skip, 0))
    return pl.pallas_call(
        _even_out_kernel,
        grid=(bsz, nt),
        in_specs=[tok(d), tok(ML_W), tok(ML_W), tok(ML_W), tok(AT_W),
                  _const_spec((1, ML_W)),
                  _const_spec((ML_W + AT_W, d)),
                  pl.BlockSpec((None, N_MOD, d), lambda bi, i: (jnp.where(i + skip == 0, nb, bi), 0, 0)),
                  _const_spec((1, d)),
                  _const_spec((d, D_FF)),
                  _const_spec((D_FF, d))],
        out_specs=pl.BlockSpec((None, tm, d), lambda bi, i: (bi, i, 0)),
        out_shape=jax.ShapeDtypeStruct((bsz, nt * tm, d), F32),
        compiler_params=_params(("parallel", "parallel")),
        name="even_out_mlp",
    )(z, hf, hb, mo, ya, og, w, mod, g2, w1, w2)


def _rwkv_in_kernel(z_ref, zp_ref, zn_ref, g_ref, mod_ref, mu_ref, wrkv_ref, w1_ref, a1_ref, g1_ref,
                    w2_ref, a2_ref, g2_ref, w0_ref, a0_ref, kk_ref, ka_ref, rk_ref, e_ref, et_ref,
                    sh_ref, dr_ref, lw_ref, gate_ref, bonus_ref, *, nctx, ntile):
    i = pl.program_id(1)
    tm, d = z_ref.shape
    g = g_ref[...]
    shift = mod_ref[0:1, :]
    scale = mod_ref[1:2, :]
    h = _norm_mod(z_ref[...], g, shift, scale)
    no_prev = jnp.logical_or(i == 0, i == nctx)
    no_next = jnp.logical_or(i == nctx - 1, i == ntile - 1)
    hp = jnp.where(no_prev, 0.0, _norm_mod(zp_ref[7:8, :], g, shift, scale))
    hn = jnp.where(no_next, 0.0, _norm_mod(zn_ref[0:1, :], g, shift, scale))
    row = lax.broadcasted_iota(jnp.int32, (tm, 1), 0)
    dp = jnp.where(row == 0, hp, pltpu.roll(h, 1, 0)) - h
    dn = jnp.where(row == tm - 1, hn, pltpu.roll(h, tm - 1, 0)) - h

    hb, dpb, dnb = h.astype(BF16), dp.astype(BF16), dn.astype(BF16)
    mub = mu_ref[...].astype(BF16)

    def mix(n):
        return hb + mub[2 * n:2 * n + 1, :] * dpb + mub[2 * n + 1:2 * n + 2, :] * dnb

    r = _dot(mix(0), wrkv_ref[0])
    k = _dot(mix(2), wrkv_ref[1])
    v = _dot(mix(3), wrkv_ref[2])
    gate_ref[...] = _dot(_sigmoid(_dot(mix(5), g1_ref[...])).astype(BF16), g2_ref[...]).astype(gate_ref.dtype)
    lora_w = _dot(jnp.tanh(_dot(mix(1), w1_ref[...])).astype(BF16), w2_ref[...])
    lora_a = _dot(_dot(mix(4), a1_ref[...]).astype(BF16), a2_ref[...])
    kkr = k * kk_ref[...]
    ssq = _head_sum(kkr * kkr, e_ref, et_ref, 1)
    kk = kkr * lax.rsqrt(jnp.maximum(ssq, 1e-24))
    kd_sum = None
    for dr in range(2):
        cols = slice(dr * d, (dr + 1) * d)
        lw = -RW_DECAY_SCALE * _sigmoid(w0_ref[dr:dr + 1, :] + lora_w[:, cols])
        a = _sigmoid(a0_ref[dr:dr + 1, :] + lora_a[:, cols])
        kd = k * (1.0 + (a - 1.0) * ka_ref[...])
        bvec = kk * a
        kd_sum = kd if kd_sum is None else kd_sum + kd
        for p in range(RW_PAIRS):
            ls = slice(p * LANES, (p + 1) * LANES)
            lw_ref[dr, p] = lw[:, ls]
            dr_ref[dr, 0, p] = kd[:, ls].astype(BF16)
            dr_ref[dr, 1, p] = bvec[:, ls].astype(BF16)
    for p in range(RW_PAIRS):
        ls = slice(p * LANES, (p + 1) * LANES)
        sh_ref[0, p] = r[:, ls].astype(BF16)
        sh_ref[1, p] = v[:, ls].astype(BF16)
        sh_ref[2, p] = kk[:, ls].astype(BF16)
    bsum = _head_sum(r * kd_sum * rk_ref[...], e_ref, et_ref, 1)
    bonus_ref[...] = (bsum * v).astype(bonus_ref.dtype)


def _rwkv_in_call(z, g, mod, mu, wrkv, w1, a1, g1, w2, a2, g2, w0, a0, k_k, k_a, r_k, e, et, nb, lc):
    bsz, t, d = z.shape
    tm = TOKEN_TILE
    ntile = t // tm
    nctx = lc // tm
    r8 = tm // 8
    tok = pl.BlockSpec((None, tm, d), lambda bi, i: (bi, i, 0))
    return pl.pallas_call(
        functools.partial(_rwkv_in_kernel, nctx=nctx, ntile=ntile),
        grid=(bsz, ntile),
        in_specs=[tok,
                  pl.BlockSpec((None, 8, d), lambda bi, i: (bi, jnp.maximum(i * r8 - 1, 0), 0)),
                  pl.BlockSpec((None, 8, d), lambda bi, i: (bi, jnp.minimum((i + 1) * r8, t // 8 - 1), 0)),
                  _const_spec((1, d)),
                  pl.BlockSpec((None, N_MOD, d), lambda bi, i: (jnp.where(i < nctx, nb, bi), 0, 0)),
                  _const_spec((12, d)),
                  _const_spec((3, d, d)),
                  _const_spec((d, LANES)),
                  _const_spec((d, LANES)),
                  _const_spec((d, RW_GATE_PAD)),
                  _const_spec((LANES, 2 * d)),
                  _const_spec((LANES, 2 * d)),
                  _const_spec((RW_GATE_PAD, d)),
                  _const_spec((2, d)),
                  _const_spec((2, d)),
                  _const_spec((1, d)),
                  _const_spec((1, d)),
                  _const_spec((1, d)),
                  _const_spec((d, LANES)),
                  _const_spec((LANES, d))],
        out_specs=[pl.BlockSpec((None, 3, RW_PAIRS, tm, LANES), lambda bi, i: (bi, 0, 0, i, 0)),
                   pl.BlockSpec((None, 2, 2, RW_PAIRS, tm, LANES), lambda bi, i: (bi, 0, 0, 0, i, 0)),
                   pl.BlockSpec((None, 2, RW_PAIRS, tm, LANES), lambda bi, i: (bi, 0, 0, i, 0)),
                   tok, tok],
        out_shape=[jax.ShapeDtypeStruct((bsz, 3, RW_PAIRS, t, LANES), BF16),
                   jax.ShapeDtypeStruct((bsz, 2, 2, RW_PAIRS, t, LANES), BF16),
                   jax.ShapeDtypeStruct((bsz, 2, RW_PAIRS, t, LANES), F32),
                   jax.ShapeDtypeStruct((bsz, t, d), BF16),
                   jax.ShapeDtypeStruct((bsz, t, d), BF16)],
        compiler_params=_params(("parallel", "parallel")),
        name="rwkv_in",
    )(z, z, z, g, mod, mu, wrkv, w1, a1, g1, w2, a2, g2, w0, a0, k_k, k_a, r_k, e, et)


def _rwkv_chunk_group(chains):
    ln = RW_CHUNK
    row = lax.broadcasted_iota(jnp.int32, (ln, 2 * ln), 0)
    col = lax.broadcasted_iota(jnp.int32, (ln, 2 * ln), 1)
    sidx = jnp.where(col >= ln, col - ln, col)
    lane_a = lax.broadcasted_iota(jnp.int32, (1, LANES), 1) < RW_HEAD
    tr = lax.broadcasted_iota(jnp.int32, (ln, ln), 0)
    tc = lax.broadcasted_iota(jnp.int32, (ln, ln), 1)
    tri = {False: (tc <= tr).astype(F32), True: (tc >= tr).astype(F32)}
    strict = {False: sidx < row, True: sidx > row}
    incl = {False: sidx <= row, True: sidx >= row}
    eye_w = jnp.where(sidx == row, 1.0, 0.0)
    diff = row ^ sidx
    vr = lax.broadcasted_iota(jnp.int32, (LANES, LANES), 0) < RW_HEAD
    kc = lax.broadcasted_iota(jnp.int32, (LANES, LANES), 1) < RW_HEAD
    same_head = vr == kc

    def bd(x):
        zero = jnp.zeros((), x.dtype)
        return jnp.concatenate([jnp.where(lane_a, x, zero), jnp.where(lane_a, zero, x)], axis=0)

    revs = [c[7] for c in chains]
    vs = [c[1] for c in chains]
    s_olds = [c[6] for c in chains]
    gcum = [jnp.dot(tri[c[7]], c[5], preferred_element_type=F32, precision=lax.Precision.HIGHEST) for c in chains]
    gtot = [jnp.sum(c[5], axis=0, keepdims=True) for c in chains]

    def scaled(c, g, gt):
        r, v, kk, kd, bv, lw = (x.astype(F32) for x in c[:6])
        e_pos = jnp.exp(g)
        e_neg = jnp.exp(-g)
        e_end = jnp.exp(gt - g)
        ar = jnp.concatenate([(-kk * jnp.exp(g - lw)).astype(RW_MM), (r * e_pos).astype(RW_MM)], axis=0)
        bk_end = jnp.concatenate([(bv * e_end).astype(RW_MM), (kd * e_end).astype(RW_MM)], axis=0)
        return ar, (bv * e_neg).astype(RW_MM), (kd * e_neg).astype(RW_MM), bk_end

    sc = [scaled(c, g, gt) for c, g, gt in zip(chains, gcum, gtot)]
    ars_ = [x[0] for x in sc]
    x_bk = [_dot_tb(x[0], jnp.concatenate([bd(x[1]), bd(x[2])], axis=0)) for x in sc]
    x_b = [x[:, :2 * ln] for x in x_bk]
    x_k = [x[:, 2 * ln:] for x in x_bk]
    n_w =[jnp.where(strict[rv], x[:ln], 0.0) for x, rv in zip(x_b, revs)]
    m_rb = [jnp.where(incl[rv], x[ln:], 0.0).astype(RW_MM) for x, rv in zip(x_b, revs)]
    m_k = [jnp.concatenate([jnp.where(strict[rv], x[:ln], 0.0), jnp.where(incl[rv], x[ln:], 0.0)],
                           axis=0).astype(RW_MM) for x, rv in zip(x_k, revs)]
    x_w = [eye_w + jnp.where(diff == 1, n, 0.0) for n in n_w]

    def take_rows(x, h, odd):
        return jnp.concatenate([x[b * h:(b + 1) * h] for b in range(ln // h) if (b % 2 == 1) == odd], axis=0)

    def put_rows(base, upd, h, odd):
        parts, j = [], 0
        for b in range(ln // h):
            if (b % 2 == 1) == odd:
                blk = upd[j * h:(j + 1) * h]
                parts.append(blk if base is None else base[b * h:(b + 1) * h] + blk)
                j += 1
            else:
                parts.append(jnp.zeros((h, upd.shape[1]), upd.dtype) if base is None else base[b * h:(b + 1) * h])
        return jnp.concatenate(parts, axis=0)

    h = 2
    while h < ln:
        lvl = jnp.logical_and(diff >= h, diff < 2 * h)
        n_l = [jnp.where(lvl, n, 0.0) for n in n_w]
        if h < RW_ROW_TILE:
            tmp = [_dot(n.astype(RW_MM), bd(x.astype(RW_MM))) for n, x in zip(n_l, x_w)]
            x_w = [x + _dot(x.astype(RW_MM), bd(t.astype(RW_MM))) for x, t in zip(x_w, tmp)]
        else:
            odd = [not rv for rv in revs]
            tmp = [_dot(take_rows(n, h, o).astype(RW_MM), bd(x.astype(RW_MM))) for n, x, o in zip(n_l, x_w, odd)]
            tmp = [put_rows(None, t, h, o) for t, o in zip(tmp, odd)]
            cor = [_dot(take_rows(x, h, o).astype(RW_MM), bd(t.astype(RW_MM))) for x, t, o in zip(x_w, tmp, odd)]
            x_w = [put_rows(x, c, h, o) for x, c, o in zip(x_w, cor, odd)]
        h *= 2
    ars = [_dot_tb(a, s.astype(RW_MM)) for a, s in zip(ars_, s_olds)]
    mv = [_dot(m, bd(v)) for m, v in zip(m_k, vs)]
    u = [_dot(x.astype(RW_MM), bd((a[:ln] + m[:ln]).astype(RW_MM))) for x, a, m in zip(x_w, ars, mv)]
    y = [a[ln:] + m[ln:] + _dot(rb, bd(uu.astype(RW_MM))) for a, m, rb, uu in zip(ars, mv, m_rb, u)]
    upd = [_dot_ta(jnp.concatenate([uu.astype(RW_MM), v], axis=0), x[3]) for uu, v, x in zip(u, vs, sc)]
    s_new = [s * jnp.exp(gt) + jnp.where(same_head, up, 0.0) for s, gt, up in zip(s_olds, gtot, upd)]
    return list(zip(y, s_new))


def _rwkv_scan_kernel(shf_ref, shb_ref, drf_ref, drb_ref, lwf_ref, lwb_ref, yf_ref, yb_ref, s_s):
    @pl.when(pl.program_id(1) == 0)
    def _():
        s_s[...] = jnp.zeros_like(s_s)

    chains, where = [], []
    for bb in range(RW_BATCH):
        for p in range(RW_PAIRS):
            chains.append((shf_ref[bb, 0, p], shf_ref[bb, 1, p], shf_ref[bb, 2, p], drf_ref[bb, 0, p],
                           drf_ref[bb, 1, p], lwf_ref[bb, p], s_s[bb, 0, p], False))
            where.append((yf_ref, bb, 0, p))
            chains.append((shb_ref[bb, 0, p], shb_ref[bb, 1, p], shb_ref[bb, 2, p], drb_ref[bb, 0, p],
                           drb_ref[bb, 1, p], lwb_ref[bb, p], s_s[bb, 1, p], True))
            where.append((yb_ref, bb, 1, p))
    for (y, s_new), (y_ref, bb, d, p) in zip(_rwkv_chunk_group(chains), where):
        y_ref[bb, p] = y.astype(y_ref.dtype)
        s_s[bb, d, p] = s_new


def _rwkv_scan_call(sh, dr, lw, lc):
    bsz, _, _, t, _ = sh.shape
    ln = RW_CHUNK
    nb = RW_BATCH
    assert bsz % nb == 0
    n = t // ln
    nctx = lc // ln
    rv = lambda i: _rev_chunk(i, nctx, n)
    return pl.pallas_call(
        _rwkv_scan_kernel,
        grid=(bsz // nb, n),
        in_specs=[pl.BlockSpec((nb, 3, RW_PAIRS, ln, LANES), lambda bi, i: (bi, 0, 0, i, 0)),
                  pl.BlockSpec((nb, 3, RW_PAIRS, ln, LANES), lambda bi, i: (bi, 0, 0, rv(i), 0)),
                  pl.BlockSpec((nb, None, 2, RW_PAIRS, ln, LANES), lambda bi, i: (bi, 0, 0, 0, i, 0)),
                  pl.BlockSpec((nb, None, 2, RW_PAIRS, ln, LANES), lambda bi, i: (bi, 1, 0, 0, rv(i), 0)),
                  pl.BlockSpec((nb, None, RW_PAIRS, ln, LANES), lambda bi, i: (bi, 0, 0, i, 0)),
                  pl.BlockSpec((nb, None, RW_PAIRS, ln, LANES), lambda bi, i: (bi, 1, 0, rv(i), 0))],
        out_specs=[pl.BlockSpec((nb, RW_PAIRS, ln, LANES), lambda bi, i: (bi, 0, i, 0)),
                   pl.BlockSpec((nb, RW_PAIRS, ln, LANES), lambda bi, i: (bi, 0, rv(i), 0))],
        out_shape=[jax.ShapeDtypeStruct((bsz, RW_PAIRS, t, LANES), BF16)] * 2,
        scratch_shapes=[pltpu.VMEM((nb, 2, RW_PAIRS, LANES, LANES), F32)],
        compiler_params=_params(("parallel", "arbitrary")),
        name="rwkv_scan",
    )(sh, sh, dr, dr, lw, lw)


def _rwkv_out_kernel(z_ref, yf_ref, yb_ref, gate_ref, bonus_ref, lng_ref, lnb_ref, e_ref, et_ref, w_ref, mod_ref,
                     g2_ref, w1_ref, w2_ref, o_ref):
    y = jnp.concatenate([yf_ref[p].astype(F32) + yb_ref[p].astype(F32) for p in range(RW_PAIRS)], axis=1)
    yc = y - _head_sum(y, e_ref, et_ref, 2) * (1.0 / RW_HEAD)
    var = _head_sum(yc * yc, e_ref, et_ref, 1) * (1.0 / RW_HEAD)
    yn = yc * lax.rsqrt(var + RW_LN_EPS) * lng_ref[...] + lnb_ref[...] + bonus_ref[...]
    out = _dot((yn * gate_ref[...]).astype(BF16), w_ref[...])
    o_ref[...] = _mlp_tail(z_ref[...] + mod_ref[2:3, :] * out, g2_ref[...], mod_ref, w1_ref, w2_ref)


def _rwkv_out_call(z, yf, yb, gate, bonus, ln_g, ln_b, e, et, w, mod, g2, w1, w2, nb, lc, skip):
    bsz, t, d = z.shape
    tm = TOKEN_TILE
    nctx = lc // tm
    nt = t // tm - skip
    tok = pl.BlockSpec((None, tm, d), lambda bi, i: (bi, i + skip, 0))
    pair = pl.BlockSpec((None, RW_PAIRS, tm, LANES), lambda bi, i: (bi, 0, i + skip, 0))
    return pl.pallas_call(
        _rwkv_out_kernel,
        grid=(bsz, nt),
        in_specs=[tok, pair, pair, tok, tok,
                  _const_spec((1, d)), _const_spec((1, d)), _const_spec((d, LANES)), _const_spec((LANES, d)),
                  _const_spec((d, d)),
                  pl.BlockSpec((None, N_MOD, d), lambda bi, i: (jnp.where(i + skip < nctx, nb, bi), 0, 0)),
                  _const_spec((1, d)),
                  _const_spec((d, D_FF)),
                  _const_spec((D_FF, d))],
        out_specs=pl.BlockSpec((None, tm, d), lambda bi, i: (bi, i, 0)),
        out_shape=jax.ShapeDtypeStruct((bsz, nt * tm, d), F32),
        compiler_params=_params(("parallel", "parallel")),
        name="rwkv_out_mlp",
    )(z, yf, yb, gate, bonus, ln_g, ln_b, e, et, w, mod, g2, w1, w2)


def _group_mean_matrix(n, width):
    idx = jnp.arange(n) // width
    return ((idx[:, None] == idx[None, :]).astype(F32) / width).astype(BF16)


def _rope_tables(lc, s):
    quarter = AT_DH // 4
    inv = ROPE_BASE ** (-jnp.arange(quarter, dtype=F32) / quarter)
    pos = jnp.arange(s)
    rpos = (pos // GRID_W).astype(F32)
    cpos = (pos % GRID_W).astype(F32)
    ang_r = rpos[:, None] * inv[None, :]
    ang_c = cpos[:, None] * inv[None, :]
    cos64 = jnp.concatenate([jnp.cos(ang_r), jnp.cos(ang_r), jnp.cos(ang_c), jnp.cos(ang_c)], axis=1)
    sin64 = jnp.concatenate([-jnp.sin(ang_r), jnp.sin(ang_r), -jnp.sin(ang_c), jnp.sin(ang_c)], axis=1)
    cos = jnp.concatenate([jnp.ones((lc, AT_DH), F32), cos64], axis=0)
    sin = jnp.concatenate([jnp.zeros((lc, AT_DH), F32), sin64], axis=0)
    return jnp.tile(cos, (1, 2)), jnp.tile(sin, (1, 2))


def _even_weights(w_in, b_in):
    def cols(m):
        mq, mk, mv, mo, mg, aq, ak, av = jnp.split(m, [512, 1024, 1536, 2048, 2064, 2576, 2704], axis=-1)
        dup = lambda u: jnp.concatenate([u[..., :64], u[..., :64], u[..., 64:], u[..., 64:]], axis=-1)
        mgp = jnp.pad(mg, [(0, 0)] * (m.ndim - 1) + [(0, LANES - 16)])
        return jnp.concatenate([mq, mk, mv, mo, aq, dup(ak), dup(av), mgp], axis=-1)
    return cols(w_in).astype(BF16), cols(b_in[None, :])


def kernel(x, c, ctx, c_ctx, ada_w, ada_b, norm1_g, norm2_g, mlp_w1, mlp_w2, ev_w_in, ev_b_in, ev_w_out, ml_f_bias, ml_out_g, at_q_g, at_k_g, at_sink, rw_mu, rw_w_rkv, rw_w0, rw_w1, rw_w2, rw_a0, rw_a1, rw_a2, rw_g1, rw_g2, rw_k_k, rw_k_a, rw_r_k, rw_ln_g, rw_ln_b, rw_w_out):
    bsz, s, d = x.shape
    lc = ctx.shape[1]
    depth = ada_w.shape[0]
    assert d == D_MODEL and lc == TOKEN_TILE and s % TOKEN_TILE == 0 and bsz < MOD_ROWS
    z = jnp.concatenate([ctx, x], axis=1)

    cstack = jnp.zeros((MOD_ROWS, d), F32).at[:bsz].set(c).at[bsz].set(c_ctx)
    mod_all = _ada_call(cstack, ada_w, ada_b).reshape(depth, MOD_ROWS, N_MOD, d)

    grp512 = _group_mean_matrix(AT_W, AT_DH)
    head_e = (jnp.arange(d)[:, None] // RW_HEAD == jnp.arange(LANES)[None, :]).astype(BF16)
    head_et = head_e.T
    cos_t, sin_t = _rope_tables(lc, s)
    gate_is_f = ((jnp.arange(LANES) % 8 >= 4) & (jnp.arange(LANES) < 16)).astype(F32)[None, :]

    for layer in range(depth):
        mod = mod_all[layer]
        j = layer // 2
        n1 = norm1_g[layer][None, :]
        mlp_args = (norm2_g[layer][None, :], mlp_w1[layer].astype(BF16), mlp_w2[layer].astype(BF16))
        skip = lc // TOKEN_TILE if layer == depth - 1 else 0
        if layer % 2 == 0:
            w_in, b_in = _even_weights(ev_w_in[j], ev_b_in[j])
            fb = jnp.zeros((LANES,), F32)
            for dr in range(2):
                fb = fb.at[8 * dr + 4:8 * dr + 8].set(ml_f_bias[j, dr])
            mqv, kt, mo, aq, akt, av, gates, gates_t = _even_in_call(
                z, n1, mod, w_in, b_in, jnp.tile(at_q_g[j], AT_HEADS)[None, :], jnp.tile(at_k_g[j], 4)[None, :],
                grp512, cos_t, sin_t, fb[None, :], gate_is_f, bsz)
            hf, hb = _mlstm_call(mqv, kt, gates, gates_t, lc)
            sink = jnp.broadcast_to(at_sink[j][:, None], (AT_HEADS, LANES))
            ya = _attn_call(aq, akt, av, sink, lc)
            z = _even_out_call(z, hf, hb, mo, ya, ml_out_g[j][None, :], ev_w_out[j].astype(BF16), mod, *mlp_args,
                               bsz, skip)
        else:
            pad_g = RW_GATE_PAD - RW_GATE_LORA
            zeros = jnp.zeros((RW_DECAY_LORA, d), F32)
            w2 = jnp.concatenate([jnp.concatenate([rw_w2[j, 0], zeros], axis=1),
                                  jnp.concatenate([zeros, rw_w2[j, 1]], axis=1)], axis=0)
            a2 = jnp.concatenate([jnp.concatenate([rw_a2[j, 0], zeros], axis=1),
                                  jnp.concatenate([zeros, rw_a2[j, 1]], axis=1)], axis=0)
            sh, dr_, lw, gate, bonus = _rwkv_in_call(
                z, n1, mod, rw_mu[j].reshape(12, d), rw_w_rkv[j].astype(BF16),
                jnp.concatenate([rw_w1[j, 0], rw_w1[j, 1]], axis=1).astype(BF16),
                jnp.concatenate([rw_a1[j, 0], rw_a1[j, 1]], axis=1).astype(BF16),
                jnp.pad(rw_g1[j], ((0, 0), (0, pad_g))).astype(BF16),
                w2.astype(BF16), a2.astype(BF16),
                jnp.pad(rw_g2[j], ((0, pad_g), (0, 0))).astype(BF16),
                rw_w0[j], rw_a0[j], rw_k_k[j][None, :], rw_k_a[j][None, :], rw_r_k[j].reshape(1, d),
                head_e, head_et, bsz, lc)
            yf, yb = _rwkv_scan_call(sh, dr_, lw, lc)
            z = _rwkv_out_call(z, yf, yb, gate, bonus, rw_ln_g[j][None, :], rw_ln_b[j][None, :], head_e, head_et,
                               rw_w_out[j].astype(BF16), mod, *mlp_args, bsz, lc, skip)
    return z
```

```python
import functools

import jax
import jax.numpy as jnp
from jax import lax
from jax.experimental import pallas as pl
from jax.experimental.pallas import tpu as pltpu

F32 = jnp.float32
BF16 = jnp.bfloat16

D_MODEL = 1024
N_MOD = 6
D_FF = 4 * D_MODEL
EPS = 1e-6
GRID_W = 64

ML_HEADS = 4
ML_D = 128
ML_W = ML_HEADS * ML_D
ML_CHUNK = 128
GATE_CAP = 15.0

AT_DH = 64
AT_HEADS = 8
AT_KV_HEADS = 2
AT_W = AT_HEADS * AT_DH
AT_BLOCK = 128
ROPE_BASE = 10000.0

RW_HEAD = 64
RW_PAIRS = D_MODEL // 128
RW_CHUNK = 64
RW_LN_EPS = 64e-5
RW_DECAY_LORA = 64
RW_AAA_LORA = 64
RW_GATE_LORA = 160
RW_GATE_PAD = 256
RW_DECAY_SCALE = 0.6065306597126334

LANES = 128
TOKEN_TILE = 256
FF_TILE = 1024
MOD_ROWS = 16
VMEM_LIMIT = 56 * 1024 * 1024
NEG = -1e30
RW_MM = BF16
RW_BATCH = 4
RW_ROW_TILE = 8

EV_MQKV = 0
EV_MO = 3 * ML_W
EV_AQ = EV_MO + ML_W
EV_AK = EV_AQ + AT_W
EV_AV = EV_AK + 256
EV_GATE = EV_AV + 256
EV_COLS = EV_GATE + LANES


def _dot(a, b):
    return jnp.dot(a, b, preferred_element_type=F32)


def _dot_tb(a, b):
    return lax.dot_general(a, b, (((1,), (1,)), ((), ())), preferred_element_type=F32)


def _dot_ta(a, b):
    return lax.dot_general(a, b, (((0,), (0,)), ((), ())), preferred_element_type=F32)


def _norm_mod(z, g, shift, scale):
    ms = jnp.mean(z * z, axis=-1, keepdims=True)
    return (z * lax.rsqrt(ms + EPS) * g) * (1.0 + scale) + shift


def _sigmoid(x):
    return 1.0 / (1.0 + jnp.exp(-x))


def _head_sum(x, e_ref, et_ref, terms):
    s = _dot(x.astype(BF16), e_ref[...])
    hi = s.astype(BF16)
    out = _dot(hi, et_ref[...])
    if terms > 1:
        out = out + _dot((s - hi.astype(F32)).astype(BF16), et_ref[...])
    return out


def _const_spec(shape):
    nd = len(shape)
    return pl.BlockSpec(shape, lambda *_: (0,) * nd, pipeline_mode=pl.Buffered(1))


def _params(sem):
    return pltpu.CompilerParams(dimension_semantics=sem, vmem_limit_bytes=VMEM_LIMIT)


def _ada_kernel(c_ref, w_ref, b_ref, o_ref):
    cv = c_ref[...]
    s = cv * _sigmoid(cv)
    o_ref[...] = jnp.dot(s, w_ref[...], preferred_element_type=F32,
                         precision=lax.Precision.HIGHEST) + b_ref[...]


def _ada_call(cstack, ada_w, ada_b):
    depth = ada_w.shape[0]
    n = N_MOD * D_MODEL
    tn = 1024
    return pl.pallas_call(
        _ada_kernel,
        grid=(depth, n // tn),
        in_specs=[pl.BlockSpec((MOD_ROWS, D_MODEL), lambda l, j: (0, 0)),
                  pl.BlockSpec((None, D_MODEL, tn), lambda l, j: (l, 0, j)),
                  pl.BlockSpec((None, 1, tn), lambda l, j: (l, 0, j))],
        out_specs=pl.BlockSpec((None, MOD_ROWS, tn), lambda l, j: (l, 0, j)),
        out_shape=jax.ShapeDtypeStruct((depth, MOD_ROWS, n), F32),
        compiler_params=_params(("parallel", "parallel")),
        name="ada_ln",
    )(cstack, ada_w, ada_b.reshape(depth, 1, n))


def _rope(x, cos, sin, lane_lo):
    n = x.shape[1]
    up = pltpu.roll(x, n - 16, 1)
    dn = pltpu.roll(x, 16, 1)
    reps = n // LANES
    c = jnp.concatenate([cos] * reps, axis=1) if reps > 1 else cos
    s = jnp.concatenate([sin] * reps, axis=1) if reps > 1 else sin
    lo = jnp.concatenate([lane_lo] * reps, axis=1) if reps > 1 else lane_lo
    return x * c + jnp.where(lo, up, dn) * s


def _even_in_kernel(z_ref, g_ref, mod_ref, w_ref, b_ref, gq_ref, gk_ref, grp_ref, cos_ref, sin_ref,
                    fb_ref, isf_ref, mqv_ref, kt_ref, mo_ref, aq_ref, akt_ref, av_ref, gate_ref, gatet_ref):
    h = _norm_mod(z_ref[...], g_ref[...], mod_ref[0:1, :], mod_ref[1:2, :])
    p = _dot(h.astype(BF16), w_ref[...]) + b_ref[...]
    mqv_ref[:, 0:ML_W] = p[:, 0:ML_W].astype(BF16)
    mqv_ref[:, ML_W:2 * ML_W] = p[:, 2 * ML_W:3 * ML_W].astype(BF16)
    kt_ref[...] = (p[:, ML_W:2 * ML_W] * (ML_D ** -0.5)).T.astype(BF16)
    mo_ref[...] = p[:, EV_MO:EV_MO + ML_W].astype(mo_ref.dtype)
    cos = cos_ref[...]
    sin = sin_ref[...]
    lane_lo = (lax.broadcasted_iota(jnp.int32, (1, LANES), 1) % 32) < 16
    q = p[:, EV_AQ:EV_AQ + AT_W]
    qms = _dot((q * q).astype(BF16), grp_ref[...])
    q = q * lax.rsqrt(qms + EPS) * gq_ref[...]
    aq_ref[...] = (_rope(q, cos, sin, lane_lo) * (AT_DH ** -0.5)).astype(BF16)
    k = p[:, EV_AK:EV_AK + 256]
    kms = _dot((k * k).astype(BF16), grp_ref[0:256, 0:256])
    k = k * lax.rsqrt(kms + EPS) * gk_ref[...]
    akt_ref[...] = _rope(k, cos, sin, lane_lo).T.astype(BF16)
    av_ref[...] = p[:, EV_AV:EV_AV + 256].astype(BF16)
    gt = p[:, EV_GATE:EV_GATE + LANES] + fb_ref[...]
    sc = GATE_CAP * jnp.tanh(gt * (1.0 / GATE_CAP))
    logsig = jnp.minimum(sc, 0.0) - jnp.log(1.0 + jnp.exp(-jnp.abs(sc)))
    gates = jnp.where(isf_ref[...] > 0.5, logsig, sc)
    gate_ref[...] = gates
    gatet_ref[...] = gates.T[0:16, :]


def _even_in_call(z, g, mod, w, b, gq, gk, grp, cos, sin, fb, isf, nb):
    bsz, t, d = z.shape
    tm = TOKEN_TILE
    tok = lambda width: pl.BlockSpec((None, tm, width), lambda bi, i: (bi, i, 0))
    tpose = lambda rows: pl.BlockSpec((None, rows, tm), lambda bi, i: (bi, 0, i))
    return pl.pallas_call(
        _even_in_kernel,
        grid=(bsz, t // tm),
        in_specs=[tok(d),
                  _const_spec((1, d)),
                  pl.BlockSpec((None, N_MOD, d), lambda bi, i: (jnp.where(i == 0, nb, bi), 0, 0)),
                  _const_spec((d, EV_COLS)),
                  _const_spec((1, EV_COLS)),
                  _const_spec((1, AT_W)),
                  _const_spec((1, 256)),
                  _const_spec((AT_W, AT_W)),
                  pl.BlockSpec((tm, LANES), lambda bi, i: (i, 0)),
                  pl.BlockSpec((tm, LANES), lambda bi, i: (i, 0)),
                  _const_spec((1, LANES)),
                  _const_spec((1, LANES))],
        out_specs=[tok(2 * ML_W), tpose(ML_W), tok(ML_W), tok(AT_W), tpose(256), tok(256), tok(LANES), tpose(16)],
        out_shape=[jax.ShapeDtypeStruct((bsz, t, 2 * ML_W), BF16),
                   jax.ShapeDtypeStruct((bsz, ML_W, t), BF16),
                   jax.ShapeDtypeStruct((bsz, t, ML_W), BF16),
                   jax.ShapeDtypeStruct((bsz, t, AT_W), BF16),
                   jax.ShapeDtypeStruct((bsz, 256, t), BF16),
                   jax.ShapeDtypeStruct((bsz, t, 256), BF16),
                   jax.ShapeDtypeStruct((bsz, t, LANES), F32),
                   jax.ShapeDtypeStruct((bsz, 16, t), F32)],
        compiler_params=_params(("parallel", "parallel")),
        name="even_in",
    )(z, g, mod, w, b, gq, gk, grp, cos, sin, fb, isf)


def _split3(x):
    hi = x.astype(BF16)
    r1 = x - hi.astype(F32)
    mid = r1.astype(BF16)
    return hi, mid, (r1 - mid.astype(F32)).astype(BF16)


def _mlstm_kernel(qf_ref, ktf_ref, vf_ref, qb_ref, ktb_ref, vb_ref, gcf_ref, gcb_ref, grf_ref, grb_ref, sel_ref,
                  hf_ref, hb_ref, c_s, m_s):
    ln = ML_CHUNK

    @pl.when(pl.program_id(1) == 0)
    def _():
        c_s[...] = jnp.zeros_like(c_s)
        m_s[...] = jnp.zeros_like(m_s)

    ri = lax.broadcasted_iota(jnp.int32, (ln, ln), 0)
    ci = lax.broadcasted_iota(jnp.int32, (ln, ln), 1)
    hi = lax.Precision.HIGHEST
    before = (ci <= ri, ci >= ri)
    refs = ((qf_ref, ktf_ref, vf_ref, gcf_ref, grf_ref, hf_ref), (qb_ref, ktb_ref, vb_ref, gcb_ref, grb_ref, hb_ref))
    cum_r = [jnp.dot(refs[d][4][...], before[1 - d].astype(F32), preferred_element_type=F32, precision=hi)
             for d in range(2)]
    bc_all = []
    for d in range(2):
        cum_c = sum(_dot(before[d].astype(BF16), part) for part in _split3(refs[d][3][...]))
        sel = sel_ref[d]
        bc_all.append(sum(_dot(part, sel) for part in _split3(cum_c)))
    chains = [(d, hd) for d in range(2) for hd in range(ML_HEADS)]
    lanes = [slice(hd * ML_D, (hd + 1) * ML_D) for _, hd in chains]
    st = [d * ML_HEADS + hd for d, hd in chains]
    ones = jnp.ones((ln, ML_D), BF16)
    q = [refs[d][0][:, ls] for (d, _), ls in zip(chains, lanes)]
    kt = [refs[d][1][ls, :] for (d, _), ls in zip(chains, lanes)]
    v1 = [jnp.concatenate([refs[d][2][:, ls], ones], axis=1) for (d, _), ls in zip(chains, lanes)]
    i_row = [refs[d][4][8 * d + hd:8 * d + hd + 1, :] for d, hd in chains]
    f_row = [refs[d][4][8 * d + 4 + hd:8 * d + 5 + hd, :] for d, hd in chains]
    bcum_row = [cum_r[d][8 * d + 4 + hd:8 * d + 5 + hd, :] for d, hd in chains]
    bcum = [bc_all[d][:, ls] for (d, _), ls in zip(chains, lanes)]
    m_old = [m_s[s:s + 1, 0:1] for s in st]
    c_old = [c_s[s] for s in st]
    qk = [_dot(a, b) for a, b in zip(q, kt)]
    r2 = [_dot(a, c.astype(BF16)) for a, c in zip(q, c_old)]
    dlog = [jnp.where(before[d], bc - br + ir, NEG) for (d, _), bc, br, ir in zip(chains, bcum, bcum_row, i_row)]
    inter = [bc + m for bc, m in zip(bcum, m_old)]
    mt = [jnp.maximum(jnp.max(dl, axis=1, keepdims=True), it) for dl, it in zip(dlog, inter)]
    sc = [x * jnp.exp(dl - m) for x, dl, m in zip(qk, dlog, mt)]
    iw = [jnp.exp(it - m) for it, m in zip(inter, mt)]
    r1 = [_dot(s.astype(BF16), b) for s, b in zip(sc, v1)]
    for j, (d, _) in enumerate(chains):
        tot = r1[j] + jnp.concatenate([iw[j], iw[j]], axis=1) * r2[j]
        den = jnp.maximum(jnp.abs(tot[:, ML_D:]), jnp.exp(-mt[j]))
        refs[d][5][:, lanes[j]] = (tot[:, :ML_D] / den).astype(refs[d][5].dtype)
    bl = [jnp.sum(fr, axis=1, keepdims=True) for fr in f_row]
    ws_log = [b - br + ir for b, br, ir in zip(bl, bcum_row, i_row)]
    m_new = [jnp.maximum(b + m, jnp.max(w, axis=1, keepdims=True)) for b, m, w in zip(bl, m_old, ws_log)]
    ws = [jnp.exp(w - m) for w, m in zip(ws_log, m_new)]
    upd = [_dot((a.astype(F32) * w).astype(BF16), b) for a, w, b in zip(kt, ws, v1)]
    for j, s in enumerate(st):
        c_s[s] = jnp.exp(bl[j] + m_old[j] - m_new[j]) * c_old[j] + upd[j]
        m_s[s:s + 1, :] = jnp.broadcast_to(m_new[j], (1, LANES))


def _rev_chunk(i, nctx, n):
    return jnp.where(i < nctx, nctx - 1 - i, n + nctx - 1 - i)


def _mlstm_call(mqv, kt, gates, gates_t, lc):
    bsz, t, _ = mqv.shape
    ln = ML_CHUNK
    assert ln == ML_D == LANES
    n = t // ln
    nctx = lc // ln
    rv = lambda i: _rev_chunk(i, nctx, n)
    col = jnp.arange(LANES)[None, :, None]
    want = (8 * jnp.arange(2)[:, None, None] + 4 + jnp.arange(ML_HEADS * ln)[None, None, :] // ln)
    sel = (col == want).astype(BF16)
    return pl.pallas_call(
        _mlstm_kernel,
        grid=(bsz, n),
        in_specs=[pl.BlockSpec((None, ln, ML_W), lambda bi, i: (bi, i, 0)),
                  pl.BlockSpec((None, ML_W, ln), lambda bi, i: (bi, 0, i)),
                  pl.BlockSpec((None, ln, ML_W), lambda bi, i: (bi, i, 1)),
                  pl.BlockSpec((None, ln, ML_W), lambda bi, i: (bi, rv(i), 0)),
                  pl.BlockSpec((None, ML_W, ln), lambda bi, i: (bi, 0, rv(i))),
                  pl.BlockSpec((None, ln, ML_W), lambda bi, i: (bi, rv(i), 1)),
                  pl.BlockSpec((None, ln, LANES), lambda bi, i: (bi, i, 0)),
                  pl.BlockSpec((None, ln, LANES), lambda bi, i: (bi, rv(i), 0)),
                  pl.BlockSpec((None, 16, ln), lambda bi, i: (bi, 0, i)),
                  pl.BlockSpec((None, 16, ln), lambda bi, i: (bi, 0, rv(i))),
                  _const_spec((2, LANES, ML_HEADS * ln))],
        out_specs=[pl.BlockSpec((None, ln, ML_W), lambda bi, i: (bi, i, 0)),
                   pl.BlockSpec((None, ln, ML_W), lambda bi, i: (bi, rv(i), 0))],
        out_shape=[jax.ShapeDtypeStruct((bsz, t, ML_W), BF16)] * 2,
        scratch_shapes=[pltpu.VMEM((2 * ML_HEADS, ML_D, 2 * ML_D), F32),
                        pltpu.VMEM((2 * ML_HEADS, LANES), F32)],
        compiler_params=_params(("parallel", "arbitrary")),
        name="mlstm_scan",
    )(mqv, kt, mqv, mqv, kt, mqv, gates, gates, gates_t, gates_t, sel)


def _attn_kernel(q_ref, ktp_ref, kto_ref, ktn_ref, ktc_ref, vp_ref, vo_ref, vn_ref, vc_ref, sink_ref, o_ref,
                 *, nctx, n):
    j = pl.program_id(1)
    blk = AT_BLOCK
    latent = j >= nctx
    qi = lax.broadcasted_iota(jnp.int32, (blk, blk), 0)
    ki = lax.broadcasted_iota(jnp.int32, (blk, blk), 1)
    ok_prev = jnp.logical_and(jnp.logical_and(latent, j > nctx), ki >= qi)
    ok_own = jnp.logical_and(latent, ki >= 0)
    ok_next = jnp.logical_and(jnp.logical_and(latent, j < n - 1), ki <= qi)
    lane = lax.broadcasted_iota(jnp.int32, (1, LANES), 1)
    sub = lax.broadcasted_iota(jnp.int32, (LANES, 1), 0)
    lane_half = (lane < AT_DH, lane >= AT_DH)
    sub_half = (sub < AT_DH, sub >= AT_DH)
    kts = (ktp_ref, kto_ref, ktn_ref, ktc_ref)
    vs = (vp_ref, vo_ref, vn_ref, vc_ref)
    oks = (ok_prev, ok_own, ok_next, None)
    zero = jnp.zeros((), BF16)
    kx, vx = {}, {}
    for g in range(AT_KV_HEADS):
        rows = slice(g * LANES, (g + 1) * LANES)
        for e in range(2):
            kx[g, e] = [jnp.where(sub_half[e], r[rows, :], zero) for r in kts]
            vx[g, e] = [jnp.concatenate([jnp.where(lane_half[e], r[:, rows], zero),
                                         jnp.ones((r.shape[0], LANES), BF16)], axis=1) for r in vs]
    heads = [(hd // 2, hd % 2, hd // 4) for hd in range(AT_HEADS)]
    qc = [q_ref[:, c * LANES:(c + 1) * LANES] for c in range(AT_HEADS // 2)]
    sink = [sink_ref[hd:hd + 1, 0:1] for hd in range(AT_HEADS)]
    ss = [[_dot(qc[c], kx[g, e][p]) if oks[p] is None else jnp.where(oks[p], _dot(qc[c], kx[g, e][p]), NEG)
           for p in range(4)] for c, e, g in heads]
    m = []
    for hd in range(AT_HEADS):
        band = jnp.maximum(jnp.maximum(ss[hd][0], ss[hd][1]), ss[hd][2])
        m.append(jnp.maximum(jnp.maximum(jnp.max(band, axis=1, keepdims=True),
                                         jnp.max(ss[hd][3], axis=1, keepdims=True)), sink[hd]))
    res = []
    for hd, (c, e, g) in enumerate(heads):
        acc = None
        for p in range(4):
            pv = _dot(jnp.exp(ss[hd][p] - m[hd]).astype(BF16), vx[g, e][p])
            acc = pv if acc is None else acc + pv
        res.append(acc)
    outs = [r[:, :LANES] / (r[:, LANES:] + jnp.exp(s - mm)) for r, s, mm in zip(res, sink, m)]
    for c in range(AT_HEADS // 2):
        o_ref[:, c * LANES:(c + 1) * LANES] = (outs[2 * c] + outs[2 * c + 1]).astype(o_ref.dtype)


def _attn_call(aq, akt, av, sink, lc):
    bsz, t, _ = aq.shape
    blk = AT_BLOCK
    n = t // blk
    nctx = lc // blk
    prev = lambda j: jnp.clip(j - 1, nctx, n - 1)
    own = lambda j: jnp.clip(j, nctx, n - 1)
    nxt = lambda j: jnp.clip(j + 1, nctx, n - 1)
    kt = lambda f: pl.BlockSpec((None, 256, blk), lambda bi, j: (bi, 0, f(j)))
    vv = lambda f: pl.BlockSpec((None, blk, 256), lambda bi, j: (bi, f(j), 0))
    return pl.pallas_call(
        functools.partial(_attn_kernel, nctx=nctx, n=n),
        grid=(bsz, n),
        in_specs=[pl.BlockSpec((None, blk, AT_W), lambda bi, j: (bi, j, 0)),
                  kt(prev), kt(own), kt(nxt),
                  pl.BlockSpec((None, 256, lc), lambda bi, j: (bi, 0, 0)),
                  vv(prev), vv(own), vv(nxt),
                  pl.BlockSpec((None, lc, 256), lambda bi, j: (bi, 0, 0)),
                  _const_spec((AT_HEADS, LANES))],
        out_specs=pl.BlockSpec((None, blk, AT_W), lambda bi, j: (bi, j, 0)),
        out_shape=jax.ShapeDtypeStruct((bsz, t, AT_W), BF16),
        compiler_params=_params(("parallel", "parallel")),
        name="window_attn",
    )(aq, akt, akt, akt, akt, av, av, av, av, sink)


def _mlp_tail(z, g, mod_ref, w1_ref, w2_ref):
    h = _norm_mod(z, g, mod_ref[3:4, :], mod_ref[4:5, :]).astype(BF16)
    acc = jnp.zeros(z.shape, F32)
    for f in range(D_FF // FF_TILE):
        a = jnp.maximum(_dot(h, w1_ref[:, f * FF_TILE:(f + 1) * FF_TILE]), 0.0)
        acc = acc + _dot((a * a).astype(BF16), w2_ref[f * FF_TILE:(f + 1) * FF_TILE, :])
    return z + mod_ref[5:6, :] * acc


def _even_out_kernel(z_ref, hf_ref, hb_ref, mo_ref, ya_ref, og_ref, w_ref, mod_ref, g2_ref, w1_ref, w2_ref, o_ref):
    hs = hf_ref[...].astype(F32) + hb_ref[...].astype(F32)
    parts = []
    for hd in range(ML_HEADS):
        x = hs[:, hd * ML_D:(hd + 1) * ML_D]
        parts.append(x * lax.rsqrt(jnp.mean(x * x, axis=1, keepdims=True) + EPS))
    ym = jnp.concatenate(parts, axis=1) * og_ref[...] * _sigmoid(mo_ref[...].astype(F32))
    y = _dot(ym.astype(BF16), w_ref[0:ML_W, :]) + _dot(ya_ref[...], w_ref[ML_W:ML_W + AT_W, :])
    o_ref[...] = _mlp_tail(z_ref[...] + mod_ref[2:3, :] * y, g2_ref[...], mod_ref, w1_ref, w2_ref)


def _even_out_call(z, hf, hb, mo, ya, og, w, mod, g2, w1, w2, nb, skip):
    bsz, t, d = z.shape
    tm = TOKEN_TILE
    nt = t // tm - skip
    tok = lambda width: pl.BlockSpec((None, tm, width), lambda bi, i: (bi, i + skip, 0))
    return pl.pallas_call(
        _even_out_kernel,
        grid=(bsz, nt),
        in_specs=[tok(d), tok(ML_W), tok(ML_W), tok(ML_W), tok(AT_W),
                  _const_spec((1, ML_W)),
                  _const_spec((ML_W + AT_W, d)),
                  pl.BlockSpec((None, N_MOD, d), lambda bi, i: (jnp.where(i + skip == 0, nb, bi), 0, 0)),
                  _const_spec((1, d)),
                  _const_spec((d, D_FF)),
                  _const_spec((D_FF, d))],
        out_specs=pl.BlockSpec((None, tm, d), lambda bi, i: (bi, i, 0)),
        out_shape=jax.ShapeDtypeStruct((bsz, nt * tm, d), F32),
        compiler_params=_params(("parallel", "parallel")),
        name="even_out_mlp",
    )(z, hf, hb, mo, ya, og, w, mod, g2, w1, w2)


def _rwkv_in_kernel(z_ref, zp_ref, zn_ref, g_ref, mod_ref, mu_ref, wrkv_ref, w1_ref, a1_ref, g1_ref,
                    w2_ref, a2_ref, g2_ref, w0_ref, a0_ref, kk_ref, ka_ref, rk_ref, e_ref, et_ref,
                    sh_ref, dr_ref, lw_ref, gate_ref, bonus_ref, *, nctx, ntile):
    i = pl.program_id(1)
    tm, d = z_ref.shape
    g = g_ref[...]
    shift = mod_ref[0:1, :]
    scale = mod_ref[1:2, :]
    h = _norm_mod(z_ref[...], g, shift, scale)
    no_prev = jnp.logical_or(i == 0, i == nctx)
    no_next = jnp.logical_or(i == nctx - 1, i == ntile - 1)
    hp = jnp.where(no_prev, 0.0, _norm_mod(zp_ref[7:8, :], g, shift, scale))
    hn = jnp.where(no_next, 0.0, _norm_mod(zn_ref[0:1, :], g, shift, scale))
    row = lax.broadcasted_iota(jnp.int32, (tm, 1), 0)
    dp = jnp.where(row == 0, hp, pltpu.roll(h, 1, 0)) - h
    dn = jnp.where(row == tm - 1, hn, pltpu.roll(h, tm - 1, 0)) - h

    hb, dpb, dnb = h.astype(BF16), dp.astype(BF16), dn.astype(BF16)
    mub = mu_ref[...].astype(BF16)

    def mix(n):
        return hb + mub[2 * n:2 * n + 1, :] * dpb + mub[2 * n + 1:2 * n + 2, :] * dnb

    r = _dot(mix(0), wrkv_ref[0])
    k = _dot(mix(2), wrkv_ref[1])
    v = _dot(mix(3), wrkv_ref[2])
    gate_ref[...] = _dot(_sigmoid(_dot(mix(5), g1_ref[...])).astype(BF16), g2_ref[...]).astype(gate_ref.dtype)
    lora_w = _dot(jnp.tanh(_dot(mix(1), w1_ref[...])).astype(BF16), w2_ref[...])
    lora_a = _dot(_dot(mix(4), a1_ref[...]).astype(BF16), a2_ref[...])
    kkr = k * kk_ref[...]
    ssq = _head_sum(kkr * kkr, e_ref, et_ref, 1)
    kk = kkr * lax.rsqrt(jnp.maximum(ssq, 1e-24))
    kd_sum = None
    for dr in range(2):
        cols = slice(dr * d, (dr + 1) * d)
        lw = -RW_DECAY_SCALE * _sigmoid(w0_ref[dr:dr + 1, :] + lora_w[:, cols])
        a = _sigmoid(a0_ref[dr:dr + 1, :] + lora_a[:, cols])
        kd = k * (1.0 + (a - 1.0) * ka_ref[...])
        bvec = kk * a
        kd_sum = kd if kd_sum is None else kd_sum + kd
        for p in range(RW_PAIRS):
            ls = slice(p * LANES, (p + 1) * LANES)
            lw_ref[dr, p] = lw[:, ls]
            dr_ref[dr, 0, p] = kd[:, ls].astype(BF16)
            dr_ref[dr, 1, p] = bvec[:, ls].astype(BF16)
    for p in range(RW_PAIRS):
        ls = slice(p * LANES, (p + 1) * LANES)
        sh_ref[0, p] = r[:, ls].astype(BF16)
        sh_ref[1, p] = v[:, ls].astype(BF16)
        sh_ref[2, p] = kk[:, ls].astype(BF16)
    bsum = _head_sum(r * kd_sum * rk_ref[...], e_ref, et_ref, 1)
    bonus_ref[...] = (bsum * v).astype(bonus_ref.dtype)


def _rwkv_in_call(z, g, mod, mu, wrkv, w1, a1, g1, w2, a2, g2, w0, a0, k_k, k_a, r_k, e, et, nb, lc):
    bsz, t, d = z.shape
    tm = TOKEN_TILE
    ntile = t // tm
    nctx = lc // tm
    r8 = tm // 8
    tok = pl.BlockSpec((None, tm, d), lambda bi, i: (bi, i, 0))
    return pl.pallas_call(
        functools.partial(_rwkv_in_kernel, nctx=nctx, ntile=ntile),
        grid=(bsz, ntile),
        in_specs=[tok,
                  pl.BlockSpec((None, 8, d), lambda bi, i: (bi, jnp.maximum(i * r8 - 1, 0), 0)),
                  pl.BlockSpec((None, 8, d), lambda bi, i: (bi, jnp.minimum((i + 1) * r8, t // 8 - 1), 0)),
                  _const_spec((1, d)),
                  pl.BlockSpec((None, N_MOD, d), lambda bi, i: (jnp.where(i < nctx, nb, bi), 0, 0)),
                  _const_spec((12, d)),
                  _const_spec((3, d, d)),
                  _const_spec((d, LANES)),
                  _const_spec((d, LANES)),
                  _const_spec((d, RW_GATE_PAD)),
                  _const_spec((LANES, 2 * d)),
                  _const_spec((LANES, 2 * d)),
                  _const_spec((RW_GATE_PAD, d)),
                  _const_spec((2, d)),
                  _const_spec((2, d)),
                  _const_spec((1, d)),
                  _const_spec((1, d)),
                  _const_spec((1, d)),
                  _const_spec((d, LANES)),
                  _const_spec((LANES, d))],
        out_specs=[pl.BlockSpec((None, 3, RW_PAIRS, tm, LANES), lambda bi, i: (bi, 0, 0, i, 0)),
                   pl.BlockSpec((None, 2, 2, RW_PAIRS, tm, LANES), lambda bi, i: (bi, 0, 0, 0, i, 0)),
                   pl.BlockSpec((None, 2, RW_PAIRS, tm, LANES), lambda bi, i: (bi, 0, 0, i, 0)),
                   tok, tok],
        out_shape=[jax.ShapeDtypeStruct((bsz, 3, RW_PAIRS, t, LANES), BF16),
                   jax.ShapeDtypeStruct((bsz, 2, 2, RW_PAIRS, t, LANES), BF16),
                   jax.ShapeDtypeStruct((bsz, 2, RW_PAIRS, t, LANES), F32),
                   jax.ShapeDtypeStruct((bsz, t, d), BF16),
                   jax.ShapeDtypeStruct((bsz, t, d), BF16)],
        compiler_params=_params(("parallel", "parallel")),
        name="rwkv_in",
    )(z, z, z, g, mod, mu, wrkv, w1, a1, g1, w2, a2, g2, w0, a0, k_k, k_a, r_k, e, et)


def _rwkv_chunk_group(chains):
    ln = RW_CHUNK
    row = lax.broadcasted_iota(jnp.int32, (ln, 2 * ln), 0)
    col = lax.broadcasted_iota(jnp.int32, (ln, 2 * ln), 1)
    sidx = jnp.where(col >= ln, col - ln, col)
    lane_a = lax.broadcasted_iota(jnp.int32, (1, LANES), 1) < RW_HEAD
    tr = lax.broadcasted_iota(jnp.int32, (ln, ln), 0)
    tc = lax.broadcasted_iota(jnp.int32, (ln, ln), 1)
    tri = {False: (tc <= tr).astype(F32), True: (tc >= tr).astype(F32)}
    strict = {False: sidx < row, True: sidx > row}
    incl = {False: sidx <= row, True: sidx >= row}
    eye_w = jnp.where(sidx == row, 1.0, 0.0)
    diff = row ^ sidx
    vr = lax.broadcasted_iota(jnp.int32, (LANES, LANES), 0) < RW_HEAD
    kc = lax.broadcasted_iota(jnp.int32, (LANES, LANES), 1) < RW_HEAD
    same_head = vr == kc

    def bd(x):
        zero = jnp.zeros((), x.dtype)
        return jnp.concatenate([jnp.where(lane_a, x, zero), jnp.where(lane_a, zero, x)], axis=0)

    revs = [c[7] for c in chains]
    vs = [c[1] for c in chains]
    s_olds = [c[6] for c in chains]
    gcum = [sum(_dot(tri[c[7]].astype(BF16), part) for part in _split3(c[5])) for c in chains]
    gtot = [jnp.sum(c[5], axis=0, keepdims=True) for c in chains]

    def scaled(c, g, gt):
        r, v, kk, kd, bv, lw = (x.astype(F32) for x in c[:6])
        e_pos = jnp.exp(g)
        e_neg = jnp.exp(-g)
        e_end = jnp.exp(gt - g)
        ar = jnp.concatenate([(-kk * jnp.exp(g - lw)).astype(RW_MM), (r * e_pos).astype(RW_MM)], axis=0)
        bk_end = jnp.concatenate([(bv * e_end).astype(RW_MM), (kd * e_end).astype(RW_MM)], axis=0)
        return ar, (bv * e_neg).astype(RW_MM), (kd * e_neg).astype(RW_MM), bk_end

    sc = [scaled(c, g, gt) for c, g, gt in zip(chains, gcum, gtot)]
    ars_ = [x[0] for x in sc]
    x_bk = [_dot_tb(x[0], jnp.concatenate([bd(x[1]), bd(x[2])], axis=0)) for x in sc]
    x_b = [x[:, :2 * ln] for x in x_bk]
    x_k = [x[:, 2 * ln:] for x in x_bk]
    n_w =[jnp.where(strict[rv], x[:ln], 0.0) for x, rv in zip(x_b, revs)]
    m_rb = [jnp.where(incl[rv], x[ln:], 0.0).astype(RW_MM) for x, rv in zip(x_b, revs)]
    m_k = [jnp.concatenate([jnp.where(strict[rv], x[:ln], 0.0), jnp.where(incl[rv], x[ln:], 0.0)],
                           axis=0).astype(RW_MM) for x, rv in zip(x_k, revs)]
    x_w = [eye_w + jnp.where(diff == 1, n, 0.0) for n in n_w]

    def take_rows(x, h, odd):
        return jnp.concatenate([x[b * h:(b + 1) * h] for b in range(ln // h) if (b % 2 == 1) == odd], axis=0)

    def put_rows(base, upd, h, odd):
        parts, j = [], 0
        for b in range(ln // h):
            if (b % 2 == 1) == odd:
                blk = upd[j * h:(j + 1) * h]
                parts.append(blk if base is None else base[b * h:(b + 1) * h] + blk)
                j += 1
            else:
                parts.append(jnp.zeros((h, upd.shape[1]), upd.dtype) if base is None else base[b * h:(b + 1) * h])
        return jnp.concatenate(parts, axis=0)

    h = 2
    while h < ln:
        lvl = jnp.logical_and(diff >= h, diff < 2 * h)
        n_l = [jnp.where(lvl, n, 0.0) for n in n_w]
        if h < RW_ROW_TILE:
            tmp = [_dot(n.astype(RW_MM), bd(x.astype(RW_MM))) for n, x in zip(n_l, x_w)]
            x_w = [x + _dot(x.astype(RW_MM), bd(t.astype(RW_MM))) for x, t in zip(x_w, tmp)]
        else:
            odd = [not rv for rv in revs]
            tmp = [_dot(take_rows(n, h, o).astype(RW_MM), bd(x.astype(RW_MM))) for n, x, o in zip(n_l, x_w, odd)]
            tmp = [put_rows(None, t, h, o) for t, o in zip(tmp, odd)]
            cor = [_dot(take_rows(x, h, o).astype(RW_MM), bd(t.astype(RW_MM))) for x, t, o in zip(x_w, tmp, odd)]
            x_w = [put_rows(x, c, h, o) for x, c, o in zip(x_w, cor, odd)]
        h *= 2
    ars = [_dot_tb(a, s.astype(RW_MM)) for a, s in zip(ars_, s_olds)]
    mv = [_dot(m, bd(v)) for m, v in zip(m_k, vs)]
    u = [_dot(x.astype(RW_MM), bd((a[:ln] + m[:ln]).astype(RW_MM))) for x, a, m in zip(x_w, ars, mv)]
    y = [a[ln:] + m[ln:] + _dot(rb, bd(uu.astype(RW_MM))) for a, m, rb, uu in zip(ars, mv, m_rb, u)]
    upd = [_dot_ta(jnp.concatenate([uu.astype(RW_MM), v], axis=0), x[3]) for uu, v, x in zip(u, vs, sc)]
    s_new = [s * jnp.exp(gt) + jnp.where(same_head, up, 0.0) for s, gt, up in zip(s_olds, gtot, upd)]
    return list(zip(y, s_new))


def _rwkv_scan_kernel(shf_ref, shb_ref, drf_ref, drb_ref, lwf_ref, lwb_ref, yf_ref, yb_ref, s_s):
    @pl.when(pl.program_id(1) == 0)
    def _():
        s_s[...] = jnp.zeros_like(s_s)

    chains, where = [], []
    for bb in range(RW_BATCH):
        for p in range(RW_PAIRS):
            chains.append((shf_ref[bb, 0, p], shf_ref[bb, 1, p], shf_ref[bb, 2, p], drf_ref[bb, 0, p],
                           drf_ref[bb, 1, p], lwf_ref[bb, p], s_s[bb, 0, p], False))
            where.append((yf_ref, bb, 0, p))
            chains.append((shb_ref[bb, 0, p], shb_ref[bb, 1, p], shb_ref[bb, 2, p], drb_ref[bb, 0, p],
                           drb_ref[bb, 1, p], lwb_ref[bb, p], s_s[bb, 1, p], True))
            where.append((yb_ref, bb, 1, p))
    for (y, s_new), (y_ref, bb, d, p) in zip(_rwkv_chunk_group(chains), where):
        y_ref[bb, p] = y.astype(y_ref.dtype)
        s_s[bb, d, p] = s_new


def _rwkv_scan_call(sh, dr, lw, lc):
    bsz, _, _, t, _ = sh.shape
    ln = RW_CHUNK
    nb = RW_BATCH
    assert bsz % nb == 0
    n = t // ln
    nctx = lc // ln
    rv = lambda i: _rev_chunk(i, nctx, n)
    return pl.pallas_call(
        _rwkv_scan_kernel,
        grid=(bsz // nb, n),
        in_specs=[pl.BlockSpec((nb, 3, RW_PAIRS, ln, LANES), lambda bi, i: (bi, 0, 0, i, 0)),
                  pl.BlockSpec((nb, 3, RW_PAIRS, ln, LANES), lambda bi, i: (bi, 0, 0, rv(i), 0)),
                  pl.BlockSpec((nb, None, 2, RW_PAIRS, ln, LANES), lambda bi, i: (bi, 0, 0, 0, i, 0)),
                  pl.BlockSpec((nb, None, 2, RW_PAIRS, ln, LANES), lambda bi, i: (bi, 1, 0, 0, rv(i), 0)),
                  pl.BlockSpec((nb, None, RW_PAIRS, ln, LANES), lambda bi, i: (bi, 0, 0, i, 0)),
                  pl.BlockSpec((nb, None, RW_PAIRS, ln, LANES), lambda bi, i: (bi, 1, 0, rv(i), 0))],
        out_specs=[pl.BlockSpec((nb, RW_PAIRS, ln, LANES), lambda bi, i: (bi, 0, i, 0)),
                   pl.BlockSpec((nb, RW_PAIRS, ln, LANES), lambda bi, i: (bi, 0, rv(i), 0))],
        out_shape=[jax.ShapeDtypeStruct((bsz, RW_PAIRS, t, LANES), BF16)] * 2,
        scratch_shapes=[pltpu.VMEM((nb, 2, RW_PAIRS, LANES, LANES), F32)],
        compiler_params=_params(("parallel", "arbitrary")),
        name="rwkv_scan",
    )(sh, sh, dr, dr, lw, lw)


def _rwkv_out_kernel(z_ref, yf_ref, yb_ref, gate_ref, bonus_ref, lng_ref, lnb_ref, e_ref, et_ref, w_ref, mod_ref,
                     g2_ref, w1_ref, w2_ref, o_ref):
    y = jnp.concatenate([yf_ref[p].astype(F32) + yb_ref[p].astype(F32) for p in range(RW_PAIRS)], axis=1)
    yc = y - _head_sum(y, e_ref, et_ref, 2) * (1.0 / RW_HEAD)
    var = _head_sum(yc * yc, e_ref, et_ref, 1) * (1.0 / RW_HEAD)
    yn = yc * lax.rsqrt(var + RW_LN_EPS) * lng_ref[...] + lnb_ref[...] + bonus_ref[...]
    out = _dot((yn * gate_ref[...]).astype(BF16), w_ref[...])
    o_ref[...] = _mlp_tail(z_ref[...] + mod_ref[2:3, :] * out, g2_ref[...], mod_ref, w1_ref, w2_ref)


def _rwkv_out_call(z, yf, yb, gate, bonus, ln_g, ln_b, e, et, w, mod, g2, w1, w2, nb, lc, skip):
    bsz, t, d = z.shape
    tm = TOKEN_TILE
    nctx = lc // tm
    nt = t // tm - skip
    tok = pl.BlockSpec((None, tm, d), lambda bi, i: (bi, i + skip, 0))
    pair = pl.BlockSpec((None, RW_PAIRS, tm, LANES), lambda bi, i: (bi, 0, i + skip, 0))
    return pl.pallas_call(
        _rwkv_out_kernel,
        grid=(bsz, nt),
        in_specs=[tok, pair, pair, tok, tok,
                  _const_spec((1, d)), _const_spec((1, d)), _const_spec((d, LANES)), _const_spec((LANES, d)),
                  _const_spec((d, d)),
                  pl.BlockSpec((None, N_MOD, d), lambda bi, i: (jnp.where(i + skip < nctx, nb, bi), 0, 0)),
                  _const_spec((1, d)),
                  _const_spec((d, D_FF)),
                  _const_spec((D_FF, d))],
        out_specs=pl.BlockSpec((None, tm, d), lambda bi, i: (bi, i, 0)),
        out_shape=jax.ShapeDtypeStruct((bsz, nt * tm, d), F32),
        compiler_params=_params(("parallel", "parallel")),
        name="rwkv_out_mlp",
    )(z, yf, yb, gate, bonus, ln_g, ln_b, e, et, w, mod, g2, w1, w2)


def _group_mean_matrix(n, width):
    idx = jnp.arange(n) // width
    return ((idx[:, None] == idx[None, :]).astype(F32) / width).astype(BF16)


def _rope_tables(lc, s):
    quarter = AT_DH // 4
    inv = ROPE_BASE ** (-jnp.arange(quarter, dtype=F32) / quarter)
    pos = jnp.arange(s)
    rpos = (pos // GRID_W).astype(F32)
    cpos = (pos % GRID_W).astype(F32)
    ang_r = rpos[:, None] * inv[None, :]
    ang_c = cpos[:, None] * inv[None, :]
    cos64 = jnp.concatenate([jnp.cos(ang_r), jnp.cos(ang_r), jnp.cos(ang_c), jnp.cos(ang_c)], axis=1)
    sin64 = jnp.concatenate([-jnp.sin(ang_r), jnp.sin(ang_r), -jnp.sin(ang_c), jnp.sin(ang_c)], axis=1)
    cos = jnp.concatenate([jnp.ones((lc, AT_DH), F32), cos64], axis=0)
    sin = jnp.concatenate([jnp.zeros((lc, AT_DH), F32), sin64], axis=0)
    return jnp.tile(cos, (1, 2)), jnp.tile(sin, (1, 2))


def _even_weights(w_in, b_in):
    def cols(m):
        mq, mk, mv, mo, mg, aq, ak, av = jnp.split(m, [512, 1024, 1536, 2048, 2064, 2576, 2704], axis=-1)
        dup = lambda u: jnp.concatenate([u[..., :64], u[..., :64], u[..., 64:], u[..., 64:]], axis=-1)
        mgp = jnp.pad(mg, [(0, 0)] * (m.ndim - 1) + [(0, LANES - 16)])
        return jnp.concatenate([mq, mk, mv, mo, aq, dup(ak), dup(av), mgp], axis=-1)
    return cols(w_in).astype(BF16), cols(b_in[None, :])


def kernel(x, c, ctx, c_ctx, ada_w, ada_b, norm1_g, norm2_g, mlp_w1, mlp_w2, ev_w_in, ev_b_in, ev_w_out, ml_f_bias, ml_out_g, at_q_g, at_k_g, at_sink, rw_mu, rw_w_rkv, rw_w0, rw_w1, rw_w2, rw_a0, rw_a1, rw_a2, rw_g1, rw_g2, rw_k_k, rw_k_a, rw_r_k, rw_ln_g, rw_ln_b, rw_w_out):
    bsz, s, d = x.shape
    lc = ctx.shape[1]
    depth = ada_w.shape[0]
    assert d == D_MODEL and lc == TOKEN_TILE and s % TOKEN_TILE == 0 and bsz < MOD_ROWS
    z = jnp.concatenate([ctx, x], axis=1)

    cstack = jnp.zeros((MOD_ROWS, d), F32).at[:bsz].set(c).at[bsz].set(c_ctx)
    mod_all = _ada_call(cstack, ada_w, ada_b).reshape(depth, MOD_ROWS, N_MOD, d)

    grp512 = _group_mean_matrix(AT_W, AT_DH)
    head_e = (jnp.arange(d)[:, None] // RW_HEAD == jnp.arange(LANES)[None, :]).astype(BF16)
    head_et = head_e.T
    cos_t, sin_t = _rope_tables(lc, s)
    gate_is_f = ((jnp.arange(LANES) % 8 >= 4) & (jnp.arange(LANES) < 16)).astype(F32)[None, :]

    for layer in range(depth):
        mod = mod_all[layer]
        j = layer // 2
        n1 = norm1_g[layer][None, :]
        mlp_args = (norm2_g[layer][None, :], mlp_w1[layer].astype(BF16), mlp_w2[layer].astype(BF16))
        skip = lc // TOKEN_TILE if layer == depth - 1 else 0
        if layer % 2 == 0:
            w_in, b_in = _even_weights(ev_w_in[j], ev_b_in[j])
            fb = jnp.zeros((LANES,), F32)
            for dr in range(2):
                fb = fb.at[8 * dr + 4:8 * dr + 8].set(ml_f_bias[j, dr])
            mqv, kt, mo, aq, akt, av, gates, gates_t = _even_in_call(
                z, n1, mod, w_in, b_in, jnp.tile(at_q_g[j], AT_HEADS)[None, :], jnp.tile(at_k_g[j], 4)[None, :],
                grp512, cos_t, sin_t, fb[None, :], gate_is_f, bsz)
            hf, hb = _mlstm_call(mqv, kt, gates, gates_t, lc)
            sink = jnp.broadcast_to(at_sink[j][:, None], (AT_HEADS, LANES))
            ya = _attn_call(aq, akt, av, sink, lc)
            z = _even_out_call(z, hf, hb, mo, ya, ml_out_g[j][None, :], ev_w_out[j].astype(BF16), mod, *mlp_args,
                               bsz, skip)
        else:
            pad_g = RW_GATE_PAD - RW_GATE_LORA
            zeros = jnp.zeros((RW_DECAY_LORA, d), F32)
            w2 = jnp.concatenate([jnp.concatenate([rw_w2[j, 0], zeros], axis=1),
                                  jnp.concatenate([zeros, rw_w2[j, 1]], axis=1)], axis=0)
            a2 = jnp.concatenate([jnp.concatenate([rw_a2[j, 0], zeros], axis=1),
                                  jnp.concatenate([zeros, rw_a2[j, 1]], axis=1)], axis=0)
            sh, dr_, lw, gate, bonus = _rwkv_in_call(
                z, n1, mod, rw_mu[j].reshape(12, d), rw_w_rkv[j].astype(BF16),
                jnp.concatenate([rw_w1[j, 0], rw_w1[j, 1]], axis=1).astype(BF16),
                jnp.concatenate([rw_a1[j, 0], rw_a1[j, 1]], axis=1).astype(BF16),
                jnp.pad(rw_g1[j], ((0, 0), (0, pad_g))).astype(BF16),
                w2.astype(BF16), a2.astype(BF16),
                jnp.pad(rw_g2[j], ((0, pad_g), (0, 0))).astype(BF16),
                rw_w0[j], rw_a0[j], rw_k_k[j][None, :], rw_k_a[j][None, :], rw_r_k[j].reshape(1, d),
                head_e, head_et, bsz, lc)
            yf, yb = _rwkv_scan_call(sh, dr_, lw, lc)
            z = _rwkv_out_call(z, yf, yb, gate, bonus, rw_ln_g[j][None, :], rw_ln_b[j][None, :], head_e, head_et,
                               rw_w_out[j].astype(BF16), mod, *mlp_args, bsz, lc, skip)
    return z
```

```python
import functools

import jax
import jax.numpy as jnp
from jax import lax
from jax.experimental import pallas as pl
from jax.experimental.pallas import tpu as pltpu

F32 = jnp.float32
BF16 = jnp.bfloat16

D_MODEL = 1024
N_MOD = 6
D_FF = 4 * D_MODEL
EPS = 1e-6
GRID_W = 64

ML_HEADS = 4
ML_D = 128
ML_W = ML_HEADS * ML_D
ML_CHUNK = 128
GATE_CAP = 15.0

AT_DH = 64
AT_HEADS = 8
AT_KV_HEADS = 2
AT_W = AT_HEADS * AT_DH
AT_BLOCK = 128
AT_KV_DUP = 2 * AT_KV_HEADS * AT_DH
AT_BATCH = 1
ROPE_BASE = 10000.0

RW_HEAD = 64
RW_PAIRS = D_MODEL // 128
RW_CHUNK = 64
RW_LN_EPS = 64e-5
RW_DECAY_LORA = 64
RW_AAA_LORA = 64
RW_GATE_LORA = 160
RW_GATE_PAD = 256
RW_DECAY_SCALE = 0.6065306597126334

LANES = 128
TOKEN_TILE = 256
FF_TILE = 1024
MOD_ROWS = 16
VMEM_LIMIT = 56 * 1024 * 1024
NEG = -1e30
RW_MM = BF16
RW_BATCH = 4
RW_ROW_TILE = 8

EV_MO = 3 * ML_W
EV_AQ = EV_MO + ML_W
EV_AK = EV_AQ + AT_W
EV_AV = EV_AK + AT_KV_DUP
EV_GATE = EV_AV + AT_KV_DUP
EV_COLS = EV_GATE + LANES
ML_GATE_COLS = 4 * ML_HEADS


def _dot(a, b):
    return jnp.dot(a, b, preferred_element_type=F32)


def _dot_tb(a, b):
    return lax.dot_general(a, b, (((1,), (1,)), ((), ())), preferred_element_type=F32)


def _dot_ta(a, b):
    return lax.dot_general(a, b, (((0,), (0,)), ((), ())), preferred_element_type=F32)


def _norm_mod(z, g, shift, scale):
    ms = jnp.mean(z * z, axis=-1, keepdims=True)
    return (z * lax.rsqrt(ms + EPS) * g) * (1.0 + scale) + shift


def _sigmoid(x):
    return 1.0 / (1.0 + jnp.exp(-x))


def _head_sum(x, e_ref, et_ref, terms):
    s = _dot(x.astype(BF16), e_ref[...])
    hi = s.astype(BF16)
    out = _dot(hi, et_ref[...])
    if terms > 1:
        out = out + _dot((s - hi.astype(F32)).astype(BF16), et_ref[...])
    return out


def _const_spec(shape):
    nd = len(shape)
    return pl.BlockSpec(shape, lambda *_: (0,) * nd, pipeline_mode=pl.Buffered(1))


def _params(sem):
    return pltpu.CompilerParams(dimension_semantics=sem, vmem_limit_bytes=VMEM_LIMIT)


def _ada_kernel(c_ref, w_ref, b_ref, o_ref):
    cv = c_ref[...]
    s = cv * _sigmoid(cv)
    o_ref[...] = jnp.dot(s, w_ref[...], preferred_element_type=F32,
                         precision=lax.Precision.HIGHEST) + b_ref[...]


def _ada_call(cstack, ada_w, ada_b):
    depth = ada_w.shape[0]
    n = N_MOD * D_MODEL
    tn = 1024
    return pl.pallas_call(
        _ada_kernel,
        grid=(depth, n // tn),
        in_specs=[pl.BlockSpec((MOD_ROWS, D_MODEL), lambda l, j: (0, 0)),
                  pl.BlockSpec((None, D_MODEL, tn), lambda l, j: (l, 0, j)),
                  pl.BlockSpec((None, 1, tn), lambda l, j: (l, 0, j))],
        out_specs=pl.BlockSpec((None, MOD_ROWS, tn), lambda l, j: (l, 0, j)),
        out_shape=jax.ShapeDtypeStruct((depth, MOD_ROWS, n), F32),
        compiler_params=_params(("parallel", "parallel")),
        name="ada_ln",
    )(cstack, ada_w, ada_b.reshape(depth, 1, n))


def _rope(x, cos, sin, lane_lo):
    n = x.shape[1]
    up = pltpu.roll(x, n - 16, 1)
    dn = pltpu.roll(x, 16, 1)
    reps = n // LANES
    c = jnp.concatenate([cos] * reps, axis=1) if reps > 1 else cos
    s = jnp.concatenate([sin] * reps, axis=1) if reps > 1 else sin
    lo = jnp.concatenate([lane_lo] * reps, axis=1) if reps > 1 else lane_lo
    return x * c + jnp.where(lo, up, dn) * s


def _even_in_kernel(z_ref, g_ref, mod_ref, w_ref, b_ref, gq_ref, gk_ref, grp_ref, cos_ref, sin_ref,
                    fb_ref, isf_ref, mqv_ref, kt_ref, mo_ref, aq_ref, akt_ref, av_ref, gate_ref, gatet_ref):
    h = _norm_mod(z_ref[...], g_ref[...], mod_ref[0:1, :], mod_ref[1:2, :])
    p = _dot(h.astype(BF16), w_ref[...]) + b_ref[...]
    mqv_ref[:, 0:ML_W] = p[:, 0:ML_W].astype(BF16)
    mqv_ref[:, ML_W:2 * ML_W] = p[:, 2 * ML_W:3 * ML_W].astype(BF16)
    kt_ref[...] = (p[:, ML_W:2 * ML_W] * (ML_D ** -0.5)).T.astype(BF16)
    mo_ref[...] = p[:, EV_MO:EV_MO + ML_W].astype(mo_ref.dtype)
    cos = cos_ref[...]
    sin = sin_ref[...]
    lane_lo = (lax.broadcasted_iota(jnp.int32, (1, LANES), 1) % 32) < 16
    q = p[:, EV_AQ:EV_AQ + AT_W]
    qms = _dot((q * q).astype(BF16), grp_ref[...])
    q = q * lax.rsqrt(qms + EPS) * gq_ref[...]
    aq_ref[...] = (_rope(q, cos, sin, lane_lo) * (AT_DH ** -0.5)).astype(BF16)
    k = p[:, EV_AK:EV_AK + AT_KV_DUP]
    kms = _dot((k * k).astype(BF16), grp_ref[0:AT_KV_DUP, 0:AT_KV_DUP])
    k = k * lax.rsqrt(kms + EPS) * gk_ref[...]
    akt_ref[...] = _rope(k, cos, sin, lane_lo).T.astype(BF16)
    av_ref[...] = p[:, EV_AV:EV_AV + AT_KV_DUP].astype(BF16)
    gt = p[:, EV_GATE:EV_GATE + LANES] + fb_ref[...]
    sc = GATE_CAP * jnp.tanh(gt * (1.0 / GATE_CAP))
    logsig = jnp.minimum(sc, 0.0) - jnp.log(1.0 + jnp.exp(-jnp.abs(sc)))
    gates = jnp.where(isf_ref[...] > 0.5, logsig, sc)
    gate_ref[...] = gates
    gatet_ref[...] = gates.T[0:ML_GATE_COLS, :]


def _even_in_call(z, g, mod, w, b, gq, gk, grp, cos, sin, fb, isf, nb):
    bsz, t, d = z.shape
    tm = TOKEN_TILE
    tok = lambda width: pl.BlockSpec((None, tm, width), lambda bi, i: (bi, i, 0))
    tpose = lambda rows: pl.BlockSpec((None, rows, tm), lambda bi, i: (bi, 0, i))
    return pl.pallas_call(
        _even_in_kernel,
        grid=(bsz, t // tm),
        in_specs=[tok(d),
                  _const_spec((1, d)),
                  pl.BlockSpec((None, N_MOD, d), lambda bi, i: (jnp.where(i == 0, nb, bi), 0, 0)),
                  _const_spec((d, EV_COLS)),
                  _const_spec((1, EV_COLS)),
                  _const_spec((1, AT_W)),
                  _const_spec((1, AT_KV_DUP)),
                  _const_spec((AT_W, AT_W)),
                  pl.BlockSpec((tm, LANES), lambda bi, i: (i, 0)),
                  pl.BlockSpec((tm, LANES), lambda bi, i: (i, 0)),
                  _const_spec((1, LANES)),
                  _const_spec((1, LANES))],
        out_specs=[tok(2 * ML_W), tpose(ML_W), tok(ML_W), tok(AT_W), tpose(AT_KV_DUP), tok(AT_KV_DUP), tok(LANES),
                   tpose(ML_GATE_COLS)],
        out_shape=[jax.ShapeDtypeStruct((bsz, t, 2 * ML_W), BF16),
                   jax.ShapeDtypeStruct((bsz, ML_W, t), BF16),
                   jax.ShapeDtypeStruct((bsz, t, ML_W), BF16),
                   jax.ShapeDtypeStruct((bsz, t, AT_W), BF16),
                   jax.ShapeDtypeStruct((bsz, AT_KV_DUP, t), BF16),
                   jax.ShapeDtypeStruct((bsz, t, AT_KV_DUP), BF16),
                   jax.ShapeDtypeStruct((bsz, t, LANES), F32),
                   jax.ShapeDtypeStruct((bsz, ML_GATE_COLS, t), F32)],
        compiler_params=_params(("parallel", "parallel")),
        name="even_in",
    )(z, g, mod, w, b, gq, gk, grp, cos, sin, fb, isf)


def _split3(x):
    hi = x.astype(BF16)
    r1 = x - hi.astype(F32)
    mid = r1.astype(BF16)
    return hi, mid, (r1 - mid.astype(F32)).astype(BF16)


def _mlstm_kernel(qf_ref, ktf_ref, vf_ref, qb_ref, ktb_ref, vb_ref, gcf_ref, gcb_ref, grf_ref, grb_ref, sel_ref,
                  hf_ref, hb_ref, c_s, m_s):
    ln = ML_CHUNK

    @pl.when(pl.program_id(1) == 0)
    def _():
        c_s[...] = jnp.zeros_like(c_s)
        m_s[...] = jnp.zeros_like(m_s)

    ri = lax.broadcasted_iota(jnp.int32, (ln, ln), 0)
    ci = lax.broadcasted_iota(jnp.int32, (ln, ln), 1)
    hi = lax.Precision.HIGHEST
    before = (ci <= ri, ci >= ri)
    refs = ((qf_ref, ktf_ref, vf_ref, gcf_ref, grf_ref, hf_ref), (qb_ref, ktb_ref, vb_ref, gcb_ref, grb_ref, hb_ref))
    cum_r = [jnp.dot(refs[d][4][...], before[1 - d].astype(F32), preferred_element_type=F32, precision=hi)
             for d in range(2)]
    bc_all = []
    for d in range(2):
        cum_c = sum(_dot(before[d].astype(BF16), part) for part in _split3(refs[d][3][...]))
        sel = sel_ref[d]
        bc_all.append(sum(_dot(part, sel) for part in _split3(cum_c)))
    chains = [(d, hd) for d in range(2) for hd in range(ML_HEADS)]
    lanes = [slice(hd * ML_D, (hd + 1) * ML_D) for _, hd in chains]
    st = [d * ML_HEADS + hd for d, hd in chains]
    ones = jnp.ones((ln, ML_D), BF16)
    q = [refs[d][0][:, ls] for (d, _), ls in zip(chains, lanes)]
    kt = [refs[d][1][ls, :] for (d, _), ls in zip(chains, lanes)]
    v1 = [jnp.concatenate([refs[d][2][:, ls], ones], axis=1) for (d, _), ls in zip(chains, lanes)]
    i_row = [refs[d][4][8 * d + hd:8 * d + hd + 1, :] for d, hd in chains]
    f_row = [refs[d][4][8 * d + 4 + hd:8 * d + 5 + hd, :] for d, hd in chains]
    bcum_row = [cum_r[d][8 * d + 4 + hd:8 * d + 5 + hd, :] for d, hd in chains]
    bcum = [bc_all[d][:, ls] for (d, _), ls in zip(chains, lanes)]
    m_old = [m_s[s:s + 1, 0:1] for s in st]
    c_old = [c_s[s] for s in st]
    qk = [_dot(a, b) for a, b in zip(q, kt)]
    r2 = [_dot(a, c.astype(BF16)) for a, c in zip(q, c_old)]
    dlog = [jnp.where(before[d], bc - br + ir, NEG) for (d, _), bc, br, ir in zip(chains, bcum, bcum_row, i_row)]
    inter = [bc + m for bc, m in zip(bcum, m_old)]
    mt = [jnp.maximum(jnp.max(dl, axis=1, keepdims=True), it) for dl, it in zip(dlog, inter)]
    sc = [x * jnp.exp(dl - m) for x, dl, m in zip(qk, dlog, mt)]
    iw = [jnp.exp(it - m) for it, m in zip(inter, mt)]
    r1 = [_dot(s.astype(BF16), b) for s, b in zip(sc, v1)]
    for j, (d, _) in enumerate(chains):
        tot = r1[j] + jnp.concatenate([iw[j], iw[j]], axis=1) * r2[j]
        den = jnp.maximum(jnp.abs(tot[:, ML_D:]), jnp.exp(-mt[j]))
        refs[d][5][:, lanes[j]] = (tot[:, :ML_D] / den).astype(refs[d][5].dtype)
    bl = [jnp.sum(fr, axis=1, keepdims=True) for fr in f_row]
    ws_log = [b - br + ir for b, br, ir in zip(bl, bcum_row, i_row)]
    m_new = [jnp.maximum(b + m, jnp.max(w, axis=1, keepdims=True)) for b, m, w in zip(bl, m_old, ws_log)]
    ws = [jnp.exp(w - m) for w, m in zip(ws_log, m_new)]
    upd = [_dot((a.astype(F32) * w).astype(BF16), b) for a, w, b in zip(kt, ws, v1)]
    for j, s in enumerate(st):
        c_s[s] = jnp.exp(bl[j] + m_old[j] - m_new[j]) * c_old[j] + upd[j]
        m_s[s:s + 1, :] = jnp.broadcast_to(m_new[j], (1, LANES))


def _rev_chunk(i, nctx, n):
    return jnp.where(i < nctx, nctx - 1 - i, n + nctx - 1 - i)


def _mlstm_call(mqv, kt, gates, gates_t, lc):
    bsz, t, _ = mqv.shape
    ln = ML_CHUNK
    assert ln == ML_D == LANES
    n = t // ln
    nctx = lc // ln
    rv = lambda i: _rev_chunk(i, nctx, n)
    col = jnp.arange(LANES)[None, :, None]
    want = (8 * jnp.arange(2)[:, None, None] + 4 + jnp.arange(ML_HEADS * ln)[None, None, :] // ln)
    sel = (col == want).astype(BF16)
    return pl.pallas_call(
        _mlstm_kernel,
        grid=(bsz, n),
        in_specs=[pl.BlockSpec((None, ln, ML_W), lambda bi, i: (bi, i, 0)),
                  pl.BlockSpec((None, ML_W, ln), lambda bi, i: (bi, 0, i)),
                  pl.BlockSpec((None, ln, ML_W), lambda bi, i: (bi, i, 1)),
                  pl.BlockSpec((None, ln, ML_W), lambda bi, i: (bi, rv(i), 0)),
                  pl.BlockSpec((None, ML_W, ln), lambda bi, i: (bi, 0, rv(i))),
                  pl.BlockSpec((None, ln, ML_W), lambda bi, i: (bi, rv(i), 1)),
                  pl.BlockSpec((None, ln, LANES), lambda bi, i: (bi, i, 0)),
                  pl.BlockSpec((None, ln, LANES), lambda bi, i: (bi, rv(i), 0)),
                  pl.BlockSpec((None, ML_GATE_COLS, ln), lambda bi, i: (bi, 0, i)),
                  pl.BlockSpec((None, ML_GATE_COLS, ln), lambda bi, i: (bi, 0, rv(i))),
                  _const_spec((2, LANES, ML_HEADS * ln))],
        out_specs=[pl.BlockSpec((None, ln, ML_W), lambda bi, i: (bi, i, 0)),
                   pl.BlockSpec((None, ln, ML_W), lambda bi, i: (bi, rv(i), 0))],
        out_shape=[jax.ShapeDtypeStruct((bsz, t, ML_W), BF16)] * 2,
        scratch_shapes=[pltpu.VMEM((2 * ML_HEADS, ML_D, 2 * ML_D), F32),
                        pltpu.VMEM((2 * ML_HEADS, LANES), F32)],
        compiler_params=_params(("parallel", "arbitrary")),
        name="mlstm_scan",
    )(mqv, kt, mqv, mqv, kt, mqv, gates, gates, gates_t, gates_t, sel)


def _attn_kernel(q_ref, ktp_ref, kto_ref, ktn_ref, ktc_ref, vp_ref, vo_ref, vn_ref, vc_ref, sink_ref, o_ref,
                 *, nctx, n):
    j = pl.program_id(1)
    blk = AT_BLOCK
    latent = j >= nctx
    qi = lax.broadcasted_iota(jnp.int32, (blk, blk), 0)
    ki = lax.broadcasted_iota(jnp.int32, (blk, blk), 1)
    ok_prev = jnp.logical_and(jnp.logical_and(latent, j > nctx), ki >= qi)
    ok_own = jnp.logical_and(latent, ki >= 0)
    ok_next = jnp.logical_and(jnp.logical_and(latent, j < n - 1), ki <= qi)
    lane = lax.broadcasted_iota(jnp.int32, (1, LANES), 1)
    sub = lax.broadcasted_iota(jnp.int32, (LANES, 1), 0)
    lane_half = (lane < AT_DH, lane >= AT_DH)
    sub_half = (sub < AT_DH, sub >= AT_DH)
    kts = (ktp_ref, kto_ref, ktn_ref, ktc_ref)
    vs = (vp_ref, vo_ref, vn_ref, vc_ref)
    oks = (ok_prev, ok_own, ok_next, None)
    zero = jnp.zeros((), BF16)
    nbat = q_ref.shape[0]
    kx, vx = {}, {}
    for bb in range(nbat):
        for g in range(AT_KV_HEADS):
            rows = slice(g * LANES, (g + 1) * LANES)
            for e in range(2):
                kx[bb, g, e] = [jnp.where(sub_half[e], r[bb, rows, :], zero) for r in kts]
                vx[bb, g, e] = [jnp.concatenate([jnp.where(lane_half[e], r[bb, :, rows], zero),
                                                 jnp.ones((r.shape[1], LANES), BF16)], axis=1) for r in vs]
    heads = [(bb, hd // 2, hd % 2, hd // 4) for bb in range(nbat) for hd in range(AT_HEADS)]
    qc = {(bb, c): q_ref[bb, :, c * LANES:(c + 1) * LANES] for bb in range(nbat) for c in range(AT_HEADS // 2)}
    sink = [sink_ref[2 * c + e:2 * c + e + 1, 0:1] for _, c, e, _ in heads]
    ss = [[_dot(qc[bb, c], kx[bb, g, e][p]) if oks[p] is None
           else jnp.where(oks[p], _dot(qc[bb, c], kx[bb, g, e][p]), NEG) for p in range(4)]
          for bb, c, e, g in heads]
    m = []
    for s4, sk in zip(ss, sink):
        band = jnp.maximum(jnp.maximum(s4[0], s4[1]), s4[2])
        m.append(jnp.maximum(jnp.maximum(jnp.max(band, axis=1, keepdims=True),
                                         jnp.max(s4[3], axis=1, keepdims=True)), sk))
    res = []
    for s4, mm, (bb, c, e, g) in zip(ss, m, heads):
        acc = None
        for p in range(4):
            pv = _dot(jnp.exp(s4[p] - mm).astype(BF16), vx[bb, g, e][p])
            acc = pv if acc is None else acc + pv
        res.append(acc)
    outs = [r[:, :LANES] / (r[:, LANES:] + jnp.exp(sk - mm)) for r, sk, mm in zip(res, sink, m)]
    for i in range(0, len(heads), 2):
        bb, c = heads[i][0], heads[i][1]
        o_ref[bb, :, c * LANES:(c + 1) * LANES] = (outs[i] + outs[i + 1]).astype(o_ref.dtype)


def _attn_call(aq, akt, av, sink, lc):
    bsz, t, _ = aq.shape
    blk = AT_BLOCK
    nb = AT_BATCH
    assert bsz % nb == 0
    n = t // blk
    nctx = lc // blk
    prev = lambda j: jnp.clip(j - 1, nctx, n - 1)
    own = lambda j: jnp.clip(j, nctx, n - 1)
    nxt = lambda j: jnp.clip(j + 1, nctx, n - 1)
    kt = lambda f: pl.BlockSpec((nb, AT_KV_DUP, blk), lambda bi, j: (bi, 0, f(j)))
    vv = lambda f: pl.BlockSpec((nb, blk, AT_KV_DUP), lambda bi, j: (bi, f(j), 0))
    return pl.pallas_call(
        functools.partial(_attn_kernel, nctx=nctx, n=n),
        grid=(bsz // nb, n),
        in_specs=[pl.BlockSpec((nb, blk, AT_W), lambda bi, j: (bi, j, 0)),
                  kt(prev), kt(own), kt(nxt),
                  pl.BlockSpec((nb, AT_KV_DUP, lc), lambda bi, j: (bi, 0, 0)),
                  vv(prev), vv(own), vv(nxt),
                  pl.BlockSpec((nb, lc, AT_KV_DUP), lambda bi, j: (bi, 0, 0)),
                  _const_spec((AT_HEADS, LANES))],
        out_specs=pl.BlockSpec((nb, blk, AT_W), lambda bi, j: (bi, j, 0)),
        out_shape=jax.ShapeDtypeStruct((bsz, t, AT_W), BF16),
        compiler_params=_params(("parallel", "parallel")),
        name="window_attn",
    )(aq, akt, akt, akt, akt, av, av, av, av, sink)


def _mlp_tail(z, g, mod_ref, w1_ref, w2_ref):
    h = _norm_mod(z, g, mod_ref[3:4, :], mod_ref[4:5, :]).astype(BF16)
    acc = jnp.zeros(z.shape, F32)
    for f in range(D_FF // FF_TILE):
        a = jnp.maximum(_dot(h, w1_ref[:, f * FF_TILE:(f + 1) * FF_TILE]), 0.0)
        acc = acc + _dot((a * a).astype(BF16), w2_ref[f * FF_TILE:(f + 1) * FF_TILE, :])
    return z + mod_ref[5:6, :] * acc


def _even_out_kernel(z_ref, hf_ref, hb_ref, mo_ref, ya_ref, og_ref, w_ref, mod_ref, g2_ref, w1_ref, w2_ref, o_ref):
    hs = hf_ref[...].astype(F32) + hb_ref[...].astype(F32)
    parts = []
    for hd in range(ML_HEADS):
        x = hs[:, hd * ML_D:(hd + 1) * ML_D]
        parts.append(x * lax.rsqrt(jnp.mean(x * x, axis=1, keepdims=True) + EPS))
    ym = jnp.concatenate(parts, axis=1) * og_ref[...] * _sigmoid(mo_ref[...].astype(F32))
    y = _dot(ym.astype(BF16), w_ref[0:ML_W, :]) + _dot(ya_ref[...], w_ref[ML_W:ML_W + AT_W, :])
    o_ref[...] = _mlp_tail(z_ref[...] + mod_ref[2:3, :] * y, g2_ref[...], mod_ref, w1_ref, w2_ref)


def _even_out_call(z, hf, hb, mo, ya, og, w, mod, g2, w1, w2, nb, skip):
    bsz, t, d = z.shape
    tm = TOKEN_TILE
    nt = t // tm - skip
    tok = lambda width: pl.BlockSpec((None, tm, width), lambda bi, i: (bi, i + skip, 0))
    return pl.pallas_call(
        _even_out_kernel,
        grid=(bsz, nt),
        in_specs=[tok(d), tok(ML_W), tok(ML_W), tok(ML_W), tok(AT_W),
                  _const_spec((1, ML_W)),
                  _const_spec((ML_W + AT_W, d)),
                  pl.BlockSpec((None, N_MOD, d), lambda bi, i: (jnp.where(i + skip == 0, nb, bi), 0, 0)),
                  _const_spec((1, d)),
                  _const_spec((d, D_FF)),
                  _const_spec((D_FF, d))],
        out_specs=pl.BlockSpec((None, tm, d), lambda bi, i: (bi, i, 0)),
        out_shape=jax.ShapeDtypeStruct((bsz, nt * tm, d), F32),
        compiler_params=_params(("parallel", "parallel")),
        name="even_out_mlp",
    )(z, hf, hb, mo, ya, og, w, mod, g2, w1, w2)


def _rwkv_in_kernel(z_ref, zp_ref, zn_ref, g_ref, mod_ref, mu_ref, wrkv_ref, w1_ref, a1_ref, g1_ref,
                    w2_ref, a2_ref, g2_ref, w0_ref, a0_ref, kk_ref, ka_ref, rk_ref, e_ref, et_ref,
                    sh_ref, dr_ref, lw_ref, gate_ref, bonus_ref, *, nctx, ntile):
    i = pl.program_id(1)
    tm, d = z_ref.shape
    g = g_ref[...]
    shift = mod_ref[0:1, :]
    scale = mod_ref[1:2, :]
    h = _norm_mod(z_ref[...], g, shift, scale)
    no_prev = jnp.logical_or(i == 0, i == nctx)
    no_next = jnp.logical_or(i == nctx - 1, i == ntile - 1)
    hp = jnp.where(no_prev, 0.0, _norm_mod(zp_ref[7:8, :], g, shift, scale))
    hn = jnp.where(no_next, 0.0, _norm_mod(zn_ref[0:1, :], g, shift, scale))
    row = lax.broadcasted_iota(jnp.int32, (tm, 1), 0)
    dp = jnp.where(row == 0, hp, pltpu.roll(h, 1, 0)) - h
    dn = jnp.where(row == tm - 1, hn, pltpu.roll(h, tm - 1, 0)) - h

    hb, dpb, dnb = h.astype(BF16), dp.astype(BF16), dn.astype(BF16)
    mub = mu_ref[...].astype(BF16)

    def mix(n):
        return hb + mub[2 * n:2 * n + 1, :] * dpb + mub[2 * n + 1:2 * n + 2, :] * dnb

    r = _dot(mix(0), wrkv_ref[0])
    k = _dot(mix(2), wrkv_ref[1])
    v = _dot(mix(3), wrkv_ref[2])
    gate_ref[...] = _dot(_sigmoid(_dot(mix(5), g1_ref[...])).astype(BF16), g2_ref[...]).astype(gate_ref.dtype)
    lora_w = _dot(jnp.tanh(_dot(mix(1), w1_ref[...])).astype(BF16), w2_ref[...])
    lora_a = _dot(_dot(mix(4), a1_ref[...]).astype(BF16), a2_ref[...])
    kkr = k * kk_ref[...]
    ssq = _head_sum(kkr * kkr, e_ref, et_ref, 1)
    kk = kkr * lax.rsqrt(jnp.maximum(ssq, 1e-24))
    kd_sum = None
    for dr in range(2):
        cols = slice(dr * d, (dr + 1) * d)
        lw = -RW_DECAY_SCALE * _sigmoid(w0_ref[dr:dr + 1, :] + lora_w[:, cols])
        a = _sigmoid(a0_ref[dr:dr + 1, :] + lora_a[:, cols])
        kd = k * (1.0 + (a - 1.0) * ka_ref[...])
        bvec = kk * a
        kd_sum = kd if kd_sum is None else kd_sum + kd
        for p in range(RW_PAIRS):
            ls = slice(p * LANES, (p + 1) * LANES)
            lw_ref[dr, p] = lw[:, ls]
            dr_ref[dr, 0, p] = kd[:, ls].astype(BF16)
            dr_ref[dr, 1, p] = bvec[:, ls].astype(BF16)
    for p in range(RW_PAIRS):
        ls = slice(p * LANES, (p + 1) * LANES)
        sh_ref[0, p] = r[:, ls].astype(BF16)
        sh_ref[1, p] = v[:, ls].astype(BF16)
        sh_ref[2, p] = kk[:, ls].astype(BF16)
    bsum = _head_sum(r * kd_sum * rk_ref[...], e_ref, et_ref, 1)
    bonus_ref[...] = (bsum * v).astype(bonus_ref.dtype)


def _rwkv_in_call(z, g, mod, mu, wrkv, w1, a1, g1, w2, a2, g2, w0, a0, k_k, k_a, r_k, e, et, nb, lc):
    bsz, t, d = z.shape
    tm = TOKEN_TILE
    ntile = t // tm
    nctx = lc // tm
    r8 = tm // 8
    tok = pl.BlockSpec((None, tm, d), lambda bi, i: (bi, i, 0))
    return pl.pallas_call(
        functools.partial(_rwkv_in_kernel, nctx=nctx, ntile=ntile),
        grid=(bsz, ntile),
        in_specs=[tok,
                  pl.BlockSpec((None, 8, d), lambda bi, i: (bi, jnp.maximum(i * r8 - 1, 0), 0)),
                  pl.BlockSpec((None, 8, d), lambda bi, i: (bi, jnp.minimum((i + 1) * r8, t // 8 - 1), 0)),
                  _const_spec((1, d)),
                  pl.BlockSpec((None, N_MOD, d), lambda bi, i: (jnp.where(i < nctx, nb, bi), 0, 0)),
                  _const_spec((12, d)),
                  _const_spec((3, d, d)),
                  _const_spec((d, LANES)),
                  _const_spec((d, LANES)),
                  _const_spec((d, RW_GATE_PAD)),
                  _const_spec((LANES, 2 * d)),
                  _const_spec((LANES, 2 * d)),
                  _const_spec((RW_GATE_PAD, d)),
                  _const_spec((2, d)),
                  _const_spec((2, d)),
                  _const_spec((1, d)),
                  _const_spec((1, d)),
                  _const_spec((1, d)),
                  _const_spec((d, LANES)),
                  _const_spec((LANES, d))],
        out_specs=[pl.BlockSpec((None, 3, RW_PAIRS, tm, LANES), lambda bi, i: (bi, 0, 0, i, 0)),
                   pl.BlockSpec((None, 2, 2, RW_PAIRS, tm, LANES), lambda bi, i: (bi, 0, 0, 0, i, 0)),
                   pl.BlockSpec((None, 2, RW_PAIRS, tm, LANES), lambda bi, i: (bi, 0, 0, i, 0)),
                   tok, tok],
        out_shape=[jax.ShapeDtypeStruct((bsz, 3, RW_PAIRS, t, LANES), BF16),
                   jax.ShapeDtypeStruct((bsz, 2, 2, RW_PAIRS, t, LANES), BF16),
                   jax.ShapeDtypeStruct((bsz, 2, RW_PAIRS, t, LANES), F32),
                   jax.ShapeDtypeStruct((bsz, t, d), BF16),
                   jax.ShapeDtypeStruct((bsz, t, d), BF16)],
        compiler_params=_params(("parallel", "parallel")),
        name="rwkv_in",
    )(z, z, z, g, mod, mu, wrkv, w1, a1, g1, w2, a2, g2, w0, a0, k_k, k_a, r_k, e, et)


def _rwkv_chunk_group(chains):
    ln = RW_CHUNK
    row = lax.broadcasted_iota(jnp.int32, (ln, 2 * ln), 0)
    col = lax.broadcasted_iota(jnp.int32, (ln, 2 * ln), 1)
    sidx = jnp.where(col >= ln, col - ln, col)
    lane_a = lax.broadcasted_iota(jnp.int32, (1, LANES), 1) < RW_HEAD
    tr = lax.broadcasted_iota(jnp.int32, (ln, ln), 0)
    tc = lax.broadcasted_iota(jnp.int32, (ln, ln), 1)
    tri = {False: (tc <= tr).astype(F32), True: (tc >= tr).astype(F32)}
    strict = {False: sidx < row, True: sidx > row}
    incl = {False: sidx <= row, True: sidx >= row}
    eye_w = jnp.where(sidx == row, 1.0, 0.0)
    diff = row ^ sidx
    vr = lax.broadcasted_iota(jnp.int32, (LANES, LANES), 0) < RW_HEAD
    kc = lax.broadcasted_iota(jnp.int32, (LANES, LANES), 1) < RW_HEAD
    same_head = vr == kc

    def bd(x):
        zero = jnp.zeros((), x.dtype)
        return jnp.concatenate([jnp.where(lane_a, x, zero), jnp.where(lane_a, zero, x)], axis=0)

    revs = [c[7] for c in chains]
    vs = [c[1] for c in chains]
    s_olds = [c[6] for c in chains]
    gcum = [sum(_dot(tri[c[7]].astype(BF16), part) for part in _split3(c[5])) for c in chains]
    gtot = [jnp.sum(c[5], axis=0, keepdims=True) for c in chains]

    def scaled(c, g, gt):
        r, v, kk, kd, bv, lw = (x.astype(F32) for x in c[:6])
        e_pos = jnp.exp(g)
        e_neg = jnp.exp(-g)
        e_end = jnp.exp(gt - g)
        ar = jnp.concatenate([(-kk * jnp.exp(g - lw)).astype(RW_MM), (r * e_pos).astype(RW_MM)], axis=0)
        bk_end = jnp.concatenate([(bv * e_end).astype(RW_MM), (kd * e_end).astype(RW_MM)], axis=0)
        return ar, (bv * e_neg).astype(RW_MM), (kd * e_neg).astype(RW_MM), bk_end

    sc = [scaled(c, g, gt) for c, g, gt in zip(chains, gcum, gtot)]
    ars_ = [x[0] for x in sc]
    x_bk = [_dot_tb(x[0], jnp.concatenate([bd(x[1]), bd(x[2])], axis=0)) for x in sc]
    x_b = [x[:, :2 * ln] for x in x_bk]
    x_k = [x[:, 2 * ln:] for x in x_bk]
    n_w =[jnp.where(strict[rv], x[:ln], 0.0) for x, rv in zip(x_b, revs)]
    m_rb = [jnp.where(incl[rv], x[ln:], 0.0).astype(RW_MM) for x, rv in zip(x_b, revs)]
    m_k = [jnp.concatenate([jnp.where(strict[rv], x[:ln], 0.0), jnp.where(incl[rv], x[ln:], 0.0)],
                           axis=0).astype(RW_MM) for x, rv in zip(x_k, revs)]
    x_w = [eye_w + jnp.where(diff == 1, n, 0.0) for n in n_w]

    def take_rows(x, h, odd):
        return jnp.concatenate([x[b * h:(b + 1) * h] for b in range(ln // h) if (b % 2 == 1) == odd], axis=0)

    def put_rows(base, upd, h, odd):
        parts, j = [], 0
        for b in range(ln // h):
            if (b % 2 == 1) == odd:
                blk = upd[j * h:(j + 1) * h]
                parts.append(blk if base is None else base[b * h:(b + 1) * h] + blk)
                j += 1
            else:
                parts.append(jnp.zeros((h, upd.shape[1]), upd.dtype) if base is None else base[b * h:(b + 1) * h])
        return jnp.concatenate(parts, axis=0)

    h = 2
    while h < ln:
        lvl = jnp.logical_and(diff >= h, diff < 2 * h)
        n_l = [jnp.where(lvl, n, 0.0) for n in n_w]
        if h < RW_ROW_TILE:
            tmp = [_dot(n.astype(RW_MM), bd(x.astype(RW_MM))) for n, x in zip(n_l, x_w)]
            x_w = [x + _dot(x.astype(RW_MM), bd(t.astype(RW_MM))) for x, t in zip(x_w, tmp)]
        else:
            odd = [not rv for rv in revs]
            tmp = [_dot(take_rows(n, h, o).astype(RW_MM), bd(x.astype(RW_MM))) for n, x, o in zip(n_l, x_w, odd)]
            tmp = [put_rows(None, t, h, o) for t, o in zip(tmp, odd)]
            cor = [_dot(take_rows(x, h, o).astype(RW_MM), bd(t.astype(RW_MM))) for x, t, o in zip(x_w, tmp, odd)]
            x_w = [put_rows(x, c, h, o) for x, c, o in zip(x_w, cor, odd)]
        h *= 2
    gy = [_dot(jnp.concatenate([m, a], axis=1), jnp.concatenate([bd(v), s.T.astype(RW_MM)], axis=0))
          for m, a, v, s in zip(m_k, ars_, vs, s_olds)]
    u = [_dot(x.astype(RW_MM), bd(g[:ln].astype(RW_MM))) for x, g in zip(x_w, gy)]
    y = [g[ln:] + _dot(rb, bd(uu.astype(RW_MM))) for g, rb, uu in zip(gy, m_rb, u)]
    upd = [_dot_ta(jnp.concatenate([uu.astype(RW_MM), v], axis=0), x[3]) for uu, v, x in zip(u, vs, sc)]
    s_new = [s * jnp.exp(gt) + jnp.where(same_head, up, 0.0) for s, gt, up in zip(s_olds, gtot, upd)]
    return list(zip(y, s_new))


def _rwkv_scan_kernel(shf_ref, shb_ref, drf_ref, drb_ref, lwf_ref, lwb_ref, yf_ref, yb_ref, s_s):
    @pl.when(pl.program_id(1) == 0)
    def _():
        s_s[...] = jnp.zeros_like(s_s)

    chains, where = [], []
    for bb in range(RW_BATCH):
        for p in range(RW_PAIRS):
            chains.append((shf_ref[bb, 0, p], shf_ref[bb, 1, p], shf_ref[bb, 2, p], drf_ref[bb, 0, p],
                           drf_ref[bb, 1, p], lwf_ref[bb, p], s_s[bb, 0, p], False))
            where.append((yf_ref, bb, 0, p))
            chains.append((shb_ref[bb, 0, p], shb_ref[bb, 1, p], shb_ref[bb, 2, p], drb_ref[bb, 0, p],
                           drb_ref[bb, 1, p], lwb_ref[bb, p], s_s[bb, 1, p], True))
            where.append((yb_ref, bb, 1, p))
    for (y, s_new), (y_ref, bb, d, p) in zip(_rwkv_chunk_group(chains), where):
        y_ref[bb, p] = y.astype(y_ref.dtype)
        s_s[bb, d, p] = s_new


def _rwkv_scan_call(sh, dr, lw, lc):
    bsz, _, _, t, _ = sh.shape
    ln = RW_CHUNK
    nb = RW_BATCH
    assert bsz % nb == 0
    n = t // ln
    nctx = lc // ln
    rv = lambda i: _rev_chunk(i, nctx, n)
    return pl.pallas_call(
        _rwkv_scan_kernel,
        grid=(bsz // nb, n),
        in_specs=[pl.BlockSpec((nb, 3, RW_PAIRS, ln, LANES), lambda bi, i: (bi, 0, 0, i, 0)),
                  pl.BlockSpec((nb, 3, RW_PAIRS, ln, LANES), lambda bi, i: (bi, 0, 0, rv(i), 0)),
                  pl.BlockSpec((nb, None, 2, RW_PAIRS, ln, LANES), lambda bi, i: (bi, 0, 0, 0, i, 0)),
                  pl.BlockSpec((nb, None, 2, RW_PAIRS, ln, LANES), lambda bi, i: (bi, 1, 0, 0, rv(i), 0)),
                  pl.BlockSpec((nb, None, RW_PAIRS, ln, LANES), lambda bi, i: (bi, 0, 0, i, 0)),
                  pl.BlockSpec((nb, None, RW_PAIRS, ln, LANES), lambda bi, i: (bi, 1, 0, rv(i), 0))],
        out_specs=[pl.BlockSpec((nb, RW_PAIRS, ln, LANES), lambda bi, i: (bi, 0, i, 0)),
                   pl.BlockSpec((nb, RW_PAIRS, ln, LANES), lambda bi, i: (bi, 0, rv(i), 0))],
        out_shape=[jax.ShapeDtypeStruct((bsz, RW_PAIRS, t, LANES), BF16)] * 2,
        scratch_shapes=[pltpu.VMEM((nb, 2, RW_PAIRS, LANES, LANES), F32)],
        compiler_params=_params(("parallel", "arbitrary")),
        name="rwkv_scan",
    )(sh, sh, dr, dr, lw, lw)


def _rwkv_out_kernel(z_ref, yf_ref, yb_ref, gate_ref, bonus_ref, lng_ref, lnb_ref, e_ref, et_ref, w_ref, mod_ref,
                     g2_ref, w1_ref, w2_ref, o_ref):
    y = jnp.concatenate([yf_ref[p].astype(F32) + yb_ref[p].astype(F32) for p in range(RW_PAIRS)], axis=1)
    yc = y - _head_sum(y, e_ref, et_ref, 2) * (1.0 / RW_HEAD)
    var = _head_sum(yc * yc, e_ref, et_ref, 1) * (1.0 / RW_HEAD)
    yn = yc * lax.rsqrt(var + RW_LN_EPS) * lng_ref[...] + lnb_ref[...] + bonus_ref[...]
    out = _dot((yn * gate_ref[...]).astype(BF16), w_ref[...])
    o_ref[...] = _mlp_tail(z_ref[...] + mod_ref[2:3, :] * out, g2_ref[...], mod_ref, w1_ref, w2_ref)


def _rwkv_out_call(z, yf, yb, gate, bonus, ln_g, ln_b, e, et, w, mod, g2, w1, w2, nb, lc, skip):
    bsz, t, d = z.shape
    tm = TOKEN_TILE
    nctx = lc // tm
    nt = t // tm - skip
    tok = pl.BlockSpec((None, tm, d), lambda bi, i: (bi, i + skip, 0))
    pair = pl.BlockSpec((None, RW_PAIRS, tm, LANES), lambda bi, i: (bi, 0, i + skip, 0))
    return pl.pallas_call(
        _rwkv_out_kernel,
        grid=(bsz, nt),
        in_specs=[tok, pair, pair, tok, tok,
                  _const_spec((1, d)), _const_spec((1, d)), _const_spec((d, LANES)), _const_spec((LANES, d)),
                  _const_spec((d, d)),
                  pl.BlockSpec((None, N_MOD, d), lambda bi, i: (jnp.where(i + skip < nctx, nb, bi), 0, 0)),
                  _const_spec((1, d)),
                  _const_spec((d, D_FF)),
                  _const_spec((D_FF, d))],
        out_specs=pl.BlockSpec((None, tm, d), lambda bi, i: (bi, i, 0)),
        out_shape=jax.ShapeDtypeStruct((bsz, nt * tm, d), F32),
        compiler_params=_params(("parallel", "parallel")),
        name="rwkv_out_mlp",
    )(z, yf, yb, gate, bonus, ln_g, ln_b, e, et, w, mod, g2, w1, w2)


def _group_mean_matrix(n, width):
    idx = jnp.arange(n) // width
    return ((idx[:, None] == idx[None, :]).astype(F32) / width).astype(BF16)


def _rope_tables(lc, s):
    quarter = AT_DH // 4
    inv = ROPE_BASE ** (-jnp.arange(quarter, dtype=F32) / quarter)
    pos = jnp.arange(s)
    rpos = (pos // GRID_W).astype(F32)
    cpos = (pos % GRID_W).astype(F32)
    ang_r = rpos[:, None] * inv[None, :]
    ang_c = cpos[:, None] * inv[None, :]
    cos64 = jnp.concatenate([jnp.cos(ang_r), jnp.cos(ang_r), jnp.cos(ang_c), jnp.cos(ang_c)], axis=1)
    sin64 = jnp.concatenate([-jnp.sin(ang_r), jnp.sin(ang_r), -jnp.sin(ang_c), jnp.sin(ang_c)], axis=1)
    cos = jnp.concatenate([jnp.ones((lc, AT_DH), F32), cos64], axis=0)
    sin = jnp.concatenate([jnp.zeros((lc, AT_DH), F32), sin64], axis=0)
    return jnp.tile(cos, (1, 2)), jnp.tile(sin, (1, 2))


def _even_weights(w_in, b_in):
    def cols(m):
        sizes = (ML_W, ML_W, ML_W, ML_W, ML_GATE_COLS, AT_W, AT_KV_HEADS * AT_DH, AT_KV_HEADS * AT_DH)
        bounds = [sum(sizes[:i + 1]) for i in range(len(sizes) - 1)]
        mq, mk, mv, mo, mg, aq, ak, av = jnp.split(m, bounds, axis=-1)
        dup = lambda u: jnp.concatenate([u[..., :AT_DH], u[..., :AT_DH], u[..., AT_DH:], u[..., AT_DH:]], axis=-1)
        mgp = jnp.pad(mg, [(0, 0)] * (m.ndim - 1) + [(0, LANES - ML_GATE_COLS)])
        return jnp.concatenate([mq, mk, mv, mo, aq, dup(ak), dup(av), mgp], axis=-1)
    return cols(w_in).astype(BF16), cols(b_in[None, :])


def kernel(x, c, ctx, c_ctx, ada_w, ada_b, norm1_g, norm2_g, mlp_w1, mlp_w2, ev_w_in, ev_b_in, ev_w_out, ml_f_bias, ml_out_g, at_q_g, at_k_g, at_sink, rw_mu, rw_w_rkv, rw_w0, rw_w1, rw_w2, rw_a0, rw_a1, rw_a2, rw_g1, rw_g2, rw_k_k, rw_k_a, rw_r_k, rw_ln_g, rw_ln_b, rw_w_out):
    bsz, s, d = x.shape
    lc = ctx.shape[1]
    depth = ada_w.shape[0]
    assert d == D_MODEL and lc == TOKEN_TILE and s % TOKEN_TILE == 0 and bsz < MOD_ROWS
    z = jnp.concatenate([ctx, x], axis=1)

    cstack = jnp.zeros((MOD_ROWS, d), F32).at[:bsz].set(c).at[bsz].set(c_ctx)
    mod_all = _ada_call(cstack, ada_w, ada_b).reshape(depth, MOD_ROWS, N_MOD, d)

    grp512 = _group_mean_matrix(AT_W, AT_DH)
    head_e = (jnp.arange(d)[:, None] // RW_HEAD == jnp.arange(LANES)[None, :]).astype(BF16)
    head_et = head_e.T
    cos_t, sin_t = _rope_tables(lc, s)
    gate_is_f = ((jnp.arange(LANES) % 8 >= 4) & (jnp.arange(LANES) < 16)).astype(F32)[None, :]

    for layer in range(depth):
        mod = mod_all[layer]
        j = layer // 2
        n1 = norm1_g[layer][None, :]
        mlp_args = (norm2_g[layer][None, :], mlp_w1[layer].astype(BF16), mlp_w2[layer].astype(BF16))
        skip = lc // TOKEN_TILE if layer == depth - 1 else 0
        if layer % 2 == 0:
            w_in, b_in = _even_weights(ev_w_in[j], ev_b_in[j])
            fb = jnp.zeros((LANES,), F32)
            for dr in range(2):
                fb = fb.at[8 * dr + 4:8 * dr + 8].set(ml_f_bias[j, dr])
            mqv, kt, mo, aq, akt, av, gates, gates_t = _even_in_call(
                z, n1, mod, w_in, b_in, jnp.tile(at_q_g[j], AT_HEADS)[None, :], jnp.tile(at_k_g[j], 2 * AT_KV_HEADS)[None, :],
                grp512, cos_t, sin_t, fb[None, :], gate_is_f, bsz)
            hf, hb = _mlstm_call(mqv, kt, gates, gates_t, lc)
            sink = jnp.broadcast_to(at_sink[j][:, None], (AT_HEADS, LANES))
            ya = _attn_call(aq, akt, av, sink, lc)
            z = _even_out_call(z, hf, hb, mo, ya, ml_out_g[j][None, :], ev_w_out[j].astype(BF16), mod, *mlp_args,
                               bsz, skip)
        else:
            pad_g = RW_GATE_PAD - RW_GATE_LORA
            zeros = jnp.zeros((RW_DECAY_LORA, d), F32)
            w2 = jnp.concatenate([jnp.concatenate([rw_w2[j, 0], zeros], axis=1),
                                  jnp.concatenate([zeros, rw_w2[j, 1]], axis=1)], axis=0)
            a2 = jnp.concatenate([jnp.concatenate([rw_a2[j, 0], zeros], axis=1),
                                  jnp.concatenate([zeros, rw_a2[j, 1]], axis=1)], axis=0)
            sh, dr_, lw, gate, bonus = _rwkv_in_call(
                z, n1, mod, rw_mu[j].reshape(12, d), rw_w_rkv[j].astype(BF16),
                jnp.concatenate([rw_w1[j, 0], rw_w1[j, 1]], axis=1).astype(BF16),
                jnp.concatenate([rw_a1[j, 0], rw_a1[j, 1]], axis=1).astype(BF16),
                jnp.pad(rw_g1[j], ((0, 0), (0, pad_g))).astype(BF16),
                w2.astype(BF16), a2.astype(BF16),
                jnp.pad(rw_g2[j], ((0, pad_g), (0, 0))).astype(BF16),
                rw_w0[j], rw_a0[j], rw_k_k[j][None, :], rw_k_a[j][None, :], rw_r_k[j].reshape(1, d),
                head_e, head_et, bsz, lc)
            yf, yb = _rwkv_scan_call(sh, dr_, lw, lc)
            z = _rwkv_out_call(z, yf, yb, gate, bonus, rw_ln_g[j][None, :], rw_ln_b[j][None, :], head_e, head_et,
                               rw_w_out[j].astype(BF16), mod, *mlp_args, bsz, lc, skip)
    return z
```

```python
import functools

import jax
import jax.numpy as jnp
from jax import lax
from jax.experimental import pallas as pl
from jax.experimental.pallas import tpu as pltpu

F32 = jnp.float32
BF16 = jnp.bfloat16

D_MODEL = 1024
N_MOD = 6
D_FF = 4 * D_MODEL
EPS = 1e-6
GRID_W = 64

ML_HEADS = 4
ML_D = 128
ML_W = ML_HEADS * ML_D
ML_CHUNK = 128
GATE_CAP = 15.0

AT_DH = 64
AT_HEADS = 8
AT_KV_HEADS = 2
AT_W = AT_HEADS * AT_DH
AT_BLOCK = 128
AT_KV_DUP = 2 * AT_KV_HEADS * AT_DH
AT_BATCH = 1
ROPE_BASE = 10000.0

RW_HEAD = 64
RW_PAIRS = D_MODEL // 128
RW_CHUNK = 64
RW_LN_EPS = 64e-5
RW_DECAY_LORA = 64
RW_AAA_LORA = 64
RW_GATE_LORA = 160
RW_GATE_PAD = 256
RW_DECAY_SCALE = 0.6065306597126334

LANES = 128
TOKEN_TILE = 256
FF_TILE = 1024
MOD_ROWS = 16
VMEM_LIMIT = 56 * 1024 * 1024
NEG = -1e30
RW_MM = BF16
RW_BATCH = 4
RW_ROW_TILE = 8

EV_MO = 3 * ML_W
EV_AQ = EV_MO + ML_W
EV_AK = EV_AQ + AT_W
EV_AV = EV_AK + AT_KV_DUP
EV_GATE = EV_AV + AT_KV_DUP
EV_COLS = EV_GATE + LANES
ML_GATE_COLS = 4 * ML_HEADS


def _dot(a, b):
    return jnp.dot(a, b, preferred_element_type=F32)


def _dot_tb(a, b):
    return lax.dot_general(a, b, (((1,), (1,)), ((), ())), preferred_element_type=F32)


def _dot_ta(a, b):
    return lax.dot_general(a, b, (((0,), (0,)), ((), ())), preferred_element_type=F32)


def _norm_mod(z, g, shift, scale):
    ms = jnp.mean(z * z, axis=-1, keepdims=True)
    return (z * lax.rsqrt(ms + EPS) * g) * (1.0 + scale) + shift


def _sigmoid(x):
    return 1.0 / (1.0 + jnp.exp(-x))


def _head_sum(x, e_ref, et_ref, terms):
    s = _dot(x.astype(BF16), e_ref[...])
    hi = s.astype(BF16)
    out = _dot(hi, et_ref[...])
    if terms > 1:
        out = out + _dot((s - hi.astype(F32)).astype(BF16), et_ref[...])
    return out


def _const_spec(shape):
    nd = len(shape)
    return pl.BlockSpec(shape, lambda *_: (0,) * nd, pipeline_mode=pl.Buffered(1))


def _params(sem):
    return pltpu.CompilerParams(dimension_semantics=sem, vmem_limit_bytes=VMEM_LIMIT)


def _ada_kernel(c_ref, w_ref, b_ref, o_ref):
    cv = c_ref[...]
    s = cv * _sigmoid(cv)
    o_ref[...] = jnp.dot(s, w_ref[...], preferred_element_type=F32,
                         precision=lax.Precision.HIGHEST) + b_ref[...]


def _ada_call(cstack, ada_w, ada_b):
    depth = ada_w.shape[0]
    n = N_MOD * D_MODEL
    tn = 1024
    return pl.pallas_call(
        _ada_kernel,
        grid=(depth, n // tn),
        in_specs=[pl.BlockSpec((MOD_ROWS, D_MODEL), lambda l, j: (0, 0)),
                  pl.BlockSpec((None, D_MODEL, tn), lambda l, j: (l, 0, j)),
                  pl.BlockSpec((None, 1, tn), lambda l, j: (l, 0, j))],
        out_specs=pl.BlockSpec((None, MOD_ROWS, tn), lambda l, j: (l, 0, j)),
        out_shape=jax.ShapeDtypeStruct((depth, MOD_ROWS, n), F32),
        compiler_params=_params(("parallel", "parallel")),
        name="ada_ln",
    )(cstack, ada_w, ada_b.reshape(depth, 1, n))


def _rope(x, cos, sin, lane_lo):
    n = x.shape[1]
    up = pltpu.roll(x, n - 16, 1)
    dn = pltpu.roll(x, 16, 1)
    reps = n // LANES
    c = jnp.concatenate([cos] * reps, axis=1) if reps > 1 else cos
    s = jnp.concatenate([sin] * reps, axis=1) if reps > 1 else sin
    lo = jnp.concatenate([lane_lo] * reps, axis=1) if reps > 1 else lane_lo
    return x * c + jnp.where(lo, up, dn) * s


def _even_in_kernel(*refs, assemble):
    if assemble:
        ctx_ref, x_ref, *refs, z_out_ref = refs
        z = jnp.where(pl.program_id(1) == 0, ctx_ref[...], x_ref[...])
        z_out_ref[...] = z
    else:
        z_ref, *refs = refs
        z = z_ref[...]
    (g_ref, mod_ref, w_ref, b_ref, gq_ref, gk_ref, grp_ref, cos_ref, sin_ref, fb_ref, isf_ref,
     mqv_ref, kt_ref, mo_ref, aq_ref, akt_ref, av_ref, gate_ref, gatet_ref) = refs
    h = _norm_mod(z, g_ref[...], mod_ref[0:1, :], mod_ref[1:2, :])
    p = _dot(h.astype(BF16), w_ref[...]) + b_ref[...]
    mqv_ref[:, 0:ML_W] = p[:, 0:ML_W].astype(BF16)
    mqv_ref[:, ML_W:2 * ML_W] = p[:, 2 * ML_W:3 * ML_W].astype(BF16)
    kt_ref[...] = (p[:, ML_W:2 * ML_W] * (ML_D ** -0.5)).T.astype(BF16)
    mo_ref[...] = p[:, EV_MO:EV_MO + ML_W].astype(mo_ref.dtype)
    cos = cos_ref[...]
    sin = sin_ref[...]
    lane_lo = (lax.broadcasted_iota(jnp.int32, (1, LANES), 1) % 32) < 16
    q = p[:, EV_AQ:EV_AQ + AT_W]
    qms = _dot((q * q).astype(BF16), grp_ref[...])
    q = q * lax.rsqrt(qms + EPS) * gq_ref[...]
    aq_ref[...] = (_rope(q, cos, sin, lane_lo) * (AT_DH ** -0.5)).astype(BF16)
    k = p[:, EV_AK:EV_AK + AT_KV_DUP]
    kms = _dot((k * k).astype(BF16), grp_ref[0:AT_KV_DUP, 0:AT_KV_DUP])
    k = k * lax.rsqrt(kms + EPS) * gk_ref[...]
    akt_ref[...] = _rope(k, cos, sin, lane_lo).T.astype(BF16)
    av_ref[...] = p[:, EV_AV:EV_AV + AT_KV_DUP].astype(BF16)
    gt = p[:, EV_GATE:EV_GATE + LANES] + fb_ref[...]
    sc = GATE_CAP * jnp.tanh(gt * (1.0 / GATE_CAP))
    logsig = jnp.minimum(sc, 0.0) - jnp.log(1.0 + jnp.exp(-jnp.abs(sc)))
    gates = jnp.where(isf_ref[...] > 0.5, logsig, sc)
    gate_ref[...] = gates
    gatet_ref[...] = gates.T[0:ML_GATE_COLS, :]


def _even_in_call(z, g, mod, w, b, gq, gk, grp, cos, sin, fb, isf, nb):
    tm = TOKEN_TILE
    tok = lambda width: pl.BlockSpec((None, tm, width), lambda bi, i: (bi, i, 0))
    tpose = lambda rows: pl.BlockSpec((None, rows, tm), lambda bi, i: (bi, 0, i))
    assemble = isinstance(z, tuple)
    if assemble:
        ctx, x = z
        bsz, s, d = x.shape
        t = ctx.shape[1] + s
        z_args = (ctx, x)
        z_specs = [pl.BlockSpec((None, tm, d), lambda bi, i: (bi, 0, 0)),
                   pl.BlockSpec((None, tm, d), lambda bi, i: (bi, jnp.maximum(i - 1, 0), 0))]
        extra_specs, extra_shapes = [tok(d)], [jax.ShapeDtypeStruct((bsz, t, d), F32)]
    else:
        bsz, t, d = z.shape
        z_args, z_specs, extra_specs, extra_shapes = (z,), [tok(d)], [], []
    return pl.pallas_call(
        functools.partial(_even_in_kernel, assemble=assemble),
        grid=(bsz, t // tm),
        in_specs=z_specs + [
                  _const_spec((1, d)),
                  pl.BlockSpec((None, N_MOD, d), lambda bi, i: (jnp.where(i == 0, nb, bi), 0, 0)),
                  _const_spec((d, EV_COLS)),
                  _const_spec((1, EV_COLS)),
                  _const_spec((1, AT_W)),
                  _const_spec((1, AT_KV_DUP)),
                  _const_spec((AT_W, AT_W)),
                  pl.BlockSpec((tm, LANES), lambda bi, i: (i, 0)),
                  pl.BlockSpec((tm, LANES), lambda bi, i: (i, 0)),
                  _const_spec((1, LANES)),
                  _const_spec((1, LANES))],
        out_specs=[tok(2 * ML_W), tpose(ML_W), tok(ML_W), tok(AT_W), tpose(AT_KV_DUP), tok(AT_KV_DUP), tok(LANES),
                   tpose(ML_GATE_COLS)] + extra_specs,
        out_shape=[jax.ShapeDtypeStruct((bsz, t, 2 * ML_W), BF16),
                   jax.ShapeDtypeStruct((bsz, ML_W, t), BF16),
                   jax.ShapeDtypeStruct((bsz, t, ML_W), BF16),
                   jax.ShapeDtypeStruct((bsz, t, AT_W), BF16),
                   jax.ShapeDtypeStruct((bsz, AT_KV_DUP, t), BF16),
                   jax.ShapeDtypeStruct((bsz, t, AT_KV_DUP), BF16),
                   jax.ShapeDtypeStruct((bsz, t, LANES), F32),
                   jax.ShapeDtypeStruct((bsz, ML_GATE_COLS, t), F32)] + extra_shapes,
        compiler_params=_params(("parallel", "parallel")),
        name="even_in",
    )(*z_args, g, mod, w, b, gq, gk, grp, cos, sin, fb, isf)


def _split3(x):
    hi = x.astype(BF16)
    r1 = x - hi.astype(F32)
    mid = r1.astype(BF16)
    return hi, mid, (r1 - mid.astype(F32)).astype(BF16)


def _mlstm_kernel(qf_ref, ktf_ref, vf_ref, qb_ref, ktb_ref, vb_ref, gcf_ref, gcb_ref, grf_ref, grb_ref, sel_ref,
                  hf_ref, hb_ref, c_s, m_s):
    ln = ML_CHUNK

    @pl.when(pl.program_id(1) == 0)
    def _():
        c_s[...] = jnp.zeros_like(c_s)
        m_s[...] = jnp.zeros_like(m_s)

    ri = lax.broadcasted_iota(jnp.int32, (ln, ln), 0)
    ci = lax.broadcasted_iota(jnp.int32, (ln, ln), 1)
    hi = lax.Precision.HIGHEST
    before = (ci <= ri, ci >= ri)
    refs = ((qf_ref, ktf_ref, vf_ref, gcf_ref, grf_ref, hf_ref), (qb_ref, ktb_ref, vb_ref, gcb_ref, grb_ref, hb_ref))
    cum_r = [jnp.dot(refs[d][4][...], before[1 - d].astype(F32), preferred_element_type=F32, precision=hi)
             for d in range(2)]
    bc_all = []
    for d in range(2):
        cum_c = sum(_dot(before[d].astype(BF16), part) for part in _split3(refs[d][3][...]))
        sel = sel_ref[d]
        bc_all.append(sum(_dot(part, sel) for part in _split3(cum_c)))
    chains = [(d, hd) for d in range(2) for hd in range(ML_HEADS)]
    lanes = [slice(hd * ML_D, (hd + 1) * ML_D) for _, hd in chains]
    st = [d * ML_HEADS + hd for d, hd in chains]
    ones = jnp.ones((ln, ML_D), BF16)
    q = [refs[d][0][:, ls] for (d, _), ls in zip(chains, lanes)]
    kt = [refs[d][1][ls, :] for (d, _), ls in zip(chains, lanes)]
    v1 = [jnp.concatenate([refs[d][2][:, ls], ones], axis=1) for (d, _), ls in zip(chains, lanes)]
    i_row = [refs[d][4][8 * d + hd:8 * d + hd + 1, :] for d, hd in chains]
    f_row = [refs[d][4][8 * d + 4 + hd:8 * d + 5 + hd, :] for d, hd in chains]
    bcum_row = [cum_r[d][8 * d + 4 + hd:8 * d + 5 + hd, :] for d, hd in chains]
    bcum = [bc_all[d][:, ls] for (d, _), ls in zip(chains, lanes)]
    m_old = [m_s[s:s + 1, 0:1] for s in st]
    c_old = [c_s[s] for s in st]
    qk = [_dot(a, b) for a, b in zip(q, kt)]
    r2 = [_dot(a, c.astype(BF16)) for a, c in zip(q, c_old)]
    dlog = [jnp.where(before[d], bc - br + ir, NEG) for (d, _), bc, br, ir in zip(chains, bcum, bcum_row, i_row)]
    inter = [bc + m for bc, m in zip(bcum, m_old)]
    mt = [jnp.maximum(jnp.max(dl, axis=1, keepdims=True), it) for dl, it in zip(dlog, inter)]
    sc = [x * jnp.exp(dl - m) for x, dl, m in zip(qk, dlog, mt)]
    iw = [jnp.exp(it - m) for it, m in zip(inter, mt)]
    r1 = [_dot(s.astype(BF16), b) for s, b in zip(sc, v1)]
    for j, (d, _) in enumerate(chains):
        tot = r1[j] + jnp.concatenate([iw[j], iw[j]], axis=1) * r2[j]
        den = jnp.maximum(jnp.abs(tot[:, ML_D:]), jnp.exp(-mt[j]))
        refs[d][5][:, lanes[j]] = (tot[:, :ML_D] / den).astype(refs[d][5].dtype)
    bl = [jnp.sum(fr, axis=1, keepdims=True) for fr in f_row]
    ws_log = [b - br + ir for b, br, ir in zip(bl, bcum_row, i_row)]
    m_new = [jnp.maximum(b + m, jnp.max(w, axis=1, keepdims=True)) for b, m, w in zip(bl, m_old, ws_log)]
    ws = [jnp.exp(w - m) for w, m in zip(ws_log, m_new)]
    upd = [_dot((a.astype(F32) * w).astype(BF16), b) for a, w, b in zip(kt, ws, v1)]
    for j, s in enumerate(st):
        c_s[s] = jnp.exp(bl[j] + m_old[j] - m_new[j]) * c_old[j] + upd[j]
        m_s[s:s + 1, :] = jnp.broadcast_to(m_new[j], (1, LANES))


def _rev_chunk(i, nctx, n):
    return jnp.where(i < nctx, nctx - 1 - i, n + nctx - 1 - i)


def _mlstm_call(mqv, kt, gates, gates_t, lc):
    bsz, t, _ = mqv.shape
    ln = ML_CHUNK
    assert ln == ML_D == LANES
    n = t // ln
    nctx = lc // ln
    rv = lambda i: _rev_chunk(i, nctx, n)
    col = jnp.arange(LANES)[None, :, None]
    want = (8 * jnp.arange(2)[:, None, None] + 4 + jnp.arange(ML_HEADS * ln)[None, None, :] // ln)
    sel = (col == want).astype(BF16)
    return pl.pallas_call(
        _mlstm_kernel,
        grid=(bsz, n),
        in_specs=[pl.BlockSpec((None, ln, ML_W), lambda bi, i: (bi, i, 0)),
                  pl.BlockSpec((None, ML_W, ln), lambda bi, i: (bi, 0, i)),
                  pl.BlockSpec((None, ln, ML_W), lambda bi, i: (bi, i, 1)),
                  pl.BlockSpec((None, ln, ML_W), lambda bi, i: (bi, rv(i), 0)),
                  pl.BlockSpec((None, ML_W, ln), lambda bi, i: (bi, 0, rv(i))),
                  pl.BlockSpec((None, ln, ML_W), lambda bi, i: (bi, rv(i), 1)),
                  pl.BlockSpec((None, ln, LANES), lambda bi, i: (bi, i, 0)),
                  pl.BlockSpec((None, ln, LANES), lambda bi, i: (bi, rv(i), 0)),
                  pl.BlockSpec((None, ML_GATE_COLS, ln), lambda bi, i: (bi, 0, i)),
                  pl.BlockSpec((None, ML_GATE_COLS, ln), lambda bi, i: (bi, 0, rv(i))),
                  _const_spec((2, LANES, ML_HEADS * ln))],
        out_specs=[pl.BlockSpec((None, ln, ML_W), lambda bi, i: (bi, i, 0)),
                   pl.BlockSpec((None, ln, ML_W), lambda bi, i: (bi, rv(i), 0))],
        out_shape=[jax.ShapeDtypeStruct((bsz, t, ML_W), BF16)] * 2,
        scratch_shapes=[pltpu.VMEM((2 * ML_HEADS, ML_D, 2 * ML_D), F32),
                        pltpu.VMEM((2 * ML_HEADS, LANES), F32)],
        compiler_params=_params(("parallel", "arbitrary")),
        name="mlstm_scan",
    )(mqv, kt, mqv, mqv, kt, mqv, gates, gates, gates_t, gates_t, sel)


def _attn_kernel(q_ref, ktp_ref, kto_ref, ktn_ref, ktc_ref, vp_ref, vo_ref, vn_ref, vc_ref, sink_ref, o_ref,
                 *, nctx, n):
    j = pl.program_id(1)
    blk = AT_BLOCK
    latent = j >= nctx
    qi = lax.broadcasted_iota(jnp.int32, (blk, blk), 0)
    ki = lax.broadcasted_iota(jnp.int32, (blk, blk), 1)
    ok_prev = jnp.logical_and(jnp.logical_and(latent, j > nctx), ki >= qi)
    ok_own = jnp.logical_and(latent, ki >= 0)
    ok_next = jnp.logical_and(jnp.logical_and(latent, j < n - 1), ki <= qi)
    lane = lax.broadcasted_iota(jnp.int32, (1, LANES), 1)
    sub = lax.broadcasted_iota(jnp.int32, (LANES, 1), 0)
    lane_half = (lane < AT_DH, lane >= AT_DH)
    sub_half = (sub < AT_DH, sub >= AT_DH)
    kts = (ktp_ref, kto_ref, ktn_ref, ktc_ref)
    vs = (vp_ref, vo_ref, vn_ref, vc_ref)
    oks = (ok_prev, ok_own, ok_next, None)
    zero = jnp.zeros((), BF16)
    nbat = q_ref.shape[0]
    kx, vx = {}, {}
    for bb in range(nbat):
        for g in range(AT_KV_HEADS):
            rows = slice(g * LANES, (g + 1) * LANES)
            for e in range(2):
                kx[bb, g, e] = [jnp.where(sub_half[e], r[bb, rows, :], zero) for r in kts]
                vx[bb, g, e] = [jnp.concatenate([jnp.where(lane_half[e], r[bb, :, rows], zero),
                                                 jnp.ones((r.shape[1], LANES), BF16)], axis=1) for r in vs]
    heads = [(bb, hd // 2, hd % 2, hd // 4) for bb in range(nbat) for hd in range(AT_HEADS)]
    qc = {(bb, c): q_ref[bb, :, c * LANES:(c + 1) * LANES] for bb in range(nbat) for c in range(AT_HEADS // 2)}
    sink = [sink_ref[2 * c + e:2 * c + e + 1, 0:1] for _, c, e, _ in heads]
    ss = [[_dot(qc[bb, c], kx[bb, g, e][p]) if oks[p] is None
           else jnp.where(oks[p], _dot(qc[bb, c], kx[bb, g, e][p]), NEG) for p in range(4)]
          for bb, c, e, g in heads]
    m = []
    for s4, sk in zip(ss, sink):
        band = jnp.maximum(jnp.maximum(s4[0], s4[1]), s4[2])
        m.append(jnp.maximum(jnp.maximum(jnp.max(band, axis=1, keepdims=True),
                                         jnp.max(s4[3], axis=1, keepdims=True)), sk))
    res = []
    for s4, mm, (bb, c, e, g) in zip(ss, m, heads):
        acc = None
        for p in range(4):
            pv = _dot(jnp.exp(s4[p] - mm).astype(BF16), vx[bb, g, e][p])
            acc = pv if acc is None else acc + pv
        res.append(acc)
    outs = [r[:, :LANES] / (r[:, LANES:] + jnp.exp(sk - mm)) for r, sk, mm in zip(res, sink, m)]
    for i in range(0, len(heads), 2):
        bb, c = heads[i][0], heads[i][1]
        o_ref[bb, :, c * LANES:(c + 1) * LANES] = (outs[i] + outs[i + 1]).astype(o_ref.dtype)


def _attn_call(aq, akt, av, sink, lc):
    bsz, t, _ = aq.shape
    blk = AT_BLOCK
    nb = AT_BATCH
    assert bsz % nb == 0
    n = t // blk
    nctx = lc // blk
    prev = lambda j: jnp.clip(j - 1, nctx, n - 1)
    own = lambda j: jnp.clip(j, nctx, n - 1)
    nxt = lambda j: jnp.clip(j + 1, nctx, n - 1)
    kt = lambda f: pl.BlockSpec((nb, AT_KV_DUP, blk), lambda bi, j: (bi, 0, f(j)))
    vv = lambda f: pl.BlockSpec((nb, blk, AT_KV_DUP), lambda bi, j: (bi, f(j), 0))
    return pl.pallas_call(
        functools.partial(_attn_kernel, nctx=nctx, n=n),
        grid=(bsz // nb, n),
        in_specs=[pl.BlockSpec((nb, blk, AT_W), lambda bi, j: (bi, j, 0)),
                  kt(prev), kt(own), kt(nxt),
                  pl.BlockSpec((nb, AT_KV_DUP, lc), lambda bi, j: (bi, 0, 0)),
                  vv(prev), vv(own), vv(nxt),
                  pl.BlockSpec((nb, lc, AT_KV_DUP), lambda bi, j: (bi, 0, 0)),
                  _const_spec((AT_HEADS, LANES))],
        out_specs=pl.BlockSpec((nb, blk, AT_W), lambda bi, j: (bi, j, 0)),
        out_shape=jax.ShapeDtypeStruct((bsz, t, AT_W), BF16),
        compiler_params=_params(("parallel", "parallel")),
        name="window_attn",
    )(aq, akt, akt, akt, akt, av, av, av, av, sink)


def _mlp_tail(z, g, mod_ref, w1_ref, w2_ref):
    h = _norm_mod(z, g, mod_ref[3:4, :], mod_ref[4:5, :]).astype(BF16)
    acc = jnp.zeros(z.shape, F32)
    for f in range(D_FF // FF_TILE):
        a = jnp.maximum(_dot(h, w1_ref[:, f * FF_TILE:(f + 1) * FF_TILE]), 0.0)
        acc = acc + _dot((a * a).astype(BF16), w2_ref[f * FF_TILE:(f + 1) * FF_TILE, :])
    return z + mod_ref[5:6, :] * acc


def _even_out_kernel(z_ref, hf_ref, hb_ref, mo_ref, ya_ref, og_ref, w_ref, mod_ref, g2_ref, w1_ref, w2_ref, o_ref):
    hs = hf_ref[...].astype(F32) + hb_ref[...].astype(F32)
    parts = []
    for hd in range(ML_HEADS):
        x = hs[:, hd * ML_D:(hd + 1) * ML_D]
        parts.append(x * lax.rsqrt(jnp.mean(x * x, axis=1, keepdims=True) + EPS))
    ym = jnp.concatenate(parts, axis=1) * og_ref[...] * _sigmoid(mo_ref[...].astype(F32))
    y = _dot(ym.astype(BF16), w_ref[0:ML_W, :]) + _dot(ya_ref[...], w_ref[ML_W:ML_W + AT_W, :])
    o_ref[...] = _mlp_tail(z_ref[...] + mod_ref[2:3, :] * y, g2_ref[...], mod_ref, w1_ref, w2_ref)


def _even_out_call(z, hf, hb, mo, ya, og, w, mod, g2, w1, w2, nb, skip):
    bsz, t, d = z.shape
    tm = TOKEN_TILE
    nt = t // tm - skip
    tok = lambda width: pl.BlockSpec((None, tm, width), lambda bi, i: (bi, i + skip, 0))
    return pl.pallas_call(
        _even_out_kernel,
        grid=(bsz, nt),
        in_specs=[tok(d), tok(ML_W), tok(ML_W), tok(ML_W), tok(AT_W),
                  _const_spec((1, ML_W)),
                  _const_spec((ML_W + AT_W, d)),
                  pl.BlockSpec((None, N_MOD, d), lambda bi, i: (jnp.where(i + skip == 0, nb, bi), 0, 0)),
                  _const_spec((1, d)),
                  _const_spec((d, D_FF)),
                  _const_spec((D_FF, d))],
        out_specs=pl.BlockSpec((None, tm, d), lambda bi, i: (bi, i, 0)),
        out_shape=jax.ShapeDtypeStruct((bsz, nt * tm, d), F32),
        compiler_params=_params(("parallel", "parallel")),
        name="even_out_mlp",
    )(z, hf, hb, mo, ya, og, w, mod, g2, w1, w2)


def _rwkv_in_kernel(z_ref, zp_ref, zn_ref, g_ref, mod_ref, mu_ref, wrkv_ref, w1_ref, a1_ref, g1_ref,
                    w2_ref, a2_ref, g2_ref, w0_ref, a0_ref, kk_ref, ka_ref, rk_ref, e_ref, et_ref,
                    sh_ref, dr_ref, lw_ref, gate_ref, bonus_ref, *, nctx, ntile):
    i = pl.program_id(1)
    tm, d = z_ref.shape
    g = g_ref[...]
    shift = mod_ref[0:1, :]
    scale = mod_ref[1:2, :]
    h = _norm_mod(z_ref[...], g, shift, scale)
    no_prev = jnp.logical_or(i == 0, i == nctx)
    no_next = jnp.logical_or(i == nctx - 1, i == ntile - 1)
    hp = jnp.where(no_prev, 0.0, _norm_mod(zp_ref[7:8, :], g, shift, scale))
    hn = jnp.where(no_next, 0.0, _norm_mod(zn_ref[0:1, :], g, shift, scale))
    row = lax.broadcasted_iota(jnp.int32, (tm, 1), 0)
    dp = jnp.where(row == 0, hp, pltpu.roll(h, 1, 0)) - h
    dn = jnp.where(row == tm - 1, hn, pltpu.roll(h, tm - 1, 0)) - h

    hb, dpb, dnb = h.astype(BF16), dp.astype(BF16), dn.astype(BF16)
    mub = mu_ref[...].astype(BF16)

    def mix(n):
        return hb + mub[2 * n:2 * n + 1, :] * dpb + mub[2 * n + 1:2 * n + 2, :] * dnb

    r = _dot(mix(0), wrkv_ref[0])
    k = _dot(mix(2), wrkv_ref[1])
    v = _dot(mix(3), wrkv_ref[2])
    gate_ref[...] = _dot(_sigmoid(_dot(mix(5), g1_ref[...])).astype(BF16), g2_ref[...]).astype(gate_ref.dtype)
    lora_w = _dot(jnp.tanh(_dot(mix(1), w1_ref[...])).astype(BF16), w2_ref[...])
    lora_a = _dot(_dot(mix(4), a1_ref[...]).astype(BF16), a2_ref[...])
    kkr = k * kk_ref[...]
    ssq = _head_sum(kkr * kkr, e_ref, et_ref, 1)
    kk = kkr * lax.rsqrt(jnp.maximum(ssq, 1e-24))
    kd_sum = None
    for dr in range(2):
        cols = slice(dr * d, (dr + 1) * d)
        lw = -RW_DECAY_SCALE * _sigmoid(w0_ref[dr:dr + 1, :] + lora_w[:, cols])
        a = _sigmoid(a0_ref[dr:dr + 1, :] + lora_a[:, cols])
        kd = k * (1.0 + (a - 1.0) * ka_ref[...])
        bvec = kk * a
        kd_sum = kd if kd_sum is None else kd_sum + kd
        for p in range(RW_PAIRS):
            ls = slice(p * LANES, (p + 1) * LANES)
            lw_ref[dr, p] = lw[:, ls]
            dr_ref[dr, 0, p] = kd[:, ls].astype(BF16)
            dr_ref[dr, 1, p] = bvec[:, ls].astype(BF16)
    for p in range(RW_PAIRS):
        ls = slice(p * LANES, (p + 1) * LANES)
        sh_ref[0, p] = r[:, ls].astype(BF16)
        sh_ref[1, p] = v[:, ls].astype(BF16)
        sh_ref[2, p] = kk[:, ls].astype(BF16)
    bsum = _head_sum(r * kd_sum * rk_ref[...], e_ref, et_ref, 1)
    bonus_ref[...] = (bsum * v).astype(bonus_ref.dtype)


def _rwkv_in_call(z, g, mod, mu, wrkv, w1, a1, g1, w2, a2, g2, w0, a0, k_k, k_a, r_k, e, et, nb, lc):
    bsz, t, d = z.shape
    tm = TOKEN_TILE
    ntile = t // tm
    nctx = lc // tm
    r8 = tm // 8
    tok = pl.BlockSpec((None, tm, d), lambda bi, i: (bi, i, 0))
    return pl.pallas_call(
        functools.partial(_rwkv_in_kernel, nctx=nctx, ntile=ntile),
        grid=(bsz, ntile),
        in_specs=[tok,
                  pl.BlockSpec((None, 8, d), lambda bi, i: (bi, jnp.maximum(i * r8 - 1, 0), 0)),
                  pl.BlockSpec((None, 8, d), lambda bi, i: (bi, jnp.minimum((i + 1) * r8, t // 8 - 1), 0)),
                  _const_spec((1, d)),
                  pl.BlockSpec((None, N_MOD, d), lambda bi, i: (jnp.where(i < nctx, nb, bi), 0, 0)),
                  _const_spec((12, d)),
                  _const_spec((3, d, d)),
                  _const_spec((d, LANES)),
                  _const_spec((d, LANES)),
                  _const_spec((d, RW_GATE_PAD)),
                  _const_spec((LANES, 2 * d)),
                  _const_spec((LANES, 2 * d)),
                  _const_spec((RW_GATE_PAD, d)),
                  _const_spec((2, d)),
                  _const_spec((2, d)),
                  _const_spec((1, d)),
                  _const_spec((1, d)),
                  _const_spec((1, d)),
                  _const_spec((d, LANES)),
                  _const_spec((LANES, d))],
        out_specs=[pl.BlockSpec((None, 3, RW_PAIRS, tm, LANES), lambda bi, i: (bi, 0, 0, i, 0)),
                   pl.BlockSpec((None, 2, 2, RW_PAIRS, tm, LANES), lambda bi, i: (bi, 0, 0, 0, i, 0)),
                   pl.BlockSpec((None, 2, RW_PAIRS, tm, LANES), lambda bi, i: (bi, 0, 0, i, 0)),
                   tok, tok],
        out_shape=[jax.ShapeDtypeStruct((bsz, 3, RW_PAIRS, t, LANES), BF16),
                   jax.ShapeDtypeStruct((bsz, 2, 2, RW_PAIRS, t, LANES), BF16),
                   jax.ShapeDtypeStruct((bsz, 2, RW_PAIRS, t, LANES), F32),
                   jax.ShapeDtypeStruct((bsz, t, d), BF16),
                   jax.ShapeDtypeStruct((bsz, t, d), BF16)],
        compiler_params=_params(("parallel", "parallel")),
        name="rwkv_in",
    )(z, z, z, g, mod, mu, wrkv, w1, a1, g1, w2, a2, g2, w0, a0, k_k, k_a, r_k, e, et)


def _rwkv_chunk_group(chains):
    ln = RW_CHUNK
    row = lax.broadcasted_iota(jnp.int32, (ln, 2 * ln), 0)
    col = lax.broadcasted_iota(jnp.int32, (ln, 2 * ln), 1)
    sidx = jnp.where(col >= ln, col - ln, col)
    lane_a = lax.broadcasted_iota(jnp.int32, (1, LANES), 1) < RW_HEAD
    tr = lax.broadcasted_iota(jnp.int32, (ln, ln), 0)
    tc = lax.broadcasted_iota(jnp.int32, (ln, ln), 1)
    tri = {False: (tc <= tr).astype(F32), True: (tc >= tr).astype(F32)}
    strict = {False: sidx < row, True: sidx > row}
    incl = {False: sidx <= row, True: sidx >= row}
    eye_w = jnp.where(sidx == row, 1.0, 0.0)
    diff = row ^ sidx
    vr = lax.broadcasted_iota(jnp.int32, (LANES, LANES), 0) < RW_HEAD
    kc = lax.broadcasted_iota(jnp.int32, (LANES, LANES), 1) < RW_HEAD
    same_head = vr == kc

    def bd(x):
        zero = jnp.zeros((), x.dtype)
        return jnp.concatenate([jnp.where(lane_a, x, zero), jnp.where(lane_a, zero, x)], axis=0)

    revs = [c[7] for c in chains]
    vs = [c[1] for c in chains]
    s_olds = [c[6] for c in chains]
    gcum = [sum(_dot(tri[c[7]].astype(BF16), part) for part in _split3(c[5])) for c in chains]
    gtot = [jnp.sum(c[5], axis=0, keepdims=True) for c in chains]

    def scaled(c, g, gt):
        r, v, kk, kd, bv, lw = (x.astype(F32) for x in c[:6])
        e_pos = jnp.exp(g)
        e_neg = jnp.exp(-g)
        e_end = jnp.exp(gt - g)
        ar = jnp.concatenate([(-kk * jnp.exp(g - lw)).astype(RW_MM), (r * e_pos).astype(RW_MM)], axis=0)
        bk_end = jnp.concatenate([(bv * e_end).astype(RW_MM), (kd * e_end).astype(RW_MM)], axis=0)
        return ar, (bv * e_neg).astype(RW_MM), (kd * e_neg).astype(RW_MM), bk_end

    sc = [scaled(c, g, gt) for c, g, gt in zip(chains, gcum, gtot)]
    ars_ = [x[0] for x in sc]
    x_bk = [_dot_tb(x[0], jnp.concatenate([bd(x[1]), bd(x[2])], axis=0)) for x in sc]
    x_b = [x[:, :2 * ln] for x in x_bk]
    x_k = [x[:, 2 * ln:] for x in x_bk]
    n_w =[jnp.where(strict[rv], x[:ln], 0.0) for x, rv in zip(x_b, revs)]
    m_rb = [jnp.where(incl[rv], x[ln:], 0.0).astype(RW_MM) for x, rv in zip(x_b, revs)]
    m_k = [jnp.concatenate([jnp.where(strict[rv], x[:ln], 0.0), jnp.where(incl[rv], x[ln:], 0.0)],
                           axis=0).astype(RW_MM) for x, rv in zip(x_k, revs)]
    x_w = [eye_w + jnp.where(diff == 1, n, 0.0) for n in n_w]

    def take_rows(x, h, odd):
        return jnp.concatenate([x[b * h:(b + 1) * h] for b in range(ln // h) if (b % 2 == 1) == odd], axis=0)

    def put_rows(base, upd, h, odd):
        parts, j = [], 0
        for b in range(ln // h):
            if (b % 2 == 1) == odd:
                blk = upd[j * h:(j + 1) * h]
                parts.append(blk if base is None else base[b * h:(b + 1) * h] + blk)
                j += 1
            else:
                parts.append(jnp.zeros((h, upd.shape[1]), upd.dtype) if base is None else base[b * h:(b + 1) * h])
        return jnp.concatenate(parts, axis=0)

    h = 2
    while h < ln:
        lvl = jnp.logical_and(diff >= h, diff < 2 * h)
        n_l = [jnp.where(lvl, n, 0.0) for n in n_w]
        if h < RW_ROW_TILE:
            tmp = [_dot(n.astype(RW_MM), bd(x.astype(RW_MM))) for n, x in zip(n_l, x_w)]
            x_w = [x + _dot(x.astype(RW_MM), bd(t.astype(RW_MM))) for x, t in zip(x_w, tmp)]
        else:
            odd = [not rv for rv in revs]
            tmp = [_dot(take_rows(n, h, o).astype(RW_MM), bd(x.astype(RW_MM))) for n, x, o in zip(n_l, x_w, odd)]
            tmp = [put_rows(None, t, h, o) for t, o in zip(tmp, odd)]
            cor = [_dot(take_rows(x, h, o).astype(RW_MM), bd(t.astype(RW_MM))) for x, t, o in zip(x_w, tmp, odd)]
            x_w = [put_rows(x, c, h, o) for x, c, o in zip(x_w, cor, odd)]
        h *= 2
    gy = [_dot(jnp.concatenate([m, a], axis=1), jnp.concatenate([bd(v), s.T.astype(RW_MM)], axis=0))
          for m, a, v, s in zip(m_k, ars_, vs, s_olds)]
    u = [_dot(x.astype(RW_MM), bd(g[:ln].astype(RW_MM))) for x, g in zip(x_w, gy)]
    y = [g[ln:] + _dot(rb, bd(uu.astype(RW_MM))) for g, rb, uu in zip(gy, m_rb, u)]
    upd = [_dot_ta(jnp.concatenate([uu.astype(RW_MM), v], axis=0), x[3]) for uu, v, x in zip(u, vs, sc)]
    s_new = [s * jnp.exp(gt) + jnp.where(same_head, up, 0.0) for s, gt, up in zip(s_olds, gtot, upd)]
    return list(zip(y, s_new))


def _rwkv_scan_kernel(shf_ref, shb_ref, drf_ref, drb_ref, lwf_ref, lwb_ref, yf_ref, yb_ref, s_s):
    @pl.when(pl.program_id(1) == 0)
    def _():
        s_s[...] = jnp.zeros_like(s_s)

    chains, where = [], []
    for bb in range(RW_BATCH):
        for p in range(RW_PAIRS):
            chains.append((shf_ref[bb, 0, p], shf_ref[bb, 1, p], shf_ref[bb, 2, p], drf_ref[bb, 0, p],
                           drf_ref[bb, 1, p], lwf_ref[bb, p], s_s[bb, 0, p], False))
            where.append((yf_ref, bb, 0, p))
            chains.append((shb_ref[bb, 0, p], shb_ref[bb, 1, p], shb_ref[bb, 2, p], drb_ref[bb, 0, p],
                           drb_ref[bb, 1, p], lwb_ref[bb, p], s_s[bb, 1, p], True))
            where.append((yb_ref, bb, 1, p))
    for (y, s_new), (y_ref, bb, d, p) in zip(_rwkv_chunk_group(chains), where):
        y_ref[bb, p] = y.astype(y_ref.dtype)
        s_s[bb, d, p] = s_new


def _rwkv_scan_call(sh, dr, lw, lc):
    bsz, _, _, t, _ = sh.shape
    ln = RW_CHUNK
    nb = RW_BATCH
    assert bsz % nb == 0
    n = t // ln
    nctx = lc // ln
    rv = lambda i: _rev_chunk(i, nctx, n)
    return pl.pallas_call(
        _rwkv_scan_kernel,
        grid=(bsz // nb, n),
        in_specs=[pl.BlockSpec((nb, 3, RW_PAIRS, ln, LANES), lambda bi, i: (bi, 0, 0, i, 0)),
                  pl.BlockSpec((nb, 3, RW_PAIRS, ln, LANES), lambda bi, i: (bi, 0, 0, rv(i), 0)),
                  pl.BlockSpec((nb, None, 2, RW_PAIRS, ln, LANES), lambda bi, i: (bi, 0, 0, 0, i, 0)),
                  pl.BlockSpec((nb, None, 2, RW_PAIRS, ln, LANES), lambda bi, i: (bi, 1, 0, 0, rv(i), 0)),
                  pl.BlockSpec((nb, None, RW_PAIRS, ln, LANES), lambda bi, i: (bi, 0, 0, i, 0)),
                  pl.BlockSpec((nb, None, RW_PAIRS, ln, LANES), lambda bi, i: (bi, 1, 0, rv(i), 0))],
        out_specs=[pl.BlockSpec((nb, RW_PAIRS, ln, LANES), lambda bi, i: (bi, 0, i, 0)),
                   pl.BlockSpec((nb, RW_PAIRS, ln, LANES), lambda bi, i: (bi, 0, rv(i), 0))],
        out_shape=[jax.ShapeDtypeStruct((bsz, RW_PAIRS, t, LANES), BF16)] * 2,
        scratch_shapes=[pltpu.VMEM((nb, 2, RW_PAIRS, LANES, LANES), F32)],
        compiler_params=_params(("parallel", "arbitrary")),
        name="rwkv_scan",
    )(sh, sh, dr, dr, lw, lw)


def _rwkv_out_kernel(z_ref, yf_ref, yb_ref, gate_ref, bonus_ref, lng_ref, lnb_ref, e_ref, et_ref, w_ref, mod_ref,
                     g2_ref, w1_ref, w2_ref, o_ref):
    y = jnp.concatenate([yf_ref[p].astype(F32) + yb_ref[p].astype(F32) for p in range(RW_PAIRS)], axis=1)
    yc = y - _head_sum(y, e_ref, et_ref, 2) * (1.0 / RW_HEAD)
    var = _head_sum(yc * yc, e_ref, et_ref, 1) * (1.0 / RW_HEAD)
    yn = yc * lax.rsqrt(var + RW_LN_EPS) * lng_ref[...] + lnb_ref[...] + bonus_ref[...]
    out = _dot((yn * gate_ref[...]).astype(BF16), w_ref[...])
    o_ref[...] = _mlp_tail(z_ref[...] + mod_ref[2:3, :] * out, g2_ref[...], mod_ref, w1_ref, w2_ref)


def _rwkv_out_call(z, yf, yb, gate, bonus, ln_g, ln_b, e, et, w, mod, g2, w1, w2, nb, lc, skip):
    bsz, t, d = z.shape
    tm = TOKEN_TILE
    nctx = lc // tm
    nt = t // tm - skip
    tok = pl.BlockSpec((None, tm, d), lambda bi, i: (bi, i + skip, 0))
    pair = pl.BlockSpec((None, RW_PAIRS, tm, LANES), lambda bi, i: (bi, 0, i + skip, 0))
    return pl.pallas_call(
        _rwkv_out_kernel,
        grid=(bsz, nt),
        in_specs=[tok, pair, pair, tok, tok,
                  _const_spec((1, d)), _const_spec((1, d)), _const_spec((d, LANES)), _const_spec((LANES, d)),
                  _const_spec((d, d)),
                  pl.BlockSpec((None, N_MOD, d), lambda bi, i: (jnp.where(i + skip < nctx, nb, bi), 0, 0)),
                  _const_spec((1, d)),
                  _const_spec((d, D_FF)),
                  _const_spec((D_FF, d))],
        out_specs=pl.BlockSpec((None, tm, d), lambda bi, i: (bi, i, 0)),
        out_shape=jax.ShapeDtypeStruct((bsz, nt * tm, d), F32),
        compiler_params=_params(("parallel", "parallel")),
        name="rwkv_out_mlp",
    )(z, yf, yb, gate, bonus, ln_g, ln_b, e, et, w, mod, g2, w1, w2)


def _group_mean_matrix(n, width):
    idx = jnp.arange(n) // width
    return ((idx[:, None] == idx[None, :]).astype(F32) / width).astype(BF16)


def _rope_tables(lc, s):
    quarter = AT_DH // 4
    inv = ROPE_BASE ** (-jnp.arange(quarter, dtype=F32) / quarter)
    pos = jnp.arange(s)
    rpos = (pos // GRID_W).astype(F32)
    cpos = (pos % GRID_W).astype(F32)
    ang_r = rpos[:, None] * inv[None, :]
    ang_c = cpos[:, None] * inv[None, :]
    cos64 = jnp.concatenate([jnp.cos(ang_r), jnp.cos(ang_r), jnp.cos(ang_c), jnp.cos(ang_c)], axis=1)
    sin64 = jnp.concatenate([-jnp.sin(ang_r), jnp.sin(ang_r), -jnp.sin(ang_c), jnp.sin(ang_c)], axis=1)
    cos = jnp.concatenate([jnp.ones((lc, AT_DH), F32), cos64], axis=0)
    sin = jnp.concatenate([jnp.zeros((lc, AT_DH), F32), sin64], axis=0)
    return jnp.tile(cos, (1, 2)), jnp.tile(sin, (1, 2))


def _even_weights(w_in, b_in):
    def cols(m):
        sizes = (ML_W, ML_W, ML_W, ML_W, ML_GATE_COLS, AT_W, AT_KV_HEADS * AT_DH, AT_KV_HEADS * AT_DH)
        bounds = [sum(sizes[:i + 1]) for i in range(len(sizes) - 1)]
        mq, mk, mv, mo, mg, aq, ak, av = jnp.split(m, bounds, axis=-1)
        dup = lambda u: jnp.concatenate([u[..., :AT_DH], u[..., :AT_DH], u[..., AT_DH:], u[..., AT_DH:]], axis=-1)
        mgp = jnp.pad(mg, [(0, 0)] * (m.ndim - 1) + [(0, LANES - ML_GATE_COLS)])
        return jnp.concatenate([mq, mk, mv, mo, aq, dup(ak), dup(av), mgp], axis=-1)
    return cols(w_in).astype(BF16), cols(b_in[None, :])


def kernel(x, c, ctx, c_ctx, ada_w, ada_b, norm1_g, norm2_g, mlp_w1, mlp_w2, ev_w_in, ev_b_in, ev_w_out, ml_f_bias, ml_out_g, at_q_g, at_k_g, at_sink, rw_mu, rw_w_rkv, rw_w0, rw_w1, rw_w2, rw_a0, rw_a1, rw_a2, rw_g1, rw_g2, rw_k_k, rw_k_a, rw_r_k, rw_ln_g, rw_ln_b, rw_w_out):
    bsz, s, d = x.shape
    lc = ctx.shape[1]
    depth = ada_w.shape[0]
    assert d == D_MODEL and lc == TOKEN_TILE and s % TOKEN_TILE == 0 and bsz < MOD_ROWS
    z = (ctx, x)

    cstack = jnp.zeros((MOD_ROWS, d), F32).at[:bsz].set(c).at[bsz].set(c_ctx)
    mod_all = _ada_call(cstack, ada_w, ada_b).reshape(depth, MOD_ROWS, N_MOD, d)

    grp512 = _group_mean_matrix(AT_W, AT_DH)
    head_e = (jnp.arange(d)[:, None] // RW_HEAD == jnp.arange(LANES)[None, :]).astype(BF16)
    head_et = head_e.T
    cos_t, sin_t = _rope_tables(lc, s)
    gate_is_f = ((jnp.arange(LANES) % 8 >= 4) & (jnp.arange(LANES) < 16)).astype(F32)[None, :]

    for layer in range(depth):
        mod = mod_all[layer]
        j = layer // 2
        n1 = norm1_g[layer][None, :]
        mlp_args = (norm2_g[layer][None, :], mlp_w1[layer].astype(BF16), mlp_w2[layer].astype(BF16))
        skip = lc // TOKEN_TILE if layer == depth - 1 else 0
        if layer % 2 == 0:
            w_in, b_in = _even_weights(ev_w_in[j], ev_b_in[j])
            fb = jnp.zeros((LANES,), F32)
            for dr in range(2):
                fb = fb.at[8 * dr + 4:8 * dr + 8].set(ml_f_bias[j, dr])
            outs = _even_in_call(
                z, n1, mod, w_in, b_in, jnp.tile(at_q_g[j], AT_HEADS)[None, :],
                jnp.tile(at_k_g[j], 2 * AT_KV_HEADS)[None, :], grp512, cos_t, sin_t, fb[None, :], gate_is_f, bsz)
            if isinstance(z, tuple):
                z = outs[-1]
            mqv, kt, mo, aq, akt, av, gates, gates_t = outs[:8]
            hf, hb = _mlstm_call(mqv, kt, gates, gates_t, lc)
            sink = jnp.broadcast_to(at_sink[j][:, None], (AT_HEADS, LANES))
            ya = _attn_call(aq, akt, av, sink, lc)
            z = _even_out_call(z, hf, hb, mo, ya, ml_out_g[j][None, :], ev_w_out[j].astype(BF16), mod, *mlp_args,
                               bsz, skip)
        else:
            pad_g = RW_GATE_PAD - RW_GATE_LORA
            zeros = jnp.zeros((RW_DECAY_LORA, d), F32)
            w2 = jnp.concatenate([jnp.concatenate([rw_w2[j, 0], zeros], axis=1),
                                  jnp.concatenate([zeros, rw_w2[j, 1]], axis=1)], axis=0)
            a2 = jnp.concatenate([jnp.concatenate([rw_a2[j, 0], zeros], axis=1),
                                  jnp.concatenate([zeros, rw_a2[j, 1]], axis=1)], axis=0)
            sh, dr_, lw, gate, bonus = _rwkv_in_call(
                z, n1, mod, rw_mu[j].reshape(12, d), rw_w_rkv[j].astype(BF16),
                jnp.concatenate([rw_w1[j, 0], rw_w1[j, 1]], axis=1).astype(BF16),
                jnp.concatenate([rw_a1[j, 0], rw_a1[j, 1]], axis=1).astype(BF16),
                jnp.pad(rw_g1[j], ((0, 0), (0, pad_g))).astype(BF16),
                w2.astype(BF16), a2.astype(BF16),
                jnp.pad(rw_g2[j], ((0, pad_g), (0, 0))).astype(BF16),
                rw_w0[j], rw_a0[j], rw_k_k[j][None, :], rw_k_a[j][None, :], rw_r_k[j].reshape(1, d),
                head_e, head_et, bsz, lc)
            yf, yb = _rwkv_scan_call(sh, dr_, lw, lc)
            z = _rwkv_out_call(z, yf, yb, gate, bonus, rw_ln_g[j][None, :], rw_ln_b[j][None, :], head_e, head_et,
                               rw_w_out[j].astype(BF16), mod, *mlp_args, bsz, lc, skip)
    return z
```

```python
import functools

import jax
import jax.numpy as jnp
from jax import lax
from jax.experimental import pallas as pl
from jax.experimental.pallas import tpu as pltpu

F32 = jnp.float32
BF16 = jnp.bfloat16

D_MODEL = 1024
N_MOD = 6
D_FF = 4 * D_MODEL
EPS = 1e-6
GRID_W = 64

ML_HEADS = 4
ML_D = 128
ML_W = ML_HEADS * ML_D
ML_CHUNK = 128
GATE_CAP = 15.0

AT_DH = 64
AT_HEADS = 8
AT_KV_HEADS = 2
AT_W = AT_HEADS * AT_DH
AT_BLOCK = 128
AT_KV_DUP = 2 * AT_KV_HEADS * AT_DH
AT_BATCH = 1
ROPE_BASE = 10000.0

RW_HEAD = 64
RW_PAIRS = D_MODEL // 128
RW_CHUNK = 64
RW_LN_EPS = 64e-5
RW_DECAY_LORA = 64
RW_AAA_LORA = 64
RW_GATE_LORA = 160
RW_GATE_PAD = 256
RW_DECAY_SCALE = 0.6065306597126334

LANES = 128
TOKEN_TILE = 256
FF_TILE = 1024
MOD_ROWS = 16
VMEM_LIMIT = 56 * 1024 * 1024
NEG = -1e30
RW_MM = BF16
RW_BATCH = 4
RW_ROW_TILE = 8

EV_MO = 3 * ML_W
EV_AQ = EV_MO + ML_W
EV_AK = EV_AQ + AT_W
EV_AV = EV_AK + AT_KV_DUP
EV_GATE = EV_AV + AT_KV_DUP
EV_COLS = EV_GATE + LANES
ML_GATE_COLS = 4 * ML_HEADS


def _dot(a, b):
    return jnp.dot(a, b, preferred_element_type=F32)


def _dot_tb(a, b):
    return lax.dot_general(a, b, (((1,), (1,)), ((), ())), preferred_element_type=F32)


def _dot_ta(a, b):
    return lax.dot_general(a, b, (((0,), (0,)), ((), ())), preferred_element_type=F32)


def _norm_mod(z, g, shift, scale):
    ms = jnp.mean(z * z, axis=-1, keepdims=True)
    return (z * lax.rsqrt(ms + EPS) * g) * (1.0 + scale) + shift


def _sigmoid(x):
    return 1.0 / (1.0 + jnp.exp(-x))


def _head_sum(x, e_ref, et_ref, terms):
    s = _dot(x.astype(BF16), e_ref[...])
    hi = s.astype(BF16)
    out = _dot(hi, et_ref[...])
    if terms > 1:
        out = out + _dot((s - hi.astype(F32)).astype(BF16), et_ref[...])
    return out


def _const_spec(shape):
    nd = len(shape)
    return pl.BlockSpec(shape, lambda *_: (0,) * nd, pipeline_mode=pl.Buffered(1))


def _params(sem):
    return pltpu.CompilerParams(dimension_semantics=sem, vmem_limit_bytes=VMEM_LIMIT)


def _ada_kernel(c_ref, w_ref, b_ref, o_ref):
    cv = c_ref[...]
    s = cv * _sigmoid(cv)
    o_ref[...] = jnp.dot(s, w_ref[...], preferred_element_type=F32,
                         precision=lax.Precision.HIGHEST) + b_ref[...]


def _ada_call(cstack, ada_w, ada_b):
    depth = ada_w.shape[0]
    n = N_MOD * D_MODEL
    tn = 1024
    return pl.pallas_call(
        _ada_kernel,
        grid=(depth, n // tn),
        in_specs=[pl.BlockSpec((MOD_ROWS, D_MODEL), lambda l, j: (0, 0)),
                  pl.BlockSpec((None, D_MODEL, tn), lambda l, j: (l, 0, j)),
                  pl.BlockSpec((None, 1, tn), lambda l, j: (l, 0, j))],
        out_specs=pl.BlockSpec((None, MOD_ROWS, tn), lambda l, j: (l, 0, j)),
        out_shape=jax.ShapeDtypeStruct((depth, MOD_ROWS, n), F32),
        compiler_params=_params(("parallel", "parallel")),
        name="ada_ln",
    )(cstack, ada_w, ada_b.reshape(depth, 1, n))


def _rope(x, cos, sin, lane_lo):
    n = x.shape[1]
    up = pltpu.roll(x, n - 16, 1)
    dn = pltpu.roll(x, 16, 1)
    reps = n // LANES
    c = jnp.concatenate([cos] * reps, axis=1) if reps > 1 else cos
    s = jnp.concatenate([sin] * reps, axis=1) if reps > 1 else sin
    lo = jnp.concatenate([lane_lo] * reps, axis=1) if reps > 1 else lane_lo
    return x * c + jnp.where(lo, up, dn) * s


def _even_in_kernel(*refs, assemble):
    if assemble:
        ctx_ref, x_ref, *refs, z_out_ref = refs
        z = jnp.where(pl.program_id(1) == 0, ctx_ref[...], x_ref[...])
        z_out_ref[...] = z
    else:
        z_ref, *refs = refs
        z = z_ref[...]
    (g_ref, mod_ref, w_ref, b_ref, gq_ref, gk_ref, grp_ref, cos_ref, sin_ref, fb_ref, isf_ref,
     mqv_ref, kt_ref, mo_ref, aq_ref, akt_ref, av_ref, gate_ref, gatet_ref) = refs
    h = _norm_mod(z, g_ref[...], mod_ref[0:1, :], mod_ref[1:2, :])
    p = _dot(h.astype(BF16), w_ref[...]) + b_ref[...]
    mqv_ref[:, 0:ML_W] = p[:, 0:ML_W].astype(BF16)
    mqv_ref[:, ML_W:2 * ML_W] = p[:, 2 * ML_W:3 * ML_W].astype(BF16)
    kt_ref[...] = (p[:, ML_W:2 * ML_W] * (ML_D ** -0.5)).T.astype(BF16)
    mo_ref[...] = p[:, EV_MO:EV_MO + ML_W].astype(mo_ref.dtype)
    cos = cos_ref[...]
    sin = sin_ref[...]
    lane_lo = (lax.broadcasted_iota(jnp.int32, (1, LANES), 1) % 32) < 16
    q = p[:, EV_AQ:EV_AQ + AT_W]
    qms = _dot((q * q).astype(BF16), grp_ref[...])
    q = q * lax.rsqrt(qms + EPS) * gq_ref[...]
    aq_ref[...] = (_rope(q, cos, sin, lane_lo) * (AT_DH ** -0.5)).astype(BF16)
    k = p[:, EV_AK:EV_AK + AT_KV_DUP]
    kms = _dot((k * k).astype(BF16), grp_ref[0:AT_KV_DUP, 0:AT_KV_DUP])
    k = k * lax.rsqrt(kms + EPS) * gk_ref[...]
    akt_ref[...] = _rope(k, cos, sin, lane_lo).T.astype(BF16)
    av_ref[...] = p[:, EV_AV:EV_AV + AT_KV_DUP].astype(BF16)
    gt = p[:, EV_GATE:EV_GATE + LANES] + fb_ref[...]
    sc = GATE_CAP * jnp.tanh(gt * (1.0 / GATE_CAP))
    logsig = jnp.minimum(sc, 0.0) - jnp.log(1.0 + jnp.exp(-jnp.abs(sc)))
    gates = jnp.where(isf_ref[...] > 0.5, logsig, sc)
    gate_ref[...] = gates
    gatet_ref[...] = gates.T[0:ML_GATE_COLS, :]


def _even_in_call(z, g, mod, w, b, gq, gk, grp, cos, sin, fb, isf, nb):
    tm = TOKEN_TILE
    tok = lambda width: pl.BlockSpec((None, tm, width), lambda bi, i: (bi, i, 0))
    tpose = lambda rows: pl.BlockSpec((None, rows, tm), lambda bi, i: (bi, 0, i))
    assemble = isinstance(z, tuple)
    if assemble:
        ctx, x = z
        bsz, s, d = x.shape
        t = ctx.shape[1] + s
        z_args = (ctx, x)
        z_specs = [pl.BlockSpec((None, tm, d), lambda bi, i: (bi, 0, 0)),
                   pl.BlockSpec((None, tm, d), lambda bi, i: (bi, jnp.maximum(i - 1, 0), 0))]
        extra_specs, extra_shapes = [tok(d)], [jax.ShapeDtypeStruct((bsz, t, d), F32)]
    else:
        bsz, t, d = z.shape
        z_args, z_specs, extra_specs, extra_shapes = (z,), [tok(d)], [], []
    return pl.pallas_call(
        functools.partial(_even_in_kernel, assemble=assemble),
        grid=(bsz, t // tm),
        in_specs=z_specs + [
                  _const_spec((1, d)),
                  pl.BlockSpec((None, N_MOD, d), lambda bi, i: (jnp.where(i == 0, nb, bi), 0, 0)),
                  _const_spec((d, EV_COLS)),
                  _const_spec((1, EV_COLS)),
                  _const_spec((1, AT_W)),
                  _const_spec((1, AT_KV_DUP)),
                  _const_spec((AT_W, AT_W)),
                  pl.BlockSpec((tm, LANES), lambda bi, i: (i, 0)),
                  pl.BlockSpec((tm, LANES), lambda bi, i: (i, 0)),
                  _const_spec((1, LANES)),
                  _const_spec((1, LANES))],
        out_specs=[tok(2 * ML_W), tpose(ML_W), tok(ML_W), tok(AT_W), tpose(AT_KV_DUP), tok(AT_KV_DUP), tok(LANES),
                   tpose(ML_GATE_COLS)] + extra_specs,
        out_shape=[jax.ShapeDtypeStruct((bsz, t, 2 * ML_W), BF16),
                   jax.ShapeDtypeStruct((bsz, ML_W, t), BF16),
                   jax.ShapeDtypeStruct((bsz, t, ML_W), BF16),
                   jax.ShapeDtypeStruct((bsz, t, AT_W), BF16),
                   jax.ShapeDtypeStruct((bsz, AT_KV_DUP, t), BF16),
                   jax.ShapeDtypeStruct((bsz, t, AT_KV_DUP), BF16),
                   jax.ShapeDtypeStruct((bsz, t, LANES), F32),
                   jax.ShapeDtypeStruct((bsz, ML_GATE_COLS, t), F32)] + extra_shapes,
        compiler_params=_params(("parallel", "parallel")),
        name="even_in",
    )(*z_args, g, mod, w, b, gq, gk, grp, cos, sin, fb, isf)


def _split3(x):
    hi = x.astype(BF16)
    r1 = x - hi.astype(F32)
    mid = r1.astype(BF16)
    return hi, mid, (r1 - mid.astype(F32)).astype(BF16)


def _mlstm_kernel(qf_ref, ktf_ref, vf_ref, qb_ref, ktb_ref, vb_ref, gcf_ref, gcb_ref, grf_ref, grb_ref, sel_ref,
                  hf_ref, hb_ref, c_s, m_s):
    ln = ML_CHUNK

    @pl.when(pl.program_id(1) == 0)
    def _():
        c_s[...] = jnp.zeros_like(c_s)
        m_s[...] = jnp.zeros_like(m_s)

    ri = lax.broadcasted_iota(jnp.int32, (ln, ln), 0)
    ci = lax.broadcasted_iota(jnp.int32, (ln, ln), 1)
    hi = lax.Precision.HIGHEST
    before = (ci <= ri, ci >= ri)
    refs = ((qf_ref, ktf_ref, vf_ref, gcf_ref, grf_ref, hf_ref), (qb_ref, ktb_ref, vb_ref, gcb_ref, grb_ref, hb_ref))
    cum_r = [jnp.dot(refs[d][4][...], before[1 - d].astype(F32), preferred_element_type=F32, precision=hi)
             for d in range(2)]
    bc_all = []
    for d in range(2):
        cum_c = sum(_dot(before[d].astype(BF16), part) for part in _split3(refs[d][3][...]))
        sel = sel_ref[d]
        bc_all.append(sum(_dot(part, sel) for part in _split3(cum_c)))
    chains = [(d, hd) for d in range(2) for hd in range(ML_HEADS)]
    lanes = [slice(hd * ML_D, (hd + 1) * ML_D) for _, hd in chains]
    st = [d * ML_HEADS + hd for d, hd in chains]
    ones = jnp.ones((ln, ML_D), BF16)
    q = [refs[d][0][:, ls] for (d, _), ls in zip(chains, lanes)]
    kt = [refs[d][1][ls, :] for (d, _), ls in zip(chains, lanes)]
    v1 = [jnp.concatenate([refs[d][2][:, ls], ones], axis=1) for (d, _), ls in zip(chains, lanes)]
    i_row = [refs[d][4][8 * d + hd:8 * d + hd + 1, :] for d, hd in chains]
    f_row = [refs[d][4][8 * d + 4 + hd:8 * d + 5 + hd, :] for d, hd in chains]
    bcum_row = [cum_r[d][8 * d + 4 + hd:8 * d + 5 + hd, :] for d, hd in chains]
    bcum = [bc_all[d][:, ls] for (d, _), ls in zip(chains, lanes)]
    m_old = [m_s[s:s + 1, 0:1] for s in st]
    c_old = [c_s[s] for s in st]
    qk = [_dot(a, b) for a, b in zip(q, kt)]
    dlog =[jnp.where(before[d], bc - br + ir, NEG) for (d, _), bc, br, ir in zip(chains, bcum, bcum_row, i_row)]
    inter = [bc + m for bc, m in zip(bcum, m_old)]
    mt = [jnp.maximum(jnp.max(dl, axis=1, keepdims=True), it) for dl, it in zip(dlog, inter)]
    sc = [x * jnp.exp(dl - m) for x, dl, m in zip(qk, dlog, mt)]
    iw = [jnp.exp(it - m) for it, m in zip(inter, mt)]
    tots = [_dot(jnp.concatenate([s.astype(BF16), (w * a.astype(F32)).astype(BF16)], axis=1),
                 jnp.concatenate([b, c.astype(BF16)], axis=0)) for s, w, a, b, c in zip(sc, iw, q, v1, c_old)]
    for j, (d, _) in enumerate(chains):
        tot = tots[j]
        den =jnp.maximum(jnp.abs(tot[:, ML_D:]), jnp.exp(-mt[j]))
        refs[d][5][:, lanes[j]] = (tot[:, :ML_D] / den).astype(refs[d][5].dtype)
    bl = [jnp.sum(fr, axis=1, keepdims=True) for fr in f_row]
    ws_log = [b - br + ir for b, br, ir in zip(bl, bcum_row, i_row)]
    m_new = [jnp.maximum(b + m, jnp.max(w, axis=1, keepdims=True)) for b, m, w in zip(bl, m_old, ws_log)]
    ws = [jnp.exp(w - m) for w, m in zip(ws_log, m_new)]
    upd = [_dot((a.astype(F32) * w).astype(BF16), b) for a, w, b in zip(kt, ws, v1)]
    for j, s in enumerate(st):
        c_s[s] = jnp.exp(bl[j] + m_old[j] - m_new[j]) * c_old[j] + upd[j]
        m_s[s:s + 1, :] = jnp.broadcast_to(m_new[j], (1, LANES))


def _rev_chunk(i, nctx, n):
    return jnp.where(i < nctx, nctx - 1 - i, n + nctx - 1 - i)


def _mlstm_call(mqv, kt, gates, gates_t, lc):
    bsz, t, _ = mqv.shape
    ln = ML_CHUNK
    assert ln == ML_D == LANES
    n = t // ln
    nctx = lc // ln
    rv = lambda i: _rev_chunk(i, nctx, n)
    col = jnp.arange(LANES)[None, :, None]
    want = (8 * jnp.arange(2)[:, None, None] + 4 + jnp.arange(ML_HEADS * ln)[None, None, :] // ln)
    sel = (col == want).astype(BF16)
    return pl.pallas_call(
        _mlstm_kernel,
        grid=(bsz, n),
        in_specs=[pl.BlockSpec((None, ln, ML_W), lambda bi, i: (bi, i, 0)),
                  pl.BlockSpec((None, ML_W, ln), lambda bi, i: (bi, 0, i)),
                  pl.BlockSpec((None, ln, ML_W), lambda bi, i: (bi, i, 1)),
                  pl.BlockSpec((None, ln, ML_W), lambda bi, i: (bi, rv(i), 0)),
                  pl.BlockSpec((None, ML_W, ln), lambda bi, i: (bi, 0, rv(i))),
                  pl.BlockSpec((None, ln, ML_W), lambda bi, i: (bi, rv(i), 1)),
                  pl.BlockSpec((None, ln, LANES), lambda bi, i: (bi, i, 0)),
                  pl.BlockSpec((None, ln, LANES), lambda bi, i: (bi, rv(i), 0)),
                  pl.BlockSpec((None, ML_GATE_COLS, ln), lambda bi, i: (bi, 0, i)),
                  pl.BlockSpec((None, ML_GATE_COLS, ln), lambda bi, i: (bi, 0, rv(i))),
                  _const_spec((2, LANES, ML_HEADS * ln))],
        out_specs=[pl.BlockSpec((None, ln, ML_W), lambda bi, i: (bi, i, 0)),
                   pl.BlockSpec((None, ln, ML_W), lambda bi, i: (bi, rv(i), 0))],
        out_shape=[jax.ShapeDtypeStruct((bsz, t, ML_W), BF16)] * 2,
        scratch_shapes=[pltpu.VMEM((2 * ML_HEADS, ML_D, 2 * ML_D), F32),
                        pltpu.VMEM((2 * ML_HEADS, LANES), F32)],
        compiler_params=_params(("parallel", "arbitrary")),
        name="mlstm_scan",
    )(mqv, kt, mqv, mqv, kt, mqv, gates, gates, gates_t, gates_t, sel)


def _attn_kernel(q_ref, ktp_ref, kto_ref, ktn_ref, ktc_ref, vp_ref, vo_ref, vn_ref, vc_ref, sink_ref, o_ref,
                 *, nctx, n):
    j = pl.program_id(1)
    blk = AT_BLOCK
    nk = 3 * blk + ktc_ref.shape[2]
    qi = lax.broadcasted_iota(jnp.int32, (blk, nk), 0)
    ki = lax.broadcasted_iota(jnp.int32, (blk, nk), 1)
    lo = jnp.where(j > nctx, qi, blk)
    hi = jnp.where(j < n - 1, qi + 2 * blk, 2 * blk - 1)
    band = jnp.where(ki < lo, NEG, jnp.where(ki > hi, NEG, jnp.where(j >= nctx, 0.0, NEG)))
    bias = jnp.where(ki >= 3 * blk, 0.0, band)
    lane = lax.broadcasted_iota(jnp.int32, (1, LANES), 1)
    sub = lax.broadcasted_iota(jnp.int32, (LANES, 1), 0)
    lane_half = (lane < AT_DH, lane >= AT_DH)
    sub_half = (sub < AT_DH, sub >= AT_DH)
    kts = (ktp_ref, kto_ref, ktn_ref, ktc_ref)
    vs = (vp_ref, vo_ref, vn_ref, vc_ref)
    zero = jnp.zeros((), BF16)
    nbat = q_ref.shape[0]
    kx, vx = {}, {}
    for bb in range(nbat):
        for g in range(AT_KV_HEADS):
            rows = slice(g * LANES, (g + 1) * LANES)
            for e in range(2):
                kx[bb, g, e] = jnp.concatenate([jnp.where(sub_half[e], r[bb, rows, :], zero) for r in kts], axis=1)
                vx[bb, g, e] = jnp.concatenate(
                    [jnp.concatenate([jnp.where(lane_half[e], r[bb, :, rows], zero),
                                      jnp.ones((r.shape[1], LANES), BF16)], axis=1) for r in vs], axis=0)
    heads = [(bb, hd // 2, hd % 2, hd // 4) for bb in range(nbat) for hd in range(AT_HEADS)]
    qc = {(bb, c): q_ref[bb, :, c * LANES:(c + 1) * LANES] for bb in range(nbat) for c in range(AT_HEADS // 2)}
    sink = [sink_ref[2 * c + e:2 * c + e + 1, 0:1] for _, c, e, _ in heads]
    ss = [_dot(qc[bb, c], kx[bb, g, e]) + bias for bb, c, e, g in heads]
    m = [jnp.maximum(jnp.max(s, axis=1, keepdims=True), sk) for s, sk in zip(ss, sink)]
    res = [_dot(jnp.exp(s - mm).astype(BF16), vx[bb, g, e]) for s, mm, (bb, c, e, g) in zip(ss, m, heads)]
    outs =[r[:, :LANES] / (r[:, LANES:] + jnp.exp(sk - mm)) for r, sk, mm in zip(res, sink, m)]
    for i in range(0, len(heads), 2):
        bb, c = heads[i][0], heads[i][1]
        o_ref[bb, :, c * LANES:(c + 1) * LANES] = (outs[i] + outs[i + 1]).astype(o_ref.dtype)


def _attn_call(aq, akt, av, sink, lc):
    bsz, t, _ = aq.shape
    blk = AT_BLOCK
    nb = AT_BATCH
    assert bsz % nb == 0
    n = t // blk
    nctx = lc // blk
    prev = lambda j: jnp.clip(j - 1, nctx, n - 1)
    own = lambda j: jnp.clip(j, nctx, n - 1)
    nxt = lambda j: jnp.clip(j + 1, nctx, n - 1)
    kt = lambda f: pl.BlockSpec((nb, AT_KV_DUP, blk), lambda bi, j: (bi, 0, f(j)))
    vv = lambda f: pl.BlockSpec((nb, blk, AT_KV_DUP), lambda bi, j: (bi, f(j), 0))
    return pl.pallas_call(
        functools.partial(_attn_kernel, nctx=nctx, n=n),
        grid=(bsz // nb, n),
        in_specs=[pl.BlockSpec((nb, blk, AT_W), lambda bi, j: (bi, j, 0)),
                  kt(prev), kt(own), kt(nxt),
                  pl.BlockSpec((nb, AT_KV_DUP, lc), lambda bi, j: (bi, 0, 0)),
                  vv(prev), vv(own), vv(nxt),
                  pl.BlockSpec((nb, lc, AT_KV_DUP), lambda bi, j: (bi, 0, 0)),
                  _const_spec((AT_HEADS, LANES))],
        out_specs=pl.BlockSpec((nb, blk, AT_W), lambda bi, j: (bi, j, 0)),
        out_shape=jax.ShapeDtypeStruct((bsz, t, AT_W), BF16),
        compiler_params=_params(("parallel", "parallel")),
        name="window_attn",
    )(aq, akt, akt, akt, akt, av, av, av, av, sink)


def _mlp_tail(z, g, mod_ref, w1_ref, w2_ref):
    h = _norm_mod(z, g, mod_ref[3:4, :], mod_ref[4:5, :]).astype(BF16)
    acc = jnp.zeros(z.shape, F32)
    for f in range(D_FF // FF_TILE):
        a = jnp.maximum(_dot(h, w1_ref[:, f * FF_TILE:(f + 1) * FF_TILE]), 0.0)
        acc = acc + _dot((a * a).astype(BF16), w2_ref[f * FF_TILE:(f + 1) * FF_TILE, :])
    return z + mod_ref[5:6, :] * acc


def _even_out_kernel(z_ref, hf_ref, hb_ref, mo_ref, ya_ref, og_ref, w_ref, mod_ref, g2_ref, w1_ref, w2_ref, o_ref):
    hs = hf_ref[...].astype(F32) + hb_ref[...].astype(F32)
    parts = []
    for hd in range(ML_HEADS):
        x = hs[:, hd * ML_D:(hd + 1) * ML_D]
        parts.append(x * lax.rsqrt(jnp.mean(x * x, axis=1, keepdims=True) + EPS))
    ym = jnp.concatenate(parts, axis=1) * og_ref[...] * _sigmoid(mo_ref[...].astype(F32))
    y = _dot(ym.astype(BF16), w_ref[0:ML_W, :]) + _dot(ya_ref[...], w_ref[ML_W:ML_W + AT_W, :])
    o_ref[...] = _mlp_tail(z_ref[...] + mod_ref[2:3, :] * y, g2_ref[...], mod_ref, w1_ref, w2_ref)


def _even_out_call(z, hf, hb, mo, ya, og, w, mod, g2, w1, w2, nb, skip):
    bsz, t, d = z.shape
    tm = TOKEN_TILE
    nt = t // tm - skip
    tok = lambda width: pl.BlockSpec((None, tm, width), lambda bi, i: (bi, i + skip, 0))
    return pl.pallas_call(
        _even_out_kernel,
        grid=(bsz, nt),
        in_specs=[tok(d), tok(ML_W), tok(ML_W), tok(ML_W), tok(AT_W),
                  _const_spec((1, ML_W)),
                  _const_spec((ML_W + AT_W, d)),
                  pl.BlockSpec((None, N_MOD, d), lambda bi, i: (jnp.where(i + skip == 0, nb, bi), 0, 0)),
                  _const_spec((1, d)),
                  _const_spec((d, D_FF)),
                  _const_spec((D_FF, d))],
        out_specs=pl.BlockSpec((None, tm, d), lambda bi, i: (bi, i, 0)),
        out_shape=jax.ShapeDtypeStruct((bsz, nt * tm, d), F32),
        compiler_params=_params(("parallel", "parallel")),
        name="even_out_mlp",
    )(z, hf, hb, mo, ya, og, w, mod, g2, w1, w2)


def _rwkv_in_kernel(z_ref, zp_ref, zn_ref, g_ref, mod_ref, mu_ref, wrkv_ref, w1_ref, a1_ref, g1_ref,
                    w2_ref, a2_ref, g2_ref, w0_ref, a0_ref, kk_ref, ka_ref, rk_ref, e_ref, et_ref,
                    sh_ref, dr_ref, lw_ref, gate_ref, bonus_ref, *, nctx, ntile):
    i = pl.program_id(1)
    tm, d = z_ref.shape
    g = g_ref[...]
    shift = mod_ref[0:1, :]
    scale = mod_ref[1:2, :]
    h = _norm_mod(z_ref[...], g, shift, scale)
    no_prev = jnp.logical_or(i == 0, i == nctx)
    no_next = jnp.logical_or(i == nctx - 1, i == ntile - 1)
    hp = jnp.where(no_prev, 0.0, _norm_mod(zp_ref[7:8, :], g, shift, scale))
    hn = jnp.where(no_next, 0.0, _norm_mod(zn_ref[0:1, :], g, shift, scale))
    row = lax.broadcasted_iota(jnp.int32, (tm, 1), 0)
    dp = jnp.where(row == 0, hp, pltpu.roll(h, 1, 0)) - h
    dn = jnp.where(row == tm - 1, hn, pltpu.roll(h, tm - 1, 0)) - h

    hb, dpb, dnb = h.astype(BF16), dp.astype(BF16), dn.astype(BF16)
    mub = mu_ref[...].astype(BF16)

    def mix(n):
        return hb + mub[2 * n:2 * n + 1, :] * dpb + mub[2 * n + 1:2 * n + 2, :] * dnb

    r = _dot(mix(0), wrkv_ref[0])
    k = _dot(mix(2), wrkv_ref[1])
    v = _dot(mix(3), wrkv_ref[2])
    gate_ref[...] = _dot(_sigmoid(_dot(mix(5), g1_ref[...])).astype(BF16), g2_ref[...]).astype(gate_ref.dtype)
    lora_w = _dot(jnp.tanh(_dot(mix(1), w1_ref[...])).astype(BF16), w2_ref[...])
    lora_a = _dot(_dot(mix(4), a1_ref[...]).astype(BF16), a2_ref[...])
    kkr = k * kk_ref[...]
    ssq = _head_sum(kkr * kkr, e_ref, et_ref, 1)
    kk = kkr * lax.rsqrt(jnp.maximum(ssq, 1e-24))
    kd_sum = None
    for dr in range(2):
        cols = slice(dr * d, (dr + 1) * d)
        lw = -RW_DECAY_SCALE * _sigmoid(w0_ref[dr:dr + 1, :] + lora_w[:, cols])
        a = _sigmoid(a0_ref[dr:dr + 1, :] + lora_a[:, cols])
        kd = k * (1.0 + (a - 1.0) * ka_ref[...])
        bvec = kk * a
        kd_sum = kd if kd_sum is None else kd_sum + kd
        for p in range(RW_PAIRS):
            ls = slice(p * LANES, (p + 1) * LANES)
            lw_ref[dr, p] = lw[:, ls]
            dr_ref[dr, 0, p] = kd[:, ls].astype(BF16)
            dr_ref[dr, 1, p] = bvec[:, ls].astype(BF16)
    for p in range(RW_PAIRS):
        ls = slice(p * LANES, (p + 1) * LANES)
        sh_ref[0, p] = r[:, ls].astype(BF16)
        sh_ref[1, p] = v[:, ls].astype(BF16)
        sh_ref[2, p] = kk[:, ls].astype(BF16)
    bsum = _head_sum(r * kd_sum * rk_ref[...], e_ref, et_ref, 1)
    bonus_ref[...] = (bsum * v).astype(bonus_ref.dtype)


def _rwkv_in_call(z, g, mod, mu, wrkv, w1, a1, g1, w2, a2, g2, w0, a0, k_k, k_a, r_k, e, et, nb, lc):
    bsz, t, d = z.shape
    tm = TOKEN_TILE
    ntile = t // tm
    nctx = lc // tm
    r8 = tm // 8
    tok = pl.BlockSpec((None, tm, d), lambda bi, i: (bi, i, 0))
    return pl.pallas_call(
        functools.partial(_rwkv_in_kernel, nctx=nctx, ntile=ntile),
        grid=(bsz, ntile),
        in_specs=[tok,
                  pl.BlockSpec((None, 8, d), lambda bi, i: (bi, jnp.maximum(i * r8 - 1, 0), 0)),
                  pl.BlockSpec((None, 8, d), lambda bi, i: (bi, jnp.minimum((i + 1) * r8, t // 8 - 1), 0)),
                  _const_spec((1, d)),
                  pl.BlockSpec((None, N_MOD, d), lambda bi, i: (jnp.where(i < nctx, nb, bi), 0, 0)),
                  _const_spec((12, d)),
                  _const_spec((3, d, d)),
                  _const_spec((d, LANES)),
                  _const_spec((d, LANES)),
                  _const_spec((d, RW_GATE_PAD)),
                  _const_spec((LANES, 2 * d)),
                  _const_spec((LANES, 2 * d)),
                  _const_spec((RW_GATE_PAD, d)),
                  _const_spec((2, d)),
                  _const_spec((2, d)),
                  _const_spec((1, d)),
                  _const_spec((1, d)),
                  _const_spec((1, d)),
                  _const_spec((d, LANES)),
                  _const_spec((LANES, d))],
        out_specs=[pl.BlockSpec((None, 3, RW_PAIRS, tm, LANES), lambda bi, i: (bi, 0, 0, i, 0)),
                   pl.BlockSpec((None, 2, 2, RW_PAIRS, tm, LANES), lambda bi, i: (bi, 0, 0, 0, i, 0)),
                   pl.BlockSpec((None, 2, RW_PAIRS, tm, LANES), lambda bi, i: (bi, 0, 0, i, 0)),
                   tok, tok],
        out_shape=[jax.ShapeDtypeStruct((bsz, 3, RW_PAIRS, t, LANES), BF16),
                   jax.ShapeDtypeStruct((bsz, 2, 2, RW_PAIRS, t, LANES), BF16),
                   jax.ShapeDtypeStruct((bsz, 2, RW_PAIRS, t, LANES), F32),
                   jax.ShapeDtypeStruct((bsz, t, d), BF16),
                   jax.ShapeDtypeStruct((bsz, t, d), BF16)],
        compiler_params=_params(("parallel", "parallel")),
        name="rwkv_in",
    )(z, z, z, g, mod, mu, wrkv, w1, a1, g1, w2, a2, g2, w0, a0, k_k, k_a, r_k, e, et)


def _rwkv_chunk_group(chains):
    ln = RW_CHUNK
    row = lax.broadcasted_iota(jnp.int32, (ln, 2 * ln), 0)
    col = lax.broadcasted_iota(jnp.int32, (ln, 2 * ln), 1)
    sidx = jnp.where(col >= ln, col - ln, col)
    lane_a = lax.broadcasted_iota(jnp.int32, (1, LANES), 1) < RW_HEAD
    tr = lax.broadcasted_iota(jnp.int32, (ln, ln), 0)
    tc = lax.broadcasted_iota(jnp.int32, (ln, ln), 1)
    tri = {False: (tc <= tr).astype(F32), True: (tc >= tr).astype(F32)}
    strict = {False: sidx < row, True: sidx > row}
    incl = {False: sidx <= row, True: sidx >= row}
    eye_w = jnp.where(sidx == row, 1.0, 0.0)
    diff = row ^ sidx
    vr = lax.broadcasted_iota(jnp.int32, (LANES, LANES), 0) < RW_HEAD
    kc = lax.broadcasted_iota(jnp.int32, (LANES, LANES), 1) < RW_HEAD
    same_head = vr == kc

    def bd(x):
        zero = jnp.zeros((), x.dtype)
        return jnp.concatenate([jnp.where(lane_a, x, zero), jnp.where(lane_a, zero, x)], axis=0)

    revs = [c[7] for c in chains]
    vs = [c[1] for c in chains]
    s_olds = [c[6] for c in chains]
    gcum = [sum(_dot(tri[c[7]].astype(BF16), part) for part in _split3(c[5])) for c in chains]
    gtot = [jnp.sum(c[5], axis=0, keepdims=True) for c in chains]

    def scaled(c, g, gt):
        r, v, kk, kd, bv, lw = (x.astype(F32) for x in c[:6])
        e_pos = jnp.exp(g)
        e_neg = jnp.exp(-g)
        e_end = jnp.exp(gt - g)
        ar = jnp.concatenate([(-kk * jnp.exp(g - lw)).astype(RW_MM), (r * e_pos).astype(RW_MM)], axis=0)
        bk_end = jnp.concatenate([(bv * e_end).astype(RW_MM), (kd * e_end).astype(RW_MM)], axis=0)
        return ar, (bv * e_neg).astype(RW_MM), (kd * e_neg).astype(RW_MM), bk_end

    sc = [scaled(c, g, gt) for c, g, gt in zip(chains, gcum, gtot)]
    ars_ = [x[0] for x in sc]
    x_bk = [_dot_tb(x[0], jnp.concatenate([bd(x[1]), bd(x[2])], axis=0)) for x in sc]
    x_b = [x[:, :2 * ln] for x in x_bk]
    x_k = [x[:, 2 * ln:] for x in x_bk]
    n_w =[jnp.where(strict[rv], x[:ln], 0.0) for x, rv in zip(x_b, revs)]
    m_rb = [jnp.where(incl[rv], x[ln:], 0.0).astype(RW_MM) for x, rv in zip(x_b, revs)]
    m_k = [jnp.concatenate([jnp.where(strict[rv], x[:ln], 0.0), jnp.where(incl[rv], x[ln:], 0.0)],
                           axis=0).astype(RW_MM) for x, rv in zip(x_k, revs)]
    x_w = [eye_w + jnp.where(diff == 1, n, 0.0) for n in n_w]

    def take_rows(x, h, odd):
        return jnp.concatenate([x[b * h:(b + 1) * h] for b in range(ln // h) if (b % 2 == 1) == odd], axis=0)

    def put_rows(base, upd, h, odd):
        parts, j = [], 0
        for b in range(ln // h):
            if (b % 2 == 1) == odd:
                blk = upd[j * h:(j + 1) * h]
                parts.append(blk if base is None else base[b * h:(b + 1) * h] + blk)
                j += 1
            else:
                parts.append(jnp.zeros((h, upd.shape[1]), upd.dtype) if base is None else base[b * h:(b + 1) * h])
        return jnp.concatenate(parts, axis=0)

    h = 2
    while h < ln:
        lvl = jnp.logical_and(diff >= h, diff < 2 * h)
        n_l = [jnp.where(lvl, n, 0.0) for n in n_w]
        if h < RW_ROW_TILE:
            tmp = [_dot(n.astype(RW_MM), bd(x.astype(RW_MM))) for n, x in zip(n_l, x_w)]
            x_w = [x + _dot(x.astype(RW_MM), bd(t.astype(RW_MM))) for x, t in zip(x_w, tmp)]
        else:
            odd = [not rv for rv in revs]
            tmp = [_dot(take_rows(n, h, o).astype(RW_MM), bd(x.astype(RW_MM))) for n, x, o in zip(n_l, x_w, odd)]
            tmp = [put_rows(None, t, h, o) for t, o in zip(tmp, odd)]
            cor = [_dot(take_rows(x, h, o).astype(RW_MM), bd(t.astype(RW_MM))) for x, t, o in zip(x_w, tmp, odd)]
            x_w = [put_rows(x, c, h, o) for x, c, o in zip(x_w, cor, odd)]
        h *= 2
    gy = [_dot(jnp.concatenate([m, a], axis=1), jnp.concatenate([bd(v), s.T.astype(RW_MM)], axis=0))
          for m, a, v, s in zip(m_k, ars_, vs, s_olds)]
    u = [_dot(x.astype(RW_MM), bd(g[:ln].astype(RW_MM))) for x, g in zip(x_w, gy)]
    y = [g[ln:] + _dot(rb, bd(uu.astype(RW_MM))) for g, rb, uu in zip(gy, m_rb, u)]
    upd = [_dot_ta(jnp.concatenate([uu.astype(RW_MM), v], axis=0), x[3]) for uu, v, x in zip(u, vs, sc)]
    s_new = [s * jnp.exp(gt) + jnp.where(same_head, up, 0.0) for s, gt, up in zip(s_olds, gtot, upd)]
    return list(zip(y, s_new))


def _rwkv_scan_kernel(shf_ref, shb_ref, drf_ref, drb_ref, lwf_ref, lwb_ref, yf_ref, yb_ref, s_s):
    @pl.when(pl.program_id(1) == 0)
    def _():
        s_s[...] = jnp.zeros_like(s_s)

    chains, where = [], []
    for bb in range(RW_BATCH):
        for p in range(RW_PAIRS):
            chains.append((shf_ref[bb, 0, p], shf_ref[bb, 1, p], shf_ref[bb, 2, p], drf_ref[bb, 0, p],
                           drf_ref[bb, 1, p], lwf_ref[bb, p], s_s[bb, 0, p], False))
            where.append((yf_ref, bb, 0, p))
            chains.append((shb_ref[bb, 0, p], shb_ref[bb, 1, p], shb_ref[bb, 2, p], drb_ref[bb, 0, p],
                           drb_ref[bb, 1, p], lwb_ref[bb, p], s_s[bb, 1, p], True))
            where.append((yb_ref, bb, 1, p))
    for (y, s_new), (y_ref, bb, d, p) in zip(_rwkv_chunk_group(chains), where):
        y_ref[bb, p] = y.astype(y_ref.dtype)
        s_s[bb, d, p] = s_new


def _rwkv_scan_call(sh, dr, lw, lc):
    bsz, _, _, t, _ = sh.shape
    ln = RW_CHUNK
    nb = RW_BATCH
    assert bsz % nb == 0
    n = t // ln
    nctx = lc // ln
    rv = lambda i: _rev_chunk(i, nctx, n)
    return pl.pallas_call(
        _rwkv_scan_kernel,
        grid=(bsz // nb, n),
        in_specs=[pl.BlockSpec((nb, 3, RW_PAIRS, ln, LANES), lambda bi, i: (bi, 0, 0, i, 0)),
                  pl.BlockSpec((nb, 3, RW_PAIRS, ln, LANES), lambda bi, i: (bi, 0, 0, rv(i), 0)),
                  pl.BlockSpec((nb, None, 2, RW_PAIRS, ln, LANES), lambda bi, i: (bi, 0, 0, 0, i, 0)),
                  pl.BlockSpec((nb, None, 2, RW_PAIRS, ln, LANES), lambda bi, i: (bi, 1, 0, 0, rv(i), 0)),
                  pl.BlockSpec((nb, None, RW_PAIRS, ln, LANES), lambda bi, i: (bi, 0, 0, i, 0)),
                  pl.BlockSpec((nb, None, RW_PAIRS, ln, LANES), lambda bi, i: (bi, 1, 0, rv(i), 0))],
        out_specs=[pl.BlockSpec((nb, RW_PAIRS, ln, LANES), lambda bi, i: (bi, 0, i, 0)),
                   pl.BlockSpec((nb, RW_PAIRS, ln, LANES), lambda bi, i: (bi, 0, rv(i), 0))],
        out_shape=[jax.ShapeDtypeStruct((bsz, RW_PAIRS, t, LANES), BF16)] * 2,
        scratch_shapes=[pltpu.VMEM((nb, 2, RW_PAIRS, LANES, LANES), F32)],
        compiler_params=_params(("parallel", "arbitrary")),
        name="rwkv_scan",
    )(sh, sh, dr, dr, lw, lw)


def _rwkv_out_kernel(z_ref, yf_ref, yb_ref, gate_ref, bonus_ref, lng_ref, lnb_ref, e_ref, et_ref, w_ref, mod_ref,
                     g2_ref, w1_ref, w2_ref, o_ref):
    y = jnp.concatenate([yf_ref[p].astype(F32) + yb_ref[p].astype(F32) for p in range(RW_PAIRS)], axis=1)
    yc = y - _head_sum(y, e_ref, et_ref, 2) * (1.0 / RW_HEAD)
    var = _head_sum(yc * yc, e_ref, et_ref, 1) * (1.0 / RW_HEAD)
    yn = yc * lax.rsqrt(var + RW_LN_EPS) * lng_ref[...] + lnb_ref[...] + bonus_ref[...]
    out = _dot((yn * gate_ref[...]).astype(BF16), w_ref[...])
    o_ref[...] = _mlp_tail(z_ref[...] + mod_ref[2:3, :] * out, g2_ref[...], mod_ref, w1_ref, w2_ref)


def _rwkv_out_call(z, yf, yb, gate, bonus, ln_g, ln_b, e, et, w, mod, g2, w1, w2, nb, lc, skip):
    bsz, t, d = z.shape
    tm = TOKEN_TILE
    nctx = lc // tm
    nt = t // tm - skip
    tok = pl.BlockSpec((None, tm, d), lambda bi, i: (bi, i + skip, 0))
    pair = pl.BlockSpec((None, RW_PAIRS, tm, LANES), lambda bi, i: (bi, 0, i + skip, 0))
    return pl.pallas_call(
        _rwkv_out_kernel,
        grid=(bsz, nt),
        in_specs=[tok, pair, pair, tok, tok,
                  _const_spec((1, d)), _const_spec((1, d)), _const_spec((d, LANES)), _const_spec((LANES, d)),
                  _const_spec((d, d)),
                  pl.BlockSpec((None, N_MOD, d), lambda bi, i: (jnp.where(i + skip < nctx, nb, bi), 0, 0)),
                  _const_spec((1, d)),
                  _const_spec((d, D_FF)),
                  _const_spec((D_FF, d))],
        out_specs=pl.BlockSpec((None, tm, d), lambda bi, i: (bi, i, 0)),
        out_shape=jax.ShapeDtypeStruct((bsz, nt * tm, d), F32),
        compiler_params=_params(("parallel", "parallel")),
        name="rwkv_out_mlp",
    )(z, yf, yb, gate, bonus, ln_g, ln_b, e, et, w, mod, g2, w1, w2)


def _group_mean_matrix(n, width):
    idx = jnp.arange(n) // width
    return ((idx[:, None] == idx[None, :]).astype(F32) / width).astype(BF16)


def _rope_tables(lc, s):
    quarter = AT_DH // 4
    inv = ROPE_BASE ** (-jnp.arange(quarter, dtype=F32) / quarter)
    pos = jnp.arange(s)
    rpos = (pos // GRID_W).astype(F32)
    cpos = (pos % GRID_W).astype(F32)
    ang_r = rpos[:, None] * inv[None, :]
    ang_c = cpos[:, None] * inv[None, :]
    cos64 = jnp.concatenate([jnp.cos(ang_r), jnp.cos(ang_r), jnp.cos(ang_c), jnp.cos(ang_c)], axis=1)
    sin64 = jnp.concatenate([-jnp.sin(ang_r), jnp.sin(ang_r), -jnp.sin(ang_c), jnp.sin(ang_c)], axis=1)
    cos = jnp.concatenate([jnp.ones((lc, AT_DH), F32), cos64], axis=0)
    sin = jnp.concatenate([jnp.zeros((lc, AT_DH), F32), sin64], axis=0)
    return jnp.tile(cos, (1, 2)), jnp.tile(sin, (1, 2))


def _even_weights(w_in, b_in):
    def cols(m):
        sizes = (ML_W, ML_W, ML_W, ML_W, ML_GATE_COLS, AT_W, AT_KV_HEADS * AT_DH, AT_KV_HEADS * AT_DH)
        bounds = [sum(sizes[:i + 1]) for i in range(len(sizes) - 1)]
        mq, mk, mv, mo, mg, aq, ak, av = jnp.split(m, bounds, axis=-1)
        dup = lambda u: jnp.concatenate([u[..., :AT_DH], u[..., :AT_DH], u[..., AT_DH:], u[..., AT_DH:]], axis=-1)
        mgp = jnp.pad(mg, [(0, 0)] * (m.ndim - 1) + [(0, LANES - ML_GATE_COLS)])
        return jnp.concatenate([mq, mk, mv, mo, aq, dup(ak), dup(av), mgp], axis=-1)
    return cols(w_in).astype(BF16), cols(b_in[None, :])


def kernel(x, c, ctx, c_ctx, ada_w, ada_b, norm1_g, norm2_g, mlp_w1, mlp_w2, ev_w_in, ev_b_in, ev_w_out, ml_f_bias, ml_out_g, at_q_g, at_k_g, at_sink, rw_mu, rw_w_rkv, rw_w0, rw_w1, rw_w2, rw_a0, rw_a1, rw_a2, rw_g1, rw_g2, rw_k_k, rw_k_a, rw_r_k, rw_ln_g, rw_ln_b, rw_w_out):
    bsz, s, d = x.shape
    lc = ctx.shape[1]
    depth = ada_w.shape[0]
    assert d == D_MODEL and lc == TOKEN_TILE and s % TOKEN_TILE == 0 and bsz < MOD_ROWS
    z = (ctx, x)

    cstack = jnp.zeros((MOD_ROWS, d), F32).at[:bsz].set(c).at[bsz].set(c_ctx)
    mod_all = _ada_call(cstack, ada_w, ada_b).reshape(depth, MOD_ROWS, N_MOD, d)

    grp512 = _group_mean_matrix(AT_W, AT_DH)
    head_e = (jnp.arange(d)[:, None] // RW_HEAD == jnp.arange(LANES)[None, :]).astype(BF16)
    head_et = head_e.T
    cos_t, sin_t = _rope_tables(lc, s)
    gate_is_f = ((jnp.arange(LANES) % 8 >= 4) & (jnp.arange(LANES) < 16)).astype(F32)[None, :]

    for layer in range(depth):
        mod = mod_all[layer]
        j = layer // 2
        n1 = norm1_g[layer][None, :]
        mlp_args = (norm2_g[layer][None, :], mlp_w1[layer].astype(BF16), mlp_w2[layer].astype(BF16))
        skip = lc // TOKEN_TILE if layer == depth - 1 else 0
        if layer % 2 == 0:
            w_in, b_in = _even_weights(ev_w_in[j], ev_b_in[j])
            fb = jnp.zeros((LANES,), F32)
            for dr in range(2):
                fb = fb.at[8 * dr + 4:8 * dr + 8].set(ml_f_bias[j, dr])
            outs = _even_in_call(
                z, n1, mod, w_in, b_in, jnp.tile(at_q_g[j], AT_HEADS)[None, :],
                jnp.tile(at_k_g[j], 2 * AT_KV_HEADS)[None, :], grp512, cos_t, sin_t, fb[None, :], gate_is_f, bsz)
            if isinstance(z, tuple):
                z = outs[-1]
            mqv, kt, mo, aq, akt, av, gates, gates_t = outs[:8]
            hf, hb = _mlstm_call(mqv, kt, gates, gates_t, lc)
            sink = jnp.broadcast_to(at_sink[j][:, None], (AT_HEADS, LANES))
            ya = _attn_call(aq, akt, av, sink, lc)
            z = _even_out_call(z, hf, hb, mo, ya, ml_out_g[j][None, :], ev_w_out[j].astype(BF16), mod, *mlp_args,
                               bsz, skip)
        else:
            pad_g = RW_GATE_PAD - RW_GATE_LORA
            zeros = jnp.zeros((RW_DECAY_LORA, d), F32)
            w2 = jnp.concatenate([jnp.concatenate([rw_w2[j, 0], zeros], axis=1),
                                  jnp.concatenate([zeros, rw_w2[j, 1]], axis=1)], axis=0)
            a2 = jnp.concatenate([jnp.concatenate([rw_a2[j, 0], zeros], axis=1),
                                  jnp.concatenate([zeros, rw_a2[j, 1]], axis=1)], axis=0)
            sh, dr_, lw, gate, bonus = _rwkv_in_call(
                z, n1, mod, rw_mu[j].reshape(12, d), rw_w_rkv[j].astype(BF16),
                jnp.concatenate([rw_w1[j, 0], rw_w1[j, 1]], axis=1).astype(BF16),
                jnp.concatenate([rw_a1[j, 0], rw_a1[j, 1]], axis=1).astype(BF16),
                jnp.pad(rw_g1[j], ((0, 0), (0, pad_g))).astype(BF16),
                w2.astype(BF16), a2.astype(BF16),
                jnp.pad(rw_g2[j], ((0, pad_g), (0, 0))).astype(BF16),
                rw_w0[j], rw_a0[j], rw_k_k[j][None, :], rw_k_a[j][None, :], rw_r_k[j].reshape(1, d),
                head_e, head_et, bsz, lc)
            yf, yb = _rwkv_scan_call(sh, dr_, lw, lc)
            z = _rwkv_out_call(z, yf, yb, gate, bonus, rw_ln_g[j][None, :], rw_ln_b[j][None, :], head_e, head_et,
                               rw_w_out[j].astype(BF16), mod, *mlp_args, bsz, lc, skip)
    return z
```

```python
import functools

import jax
import jax.numpy as jnp
from jax import lax
from jax.experimental import pallas as pl
from jax.experimental.pallas import tpu as pltpu

F32 = jnp.float32
BF16 = jnp.bfloat16

D_MODEL = 1024
N_MOD = 6
D_FF = 4 * D_MODEL
EPS = 1e-6
GRID_W = 64

ML_HEADS = 4
ML_D = 128
ML_W = ML_HEADS * ML_D
ML_CHUNK = 128
GATE_CAP = 15.0

AT_DH = 64
AT_HEADS = 8
AT_KV_HEADS = 2
AT_W = AT_HEADS * AT_DH
AT_BLOCK = 128
AT_KV_DUP = 2 * AT_KV_HEADS * AT_DH
AT_BATCH = 1
ROPE_BASE = 10000.0

RW_HEAD = 64
RW_PAIRS = D_MODEL // 128
RW_CHUNK = 64
RW_LN_EPS = 64e-5
RW_DECAY_LORA = 64
RW_AAA_LORA = 64
RW_GATE_LORA = 160
RW_GATE_PAD = 256
RW_DECAY_SCALE = 0.6065306597126334

LANES = 128
TOKEN_TILE = 256
FF_TILE = 1024
MOD_ROWS = 16
VMEM_LIMIT = 56 * 1024 * 1024
NEG = -1e30
RW_MM = BF16
RW_BATCH = 4
RW_ROW_TILE = 8

EV_MO = 3 * ML_W
EV_AQ = EV_MO + ML_W
EV_AK = EV_AQ + AT_W
EV_AV = EV_AK + AT_KV_DUP
EV_GATE = EV_AV + AT_KV_DUP
EV_COLS = EV_GATE + LANES
ML_GATE_COLS = 4 * ML_HEADS


def _dot(a, b):
    return jnp.dot(a, b, preferred_element_type=F32)


def _dot_tb(a, b):
    return lax.dot_general(a, b, (((1,), (1,)), ((), ())), preferred_element_type=F32)


def _dot_ta(a, b):
    return lax.dot_general(a, b, (((0,), (0,)), ((), ())), preferred_element_type=F32)


def _norm_mod(z, g, shift, scale):
    ms = jnp.mean(z * z, axis=-1, keepdims=True)
    return (z * lax.rsqrt(ms + EPS) * g) * (1.0 + scale) + shift


def _sigmoid(x):
    return 1.0 / (1.0 + jnp.exp(-x))


def _head_sum(x, e_ref, et_ref, terms):
    s = _dot(x.astype(BF16), e_ref[...])
    hi = s.astype(BF16)
    out = _dot(hi, et_ref[...])
    if terms > 1:
        out = out + _dot((s - hi.astype(F32)).astype(BF16), et_ref[...])
    return out


def _layer_spec(shape, layer):
    nd = len(shape)
    return pl.BlockSpec((None,) + tuple(shape), lambda *_: (layer,) + (0,) * nd, pipeline_mode=pl.Buffered(1))


def _const_spec(shape):
    nd = len(shape)
    return pl.BlockSpec(shape, lambda *_: (0,) * nd, pipeline_mode=pl.Buffered(1))


def _params(sem):
    return pltpu.CompilerParams(dimension_semantics=sem, vmem_limit_bytes=VMEM_LIMIT)


def _ada_kernel(c_ref, w_ref, b_ref, o_ref):
    cv = c_ref[...]
    s = cv * _sigmoid(cv)
    o_ref[...] = jnp.dot(s, w_ref[...], preferred_element_type=F32,
                         precision=lax.Precision.HIGHEST) + b_ref[...]


def _ada_call(cstack, ada_w, ada_b):
    depth = ada_w.shape[0]
    n = N_MOD * D_MODEL
    tn = 1024
    return pl.pallas_call(
        _ada_kernel,
        grid=(depth, n // tn),
        in_specs=[pl.BlockSpec((MOD_ROWS, D_MODEL), lambda l, j: (0, 0)),
                  pl.BlockSpec((None, D_MODEL, tn), lambda l, j: (l, 0, j)),
                  pl.BlockSpec((None, 1, tn), lambda l, j: (l, 0, j))],
        out_specs=pl.BlockSpec((None, MOD_ROWS, tn), lambda l, j: (l, 0, j)),
        out_shape=jax.ShapeDtypeStruct((depth, MOD_ROWS, n), F32),
        compiler_params=_params(("parallel", "parallel")),
        name="ada_ln",
    )(cstack, ada_w, ada_b.reshape(depth, 1, n))


def _rope(x, cos, sin, lane_lo):
    n = x.shape[1]
    up = pltpu.roll(x, n - 16, 1)
    dn = pltpu.roll(x, 16, 1)
    reps = n // LANES
    c = jnp.concatenate([cos] * reps, axis=1) if reps > 1 else cos
    s = jnp.concatenate([sin] * reps, axis=1) if reps > 1 else sin
    lo = jnp.concatenate([lane_lo] * reps, axis=1) if reps > 1 else lane_lo
    return x * c + jnp.where(lo, up, dn) * s


def _even_in_kernel(*refs, assemble):
    if assemble:
        ctx_ref, x_ref, *refs, z_out_ref = refs
        z = jnp.where(pl.program_id(1) == 0, ctx_ref[...], x_ref[...])
        z_out_ref[...] = z
    else:
        z_ref, *refs = refs
        z = z_ref[...]
    (g_ref, mod_ref, w_ref, b_ref, gq_ref, gk_ref, grp_ref, cos_ref, sin_ref, fb_ref, isf_ref,
     mqv_ref, kt_ref, mo_ref, aq_ref, akt_ref, av_ref, gate_ref, gatet_ref) = refs
    h = _norm_mod(z, g_ref[...], mod_ref[0:1, :], mod_ref[1:2, :])
    p = _dot(h.astype(BF16), w_ref[...]) + b_ref[...]
    mqv_ref[:, 0:ML_W] = p[:, 0:ML_W].astype(BF16)
    mqv_ref[:, ML_W:2 * ML_W] = p[:, 2 * ML_W:3 * ML_W].astype(BF16)
    kt_ref[...] = (p[:, ML_W:2 * ML_W] * (ML_D ** -0.5)).T.astype(BF16)
    mo_ref[...] = p[:, EV_MO:EV_MO + ML_W].astype(mo_ref.dtype)
    cos = cos_ref[...]
    sin = sin_ref[...]
    lane_lo = (lax.broadcasted_iota(jnp.int32, (1, LANES), 1) % 32) < 16
    q = p[:, EV_AQ:EV_AQ + AT_W]
    qms = _dot((q * q).astype(BF16), grp_ref[...])
    q = q * lax.rsqrt(qms + EPS) * gq_ref[...]
    aq_ref[...] = (_rope(q, cos, sin, lane_lo) * (AT_DH ** -0.5)).astype(BF16)
    k = p[:, EV_AK:EV_AK + AT_KV_DUP]
    kms = _dot((k * k).astype(BF16), grp_ref[0:AT_KV_DUP, 0:AT_KV_DUP])
    k = k * lax.rsqrt(kms + EPS) * gk_ref[...]
    akt_ref[...] = _rope(k, cos, sin, lane_lo).T.astype(BF16)
    av_ref[...] = p[:, EV_AV:EV_AV + AT_KV_DUP].astype(BF16)
    gt = p[:, EV_GATE:EV_GATE + LANES] + fb_ref[...]
    sc = GATE_CAP * jnp.tanh(gt * (1.0 / GATE_CAP))
    logsig = jnp.minimum(sc, 0.0) - jnp.log(1.0 + jnp.exp(-jnp.abs(sc)))
    gates = jnp.where(isf_ref[...] > 0.5, logsig, sc)
    gate_ref[...] = gates
    gatet_ref[...] = gates.T[0:ML_GATE_COLS, :]


def _even_in_call(z, g, mod, w, b, gq, gk, grp, cos, sin, fb, isf, nb):
    tm = TOKEN_TILE
    tok = lambda width: pl.BlockSpec((None, tm, width), lambda bi, i: (bi, i, 0))
    tpose = lambda rows: pl.BlockSpec((None, rows, tm), lambda bi, i: (bi, 0, i))
    assemble = isinstance(z, tuple)
    if assemble:
        ctx, x = z
        bsz, s, d = x.shape
        t = ctx.shape[1] + s
        z_args = (ctx, x)
        z_specs = [pl.BlockSpec((None, tm, d), lambda bi, i: (bi, 0, 0)),
                   pl.BlockSpec((None, tm, d), lambda bi, i: (bi, jnp.maximum(i - 1, 0), 0))]
        extra_specs, extra_shapes = [tok(d)], [jax.ShapeDtypeStruct((bsz, t, d), F32)]
    else:
        bsz, t, d = z.shape
        z_args, z_specs, extra_specs, extra_shapes = (z,), [tok(d)], [], []
    return pl.pallas_call(
        functools.partial(_even_in_kernel, assemble=assemble),
        grid=(bsz, t // tm),
        in_specs=z_specs + [
                  _const_spec((1, d)),
                  pl.BlockSpec((None, N_MOD, d), lambda bi, i: (jnp.where(i == 0, nb, bi), 0, 0)),
                  _const_spec((d, EV_COLS)),
                  _const_spec((1, EV_COLS)),
                  _const_spec((1, AT_W)),
                  _const_spec((1, AT_KV_DUP)),
                  _const_spec((AT_W, AT_W)),
                  pl.BlockSpec((tm, LANES), lambda bi, i: (i, 0)),
                  pl.BlockSpec((tm, LANES), lambda bi, i: (i, 0)),
                  _const_spec((1, LANES)),
                  _const_spec((1, LANES))],
        out_specs=[tok(2 * ML_W), tpose(ML_W), tok(ML_W), tok(AT_W), tpose(AT_KV_DUP), tok(AT_KV_DUP), tok(LANES),
                   tpose(ML_GATE_COLS)] + extra_specs,
        out_shape=[jax.ShapeDtypeStruct((bsz, t, 2 * ML_W), BF16),
                   jax.ShapeDtypeStruct((bsz, ML_W, t), BF16),
                   jax.ShapeDtypeStruct((bsz, t, ML_W), BF16),
                   jax.ShapeDtypeStruct((bsz, t, AT_W), BF16),
                   jax.ShapeDtypeStruct((bsz, AT_KV_DUP, t), BF16),
                   jax.ShapeDtypeStruct((bsz, t, AT_KV_DUP), BF16),
                   jax.ShapeDtypeStruct((bsz, t, LANES), F32),
                   jax.ShapeDtypeStruct((bsz, ML_GATE_COLS, t), F32)] + extra_shapes,
        compiler_params=_params(("parallel", "parallel")),
        name="even_in",
    )(*z_args, g, mod, w, b, gq, gk, grp, cos, sin, fb, isf)


def _split3(x):
    hi = x.astype(BF16)
    r1 = x - hi.astype(F32)
    mid = r1.astype(BF16)
    return hi, mid, (r1 - mid.astype(F32)).astype(BF16)


def _mlstm_kernel(qf_ref, ktf_ref, vf_ref, qb_ref, ktb_ref, vb_ref, gcf_ref, gcb_ref, grf_ref, grb_ref, sel_ref,
                  hf_ref, hb_ref, c_s, m_s):
    ln = ML_CHUNK

    @pl.when(pl.program_id(1) == 0)
    def _():
        c_s[...] = jnp.zeros_like(c_s)
        m_s[...] = jnp.zeros_like(m_s)

    ri = lax.broadcasted_iota(jnp.int32, (ln, ln), 0)
    ci = lax.broadcasted_iota(jnp.int32, (ln, ln), 1)
    hi = lax.Precision.HIGHEST
    before = (ci <= ri, ci >= ri)
    refs = ((qf_ref, ktf_ref, vf_ref, gcf_ref, grf_ref, hf_ref), (qb_ref, ktb_ref, vb_ref, gcb_ref, grb_ref, hb_ref))
    cum_r = [jnp.dot(refs[d][4][...], before[1 - d].astype(F32), preferred_element_type=F32, precision=hi)
             for d in range(2)]
    bc_all = []
    for d in range(2):
        cum_c = sum(_dot(before[d].astype(BF16), part) for part in _split3(refs[d][3][...]))
        sel = sel_ref[d]
        bc_all.append(sum(_dot(part, sel) for part in _split3(cum_c)))
    chains = [(d, hd) for d in range(2) for hd in range(ML_HEADS)]
    lanes = [slice(hd * ML_D, (hd + 1) * ML_D) for _, hd in chains]
    st = [d * ML_HEADS + hd for d, hd in chains]
    ones = jnp.ones((ln, ML_D), BF16)
    q = [refs[d][0][:, ls] for (d, _), ls in zip(chains, lanes)]
    kt = [refs[d][1][ls, :] for (d, _), ls in zip(chains, lanes)]
    v1 = [jnp.concatenate([refs[d][2][:, ls], ones], axis=1) for (d, _), ls in zip(chains, lanes)]
    i_row = [refs[d][4][8 * d + hd:8 * d + hd + 1, :] for d, hd in chains]
    f_row = [refs[d][4][8 * d + 4 + hd:8 * d + 5 + hd, :] for d, hd in chains]
    bcum_row = [cum_r[d][8 * d + 4 + hd:8 * d + 5 + hd, :] for d, hd in chains]
    bcum = [bc_all[d][:, ls] for (d, _), ls in zip(chains, lanes)]
    m_old = [m_s[s:s + 1, 0:1] for s in st]
    c_old = [c_s[s] for s in st]
    qk = [_dot(a, b) for a, b in zip(q, kt)]
    dlog =[jnp.where(before[d], bc - br + ir, NEG) for (d, _), bc, br, ir in zip(chains, bcum, bcum_row, i_row)]
    inter = [bc + m for bc, m in zip(bcum, m_old)]
    mt = [jnp.maximum(jnp.max(dl, axis=1, keepdims=True), it) for dl, it in zip(dlog, inter)]
    sc = [x * jnp.exp(dl - m) for x, dl, m in zip(qk, dlog, mt)]
    iw = [jnp.exp(it - m) for it, m in zip(inter, mt)]
    tots = [_dot(jnp.concatenate([s.astype(BF16), (w * a.astype(F32)).astype(BF16)], axis=1),
                 jnp.concatenate([b, c.astype(BF16)], axis=0)) for s, w, a, b, c in zip(sc, iw, q, v1, c_old)]
    for j, (d, _) in enumerate(chains):
        tot = tots[j]
        den =jnp.maximum(jnp.abs(tot[:, ML_D:]), jnp.exp(-mt[j]))
        refs[d][5][:, lanes[j]] = (tot[:, :ML_D] / den).astype(refs[d][5].dtype)
    bl = [jnp.sum(fr, axis=1, keepdims=True) for fr in f_row]
    ws_log = [b - br + ir for b, br, ir in zip(bl, bcum_row, i_row)]
    m_new = [jnp.maximum(b + m, jnp.max(w, axis=1, keepdims=True)) for b, m, w in zip(bl, m_old, ws_log)]
    ws = [jnp.exp(w - m) for w, m in zip(ws_log, m_new)]
    upd = [_dot((a.astype(F32) * w).astype(BF16), b) for a, w, b in zip(kt, ws, v1)]
    for j, s in enumerate(st):
        c_s[s] = jnp.exp(bl[j] + m_old[j] - m_new[j]) * c_old[j] + upd[j]
        m_s[s:s + 1, :] = jnp.broadcast_to(m_new[j], (1, LANES))


def _rev_chunk(i, nctx, n):
    return jnp.where(i < nctx, nctx - 1 - i, n + nctx - 1 - i)


def _mlstm_call(mqv, kt, gates, gates_t, lc):
    bsz, t, _ = mqv.shape
    ln = ML_CHUNK
    assert ln == ML_D == LANES
    n = t // ln
    nctx = lc // ln
    rv = lambda i: _rev_chunk(i, nctx, n)
    col = jnp.arange(LANES)[None, :, None]
    want = (8 * jnp.arange(2)[:, None, None] + 4 + jnp.arange(ML_HEADS * ln)[None, None, :] // ln)
    sel = (col == want).astype(BF16)
    return pl.pallas_call(
        _mlstm_kernel,
        grid=(bsz, n),
        in_specs=[pl.BlockSpec((None, ln, ML_W), lambda bi, i: (bi, i, 0)),
                  pl.BlockSpec((None, ML_W, ln), lambda bi, i: (bi, 0, i)),
                  pl.BlockSpec((None, ln, ML_W), lambda bi, i: (bi, i, 1)),
                  pl.BlockSpec((None, ln, ML_W), lambda bi, i: (bi, rv(i), 0)),
                  pl.BlockSpec((None, ML_W, ln), lambda bi, i: (bi, 0, rv(i))),
                  pl.BlockSpec((None, ln, ML_W), lambda bi, i: (bi, rv(i), 1)),
                  pl.BlockSpec((None, ln, LANES), lambda bi, i: (bi, i, 0)),
                  pl.BlockSpec((None, ln, LANES), lambda bi, i: (bi, rv(i), 0)),
                  pl.BlockSpec((None, ML_GATE_COLS, ln), lambda bi, i: (bi, 0, i)),
                  pl.BlockSpec((None, ML_GATE_COLS, ln), lambda bi, i: (bi, 0, rv(i))),
                  _const_spec((2, LANES, ML_HEADS * ln))],
        out_specs=[pl.BlockSpec((None, ln, ML_W), lambda bi, i: (bi, i, 0)),
                   pl.BlockSpec((None, ln, ML_W), lambda bi, i: (bi, rv(i), 0))],
        out_shape=[jax.ShapeDtypeStruct((bsz, t, ML_W), BF16)] * 2,
        scratch_shapes=[pltpu.VMEM((2 * ML_HEADS, ML_D, 2 * ML_D), F32),
                        pltpu.VMEM((2 * ML_HEADS, LANES), F32)],
        compiler_params=_params(("parallel", "arbitrary")),
        name="mlstm_scan",
    )(mqv, kt, mqv, mqv, kt, mqv, gates, gates, gates_t, gates_t, sel)


def _attn_kernel(q_ref, ktp_ref, kto_ref, ktn_ref, ktc_ref, vp_ref, vo_ref, vn_ref, vc_ref, sink_ref, o_ref,
                 *, nctx, n):
    j = pl.program_id(1)
    blk = AT_BLOCK
    nk = 3 * blk + ktc_ref.shape[2]
    qi = lax.broadcasted_iota(jnp.int32, (blk, nk), 0)
    ki = lax.broadcasted_iota(jnp.int32, (blk, nk), 1)
    lo = jnp.where(j > nctx, qi, blk)
    hi = jnp.where(j < n - 1, qi + 2 * blk, 2 * blk - 1)
    band = jnp.where(ki < lo, NEG, jnp.where(ki > hi, NEG, jnp.where(j >= nctx, 0.0, NEG)))
    bias = jnp.where(ki >= 3 * blk, 0.0, band)
    lane = lax.broadcasted_iota(jnp.int32, (1, LANES), 1)
    sub = lax.broadcasted_iota(jnp.int32, (LANES, 1), 0)
    lane_half = (lane < AT_DH, lane >= AT_DH)
    sub_half = (sub < AT_DH, sub >= AT_DH)
    kts = (ktp_ref, kto_ref, ktn_ref, ktc_ref)
    vs = (vp_ref, vo_ref, vn_ref, vc_ref)
    zero = jnp.zeros((), BF16)
    nbat = q_ref.shape[0]
    kx, vx = {}, {}
    for bb in range(nbat):
        for g in range(AT_KV_HEADS):
            rows = slice(g * LANES, (g + 1) * LANES)
            for e in range(2):
                kx[bb, g, e] = jnp.concatenate([jnp.where(sub_half[e], r[bb, rows, :], zero) for r in kts], axis=1)
                vx[bb, g, e] = jnp.concatenate(
                    [jnp.concatenate([jnp.where(lane_half[e], r[bb, :, rows], zero),
                                      jnp.ones((r.shape[1], LANES), BF16)], axis=1) for r in vs], axis=0)
    heads = [(bb, hd // 2, hd % 2, hd // 4) for bb in range(nbat) for hd in range(AT_HEADS)]
    qc = {(bb, c): q_ref[bb, :, c * LANES:(c + 1) * LANES] for bb in range(nbat) for c in range(AT_HEADS // 2)}
    sink = [sink_ref[2 * c + e:2 * c + e + 1, 0:1] for _, c, e, _ in heads]
    ss = [_dot(qc[bb, c], kx[bb, g, e]) + bias for bb, c, e, g in heads]
    m = [jnp.maximum(jnp.max(s, axis=1, keepdims=True), sk) for s, sk in zip(ss, sink)]
    res = [_dot(jnp.exp(s - mm).astype(BF16), vx[bb, g, e]) for s, mm, (bb, c, e, g) in zip(ss, m, heads)]
    outs =[r[:, :LANES] / (r[:, LANES:] + jnp.exp(sk - mm)) for r, sk, mm in zip(res, sink, m)]
    for i in range(0, len(heads), 2):
        bb, c = heads[i][0], heads[i][1]
        o_ref[bb, :, c * LANES:(c + 1) * LANES] = (outs[i] + outs[i + 1]).astype(o_ref.dtype)


def _attn_call(aq, akt, av, sink, lc):
    bsz, t, _ = aq.shape
    blk = AT_BLOCK
    nb = AT_BATCH
    assert bsz % nb == 0
    n = t // blk
    nctx = lc // blk
    prev = lambda j: jnp.clip(j - 1, nctx, n - 1)
    own = lambda j: jnp.clip(j, nctx, n - 1)
    nxt = lambda j: jnp.clip(j + 1, nctx, n - 1)
    kt = lambda f: pl.BlockSpec((nb, AT_KV_DUP, blk), lambda bi, j: (bi, 0, f(j)))
    vv = lambda f: pl.BlockSpec((nb, blk, AT_KV_DUP), lambda bi, j: (bi, f(j), 0))
    return pl.pallas_call(
        functools.partial(_attn_kernel, nctx=nctx, n=n),
        grid=(bsz // nb, n),
        in_specs=[pl.BlockSpec((nb, blk, AT_W), lambda bi, j: (bi, j, 0)),
                  kt(prev), kt(own), kt(nxt),
                  pl.BlockSpec((nb, AT_KV_DUP, lc), lambda bi, j: (bi, 0, 0)),
                  vv(prev), vv(own), vv(nxt),
                  pl.BlockSpec((nb, lc, AT_KV_DUP), lambda bi, j: (bi, 0, 0)),
                  _const_spec((AT_HEADS, LANES))],
        out_specs=pl.BlockSpec((nb, blk, AT_W), lambda bi, j: (bi, j, 0)),
        out_shape=jax.ShapeDtypeStruct((bsz, t, AT_W), BF16),
        compiler_params=_params(("parallel", "parallel")),
        name="window_attn",
    )(aq, akt, akt, akt, akt, av, av, av, av, sink)


def _mlp_tail(z, g, mod_ref, w1_ref, w2_ref):
    h = _norm_mod(z, g, mod_ref[3:4, :], mod_ref[4:5, :]).astype(BF16)
    acc = jnp.zeros(z.shape, F32)
    for f in range(D_FF // FF_TILE):
        a = jnp.maximum(_dot(h, w1_ref[:, f * FF_TILE:(f + 1) * FF_TILE]), 0.0)
        acc = acc + _dot((a * a).astype(BF16), w2_ref[f * FF_TILE:(f + 1) * FF_TILE, :])
    return z + mod_ref[5:6, :] * acc


def _even_out_kernel(z_ref, hf_ref, hb_ref, mo_ref, ya_ref, og_ref, w_ref, mod_ref, g2_ref, w1_ref, w2_ref, o_ref):
    hs = hf_ref[...].astype(F32) + hb_ref[...].astype(F32)
    parts = []
    for hd in range(ML_HEADS):
        x = hs[:, hd * ML_D:(hd + 1) * ML_D]
        parts.append(x * lax.rsqrt(jnp.mean(x * x, axis=1, keepdims=True) + EPS))
    ym = jnp.concatenate(parts, axis=1) * og_ref[...] * _sigmoid(mo_ref[...].astype(F32))
    y = _dot(ym.astype(BF16), w_ref[0:ML_W, :]) + _dot(ya_ref[...], w_ref[ML_W:ML_W + AT_W, :])
    o_ref[...] = _mlp_tail(z_ref[...] + mod_ref[2:3, :] * y, g2_ref[...], mod_ref, w1_ref, w2_ref)


def _even_out_call(z, hf, hb, mo, ya, og, w, mod, g2, w1, w2, layer, nb, skip):
    bsz, t, d = z.shape
    tm = TOKEN_TILE
    nt = t // tm - skip
    tok = lambda width: pl.BlockSpec((None, tm, width), lambda bi, i: (bi, i + skip, 0))
    return pl.pallas_call(
        _even_out_kernel,
        grid=(bsz, nt),
        in_specs=[tok(d), tok(ML_W), tok(ML_W), tok(ML_W), tok(AT_W),
                  _const_spec((1, ML_W)),
                  _const_spec((ML_W + AT_W, d)),
                  pl.BlockSpec((None, N_MOD, d), lambda bi, i: (jnp.where(i + skip == 0, nb, bi), 0, 0)),
                  _const_spec((1, d)),
                  _layer_spec((d, D_FF), layer),
                  _layer_spec((D_FF, d), layer)],
        out_specs=pl.BlockSpec((None, tm, d), lambda bi, i: (bi, i, 0)),
        out_shape=jax.ShapeDtypeStruct((bsz, nt * tm, d), F32),
        compiler_params=_params(("parallel", "parallel")),
        name="even_out_mlp",
    )(z, hf, hb, mo, ya, og, w, mod, g2, w1, w2)


def _rwkv_in_kernel(z_ref, zp_ref, zn_ref, g_ref, mod_ref, mu_ref, wrkv_ref, w1_ref, a1_ref, g1_ref,
                    w2_ref, a2_ref, g2_ref, w0_ref, a0_ref, kk_ref, ka_ref, rk_ref, e_ref, et_ref,
                    sh_ref, dr_ref, lw_ref, gate_ref, bonus_ref, *, nctx, ntile):
    i = pl.program_id(1)
    tm, d = z_ref.shape
    g = g_ref[...]
    shift = mod_ref[0:1, :]
    scale = mod_ref[1:2, :]
    h = _norm_mod(z_ref[...], g, shift, scale)
    no_prev = jnp.logical_or(i == 0, i == nctx)
    no_next = jnp.logical_or(i == nctx - 1, i == ntile - 1)
    hp = jnp.where(no_prev, 0.0, _norm_mod(zp_ref[7:8, :], g, shift, scale))
    hn = jnp.where(no_next, 0.0, _norm_mod(zn_ref[0:1, :], g, shift, scale))
    row = lax.broadcasted_iota(jnp.int32, (tm, 1), 0)
    dp = jnp.where(row == 0, hp, pltpu.roll(h, 1, 0)) - h
    dn = jnp.where(row == tm - 1, hn, pltpu.roll(h, tm - 1, 0)) - h

    hb, dpb, dnb = h.astype(BF16), dp.astype(BF16), dn.astype(BF16)
    mub = mu_ref[...].astype(BF16)

    def mix(n):
        return hb + mub[2 * n:2 * n + 1, :] * dpb + mub[2 * n + 1:2 * n + 2, :] * dnb

    r = _dot(mix(0), wrkv_ref[0])
    k = _dot(mix(2), wrkv_ref[1])
    v = _dot(mix(3), wrkv_ref[2])
    gate_ref[...] = _dot(_sigmoid(_dot(mix(5), g1_ref[...])).astype(BF16), g2_ref[...]).astype(gate_ref.dtype)
    lora_w = _dot(jnp.tanh(_dot(mix(1), w1_ref[...])).astype(BF16), w2_ref[...])
    lora_a = _dot(_dot(mix(4), a1_ref[...]).astype(BF16), a2_ref[...])
    kkr = k * kk_ref[...]
    ssq = _head_sum(kkr * kkr, e_ref, et_ref, 1)
    kk = kkr * lax.rsqrt(jnp.maximum(ssq, 1e-24))
    kd_sum = None
    for dr in range(2):
        cols = slice(dr * d, (dr + 1) * d)
        lw = -RW_DECAY_SCALE * _sigmoid(w0_ref[dr:dr + 1, :] + lora_w[:, cols])
        a = _sigmoid(a0_ref[dr:dr + 1, :] + lora_a[:, cols])
        kd = k * (1.0 + (a - 1.0) * ka_ref[...])
        bvec = kk * a
        kd_sum = kd if kd_sum is None else kd_sum + kd
        for p in range(RW_PAIRS):
            ls = slice(p * LANES, (p + 1) * LANES)
            lw_ref[dr, p] = lw[:, ls]
            dr_ref[dr, 0, p] = kd[:, ls].astype(BF16)
            dr_ref[dr, 1, p] = bvec[:, ls].astype(BF16)
    for p in range(RW_PAIRS):
        ls = slice(p * LANES, (p + 1) * LANES)
        sh_ref[0, p] = r[:, ls].astype(BF16)
        sh_ref[1, p] = v[:, ls].astype(BF16)
        sh_ref[2, p] = kk[:, ls].astype(BF16)
    bsum = _head_sum(r * kd_sum * rk_ref[...], e_ref, et_ref, 1)
    bonus_ref[...] = (bsum * v).astype(bonus_ref.dtype)


def _rwkv_in_call(z, g, mod, mu, wrkv, w1, a1, g1, w2, a2, g2, w0, a0, k_k, k_a, r_k, e, et, nb, lc):
    bsz, t, d = z.shape
    tm = TOKEN_TILE
    ntile = t // tm
    nctx = lc // tm
    r8 = tm // 8
    tok = pl.BlockSpec((None, tm, d), lambda bi, i: (bi, i, 0))
    return pl.pallas_call(
        functools.partial(_rwkv_in_kernel, nctx=nctx, ntile=ntile),
        grid=(bsz, ntile),
        in_specs=[tok,
                  pl.BlockSpec((None, 8, d), lambda bi, i: (bi, jnp.maximum(i * r8 - 1, 0), 0)),
                  pl.BlockSpec((None, 8, d), lambda bi, i: (bi, jnp.minimum((i + 1) * r8, t // 8 - 1), 0)),
                  _const_spec((1, d)),
                  pl.BlockSpec((None, N_MOD, d), lambda bi, i: (jnp.where(i < nctx, nb, bi), 0, 0)),
                  _const_spec((12, d)),
                  _const_spec((3, d, d)),
                  _const_spec((d, LANES)),
                  _const_spec((d, LANES)),
                  _const_spec((d, RW_GATE_PAD)),
                  _const_spec((LANES, 2 * d)),
                  _const_spec((LANES, 2 * d)),
                  _const_spec((RW_GATE_PAD, d)),
                  _const_spec((2, d)),
                  _const_spec((2, d)),
                  _const_spec((1, d)),
                  _const_spec((1, d)),
                  _const_spec((1, d)),
                  _const_spec((d, LANES)),
                  _const_spec((LANES, d))],
        out_specs=[pl.BlockSpec((None, 3, RW_PAIRS, tm, LANES), lambda bi, i: (bi, 0, 0, i, 0)),
                   pl.BlockSpec((None, 2, 2, RW_PAIRS, tm, LANES), lambda bi, i: (bi, 0, 0, 0, i, 0)),
                   pl.BlockSpec((None, 2, RW_PAIRS, tm, LANES), lambda bi, i: (bi, 0, 0, i, 0)),
                   tok, tok],
        out_shape=[jax.ShapeDtypeStruct((bsz, 3, RW_PAIRS, t, LANES), BF16),
                   jax.ShapeDtypeStruct((bsz, 2, 2, RW_PAIRS, t, LANES), BF16),
                   jax.ShapeDtypeStruct((bsz, 2, RW_PAIRS, t, LANES), F32),
                   jax.ShapeDtypeStruct((bsz, t, d), BF16),
                   jax.ShapeDtypeStruct((bsz, t, d), BF16)],
        compiler_params=_params(("parallel", "parallel")),
        name="rwkv_in",
    )(z, z, z, g, mod, mu, wrkv, w1, a1, g1, w2, a2, g2, w0, a0, k_k, k_a, r_k, e, et)


def _rwkv_chunk_group(chains):
    ln = RW_CHUNK
    row = lax.broadcasted_iota(jnp.int32, (ln, 2 * ln), 0)
    col = lax.broadcasted_iota(jnp.int32, (ln, 2 * ln), 1)
    sidx = jnp.where(col >= ln, col - ln, col)
    lane_a = lax.broadcasted_iota(jnp.int32, (1, LANES), 1) < RW_HEAD
    tr = lax.broadcasted_iota(jnp.int32, (ln, ln), 0)
    tc = lax.broadcasted_iota(jnp.int32, (ln, ln), 1)
    tri = {False: (tc <= tr).astype(F32), True: (tc >= tr).astype(F32)}
    strict = {False: sidx < row, True: sidx > row}
    incl = {False: sidx <= row, True: sidx >= row}
    eye_w = jnp.where(sidx == row, 1.0, 0.0)
    diff = row ^ sidx
    vr = lax.broadcasted_iota(jnp.int32, (LANES, LANES), 0) < RW_HEAD
    kc = lax.broadcasted_iota(jnp.int32, (LANES, LANES), 1) < RW_HEAD
    same_head = vr == kc

    def bd(x):
        zero = jnp.zeros((), x.dtype)
        return jnp.concatenate([jnp.where(lane_a, x, zero), jnp.where(lane_a, zero, x)], axis=0)

    revs = [c[7] for c in chains]
    vs = [c[1] for c in chains]
    s_olds = [c[6] for c in chains]
    gcum = [sum(_dot(tri[c[7]].astype(BF16), part) for part in _split3(c[5])) for c in chains]
    gtot = [jnp.sum(c[5], axis=0, keepdims=True) for c in chains]

    def scaled(c, g, gt):
        r, v, kk, kd, bv, lw = (x.astype(F32) for x in c[:6])
        e_pos = jnp.exp(g)
        e_neg = jnp.exp(-g)
        e_end = jnp.exp(gt - g)
        ar = jnp.concatenate([(-kk * jnp.exp(g - lw)).astype(RW_MM), (r * e_pos).astype(RW_MM)], axis=0)
        bk_end = jnp.concatenate([(bv * e_end).astype(RW_MM), (kd * e_end).astype(RW_MM)], axis=0)
        return ar, (bv * e_neg).astype(RW_MM), (kd * e_neg).astype(RW_MM), bk_end

    sc = [scaled(c, g, gt) for c, g, gt in zip(chains, gcum, gtot)]
    ars_ = [x[0] for x in sc]
    x_bk = [_dot_tb(x[0], jnp.concatenate([bd(x[1]), bd(x[2])], axis=0)) for x in sc]
    x_b = [x[:, :2 * ln] for x in x_bk]
    x_k = [x[:, 2 * ln:] for x in x_bk]
    n_w =[jnp.where(strict[rv], x[:ln], 0.0) for x, rv in zip(x_b, revs)]
    m_rb = [jnp.where(incl[rv], x[ln:], 0.0).astype(RW_MM) for x, rv in zip(x_b, revs)]
    m_k = [jnp.concatenate([jnp.where(strict[rv], x[:ln], 0.0), jnp.where(incl[rv], x[ln:], 0.0)],
                           axis=0).astype(RW_MM) for x, rv in zip(x_k, revs)]
    x_w = [eye_w + jnp.where(diff == 1, n, 0.0) for n in n_w]

    def take_rows(x, h, odd):
        return jnp.concatenate([x[b * h:(b + 1) * h] for b in range(ln // h) if (b % 2 == 1) == odd], axis=0)

    def put_rows(base, upd, h, odd):
        parts, j = [], 0
        for b in range(ln // h):
            if (b % 2 == 1) == odd:
                blk = upd[j * h:(j + 1) * h]
                parts.append(blk if base is None else base[b * h:(b + 1) * h] + blk)
                j += 1
            else:
                parts.append(jnp.zeros((h, upd.shape[1]), upd.dtype) if base is None else base[b * h:(b + 1) * h])
        return jnp.concatenate(parts, axis=0)

    h = 2
    while h < ln:
        lvl = jnp.logical_and(diff >= h, diff < 2 * h)
        n_l = [jnp.where(lvl, n, 0.0) for n in n_w]
        if h < RW_ROW_TILE:
            tmp = [_dot(n.astype(RW_MM), bd(x.astype(RW_MM))) for n, x in zip(n_l, x_w)]
            x_w = [x + _dot(x.astype(RW_MM), bd(t.astype(RW_MM))) for x, t in zip(x_w, tmp)]
        else:
            odd = [not rv for rv in revs]
            tmp = [_dot(take_rows(n, h, o).astype(RW_MM), bd(x.astype(RW_MM))) for n, x, o in zip(n_l, x_w, odd)]
            tmp = [put_rows(None, t, h, o) for t, o in zip(tmp, odd)]
            cor = [_dot(take_rows(x, h, o).astype(RW_MM), bd(t.astype(RW_MM))) for x, t, o in zip(x_w, tmp, odd)]
            x_w = [put_rows(x, c, h, o) for x, c, o in zip(x_w, cor, odd)]
        h *= 2
    gy = [_dot(jnp.concatenate([m, a], axis=1), jnp.concatenate([bd(v), s.T.astype(RW_MM)], axis=0))
          for m, a, v, s in zip(m_k, ars_, vs, s_olds)]
    u = [_dot(x.astype(RW_MM), bd(g[:ln].astype(RW_MM))) for x, g in zip(x_w, gy)]
    y = [g[ln:] + _dot(rb, bd(uu.astype(RW_MM))) for g, rb, uu in zip(gy, m_rb, u)]
    upd = [_dot_ta(jnp.concatenate([uu.astype(RW_MM), v], axis=0), x[3]) for uu, v, x in zip(u, vs, sc)]
    s_new = [s * jnp.exp(gt) + jnp.where(same_head, up, 0.0) for s, gt, up in zip(s_olds, gtot, upd)]
    return list(zip(y, s_new))


def _rwkv_scan_kernel(shf_ref, shb_ref, drf_ref, drb_ref, lwf_ref, lwb_ref, yf_ref, yb_ref, s_s):
    @pl.when(pl.program_id(1) == 0)
    def _():
        s_s[...] = jnp.zeros_like(s_s)

    chains, where = [], []
    for bb in range(RW_BATCH):
        for p in range(RW_PAIRS):
            chains.append((shf_ref[bb, 0, p], shf_ref[bb, 1, p], shf_ref[bb, 2, p], drf_ref[bb, 0, p],
                           drf_ref[bb, 1, p], lwf_ref[bb, p], s_s[bb, 0, p], False))
            where.append((yf_ref, bb, 0, p))
            chains.append((shb_ref[bb, 0, p], shb_ref[bb, 1, p], shb_ref[bb, 2, p], drb_ref[bb, 0, p],
                           drb_ref[bb, 1, p], lwb_ref[bb, p], s_s[bb, 1, p], True))
            where.append((yb_ref, bb, 1, p))
    for (y, s_new), (y_ref, bb, d, p) in zip(_rwkv_chunk_group(chains), where):
        y_ref[bb, p] = y.astype(y_ref.dtype)
        s_s[bb, d, p] = s_new


def _rwkv_scan_call(sh, dr, lw, lc):
    bsz, _, _, t, _ = sh.shape
    ln = RW_CHUNK
    nb = RW_BATCH
    assert bsz % nb == 0
    n = t // ln
    nctx = lc // ln
    rv = lambda i: _rev_chunk(i, nctx, n)
    return pl.pallas_call(
        _rwkv_scan_kernel,
        grid=(bsz // nb, n),
        in_specs=[pl.BlockSpec((nb, 3, RW_PAIRS, ln, LANES), lambda bi, i: (bi, 0, 0, i, 0)),
                  pl.BlockSpec((nb, 3, RW_PAIRS, ln, LANES), lambda bi, i: (bi, 0, 0, rv(i), 0)),
                  pl.BlockSpec((nb, None, 2, RW_PAIRS, ln, LANES), lambda bi, i: (bi, 0, 0, 0, i, 0)),
                  pl.BlockSpec((nb, None, 2, RW_PAIRS, ln, LANES), lambda bi, i: (bi, 1, 0, 0, rv(i), 0)),
                  pl.BlockSpec((nb, None, RW_PAIRS, ln, LANES), lambda bi, i: (bi, 0, 0, i, 0)),
                  pl.BlockSpec((nb, None, RW_PAIRS, ln, LANES), lambda bi, i: (bi, 1, 0, rv(i), 0))],
        out_specs=[pl.BlockSpec((nb, RW_PAIRS, ln, LANES), lambda bi, i: (bi, 0, i, 0)),
                   pl.BlockSpec((nb, RW_PAIRS, ln, LANES), lambda bi, i: (bi, 0, rv(i), 0))],
        out_shape=[jax.ShapeDtypeStruct((bsz, RW_PAIRS, t, LANES), BF16)] * 2,
        scratch_shapes=[pltpu.VMEM((nb, 2, RW_PAIRS, LANES, LANES), F32)],
        compiler_params=_params(("parallel", "arbitrary")),
        name="rwkv_scan",
    )(sh, sh, dr, dr, lw, lw)


def _rwkv_out_kernel(z_ref, yf_ref, yb_ref, gate_ref, bonus_ref, lng_ref, lnb_ref, e_ref, et_ref, w_ref, mod_ref,
                     g2_ref, w1_ref, w2_ref, o_ref):
    y = jnp.concatenate([yf_ref[p].astype(F32) + yb_ref[p].astype(F32) for p in range(RW_PAIRS)], axis=1)
    yc = y - _head_sum(y, e_ref, et_ref, 2) * (1.0 / RW_HEAD)
    var = _head_sum(yc * yc, e_ref, et_ref, 1) * (1.0 / RW_HEAD)
    yn = yc * lax.rsqrt(var + RW_LN_EPS) * lng_ref[...] + lnb_ref[...] + bonus_ref[...]
    out = _dot((yn * gate_ref[...]).astype(BF16), w_ref[...])
    o_ref[...] = _mlp_tail(z_ref[...] + mod_ref[2:3, :] * out, g2_ref[...], mod_ref, w1_ref, w2_ref)


def _rwkv_out_call(z, yf, yb, gate, bonus, ln_g, ln_b, e, et, w, mod, g2, w1, w2, layer, nb, lc, skip):
    bsz, t, d = z.shape
    tm = TOKEN_TILE
    nctx = lc // tm
    nt = t // tm - skip
    tok = pl.BlockSpec((None, tm, d), lambda bi, i: (bi, i + skip, 0))
    pair = pl.BlockSpec((None, RW_PAIRS, tm, LANES), lambda bi, i: (bi, 0, i + skip, 0))
    return pl.pallas_call(
        _rwkv_out_kernel,
        grid=(bsz, nt),
        in_specs=[tok, pair, pair, tok, tok,
                  _const_spec((1, d)), _const_spec((1, d)), _const_spec((d, LANES)), _const_spec((LANES, d)),
                  _const_spec((d, d)),
                  pl.BlockSpec((None, N_MOD, d), lambda bi, i: (jnp.where(i + skip < nctx, nb, bi), 0, 0)),
                  _const_spec((1, d)),
                  _layer_spec((d, D_FF), layer),
                  _layer_spec((D_FF, d), layer)],
        out_specs=pl.BlockSpec((None, tm, d), lambda bi, i: (bi, i, 0)),
        out_shape=jax.ShapeDtypeStruct((bsz, nt * tm, d), F32),
        compiler_params=_params(("parallel", "parallel")),
        name="rwkv_out_mlp",
    )(z, yf, yb, gate, bonus, ln_g, ln_b, e, et, w, mod, g2, w1, w2)


def _group_mean_matrix(n, width):
    idx = jnp.arange(n) // width
    return ((idx[:, None] == idx[None, :]).astype(F32) / width).astype(BF16)


def _rope_tables(lc, s):
    quarter = AT_DH // 4
    inv = ROPE_BASE ** (-jnp.arange(quarter, dtype=F32) / quarter)
    pos = jnp.arange(s)
    rpos = (pos // GRID_W).astype(F32)
    cpos = (pos % GRID_W).astype(F32)
    ang_r = rpos[:, None] * inv[None, :]
    ang_c = cpos[:, None] * inv[None, :]
    cos64 = jnp.concatenate([jnp.cos(ang_r), jnp.cos(ang_r), jnp.cos(ang_c), jnp.cos(ang_c)], axis=1)
    sin64 = jnp.concatenate([-jnp.sin(ang_r), jnp.sin(ang_r), -jnp.sin(ang_c), jnp.sin(ang_c)], axis=1)
    cos = jnp.concatenate([jnp.ones((lc, AT_DH), F32), cos64], axis=0)
    sin = jnp.concatenate([jnp.zeros((lc, AT_DH), F32), sin64], axis=0)
    return jnp.tile(cos, (1, 2)), jnp.tile(sin, (1, 2))


def _even_weights(w_in, b_in):
    def cols(m):
        sizes = (ML_W, ML_W, ML_W, ML_W, ML_GATE_COLS, AT_W, AT_KV_HEADS * AT_DH, AT_KV_HEADS * AT_DH)
        bounds = [sum(sizes[:i + 1]) for i in range(len(sizes) - 1)]
        mq, mk, mv, mo, mg, aq, ak, av = jnp.split(m, bounds, axis=-1)
        dup = lambda u: jnp.concatenate([u[..., :AT_DH], u[..., :AT_DH], u[..., AT_DH:], u[..., AT_DH:]], axis=-1)
        mgp = jnp.pad(mg, [(0, 0)] * (m.ndim - 1) + [(0, LANES - ML_GATE_COLS)])
        return jnp.concatenate([mq, mk, mv, mo, aq, dup(ak), dup(av), mgp], axis=-1)
    return cols(w_in).astype(BF16), cols(b_in[None, :])


def kernel(x, c, ctx, c_ctx, ada_w, ada_b, norm1_g, norm2_g, mlp_w1, mlp_w2, ev_w_in, ev_b_in, ev_w_out, ml_f_bias, ml_out_g, at_q_g, at_k_g, at_sink, rw_mu, rw_w_rkv, rw_w0, rw_w1, rw_w2, rw_a0, rw_a1, rw_a2, rw_g1, rw_g2, rw_k_k, rw_k_a, rw_r_k, rw_ln_g, rw_ln_b, rw_w_out):
    bsz, s, d = x.shape
    lc = ctx.shape[1]
    depth = ada_w.shape[0]
    assert d == D_MODEL and lc == TOKEN_TILE and s % TOKEN_TILE == 0 and bsz < MOD_ROWS
    z = (ctx, x)

    cstack = jnp.zeros((MOD_ROWS, d), F32).at[:bsz].set(c).at[bsz].set(c_ctx)
    mod_all = _ada_call(cstack, ada_w, ada_b).reshape(depth, MOD_ROWS, N_MOD, d)

    grp512 = _group_mean_matrix(AT_W, AT_DH)
    head_e = (jnp.arange(d)[:, None] // RW_HEAD == jnp.arange(LANES)[None, :]).astype(BF16)
    head_et = head_e.T
    cos_t, sin_t = _rope_tables(lc, s)
    gate_is_f = ((jnp.arange(LANES) % 8 >= 4) & (jnp.arange(LANES) < 16)).astype(F32)[None, :]

    mlp_w1_b = mlp_w1.astype(BF16)
    mlp_w2_b = mlp_w2.astype(BF16)
    for layer in range(depth):
        mod = mod_all[layer]
        j = layer // 2
        n1 = norm1_g[layer][None, :]
        mlp_args = (norm2_g[layer][None, :], mlp_w1_b, mlp_w2_b, layer)
        skip = lc // TOKEN_TILE if layer == depth - 1 else 0
        if layer % 2 == 0:
            w_in, b_in = _even_weights(ev_w_in[j], ev_b_in[j])
            fb = jnp.zeros((LANES,), F32)
            for dr in range(2):
                fb = fb.at[8 * dr + 4:8 * dr + 8].set(ml_f_bias[j, dr])
            outs = _even_in_call(
                z, n1, mod, w_in, b_in, jnp.tile(at_q_g[j], AT_HEADS)[None, :],
                jnp.tile(at_k_g[j], 2 * AT_KV_HEADS)[None, :], grp512, cos_t, sin_t, fb[None, :], gate_is_f, bsz)
            if isinstance(z, tuple):
                z = outs[-1]
            mqv, kt, mo, aq, akt, av, gates, gates_t = outs[:8]
            hf, hb = _mlstm_call(mqv, kt, gates, gates_t, lc)
            sink = jnp.broadcast_to(at_sink[j][:, None], (AT_HEADS, LANES))
            ya = _attn_call(aq, akt, av, sink, lc)
            z = _even_out_call(z, hf, hb, mo, ya, ml_out_g[j][None, :], ev_w_out[j].astype(BF16), mod, *mlp_args,
                               bsz, skip)
        else:
            pad_g = RW_GATE_PAD - RW_GATE_LORA
            zeros = jnp.zeros((RW_DECAY_LORA, d), F32)
            w2 = jnp.concatenate([jnp.concatenate([rw_w2[j, 0], zeros], axis=1),
                                  jnp.concatenate([zeros, rw_w2[j, 1]], axis=1)], axis=0)
            a2 = jnp.concatenate([jnp.concatenate([rw_a2[j, 0], zeros], axis=1),
                                  jnp.concatenate([zeros, rw_a2[j, 1]], axis=1)], axis=0)
            sh, dr_, lw, gate, bonus = _rwkv_in_call(
                z, n1, mod, rw_mu[j].reshape(12, d), rw_w_rkv[j].astype(BF16),
                jnp.concatenate([rw_w1[j, 0], rw_w1[j, 1]], axis=1).astype(BF16),
                jnp.concatenate([rw_a1[j, 0], rw_a1[j, 1]], axis=1).astype(BF16),
                jnp.pad(rw_g1[j], ((0, 0), (0, pad_g))).astype(BF16),
                w2.astype(BF16), a2.astype(BF16),
                jnp.pad(rw_g2[j], ((0, pad_g), (0, 0))).astype(BF16),
                rw_w0[j], rw_a0[j], rw_k_k[j][None, :], rw_k_a[j][None, :], rw_r_k[j].reshape(1, d),
                head_e, head_et, bsz, lc)
            yf, yb = _rwkv_scan_call(sh, dr_, lw, lc)
            z = _rwkv_out_call(z, yf, yb, gate, bonus, rw_ln_g[j][None, :], rw_ln_b[j][None, :], head_e, head_et,
                               rw_w_out[j].astype(BF16), mod, *mlp_args, bsz, lc, skip)
    return z
```

```python
import functools

import jax
import jax.numpy as jnp
from jax import lax
from jax.experimental import pallas as pl
from jax.experimental.pallas import tpu as pltpu

F32 = jnp.float32
BF16 = jnp.bfloat16

D_MODEL = 1024
N_MOD = 6
D_FF = 4 * D_MODEL
EPS = 1e-6
GRID_W = 64

ML_HEADS = 4
ML_D = 128
ML_W = ML_HEADS * ML_D
ML_CHUNK = 128
GATE_CAP = 15.0

AT_DH = 64
AT_HEADS = 8
AT_KV_HEADS = 2
AT_W = AT_HEADS * AT_DH
AT_BLOCK = 128
AT_KV_DUP = 2 * AT_KV_HEADS * AT_DH
AT_BATCH = 1
ROPE_BASE = 10000.0

RW_HEAD = 64
RW_PAIRS = D_MODEL // 128
RW_CHUNK = 64
RW_LN_EPS = 64e-5
RW_DECAY_LORA = 64
RW_AAA_LORA = 64
RW_GATE_LORA = 160
RW_GATE_PAD = 256
RW_DECAY_SCALE = 0.6065306597126334

LANES = 128
TOKEN_TILE = 256
FF_TILE = 1024
MOD_ROWS = 16
VMEM_LIMIT = 56 * 1024 * 1024
NEG = -1e30
RW_MM = BF16
RW_BATCH = 4
RW_ROW_TILE = 8

EV_MO = 3 * ML_W
EV_AQ = EV_MO + ML_W
EV_AK = EV_AQ + AT_W
EV_AV = EV_AK + AT_KV_DUP
EV_GATE = EV_AV + AT_KV_DUP
EV_COLS = EV_GATE + LANES
ML_GATE_COLS = 4 * ML_HEADS


def _dot(a, b):
    return jnp.dot(a, b, preferred_element_type=F32)


def _dot_tb(a, b):
    return lax.dot_general(a, b, (((1,), (1,)), ((), ())), preferred_element_type=F32)


def _dot_ta(a, b):
    return lax.dot_general(a, b, (((0,), (0,)), ((), ())), preferred_element_type=F32)


def _norm_mod(z, g, shift, scale):
    ms = jnp.mean(z * z, axis=-1, keepdims=True)
    return (z * lax.rsqrt(ms + EPS) * g) * (1.0 + scale) + shift


def _sigmoid(x):
    return 1.0 / (1.0 + jnp.exp(-x))


def _head_sum(x, e_ref, et_ref, terms):
    s = _dot(x.astype(BF16), e_ref[...])
    hi = s.astype(BF16)
    if terms == 1:
        return _dot(hi, et_ref[...])
    lo = (s - hi.astype(F32)).astype(BF16)
    return _dot(jnp.concatenate([hi, lo], axis=1), jnp.concatenate([et_ref[...], et_ref[...]], axis=0))


def _layer_spec(shape, layer):
    nd = len(shape)
    return pl.BlockSpec((None,) + tuple(shape), lambda *_: (layer,) + (0,) * nd, pipeline_mode=pl.Buffered(1))


def _const_spec(shape):
    nd = len(shape)
    return pl.BlockSpec(shape, lambda *_: (0,) * nd, pipeline_mode=pl.Buffered(1))


def _params(sem):
    return pltpu.CompilerParams(dimension_semantics=sem, vmem_limit_bytes=VMEM_LIMIT)


def _ada_kernel(c_ref, w_ref, b_ref, o_ref):
    cv = c_ref[...]
    s = cv * _sigmoid(cv)
    o_ref[...] = jnp.dot(s, w_ref[...], preferred_element_type=F32,
                         precision=lax.Precision.HIGHEST) + b_ref[...]


def _ada_call(cstack, ada_w, ada_b):
    depth = ada_w.shape[0]
    n = N_MOD * D_MODEL
    tn = 1024
    return pl.pallas_call(
        _ada_kernel,
        grid=(depth, n // tn),
        in_specs=[pl.BlockSpec((MOD_ROWS, D_MODEL), lambda l, j: (0, 0)),
                  pl.BlockSpec((None, D_MODEL, tn), lambda l, j: (l, 0, j)),
                  pl.BlockSpec((None, 1, tn), lambda l, j: (l, 0, j))],
        out_specs=pl.BlockSpec((None, MOD_ROWS, tn), lambda l, j: (l, 0, j)),
        out_shape=jax.ShapeDtypeStruct((depth, MOD_ROWS, n), F32),
        compiler_params=_params(("parallel", "parallel")),
        name="ada_ln",
    )(cstack, ada_w, ada_b.reshape(depth, 1, n))


def _rope(x, cos, sin, lane_lo):
    n = x.shape[1]
    up = pltpu.roll(x, n - 16, 1)
    dn = pltpu.roll(x, 16, 1)
    reps = n // LANES
    c = jnp.concatenate([cos] * reps, axis=1) if reps > 1 else cos
    s = jnp.concatenate([sin] * reps, axis=1) if reps > 1 else sin
    lo = jnp.concatenate([lane_lo] * reps, axis=1) if reps > 1 else lane_lo
    return x * c + jnp.where(lo, up, dn) * s


def _even_in_kernel(*refs, assemble):
    if assemble:
        ctx_ref, x_ref, *refs, z_out_ref = refs
        z = jnp.where(pl.program_id(1) == 0, ctx_ref[...], x_ref[...])
        z_out_ref[...] = z
    else:
        z_ref, *refs = refs
        z = z_ref[...]
    (g_ref, mod_ref, w_ref, b_ref, gq_ref, gk_ref, grp_ref, cos_ref, sin_ref, fb_ref, isf_ref,
     mqv_ref, kt_ref, mo_ref, aq_ref, akt_ref, av_ref, gate_ref, gatet_ref) = refs
    h = _norm_mod(z, g_ref[...], mod_ref[0:1, :], mod_ref[1:2, :])
    p = _dot(h.astype(BF16), w_ref[...]) + b_ref[...]
    mqv_ref[:, 0:ML_W] = p[:, 0:ML_W].astype(BF16)
    mqv_ref[:, ML_W:2 * ML_W] = p[:, 2 * ML_W:3 * ML_W].astype(BF16)
    kt_ref[...] = (p[:, ML_W:2 * ML_W] * (ML_D ** -0.5)).T.astype(BF16)
    mo_ref[...] = p[:, EV_MO:EV_MO + ML_W].astype(mo_ref.dtype)
    cos = cos_ref[...]
    sin = sin_ref[...]
    lane_lo = (lax.broadcasted_iota(jnp.int32, (1, LANES), 1) % 32) < 16
    q = p[:, EV_AQ:EV_AQ + AT_W]
    qms = _dot((q * q).astype(BF16), grp_ref[...])
    q = q * lax.rsqrt(qms + EPS) * gq_ref[...]
    aq_ref[...] = (_rope(q, cos, sin, lane_lo) * (AT_DH ** -0.5)).astype(BF16)
    k = p[:, EV_AK:EV_AK + AT_KV_DUP]
    kms = _dot((k * k).astype(BF16), grp_ref[0:AT_KV_DUP, 0:AT_KV_DUP])
    k = k * lax.rsqrt(kms + EPS) * gk_ref[...]
    akt_ref[...] = _rope(k, cos, sin, lane_lo).T.astype(BF16)
    av_ref[...] = p[:, EV_AV:EV_AV + AT_KV_DUP].astype(BF16)
    gt = p[:, EV_GATE:EV_GATE + LANES] + fb_ref[...]
    sc = GATE_CAP * jnp.tanh(gt * (1.0 / GATE_CAP))
    logsig = jnp.minimum(sc, 0.0) - jnp.log(1.0 + jnp.exp(-jnp.abs(sc)))
    gates = jnp.where(isf_ref[...] > 0.5, logsig, sc)
    gate_ref[...] = gates
    gatet_ref[...] = gates.T[0:ML_GATE_COLS, :]


def _even_in_call(z, g, mod, w, b, gq, gk, grp, cos, sin, fb, isf, nb):
    tm = TOKEN_TILE
    tok = lambda width: pl.BlockSpec((None, tm, width), lambda bi, i: (bi, i, 0))
    tpose = lambda rows: pl.BlockSpec((None, rows, tm), lambda bi, i: (bi, 0, i))
    assemble = isinstance(z, tuple)
    if assemble:
        ctx, x = z
        bsz, s, d = x.shape
        t = ctx.shape[1] + s
        z_args = (ctx, x)
        z_specs = [pl.BlockSpec((None, tm, d), lambda bi, i: (bi, 0, 0)),
                   pl.BlockSpec((None, tm, d), lambda bi, i: (bi, jnp.maximum(i - 1, 0), 0))]
        extra_specs, extra_shapes = [tok(d)], [jax.ShapeDtypeStruct((bsz, t, d), F32)]
    else:
        bsz, t, d = z.shape
        z_args, z_specs, extra_specs, extra_shapes = (z,), [tok(d)], [], []
    return pl.pallas_call(
        functools.partial(_even_in_kernel, assemble=assemble),
        grid=(bsz, t // tm),
        in_specs=z_specs + [
                  _const_spec((1, d)),
                  pl.BlockSpec((None, N_MOD, d), lambda bi, i: (jnp.where(i == 0, nb, bi), 0, 0)),
                  _const_spec((d, EV_COLS)),
                  _const_spec((1, EV_COLS)),
                  _const_spec((1, AT_W)),
                  _const_spec((1, AT_KV_DUP)),
                  _const_spec((AT_W, AT_W)),
                  pl.BlockSpec((tm, LANES), lambda bi, i: (i, 0)),
                  pl.BlockSpec((tm, LANES), lambda bi, i: (i, 0)),
                  _const_spec((1, LANES)),
                  _const_spec((1, LANES))],
        out_specs=[tok(2 * ML_W), tpose(ML_W), tok(ML_W), tok(AT_W), tpose(AT_KV_DUP), tok(AT_KV_DUP), tok(LANES),
                   tpose(ML_GATE_COLS)] + extra_specs,
        out_shape=[jax.ShapeDtypeStruct((bsz, t, 2 * ML_W), BF16),
                   jax.ShapeDtypeStruct((bsz, ML_W, t), BF16),
                   jax.ShapeDtypeStruct((bsz, t, ML_W), BF16),
                   jax.ShapeDtypeStruct((bsz, t, AT_W), BF16),
                   jax.ShapeDtypeStruct((bsz, AT_KV_DUP, t), BF16),
                   jax.ShapeDtypeStruct((bsz, t, AT_KV_DUP), BF16),
                   jax.ShapeDtypeStruct((bsz, t, LANES), F32),
                   jax.ShapeDtypeStruct((bsz, ML_GATE_COLS, t), F32)] + extra_shapes,
        compiler_params=_params(("parallel", "parallel")),
        name="even_in",
    )(*z_args, g, mod, w, b, gq, gk, grp, cos, sin, fb, isf)


def _split3(x):
    hi = x.astype(BF16)
    r1 = x - hi.astype(F32)
    mid = r1.astype(BF16)
    return hi, mid, (r1 - mid.astype(F32)).astype(BF16)


def _mlstm_kernel(qf_ref, ktf_ref, vf_ref, qb_ref, ktb_ref, vb_ref, gcf_ref, gcb_ref, grf_ref, grb_ref, sel_ref,
                  hf_ref, hb_ref, c_s, m_s):
    ln = ML_CHUNK

    @pl.when(pl.program_id(1) == 0)
    def _():
        c_s[...] = jnp.zeros_like(c_s)
        m_s[...] = jnp.zeros_like(m_s)

    ri = lax.broadcasted_iota(jnp.int32, (ln, ln), 0)
    ci = lax.broadcasted_iota(jnp.int32, (ln, ln), 1)
    before = (ci <= ri, ci >= ri)
    refs = ((qf_ref, ktf_ref, vf_ref, gcf_ref, grf_ref, hf_ref), (qb_ref, ktb_ref, vb_ref, gcb_ref, grb_ref, hb_ref))
    cum_r = [sum(_dot(part, before[1 - d].astype(BF16)) for part in _split3(refs[d][4][...])) for d in range(2)]
    bc_all = []
    for d in range(2):
        cum_c = sum(_dot(before[d].astype(BF16), part) for part in _split3(refs[d][3][...]))
        sel = sel_ref[d]
        bc_all.append(sum(_dot(part, sel) for part in _split3(cum_c)))
    chains = [(d, hd) for d in range(2) for hd in range(ML_HEADS)]
    lanes = [slice(hd * ML_D, (hd + 1) * ML_D) for _, hd in chains]
    st = [d * ML_HEADS + hd for d, hd in chains]
    ones = jnp.ones((ln, ML_D), BF16)
    q = [refs[d][0][:, ls] for (d, _), ls in zip(chains, lanes)]
    kt = [refs[d][1][ls, :] for (d, _), ls in zip(chains, lanes)]
    v1 = [jnp.concatenate([refs[d][2][:, ls], ones], axis=1) for (d, _), ls in zip(chains, lanes)]
    i_row = [refs[d][4][8 * d + hd:8 * d + hd + 1, :] for d, hd in chains]
    f_row = [refs[d][4][8 * d + 4 + hd:8 * d + 5 + hd, :] for d, hd in chains]
    bcum_row = [cum_r[d][8 * d + 4 + hd:8 * d + 5 + hd, :] for d, hd in chains]
    bcum = [bc_all[d][:, ls] for (d, _), ls in zip(chains, lanes)]
    m_old = [m_s[s:s + 1, 0:1] for s in st]
    c_old = [c_s[s] for s in st]
    qk = [_dot(a, b) for a, b in zip(q, kt)]
    dlog =[jnp.where(before[d], bc - br + ir, NEG) for (d, _), bc, br, ir in zip(chains, bcum, bcum_row, i_row)]
    inter = [bc + m for bc, m in zip(bcum, m_old)]
    mt = [jnp.maximum(jnp.max(dl, axis=1, keepdims=True), it) for dl, it in zip(dlog, inter)]
    sc = [x * jnp.exp(dl - m) for x, dl, m in zip(qk, dlog, mt)]
    iw = [jnp.exp(it - m) for it, m in zip(inter, mt)]
    tots = [_dot(jnp.concatenate([s.astype(BF16), (w * a.astype(F32)).astype(BF16)], axis=1),
                 jnp.concatenate([b, c.astype(BF16)], axis=0)) for s, w, a, b, c in zip(sc, iw, q, v1, c_old)]
    for j, (d, _) in enumerate(chains):
        tot = tots[j]
        den =jnp.maximum(jnp.abs(tot[:, ML_D:]), jnp.exp(-mt[j]))
        refs[d][5][:, lanes[j]] = (tot[:, :ML_D] / den).astype(refs[d][5].dtype)
    bl = [jnp.sum(fr, axis=1, keepdims=True) for fr in f_row]
    ws_log = [b - br + ir for b, br, ir in zip(bl, bcum_row, i_row)]
    m_new = [jnp.maximum(b + m, jnp.max(w, axis=1, keepdims=True)) for b, m, w in zip(bl, m_old, ws_log)]
    ws = [jnp.exp(w - m) for w, m in zip(ws_log, m_new)]
    upd = [_dot((a.astype(F32) * w).astype(BF16), b) for a, w, b in zip(kt, ws, v1)]
    for j, s in enumerate(st):
        c_s[s] = jnp.exp(bl[j] + m_old[j] - m_new[j]) * c_old[j] + upd[j]
        m_s[s:s + 1, :] = jnp.broadcast_to(m_new[j], (1, LANES))


def _rev_chunk(i, nctx, n):
    return jnp.where(i < nctx, nctx - 1 - i, n + nctx - 1 - i)


def _mlstm_call(mqv, kt, gates, gates_t, lc):
    bsz, t, _ = mqv.shape
    ln = ML_CHUNK
    assert ln == ML_D == LANES
    n = t // ln
    nctx = lc // ln
    rv = lambda i: _rev_chunk(i, nctx, n)
    col = jnp.arange(LANES)[None, :, None]
    want = (8 * jnp.arange(2)[:, None, None] + 4 + jnp.arange(ML_HEADS * ln)[None, None, :] // ln)
    sel = (col == want).astype(BF16)
    return pl.pallas_call(
        _mlstm_kernel,
        grid=(bsz, n),
        in_specs=[pl.BlockSpec((None, ln, ML_W), lambda bi, i: (bi, i, 0)),
                  pl.BlockSpec((None, ML_W, ln), lambda bi, i: (bi, 0, i)),
                  pl.BlockSpec((None, ln, ML_W), lambda bi, i: (bi, i, 1)),
                  pl.BlockSpec((None, ln, ML_W), lambda bi, i: (bi, rv(i), 0)),
                  pl.BlockSpec((None, ML_W, ln), lambda bi, i: (bi, 0, rv(i))),
                  pl.BlockSpec((None, ln, ML_W), lambda bi, i: (bi, rv(i), 1)),
                  pl.BlockSpec((None, ln, LANES), lambda bi, i: (bi, i, 0)),
                  pl.BlockSpec((None, ln, LANES), lambda bi, i: (bi, rv(i), 0)),
                  pl.BlockSpec((None, ML_GATE_COLS, ln), lambda bi, i: (bi, 0, i)),
                  pl.BlockSpec((None, ML_GATE_COLS, ln), lambda bi, i: (bi, 0, rv(i))),
                  _const_spec((2, LANES, ML_HEADS * ln))],
        out_specs=[pl.BlockSpec((None, ln, ML_W), lambda bi, i: (bi, i, 0)),
                   pl.BlockSpec((None, ln, ML_W), lambda bi, i: (bi, rv(i), 0))],
        out_shape=[jax.ShapeDtypeStruct((bsz, t, ML_W), BF16)] * 2,
        scratch_shapes=[pltpu.VMEM((2 * ML_HEADS, ML_D, 2 * ML_D), F32),
                        pltpu.VMEM((2 * ML_HEADS, LANES), F32)],
        compiler_params=_params(("parallel", "arbitrary")),
        name="mlstm_scan",
    )(mqv, kt, mqv, mqv, kt, mqv, gates, gates, gates_t, gates_t, sel)


def _attn_kernel(q_ref, ktp_ref, kto_ref, ktn_ref, ktc_ref, vp_ref, vo_ref, vn_ref, vc_ref, sink_ref, o_ref,
                 *, nctx, n):
    j = pl.program_id(1)
    blk = AT_BLOCK
    nk = 3 * blk + ktc_ref.shape[2]
    qi = lax.broadcasted_iota(jnp.int32, (blk, nk), 0)
    ki = lax.broadcasted_iota(jnp.int32, (blk, nk), 1)
    lo = jnp.where(j > nctx, qi, blk)
    hi = jnp.where(j < n - 1, qi + 2 * blk, 2 * blk - 1)
    band = jnp.where(ki < lo, NEG, jnp.where(ki > hi, NEG, jnp.where(j >= nctx, 0.0, NEG)))
    bias = jnp.where(ki >= 3 * blk, 0.0, band)
    lane = lax.broadcasted_iota(jnp.int32, (1, LANES), 1)
    sub = lax.broadcasted_iota(jnp.int32, (LANES, 1), 0)
    lane_half = (lane < AT_DH, lane >= AT_DH)
    sub_half = (sub < AT_DH, sub >= AT_DH)
    kts = (ktp_ref, kto_ref, ktn_ref, ktc_ref)
    vs = (vp_ref, vo_ref, vn_ref, vc_ref)
    zero = jnp.zeros((), BF16)
    nbat = q_ref.shape[0]
    kx, vx = {}, {}
    for bb in range(nbat):
        for g in range(AT_KV_HEADS):
            rows = slice(g * LANES, (g + 1) * LANES)
            for e in range(2):
                kx[bb, g, e] = jnp.concatenate([jnp.where(sub_half[e], r[bb, rows, :], zero) for r in kts], axis=1)
                vx[bb, g, e] = jnp.concatenate(
                    [jnp.concatenate([jnp.where(lane_half[e], r[bb, :, rows], zero),
                                      jnp.ones((r.shape[1], LANES), BF16)], axis=1) for r in vs], axis=0)
    heads = [(bb, hd // 2, hd % 2, hd // 4) for bb in range(nbat) for hd in range(AT_HEADS)]
    qc = {(bb, c): q_ref[bb, :, c * LANES:(c + 1) * LANES] for bb in range(nbat) for c in range(AT_HEADS // 2)}
    sink = [sink_ref[2 * c + e:2 * c + e + 1, 0:1] for _, c, e, _ in heads]
    ss = [_dot(qc[bb, c], kx[bb, g, e]) + bias for bb, c, e, g in heads]
    m = [jnp.maximum(jnp.max(s, axis=1, keepdims=True), sk) for s, sk in zip(ss, sink)]
    res = [_dot(jnp.exp(s - mm).astype(BF16), vx[bb, g, e]) for s, mm, (bb, c, e, g) in zip(ss, m, heads)]
    outs =[r[:, :LANES] / (r[:, LANES:] + jnp.exp(sk - mm)) for r, sk, mm in zip(res, sink, m)]
    for i in range(0, len(heads), 2):
        bb, c = heads[i][0], heads[i][1]
        o_ref[bb, :, c * LANES:(c + 1) * LANES] = (outs[i] + outs[i + 1]).astype(o_ref.dtype)


def _attn_call(aq, akt, av, sink, lc):
    bsz, t, _ = aq.shape
    blk = AT_BLOCK
    nb = AT_BATCH
    assert bsz % nb == 0
    n = t // blk
    nctx = lc // blk
    prev = lambda j: jnp.clip(j - 1, nctx, n - 1)
    own = lambda j: jnp.clip(j, nctx, n - 1)
    nxt = lambda j: jnp.clip(j + 1, nctx, n - 1)
    kt = lambda f: pl.BlockSpec((nb, AT_KV_DUP, blk), lambda bi, j: (bi, 0, f(j)))
    vv = lambda f: pl.BlockSpec((nb, blk, AT_KV_DUP), lambda bi, j: (bi, f(j), 0))
    return pl.pallas_call(
        functools.partial(_attn_kernel, nctx=nctx, n=n),
        grid=(bsz // nb, n),
        in_specs=[pl.BlockSpec((nb, blk, AT_W), lambda bi, j: (bi, j, 0)),
                  kt(prev), kt(own), kt(nxt),
                  pl.BlockSpec((nb, AT_KV_DUP, lc), lambda bi, j: (bi, 0, 0)),
                  vv(prev), vv(own), vv(nxt),
                  pl.BlockSpec((nb, lc, AT_KV_DUP), lambda bi, j: (bi, 0, 0)),
                  _const_spec((AT_HEADS, LANES))],
        out_specs=pl.BlockSpec((nb, blk, AT_W), lambda bi, j: (bi, j, 0)),
        out_shape=jax.ShapeDtypeStruct((bsz, t, AT_W), BF16),
        compiler_params=_params(("parallel", "parallel")),
        name="window_attn",
    )(aq, akt, akt, akt, akt, av, av, av, av, sink)


def _mlp_tail(z, g, mod_ref, w1_ref, w2_ref):
    h = _norm_mod(z, g, mod_ref[3:4, :], mod_ref[4:5, :]).astype(BF16)
    acc = jnp.zeros(z.shape, F32)
    for f in range(D_FF // FF_TILE):
        a = jnp.maximum(_dot(h, w1_ref[:, f * FF_TILE:(f + 1) * FF_TILE]), 0.0)
        acc = acc + _dot((a * a).astype(BF16), w2_ref[f * FF_TILE:(f + 1) * FF_TILE, :])
    return z + mod_ref[5:6, :] * acc


def _even_out_kernel(z_ref, hf_ref, hb_ref, mo_ref, ya_ref, og_ref, w_ref, mod_ref, g2_ref, w1_ref, w2_ref, o_ref):
    hs = hf_ref[...].astype(F32) + hb_ref[...].astype(F32)
    parts = []
    for hd in range(ML_HEADS):
        x = hs[:, hd * ML_D:(hd + 1) * ML_D]
        parts.append(x * lax.rsqrt(jnp.mean(x * x, axis=1, keepdims=True) + EPS))
    ym = jnp.concatenate(parts, axis=1) * og_ref[...] * _sigmoid(mo_ref[...].astype(F32))
    y = _dot(ym.astype(BF16), w_ref[0:ML_W, :]) + _dot(ya_ref[...], w_ref[ML_W:ML_W + AT_W, :])
    o_ref[...] = _mlp_tail(z_ref[...] + mod_ref[2:3, :] * y, g2_ref[...], mod_ref, w1_ref, w2_ref)


def _even_out_call(z, hf, hb, mo, ya, og, w, mod, g2, w1, w2, layer, nb, skip):
    bsz, t, d = z.shape
    tm = TOKEN_TILE
    nt = t // tm - skip
    tok = lambda width: pl.BlockSpec((None, tm, width), lambda bi, i: (bi, i + skip, 0))
    return pl.pallas_call(
        _even_out_kernel,
        grid=(bsz, nt),
        in_specs=[tok(d), tok(ML_W), tok(ML_W), tok(ML_W), tok(AT_W),
                  _const_spec((1, ML_W)),
                  _const_spec((ML_W + AT_W, d)),
                  pl.BlockSpec((None, N_MOD, d), lambda bi, i: (jnp.where(i + skip == 0, nb, bi), 0, 0)),
                  _const_spec((1, d)),
                  _layer_spec((d, D_FF), layer),
                  _layer_spec((D_FF, d), layer)],
        out_specs=pl.BlockSpec((None, tm, d), lambda bi, i: (bi, i, 0)),
        out_shape=jax.ShapeDtypeStruct((bsz, nt * tm, d), F32),
        compiler_params=_params(("parallel", "parallel")),
        name="even_out_mlp",
    )(z, hf, hb, mo, ya, og, w, mod, g2, w1, w2)


def _rwkv_in_kernel(z_ref, zp_ref, zn_ref, g_ref, mod_ref, mu_ref, wrkv_ref, w1_ref, a1_ref, g1_ref,
                    w2_ref, a2_ref, g2_ref, w0_ref, a0_ref, kk_ref, ka_ref, rk_ref, e_ref, et_ref,
                    sh_ref, dr_ref, lw_ref, gate_ref, bonus_ref, *, nctx, ntile):
    i = pl.program_id(1)
    tm, d = z_ref.shape
    g = g_ref[...]
    shift = mod_ref[0:1, :]
    scale = mod_ref[1:2, :]
    h = _norm_mod(z_ref[...], g, shift, scale)
    no_prev = jnp.logical_or(i == 0, i == nctx)
    no_next = jnp.logical_or(i == nctx - 1, i == ntile - 1)
    hp = jnp.where(no_prev, 0.0, _norm_mod(zp_ref[7:8, :], g, shift, scale))
    hn = jnp.where(no_next, 0.0, _norm_mod(zn_ref[0:1, :], g, shift, scale))
    row = lax.broadcasted_iota(jnp.int32, (tm, 1), 0)
    dp = jnp.where(row == 0, hp, pltpu.roll(h, 1, 0)) - h
    dn = jnp.where(row == tm - 1, hn, pltpu.roll(h, tm - 1, 0)) - h

    hb, dpb, dnb = h.astype(BF16), dp.astype(BF16), dn.astype(BF16)
    mub = mu_ref[...].astype(BF16)

    def mix(n):
        return hb + mub[2 * n:2 * n + 1, :] * dpb + mub[2 * n + 1:2 * n + 2, :] * dnb

    r = _dot(mix(0), wrkv_ref[0])
    k = _dot(mix(2), wrkv_ref[1])
    v = _dot(mix(3), wrkv_ref[2])
    gate_ref[...] = _dot(_sigmoid(_dot(mix(5), g1_ref[...])).astype(BF16), g2_ref[...]).astype(gate_ref.dtype)
    lora_w = _dot(jnp.tanh(_dot(mix(1), w1_ref[...])).astype(BF16), w2_ref[...])
    lora_a = _dot(_dot(mix(4), a1_ref[...]).astype(BF16), a2_ref[...])
    kkr = k * kk_ref[...]
    ssq = _head_sum(kkr * kkr, e_ref, et_ref, 1)
    kk = kkr * lax.rsqrt(jnp.maximum(ssq, 1e-24))
    kd_sum = None
    for dr in range(2):
        cols = slice(dr * d, (dr + 1) * d)
        lw = -RW_DECAY_SCALE * _sigmoid(w0_ref[dr:dr + 1, :] + lora_w[:, cols])
        a = _sigmoid(a0_ref[dr:dr + 1, :] + lora_a[:, cols])
        kd = k * (1.0 + (a - 1.0) * ka_ref[...])
        bvec = kk * a
        kd_sum = kd if kd_sum is None else kd_sum + kd
        for p in range(RW_PAIRS):
            ls = slice(p * LANES, (p + 1) * LANES)
            lw_ref[dr, p] = lw[:, ls]
            dr_ref[dr, 0, p] = kd[:, ls].astype(BF16)
            dr_ref[dr, 1, p] = bvec[:, ls].astype(BF16)
    for p in range(RW_PAIRS):
        ls = slice(p * LANES, (p + 1) * LANES)
        sh_ref[0, p] = r[:, ls].astype(BF16)
        sh_ref[1, p] = v[:, ls].astype(BF16)
        sh_ref[2, p] = kk[:, ls].astype(BF16)
    bsum = _head_sum(r * kd_sum * rk_ref[...], e_ref, et_ref, 1)
    bonus_ref[...] = (bsum * v).astype(bonus_ref.dtype)


def _rwkv_in_call(z, g, mod, mu, wrkv, w1, a1, g1, w2, a2, g2, w0, a0, k_k, k_a, r_k, e, et, nb, lc):
    bsz, t, d = z.shape
    tm = TOKEN_TILE
    ntile = t // tm
    nctx = lc // tm
    r8 = tm // 8
    tok = pl.BlockSpec((None, tm, d), lambda bi, i: (bi, i, 0))
    return pl.pallas_call(
        functools.partial(_rwkv_in_kernel, nctx=nctx, ntile=ntile),
        grid=(bsz, ntile),
        in_specs=[tok,
                  pl.BlockSpec((None, 8, d), lambda bi, i: (bi, jnp.maximum(i * r8 - 1, 0), 0)),
                  pl.BlockSpec((None, 8, d), lambda bi, i: (bi, jnp.minimum((i + 1) * r8, t // 8 - 1), 0)),
                  _const_spec((1, d)),
                  pl.BlockSpec((None, N_MOD, d), lambda bi, i: (jnp.where(i < nctx, nb, bi), 0, 0)),
                  _const_spec((12, d)),
                  _const_spec((3, d, d)),
                  _const_spec((d, LANES)),
                  _const_spec((d, LANES)),
                  _const_spec((d, RW_GATE_PAD)),
                  _const_spec((LANES, 2 * d)),
                  _const_spec((LANES, 2 * d)),
                  _const_spec((RW_GATE_PAD, d)),
                  _const_spec((2, d)),
                  _const_spec((2, d)),
                  _const_spec((1, d)),
                  _const_spec((1, d)),
                  _const_spec((1, d)),
                  _const_spec((d, LANES)),
                  _const_spec((LANES, d))],
        out_specs=[pl.BlockSpec((None, 3, RW_PAIRS, tm, LANES), lambda bi, i: (bi, 0, 0, i, 0)),
                   pl.BlockSpec((None, 2, 2, RW_PAIRS, tm, LANES), lambda bi, i: (bi, 0, 0, 0, i, 0)),
                   pl.BlockSpec((None, 2, RW_PAIRS, tm, LANES), lambda bi, i: (bi, 0, 0, i, 0)),
                   tok, tok],
        out_shape=[jax.ShapeDtypeStruct((bsz, 3, RW_PAIRS, t, LANES), BF16),
                   jax.ShapeDtypeStruct((bsz, 2, 2, RW_PAIRS, t, LANES), BF16),
                   jax.ShapeDtypeStruct((bsz, 2, RW_PAIRS, t, LANES), F32),
                   jax.ShapeDtypeStruct((bsz, t, d), BF16),
                   jax.ShapeDtypeStruct((bsz, t, d), BF16)],
        compiler_params=_params(("parallel", "parallel")),
        name="rwkv_in",
    )(z, z, z, g, mod, mu, wrkv, w1, a1, g1, w2, a2, g2, w0, a0, k_k, k_a, r_k, e, et)


def _rwkv_chunk_group(chains):
    ln = RW_CHUNK
    row = lax.broadcasted_iota(jnp.int32, (ln, 2 * ln), 0)
    col = lax.broadcasted_iota(jnp.int32, (ln, 2 * ln), 1)
    sidx = jnp.where(col >= ln, col - ln, col)
    lane_a = lax.broadcasted_iota(jnp.int32, (1, LANES), 1) < RW_HEAD
    tr = lax.broadcasted_iota(jnp.int32, (ln, ln), 0)
    tc = lax.broadcasted_iota(jnp.int32, (ln, ln), 1)
    tri = {False: (tc <= tr).astype(F32), True: (tc >= tr).astype(F32)}
    strict = {False: sidx < row, True: sidx > row}
    incl = {False: sidx <= row, True: sidx >= row}
    eye_w = jnp.where(sidx == row, 1.0, 0.0)
    diff = row ^ sidx
    vr = lax.broadcasted_iota(jnp.int32, (LANES, LANES), 0) < RW_HEAD
    kc = lax.broadcasted_iota(jnp.int32, (LANES, LANES), 1) < RW_HEAD
    same_head = vr == kc

    def bd(x):
        zero = jnp.zeros((), x.dtype)
        return jnp.concatenate([jnp.where(lane_a, x, zero), jnp.where(lane_a, zero, x)], axis=0)

    revs = [c[7] for c in chains]
    vs = [c[1] for c in chains]
    s_olds = [c[6] for c in chains]
    gcum = [sum(_dot(tri[c[7]].astype(BF16), part) for part in _split3(c[5])) for c in chains]
    gtot = [jnp.sum(c[5], axis=0, keepdims=True) for c in chains]

    def scaled(c, g, gt):
        r, v, kk, kd, bv, lw = (x.astype(F32) for x in c[:6])
        e_pos = jnp.exp(g)
        e_neg = jnp.exp(-g)
        e_end = jnp.exp(gt - g)
        ar = jnp.concatenate([(-kk * jnp.exp(g - lw)).astype(RW_MM), (r * e_pos).astype(RW_MM)], axis=0)
        bk_end = jnp.concatenate([(bv * e_end).astype(RW_MM), (kd * e_end).astype(RW_MM)], axis=0)
        return ar, (bv * e_neg).astype(RW_MM), (kd * e_neg).astype(RW_MM), bk_end

    sc = [scaled(c, g, gt) for c, g, gt in zip(chains, gcum, gtot)]
    ars_ = [x[0] for x in sc]
    x_bk = [_dot_tb(x[0], jnp.concatenate([bd(x[1]), bd(x[2])], axis=0)) for x in sc]
    x_b = [x[:, :2 * ln] for x in x_bk]
    x_k = [x[:, 2 * ln:] for x in x_bk]
    n_w =[jnp.where(strict[rv], x[:ln], 0.0) for x, rv in zip(x_b, revs)]
    m_rb = [jnp.where(incl[rv], x[ln:], 0.0).astype(RW_MM) for x, rv in zip(x_b, revs)]
    m_k = [jnp.concatenate([jnp.where(strict[rv], x[:ln], 0.0), jnp.where(incl[rv], x[ln:], 0.0)],
                           axis=0).astype(RW_MM) for x, rv in zip(x_k, revs)]
    x_w = [eye_w + jnp.where(diff == 1, n, 0.0) for n in n_w]

    def take_rows(x, h, odd):
        return jnp.concatenate([x[b * h:(b + 1) * h] for b in range(ln // h) if (b % 2 == 1) == odd], axis=0)

    def put_rows(base, upd, h, odd):
        parts, j = [], 0
        for b in range(ln // h):
            if (b % 2 == 1) == odd:
                blk = upd[j * h:(j + 1) * h]
                parts.append(blk if base is None else base[b * h:(b + 1) * h] + blk)
                j += 1
            else:
                parts.append(jnp.zeros((h, upd.shape[1]), upd.dtype) if base is None else base[b * h:(b + 1) * h])
        return jnp.concatenate(parts, axis=0)

    h = 2
    while h < ln:
        lvl = jnp.logical_and(diff >= h, diff < 2 * h)
        n_l = [jnp.where(lvl, n, 0.0) for n in n_w]
        if h < RW_ROW_TILE:
            tmp = [_dot(n.astype(RW_MM), bd(x.astype(RW_MM))) for n, x in zip(n_l, x_w)]
            x_w = [x + _dot(x.astype(RW_MM), bd(t.astype(RW_MM))) for x, t in zip(x_w, tmp)]
        else:
            odd = [not rv for rv in revs]
            tmp = [_dot(take_rows(n, h, o).astype(RW_MM), bd(x.astype(RW_MM))) for n, x, o in zip(n_l, x_w, odd)]
            tmp = [put_rows(None, t, h, o) for t, o in zip(tmp, odd)]
            cor = [_dot(take_rows(x, h, o).astype(RW_MM), bd(t.astype(RW_MM))) for x, t, o in zip(x_w, tmp, odd)]
            x_w = [put_rows(x, c, h, o) for x, c, o in zip(x_w, cor, odd)]
        h *= 2
    gy = [_dot(jnp.concatenate([m, a], axis=1), jnp.concatenate([bd(v), s.T.astype(RW_MM)], axis=0))
          for m, a, v, s in zip(m_k, ars_, vs, s_olds)]
    u = [_dot(x.astype(RW_MM), bd(g[:ln].astype(RW_MM))) for x, g in zip(x_w, gy)]
    y = [g[ln:] + _dot(rb, bd(uu.astype(RW_MM))) for g, rb, uu in zip(gy, m_rb, u)]
    upd = [_dot_ta(jnp.concatenate([uu.astype(RW_MM), v], axis=0), x[3]) for uu, v, x in zip(u, vs, sc)]
    s_new = [s * jnp.exp(gt) + jnp.where(same_head, up, 0.0) for s, gt, up in zip(s_olds, gtot, upd)]
    return list(zip(y, s_new))


def _rwkv_scan_kernel(shf_ref, shb_ref, drf_ref, drb_ref, lwf_ref, lwb_ref, yf_ref, yb_ref, s_s):
    @pl.when(pl.program_id(1) == 0)
    def _():
        s_s[...] = jnp.zeros_like(s_s)

    chains, where = [], []
    for bb in range(RW_BATCH):
        for p in range(RW_PAIRS):
            chains.append((shf_ref[bb, 0, p], shf_ref[bb, 1, p], shf_ref[bb, 2, p], drf_ref[bb, 0, p],
                           drf_ref[bb, 1, p], lwf_ref[bb, p], s_s[bb, 0, p], False))
            where.append((yf_ref, bb, 0, p))
            chains.append((shb_ref[bb, 0, p], shb_ref[bb, 1, p], shb_ref[bb, 2, p], drb_ref[bb, 0, p],
                           drb_ref[bb, 1, p], lwb_ref[bb, p], s_s[bb, 1, p], True))
            where.append((yb_ref, bb, 1, p))
    for (y, s_new), (y_ref, bb, d, p) in zip(_rwkv_chunk_group(chains), where):
        y_ref[bb, p] = y.astype(y_ref.dtype)
        s_s[bb, d, p] = s_new


def _rwkv_scan_call(sh, dr, lw, lc):
    bsz, _, _, t, _ = sh.shape
    ln = RW_CHUNK
    nb = RW_BATCH
    assert bsz % nb == 0
    n = t // ln
    nctx = lc // ln
    rv = lambda i: _rev_chunk(i, nctx, n)
    return pl.pallas_call(
        _rwkv_scan_kernel,
        grid=(bsz // nb, n),
        in_specs=[pl.BlockSpec((nb, 3, RW_PAIRS, ln, LANES), lambda bi, i: (bi, 0, 0, i, 0)),
                  pl.BlockSpec((nb, 3, RW_PAIRS, ln, LANES), lambda bi, i: (bi, 0, 0, rv(i), 0)),
                  pl.BlockSpec((nb, None, 2, RW_PAIRS, ln, LANES), lambda bi, i: (bi, 0, 0, 0, i, 0)),
                  pl.BlockSpec((nb, None, 2, RW_PAIRS, ln, LANES), lambda bi, i: (bi, 1, 0, 0, rv(i), 0)),
                  pl.BlockSpec((nb, None, RW_PAIRS, ln, LANES), lambda bi, i: (bi, 0, 0, i, 0)),
                  pl.BlockSpec((nb, None, RW_PAIRS, ln, LANES), lambda bi, i: (bi, 1, 0, rv(i), 0))],
        out_specs=[pl.BlockSpec((nb, RW_PAIRS, ln, LANES), lambda bi, i: (bi, 0, i, 0)),
                   pl.BlockSpec((nb, RW_PAIRS, ln, LANES), lambda bi, i: (bi, 0, rv(i), 0))],
        out_shape=[jax.ShapeDtypeStruct((bsz, RW_PAIRS, t, LANES), BF16)] * 2,
        scratch_shapes=[pltpu.VMEM((nb, 2, RW_PAIRS, LANES, LANES), F32)],
        compiler_params=_params(("parallel", "arbitrary")),
        name="rwkv_scan",
    )(sh, sh, dr, dr, lw, lw)


def _rwkv_out_kernel(z_ref, yf_ref, yb_ref, gate_ref, bonus_ref, lng_ref, lnb_ref, e_ref, et_ref, w_ref, mod_ref,
                     g2_ref, w1_ref, w2_ref, o_ref):
    y = jnp.concatenate([yf_ref[p].astype(F32) + yb_ref[p].astype(F32) for p in range(RW_PAIRS)], axis=1)
    yc = y - _head_sum(y, e_ref, et_ref, 2) * (1.0 / RW_HEAD)
    var = _head_sum(yc * yc, e_ref, et_ref, 1) * (1.0 / RW_HEAD)
    yn = yc * lax.rsqrt(var + RW_LN_EPS) * lng_ref[...] + lnb_ref[...] + bonus_ref[...]
    out = _dot((yn * gate_ref[...]).astype(BF16), w_ref[...])
    o_ref[...] = _mlp_tail(z_ref[...] + mod_ref[2:3, :] * out, g2_ref[...], mod_ref, w1_ref, w2_ref)


def _rwkv_out_call(z, yf, yb, gate, bonus, ln_g, ln_b, e, et, w, mod, g2, w1, w2, layer, nb, lc, skip):
    bsz, t, d = z.shape
    tm = TOKEN_TILE
    nctx = lc // tm
    nt = t // tm - skip
    tok = pl.BlockSpec((None, tm, d), lambda bi, i: (bi, i + skip, 0))
    pair = pl.BlockSpec((None, RW_PAIRS, tm, LANES), lambda bi, i: (bi, 0, i + skip, 0))
    return pl.pallas_call(
        _rwkv_out_kernel,
        grid=(bsz, nt),
        in_specs=[tok, pair, pair, tok, tok,
                  _const_spec((1, d)), _const_spec((1, d)), _const_spec((d, LANES)), _const_spec((LANES, d)),
                  _const_spec((d, d)),
                  pl.BlockSpec((None, N_MOD, d), lambda bi, i: (jnp.where(i + skip < nctx, nb, bi), 0, 0)),
                  _const_spec((1, d)),
                  _layer_spec((d, D_FF), layer),
                  _layer_spec((D_FF, d), layer)],
        out_specs=pl.BlockSpec((None, tm, d), lambda bi, i: (bi, i, 0)),
        out_shape=jax.ShapeDtypeStruct((bsz, nt * tm, d), F32),
        compiler_params=_params(("parallel", "parallel")),
        name="rwkv_out_mlp",
    )(z, yf, yb, gate, bonus, ln_g, ln_b, e, et, w, mod, g2, w1, w2)


def _group_mean_matrix(n, width):
    idx = jnp.arange(n) // width
    return ((idx[:, None] == idx[None, :]).astype(F32) / width).astype(BF16)


def _rope_tables(lc, s):
    quarter = AT_DH // 4
    inv = ROPE_BASE ** (-jnp.arange(quarter, dtype=F32) / quarter)
    pos = jnp.arange(s)
    rpos = (pos // GRID_W).astype(F32)
    cpos = (pos % GRID_W).astype(F32)
    ang_r = rpos[:, None] * inv[None, :]
    ang_c = cpos[:, None] * inv[None, :]
    cos64 = jnp.concatenate([jnp.cos(ang_r), jnp.cos(ang_r), jnp.cos(ang_c), jnp.cos(ang_c)], axis=1)
    sin64 = jnp.concatenate([-jnp.sin(ang_r), jnp.sin(ang_r), -jnp.sin(ang_c), jnp.sin(ang_c)], axis=1)
    cos = jnp.concatenate([jnp.ones((lc, AT_DH), F32), cos64], axis=0)
    sin = jnp.concatenate([jnp.zeros((lc, AT_DH), F32), sin64], axis=0)
    return jnp.tile(cos, (1, 2)), jnp.tile(sin, (1, 2))


def _even_weights(w_in, b_in):
    def cols(m):
        sizes = (ML_W, ML_W, ML_W, ML_W, ML_GATE_COLS, AT_W, AT_KV_HEADS * AT_DH, AT_KV_HEADS * AT_DH)
        bounds = [sum(sizes[:i + 1]) for i in range(len(sizes) - 1)]
        mq, mk, mv, mo, mg, aq, ak, av = jnp.split(m, bounds, axis=-1)
        dup = lambda u: jnp.concatenate([u[..., :AT_DH], u[..., :AT_DH], u[..., AT_DH:], u[..., AT_DH:]], axis=-1)
        mgp = jnp.pad(mg, [(0, 0)] * (m.ndim - 1) + [(0, LANES - ML_GATE_COLS)])
        return jnp.concatenate([mq, mk, mv, mo, aq, dup(ak), dup(av), mgp], axis=-1)
    return cols(w_in).astype(BF16), cols(b_in[None, :])


def kernel(x, c, ctx, c_ctx, ada_w, ada_b, norm1_g, norm2_g, mlp_w1, mlp_w2, ev_w_in, ev_b_in, ev_w_out, ml_f_bias, ml_out_g, at_q_g, at_k_g, at_sink, rw_mu, rw_w_rkv, rw_w0, rw_w1, rw_w2, rw_a0, rw_a1, rw_a2, rw_g1, rw_g2, rw_k_k, rw_k_a, rw_r_k, rw_ln_g, rw_ln_b, rw_w_out):
    bsz, s, d = x.shape
    lc = ctx.shape[1]
    depth = ada_w.shape[0]
    assert d == D_MODEL and lc == TOKEN_TILE and s % TOKEN_TILE == 0 and bsz < MOD_ROWS
    z = (ctx, x)

    cstack = jnp.zeros((MOD_ROWS, d), F32).at[:bsz].set(c).at[bsz].set(c_ctx)
    mod_all = _ada_call(cstack, ada_w, ada_b).reshape(depth, MOD_ROWS, N_MOD, d)

    grp512 = _group_mean_matrix(AT_W, AT_DH)
    head_e = (jnp.arange(d)[:, None] // RW_HEAD == jnp.arange(LANES)[None, :]).astype(BF16)
    head_et = head_e.T
    cos_t, sin_t = _rope_tables(lc, s)
    gate_is_f = ((jnp.arange(LANES) % 8 >= 4) & (jnp.arange(LANES) < 16)).astype(F32)[None, :]

    mlp_w1_b = mlp_w1.astype(BF16)
    mlp_w2_b = mlp_w2.astype(BF16)
    for layer in range(depth):
        mod = mod_all[layer]
        j = layer // 2
        n1 = norm1_g[layer][None, :]
        mlp_args = (norm2_g[layer][None, :], mlp_w1_b, mlp_w2_b, layer)
        skip = lc // TOKEN_TILE if layer == depth - 1 else 0
        if layer % 2 == 0:
            w_in, b_in = _even_weights(ev_w_in[j], ev_b_in[j])
            fb = jnp.zeros((LANES,), F32)
            for dr in range(2):
                fb = fb.at[8 * dr + 4:8 * dr + 8].set(ml_f_bias[j, dr])
            outs = _even_in_call(
                z, n1, mod, w_in, b_in, jnp.tile(at_q_g[j], AT_HEADS)[None, :],
                jnp.tile(at_k_g[j], 2 * AT_KV_HEADS)[None, :], grp512, cos_t, sin_t, fb[None, :], gate_is_f, bsz)
            if isinstance(z, tuple):
                z = outs[-1]
            mqv, kt, mo, aq, akt, av, gates, gates_t = outs[:8]
            hf, hb = _mlstm_call(mqv, kt, gates, gates_t, lc)
            sink = jnp.broadcast_to(at_sink[j][:, None], (AT_HEADS, LANES))
            ya = _attn_call(aq, akt, av, sink, lc)
            z = _even_out_call(z, hf, hb, mo, ya, ml_out_g[j][None, :], ev_w_out[j].astype(BF16), mod, *mlp_args,
                               bsz, skip)
        else:
            pad_g = RW_GATE_PAD - RW_GATE_LORA
            zeros = jnp.zeros((RW_DECAY_LORA, d), F32)
            w2 = jnp.concatenate([jnp.concatenate([rw_w2[j, 0], zeros], axis=1),
                                  jnp.concatenate([zeros, rw_w2[j, 1]], axis=1)], axis=0)
            a2 = jnp.concatenate([jnp.concatenate([rw_a2[j, 0], zeros], axis=1),
                                  jnp.concatenate([zeros, rw_a2[j, 1]], axis=1)], axis=0)
            sh, dr_, lw, gate, bonus = _rwkv_in_call(
                z, n1, mod, rw_mu[j].reshape(12, d), rw_w_rkv[j].astype(BF16),
                jnp.concatenate([rw_w1[j, 0], rw_w1[j, 1]], axis=1).astype(BF16),
                jnp.concatenate([rw_a1[j, 0], rw_a1[j, 1]], axis=1).astype(BF16),
                jnp.pad(rw_g1[j], ((0, 0), (0, pad_g))).astype(BF16),
                w2.astype(BF16), a2.astype(BF16),
                jnp.pad(rw_g2[j], ((0, pad_g), (0, 0))).astype(BF16),
                rw_w0[j], rw_a0[j], rw_k_k[j][None, :], rw_k_a[j][None, :], rw_r_k[j].reshape(1, d),
                head_e, head_et, bsz, lc)
            yf, yb = _rwkv_scan_call(sh, dr_, lw, lc)
            z = _rwkv_out_call(z, yf, yb, gate, bonus, rw_ln_g[j][None, :], rw_ln_b[j][None, :], head_e, head_et,
                               rw_w_out[j].astype(BF16), mod, *mlp_args, bsz, lc, skip)
    return z
```

```python
import functools

import jax
import jax.numpy as jnp
from jax import lax
from jax.experimental import pallas as pl
from jax.experimental.pallas import tpu as pltpu

F32 = jnp.float32
BF16 = jnp.bfloat16

D_MODEL = 1024
N_MOD = 6
D_FF = 4 * D_MODEL
EPS = 1e-6
GRID_W = 64

ML_HEADS = 4
ML_D = 128
ML_W = ML_HEADS * ML_D
ML_CHUNK = 128
GATE_CAP = 15.0

AT_DH = 64
AT_HEADS = 8
AT_KV_HEADS = 2
AT_W = AT_HEADS * AT_DH
AT_BLOCK = 128
AT_KV_DUP = 2 * AT_KV_HEADS * AT_DH
AT_BATCH = 1
ROPE_BASE = 10000.0

RW_HEAD = 64
RW_PAIRS = D_MODEL // 128
RW_CHUNK = 64
RW_LN_EPS = 64e-5
RW_DECAY_LORA = 64
RW_AAA_LORA = 64
RW_GATE_LORA = 160
RW_GATE_PAD = 256
RW_DECAY_SCALE = 0.6065306597126334

LANES = 128
TOKEN_TILE = 256
FF_TILE = 1024
MOD_ROWS = 16
VMEM_LIMIT = 56 * 1024 * 1024
NEG = -1e30
RW_MM = BF16
RW_BATCH = 4
RW_ROW_TILE = 8

EV_MO = 3 * ML_W
EV_AQ = EV_MO + ML_W
EV_AK = EV_AQ + AT_W
AT_KV_W = AT_KV_HEADS * AT_DH
EV_AV = EV_AK + AT_KV_W
EV_GATE = EV_AV + AT_KV_W
EV_COLS = EV_GATE + LANES
ML_GATE_COLS = 4 * ML_HEADS


def _dot(a, b):
    return jnp.dot(a, b, preferred_element_type=F32)


def _dot_tb(a, b):
    return lax.dot_general(a, b, (((1,), (1,)), ((), ())), preferred_element_type=F32)


def _dot_ta(a, b):
    return lax.dot_general(a, b, (((0,), (0,)), ((), ())), preferred_element_type=F32)


def _norm_mod(z, g, shift, scale):
    ms = jnp.mean(z * z, axis=-1, keepdims=True)
    return (z * lax.rsqrt(ms + EPS) * g) * (1.0 + scale) + shift


def _sigmoid(x):
    return 1.0 / (1.0 + jnp.exp(-x))


def _head_sum(x, e_ref, et_ref, terms):
    s = _dot(x.astype(BF16), e_ref[...])
    hi = s.astype(BF16)
    if terms == 1:
        return _dot(hi, et_ref[...])
    lo = (s - hi.astype(F32)).astype(BF16)
    return _dot(jnp.concatenate([hi, lo], axis=1), jnp.concatenate([et_ref[...], et_ref[...]], axis=0))


def _layer_spec(shape, layer):
    nd = len(shape)
    return pl.BlockSpec((None,) + tuple(shape), lambda *_: (layer,) + (0,) * nd, pipeline_mode=pl.Buffered(1))


def _const_spec(shape):
    nd = len(shape)
    return pl.BlockSpec(shape, lambda *_: (0,) * nd, pipeline_mode=pl.Buffered(1))


def _params(sem):
    return pltpu.CompilerParams(dimension_semantics=sem, vmem_limit_bytes=VMEM_LIMIT)


def _ada_kernel(c_ref, w_ref, b_ref, o_ref):
    cv = c_ref[...]
    s = cv * _sigmoid(cv)
    o_ref[...] = jnp.dot(s, w_ref[...], preferred_element_type=F32,
                         precision=lax.Precision.HIGHEST) + b_ref[...]


def _ada_call(cstack, ada_w, ada_b):
    depth = ada_w.shape[0]
    n = N_MOD * D_MODEL
    tn = 1024
    return pl.pallas_call(
        _ada_kernel,
        grid=(depth, n // tn),
        in_specs=[pl.BlockSpec((MOD_ROWS, D_MODEL), lambda l, j: (0, 0)),
                  pl.BlockSpec((None, D_MODEL, tn), lambda l, j: (l, 0, j)),
                  pl.BlockSpec((None, 1, tn), lambda l, j: (l, 0, j))],
        out_specs=pl.BlockSpec((None, MOD_ROWS, tn), lambda l, j: (l, 0, j)),
        out_shape=jax.ShapeDtypeStruct((depth, MOD_ROWS, n), F32),
        compiler_params=_params(("parallel", "parallel")),
        name="ada_ln",
    )(cstack, ada_w, ada_b.reshape(depth, 1, n))


def _rope(x, cos, sin, lane_lo):
    n = x.shape[1]
    up = pltpu.roll(x, n - 16, 1)
    dn = pltpu.roll(x, 16, 1)
    reps = n // LANES
    c = jnp.concatenate([cos] * reps, axis=1) if reps > 1 else cos
    s = jnp.concatenate([sin] * reps, axis=1) if reps > 1 else sin
    lo = jnp.concatenate([lane_lo] * reps, axis=1) if reps > 1 else lane_lo
    return x * c + jnp.where(lo, up, dn) * s


def _even_in_kernel(*refs, assemble):
    if assemble:
        ctx_ref, x_ref, *refs, z_out_ref = refs
        z = jnp.where(pl.program_id(1) == 0, ctx_ref[...], x_ref[...])
        z_out_ref[...] = z
    else:
        z_ref, *refs = refs
        z = z_ref[...]
    (g_ref, mod_ref, w_ref, b_ref, gq_ref, gk_ref, grp_ref, cos_ref, sin_ref, fb_ref, isf_ref,
     mqv_ref, kt_ref, mo_ref, aq_ref, akt_ref, av_ref, gate_ref, gatet_ref) = refs
    h = _norm_mod(z, g_ref[...], mod_ref[0:1, :], mod_ref[1:2, :])
    p = _dot(h.astype(BF16), w_ref[...]) + b_ref[...]
    mqv_ref[:, 0:ML_W] = p[:, 0:ML_W].astype(BF16)
    mqv_ref[:, ML_W:2 * ML_W] = p[:, 2 * ML_W:3 * ML_W].astype(BF16)
    kt_ref[...] = (p[:, ML_W:2 * ML_W] * (ML_D ** -0.5)).T.astype(BF16)
    mo_ref[...] = p[:, EV_MO:EV_MO + ML_W].astype(mo_ref.dtype)
    cos = cos_ref[...]
    sin = sin_ref[...]
    lane_lo = (lax.broadcasted_iota(jnp.int32, (1, LANES), 1) % 32) < 16
    q = p[:, EV_AQ:EV_AQ + AT_W]
    qms = _dot((q * q).astype(BF16), grp_ref[...])
    q = q * lax.rsqrt(qms + EPS) * gq_ref[...]
    aq_ref[...] = (_rope(q, cos, sin, lane_lo) * (AT_DH ** -0.5)).astype(BF16)
    k = p[:, EV_AK:EV_AK + AT_KV_W]
    kms = _dot((k * k).astype(BF16), grp_ref[0:AT_KV_W, 0:AT_KV_W])
    k = k * lax.rsqrt(kms + EPS) * gk_ref[...]
    kt = _rope(k, cos, sin, lane_lo).T
    akt_ref[...] = jnp.concatenate([kt[0:AT_DH], kt[0:AT_DH], kt[AT_DH:], kt[AT_DH:]], axis=0).astype(BF16)
    v = p[:, EV_AV:EV_AV + AT_KV_W]
    v_sw = pltpu.roll(v, AT_DH, 1)
    first = lax.broadcasted_iota(jnp.int32, (1, LANES), 1) < AT_DH
    av_ref[:, 0:LANES] = jnp.where(first, v, v_sw).astype(BF16)
    av_ref[:, LANES:2 * LANES] = jnp.where(first, v_sw, v).astype(BF16)
    gt = p[:, EV_GATE:EV_GATE + LANES] + fb_ref[...]
    sc = GATE_CAP * jnp.tanh(gt * (1.0 / GATE_CAP))
    logsig = jnp.minimum(sc, 0.0) - jnp.log(1.0 + jnp.exp(-jnp.abs(sc)))
    gates = jnp.where(isf_ref[...] > 0.5, logsig, sc)
    gate_ref[...] = gates
    gatet_ref[...] = gates.T[0:ML_GATE_COLS, :]


def _even_in_call(z, g, mod, w, b, gq, gk, grp, cos, sin, fb, isf, nb):
    tm = TOKEN_TILE
    tok = lambda width: pl.BlockSpec((None, tm, width), lambda bi, i: (bi, i, 0))
    tpose = lambda rows: pl.BlockSpec((None, rows, tm), lambda bi, i: (bi, 0, i))
    assemble = isinstance(z, tuple)
    if assemble:
        ctx, x = z
        bsz, s, d = x.shape
        t = ctx.shape[1] + s
        z_args = (ctx, x)
        z_specs = [pl.BlockSpec((None, tm, d), lambda bi, i: (bi, 0, 0)),
                   pl.BlockSpec((None, tm, d), lambda bi, i: (bi, jnp.maximum(i - 1, 0), 0))]
        extra_specs, extra_shapes = [tok(d)], [jax.ShapeDtypeStruct((bsz, t, d), F32)]
    else:
        bsz, t, d = z.shape
        z_args, z_specs, extra_specs, extra_shapes = (z,), [tok(d)], [], []
    return pl.pallas_call(
        functools.partial(_even_in_kernel, assemble=assemble),
        grid=(bsz, t // tm),
        in_specs=z_specs + [
                  _const_spec((1, d)),
                  pl.BlockSpec((None, N_MOD, d), lambda bi, i: (jnp.where(i == 0, nb, bi), 0, 0)),
                  _const_spec((d, EV_COLS)),
                  _const_spec((1, EV_COLS)),
                  _const_spec((1, AT_W)),
                  _const_spec((1, AT_KV_W)),
                  _const_spec((AT_W, AT_W)),
                  pl.BlockSpec((tm, LANES), lambda bi, i: (i, 0)),
                  pl.BlockSpec((tm, LANES), lambda bi, i: (i, 0)),
                  _const_spec((1, LANES)),
                  _const_spec((1, LANES))],
        out_specs=[tok(2 * ML_W), tpose(ML_W), tok(ML_W), tok(AT_W), tpose(AT_KV_DUP), tok(AT_KV_DUP), tok(LANES),
                   tpose(ML_GATE_COLS)] + extra_specs,
        out_shape=[jax.ShapeDtypeStruct((bsz, t, 2 * ML_W), BF16),
                   jax.ShapeDtypeStruct((bsz, ML_W, t), BF16),
                   jax.ShapeDtypeStruct((bsz, t, ML_W), BF16),
                   jax.ShapeDtypeStruct((bsz, t, AT_W), BF16),
                   jax.ShapeDtypeStruct((bsz, AT_KV_DUP, t), BF16),
                   jax.ShapeDtypeStruct((bsz, t, AT_KV_DUP), BF16),
                   jax.ShapeDtypeStruct((bsz, t, LANES), F32),
                   jax.ShapeDtypeStruct((bsz, ML_GATE_COLS, t), F32)] + extra_shapes,
        compiler_params=_params(("parallel", "parallel")),
        name="even_in",
    )(*z_args, g, mod, w, b, gq, gk, grp, cos, sin, fb, isf)


def _split3(x):
    hi = x.astype(BF16)
    r1 = x - hi.astype(F32)
    mid = r1.astype(BF16)
    return hi, mid, (r1 - mid.astype(F32)).astype(BF16)


def _mlstm_kernel(qf_ref, ktf_ref, vf_ref, qb_ref, ktb_ref, vb_ref, gcf_ref, gcb_ref, grf_ref, grb_ref, sel_ref,
                  hf_ref, hb_ref, c_s, m_s):
    ln = ML_CHUNK

    @pl.when(pl.program_id(1) == 0)
    def _():
        c_s[...] = jnp.zeros_like(c_s)
        m_s[...] = jnp.zeros_like(m_s)

    ri = lax.broadcasted_iota(jnp.int32, (ln, ln), 0)
    ci = lax.broadcasted_iota(jnp.int32, (ln, ln), 1)
    before = (ci <= ri, ci >= ri)
    refs = ((qf_ref, ktf_ref, vf_ref, gcf_ref, grf_ref, hf_ref), (qb_ref, ktb_ref, vb_ref, gcb_ref, grb_ref, hb_ref))
    cum_r = [sum(_dot(part, before[1 - d].astype(BF16)) for part in _split3(refs[d][4][...])) for d in range(2)]
    bc_all = []
    for d in range(2):
        cum_c = sum(_dot(before[d].astype(BF16), part) for part in _split3(refs[d][3][...]))
        sel = sel_ref[d]
        bc_all.append(sum(_dot(part, sel) for part in _split3(cum_c)))
    chains = [(d, hd) for d in range(2) for hd in range(ML_HEADS)]
    lanes = [slice(hd * ML_D, (hd + 1) * ML_D) for _, hd in chains]
    st = [d * ML_HEADS + hd for d, hd in chains]
    ones = jnp.ones((ln, ML_D), BF16)
    q = [refs[d][0][:, ls] for (d, _), ls in zip(chains, lanes)]
    kt = [refs[d][1][ls, :] for (d, _), ls in zip(chains, lanes)]
    v1 = [jnp.concatenate([refs[d][2][:, ls], ones], axis=1) for (d, _), ls in zip(chains, lanes)]
    i_row = [refs[d][4][8 * d + hd:8 * d + hd + 1, :] for d, hd in chains]
    f_row = [refs[d][4][8 * d + 4 + hd:8 * d + 5 + hd, :] for d, hd in chains]
    bcum_row = [cum_r[d][8 * d + 4 + hd:8 * d + 5 + hd, :] for d, hd in chains]
    bcum = [bc_all[d][:, ls] for (d, _), ls in zip(chains, lanes)]
    m_old = [m_s[s:s + 1, 0:1] for s in st]
    c_old = [c_s[s] for s in st]
    qk = [_dot(a, b) for a, b in zip(q, kt)]
    dlog =[jnp.where(before[d], bc - br + ir, NEG) for (d, _), bc, br, ir in zip(chains, bcum, bcum_row, i_row)]
    inter = [bc + m for bc, m in zip(bcum, m_old)]
    mt = [jnp.maximum(jnp.max(dl, axis=1, keepdims=True), it) for dl, it in zip(dlog, inter)]
    sc = [x * jnp.exp(dl - m) for x, dl, m in zip(qk, dlog, mt)]
    iw = [jnp.exp(it - m) for it, m in zip(inter, mt)]
    tots = [_dot(jnp.concatenate([s.astype(BF16), (w * a.astype(F32)).astype(BF16)], axis=1),
                 jnp.concatenate([b, c.astype(BF16)], axis=0)) for s, w, a, b, c in zip(sc, iw, q, v1, c_old)]
    for j, (d, _) in enumerate(chains):
        tot = tots[j]
        den =jnp.maximum(jnp.abs(tot[:, ML_D:]), jnp.exp(-mt[j]))
        refs[d][5][:, lanes[j]] = (tot[:, :ML_D] / den).astype(refs[d][5].dtype)
    bl = [jnp.sum(fr, axis=1, keepdims=True) for fr in f_row]
    ws_log = [b - br + ir for b, br, ir in zip(bl, bcum_row, i_row)]
    m_new = [jnp.maximum(b + m, jnp.max(w, axis=1, keepdims=True)) for b, m, w in zip(bl, m_old, ws_log)]
    ws = [jnp.exp(w - m) for w, m in zip(ws_log, m_new)]
    upd = [_dot((a.astype(F32) * w).astype(BF16), b) for a, w, b in zip(kt, ws, v1)]
    for j, s in enumerate(st):
        c_s[s] = jnp.exp(bl[j] + m_old[j] - m_new[j]) * c_old[j] + upd[j]
        m_s[s:s + 1, :] = jnp.broadcast_to(m_new[j], (1, LANES))


def _rev_chunk(i, nctx, n):
    return jnp.where(i < nctx, nctx - 1 - i, n + nctx - 1 - i)


def _mlstm_call(mqv, kt, gates, gates_t, lc):
    bsz, t, _ = mqv.shape
    ln = ML_CHUNK
    assert ln == ML_D == LANES
    n = t // ln
    nctx = lc // ln
    rv = lambda i: _rev_chunk(i, nctx, n)
    col = jnp.arange(LANES)[None, :, None]
    want = (8 * jnp.arange(2)[:, None, None] + 4 + jnp.arange(ML_HEADS * ln)[None, None, :] // ln)
    sel = (col == want).astype(BF16)
    return pl.pallas_call(
        _mlstm_kernel,
        grid=(bsz, n),
        in_specs=[pl.BlockSpec((None, ln, ML_W), lambda bi, i: (bi, i, 0)),
                  pl.BlockSpec((None, ML_W, ln), lambda bi, i: (bi, 0, i)),
                  pl.BlockSpec((None, ln, ML_W), lambda bi, i: (bi, i, 1)),
                  pl.BlockSpec((None, ln, ML_W), lambda bi, i: (bi, rv(i), 0)),
                  pl.BlockSpec((None, ML_W, ln), lambda bi, i: (bi, 0, rv(i))),
                  pl.BlockSpec((None, ln, ML_W), lambda bi, i: (bi, rv(i), 1)),
                  pl.BlockSpec((None, ln, LANES), lambda bi, i: (bi, i, 0)),
                  pl.BlockSpec((None, ln, LANES), lambda bi, i: (bi, rv(i), 0)),
                  pl.BlockSpec((None, ML_GATE_COLS, ln), lambda bi, i: (bi, 0, i)),
                  pl.BlockSpec((None, ML_GATE_COLS, ln), lambda bi, i: (bi, 0, rv(i))),
                  _const_spec((2, LANES, ML_HEADS * ln))],
        out_specs=[pl.BlockSpec((None, ln, ML_W), lambda bi, i: (bi, i, 0)),
                   pl.BlockSpec((None, ln, ML_W), lambda bi, i: (bi, rv(i), 0))],
        out_shape=[jax.ShapeDtypeStruct((bsz, t, ML_W), BF16)] * 2,
        scratch_shapes=[pltpu.VMEM((2 * ML_HEADS, ML_D, 2 * ML_D), F32),
                        pltpu.VMEM((2 * ML_HEADS, LANES), F32)],
        compiler_params=_params(("parallel", "arbitrary")),
        name="mlstm_scan",
    )(mqv, kt, mqv, mqv, kt, mqv, gates, gates, gates_t, gates_t, sel)


def _attn_kernel(q_ref, ktp_ref, kto_ref, ktn_ref, ktc_ref, vp_ref, vo_ref, vn_ref, vc_ref, sink_ref, o_ref,
                 *, nctx, n):
    j = pl.program_id(1)
    blk = AT_BLOCK
    nk = 3 * blk + ktc_ref.shape[2]
    qi = lax.broadcasted_iota(jnp.int32, (blk, nk), 0)
    ki = lax.broadcasted_iota(jnp.int32, (blk, nk), 1)
    lo = jnp.where(j > nctx, qi, blk)
    hi = jnp.where(j < n - 1, qi + 2 * blk, 2 * blk - 1)
    band = jnp.where(ki < lo, NEG, jnp.where(ki > hi, NEG, jnp.where(j >= nctx, 0.0, NEG)))
    bias = jnp.where(ki >= 3 * blk, 0.0, band)
    lane = lax.broadcasted_iota(jnp.int32, (1, LANES), 1)
    sub = lax.broadcasted_iota(jnp.int32, (LANES, 1), 0)
    lane_half = (lane < AT_DH, lane >= AT_DH)
    sub_half = (sub < AT_DH, sub >= AT_DH)
    kts = (ktp_ref, kto_ref, ktn_ref, ktc_ref)
    vs = (vp_ref, vo_ref, vn_ref, vc_ref)
    zero = jnp.zeros((), BF16)
    nbat = q_ref.shape[0]
    kx, vx = {}, {}
    for bb in range(nbat):
        for g in range(AT_KV_HEADS):
            rows = slice(g * LANES, (g + 1) * LANES)
            for e in range(2):
                kx[bb, g, e] = jnp.concatenate([jnp.where(sub_half[e], r[bb, rows, :], zero) for r in kts], axis=1)
                vx[bb, g, e] = jnp.concatenate(
                    [jnp.concatenate([jnp.where(lane_half[e], r[bb, :, rows], zero),
                                      jnp.ones((r.shape[1], LANES), BF16)], axis=1) for r in vs], axis=0)
    heads = [(bb, hd // 2, hd % 2, hd // 4) for bb in range(nbat) for hd in range(AT_HEADS)]
    qc = {(bb, c): q_ref[bb, :, c * LANES:(c + 1) * LANES] for bb in range(nbat) for c in range(AT_HEADS // 2)}
    sink = [sink_ref[2 * c + e:2 * c + e + 1, 0:1] for _, c, e, _ in heads]
    ss = [_dot(qc[bb, c], kx[bb, g, e]) + bias for bb, c, e, g in heads]
    m = [jnp.maximum(jnp.max(s, axis=1, keepdims=True), sk) for s, sk in zip(ss, sink)]
    res = [_dot(jnp.exp(s - mm).astype(BF16), vx[bb, g, e]) for s, mm, (bb, c, e, g) in zip(ss, m, heads)]
    outs =[r[:, :LANES] / (r[:, LANES:] + jnp.exp(sk - mm)) for r, sk, mm in zip(res, sink, m)]
    for i in range(0, len(heads), 2):
        bb, c = heads[i][0], heads[i][1]
        o_ref[bb, :, c * LANES:(c + 1) * LANES] = (outs[i] + outs[i + 1]).astype(o_ref.dtype)


def _attn_call(aq, akt, av, sink, lc):
    bsz, t, _ = aq.shape
    blk = AT_BLOCK
    nb = AT_BATCH
    assert bsz % nb == 0
    n = t // blk
    nctx = lc // blk
    prev = lambda j: jnp.clip(j - 1, nctx, n - 1)
    own = lambda j: jnp.clip(j, nctx, n - 1)
    nxt = lambda j: jnp.clip(j + 1, nctx, n - 1)
    kt = lambda f: pl.BlockSpec((nb, AT_KV_DUP, blk), lambda bi, j: (bi, 0, f(j)))
    vv = lambda f: pl.BlockSpec((nb, blk, AT_KV_DUP), lambda bi, j: (bi, f(j), 0))
    return pl.pallas_call(
        functools.partial(_attn_kernel, nctx=nctx, n=n),
        grid=(bsz // nb, n),
        in_specs=[pl.BlockSpec((nb, blk, AT_W), lambda bi, j: (bi, j, 0)),
                  kt(prev), kt(own), kt(nxt),
                  pl.BlockSpec((nb, AT_KV_DUP, lc), lambda bi, j: (bi, 0, 0)),
                  vv(prev), vv(own), vv(nxt),
                  pl.BlockSpec((nb, lc, AT_KV_DUP), lambda bi, j: (bi, 0, 0)),
                  _const_spec((AT_HEADS, LANES))],
        out_specs=pl.BlockSpec((nb, blk, AT_W), lambda bi, j: (bi, j, 0)),
        out_shape=jax.ShapeDtypeStruct((bsz, t, AT_W), BF16),
        compiler_params=_params(("parallel", "parallel")),
        name="window_attn",
    )(aq, akt, akt, akt, akt, av, av, av, av, sink)


def _mlp_tail(z, g, mod_ref, w1_ref, w2_ref):
    h = _norm_mod(z, g, mod_ref[3:4, :], mod_ref[4:5, :]).astype(BF16)
    acc = jnp.zeros(z.shape, F32)
    for f in range(D_FF // FF_TILE):
        a = jnp.maximum(_dot(h, w1_ref[:, f * FF_TILE:(f + 1) * FF_TILE]), 0.0)
        acc = acc + _dot((a * a).astype(BF16), w2_ref[f * FF_TILE:(f + 1) * FF_TILE, :])
    return z + mod_ref[5:6, :] * acc


def _even_out_kernel(z_ref, hf_ref, hb_ref, mo_ref, ya_ref, og_ref, w_ref, mod_ref, g2_ref, w1_ref, w2_ref, o_ref):
    hs = hf_ref[...].astype(F32) + hb_ref[...].astype(F32)
    parts = []
    for hd in range(ML_HEADS):
        x = hs[:, hd * ML_D:(hd + 1) * ML_D]
        parts.append(x * lax.rsqrt(jnp.mean(x * x, axis=1, keepdims=True) + EPS))
    ym = jnp.concatenate(parts, axis=1) * og_ref[...] * _sigmoid(mo_ref[...].astype(F32))
    y = _dot(ym.astype(BF16), w_ref[0:ML_W, :]) + _dot(ya_ref[...], w_ref[ML_W:ML_W + AT_W, :])
    o_ref[...] = _mlp_tail(z_ref[...] + mod_ref[2:3, :] * y, g2_ref[...], mod_ref, w1_ref, w2_ref)


def _even_out_call(z, hf, hb, mo, ya, og, w, mod, g2, w1, w2, layer, nb, skip):
    bsz, t, d = z.shape
    tm = TOKEN_TILE
    nt = t // tm - skip
    tok = lambda width: pl.BlockSpec((None, tm, width), lambda bi, i: (bi, i + skip, 0))
    return pl.pallas_call(
        _even_out_kernel,
        grid=(bsz, nt),
        in_specs=[tok(d), tok(ML_W), tok(ML_W), tok(ML_W), tok(AT_W),
                  _const_spec((1, ML_W)),
                  _const_spec((ML_W + AT_W, d)),
                  pl.BlockSpec((None, N_MOD, d), lambda bi, i: (jnp.where(i + skip == 0, nb, bi), 0, 0)),
                  _const_spec((1, d)),
                  _layer_spec((d, D_FF), layer),
                  _layer_spec((D_FF, d), layer)],
        out_specs=pl.BlockSpec((None, tm, d), lambda bi, i: (bi, i, 0)),
        out_shape=jax.ShapeDtypeStruct((bsz, nt * tm, d), F32),
        compiler_params=_params(("parallel", "parallel")),
        name="even_out_mlp",
    )(z, hf, hb, mo, ya, og, w, mod, g2, w1, w2)


def _rwkv_in_kernel(z_ref, zp_ref, zn_ref, g_ref, mod_ref, mu_ref, wrkv_ref, w1_ref, a1_ref, g1_ref,
                    w2_ref, a2_ref, g2_ref, w0_ref, a0_ref, kk_ref, ka_ref, rk_ref, e_ref, et_ref,
                    sh_ref, dr_ref, lw_ref, gate_ref, bonus_ref, *, nctx, ntile):
    i = pl.program_id(1)
    tm, d = z_ref.shape
    g = g_ref[...]
    shift = mod_ref[0:1, :]
    scale = mod_ref[1:2, :]
    h = _norm_mod(z_ref[...], g, shift, scale)
    no_prev = jnp.logical_or(i == 0, i == nctx)
    no_next = jnp.logical_or(i == nctx - 1, i == ntile - 1)
    hp = jnp.where(no_prev, 0.0, _norm_mod(zp_ref[7:8, :], g, shift, scale))
    hn = jnp.where(no_next, 0.0, _norm_mod(zn_ref[0:1, :], g, shift, scale))
    row = lax.broadcasted_iota(jnp.int32, (tm, 1), 0)
    dp = jnp.where(row == 0, hp, pltpu.roll(h, 1, 0)) - h
    dn = jnp.where(row == tm - 1, hn, pltpu.roll(h, tm - 1, 0)) - h

    hb, dpb, dnb = h.astype(BF16), dp.astype(BF16), dn.astype(BF16)
    mub = mu_ref[...].astype(BF16)

    def mix(n):
        return hb + mub[2 * n:2 * n + 1, :] * dpb + mub[2 * n + 1:2 * n + 2, :] * dnb

    r = _dot(mix(0), wrkv_ref[0])
    k = _dot(mix(2), wrkv_ref[1])
    v = _dot(mix(3), wrkv_ref[2])
    gate_ref[...] = _dot(_sigmoid(_dot(mix(5), g1_ref[...])).astype(BF16), g2_ref[...]).astype(gate_ref.dtype)
    lora_w = _dot(jnp.tanh(_dot(mix(1), w1_ref[...])).astype(BF16), w2_ref[...])
    lora_a = _dot(_dot(mix(4), a1_ref[...]).astype(BF16), a2_ref[...])
    kkr = k * kk_ref[...]
    ssq = _head_sum(kkr * kkr, e_ref, et_ref, 1)
    kk = kkr * lax.rsqrt(jnp.maximum(ssq, 1e-24))
    kd_sum = None
    for dr in range(2):
        cols = slice(dr * d, (dr + 1) * d)
        lw = -RW_DECAY_SCALE * _sigmoid(w0_ref[dr:dr + 1, :] + lora_w[:, cols])
        a = _sigmoid(a0_ref[dr:dr + 1, :] + lora_a[:, cols])
        kd = k * (1.0 + (a - 1.0) * ka_ref[...])
        bvec = kk * a
        kd_sum = kd if kd_sum is None else kd_sum + kd
        for p in range(RW_PAIRS):
            ls = slice(p * LANES, (p + 1) * LANES)
            lw_ref[dr, p] = lw[:, ls]
            dr_ref[dr, 0, p] = kd[:, ls].astype(BF16)
            dr_ref[dr, 1, p] = bvec[:, ls].astype(BF16)
    for p in range(RW_PAIRS):
        ls = slice(p * LANES, (p + 1) * LANES)
        sh_ref[0, p] = r[:, ls].astype(BF16)
        sh_ref[1, p] = v[:, ls].astype(BF16)
        sh_ref[2, p] = kk[:, ls].astype(BF16)
    bsum = _head_sum(r * kd_sum * rk_ref[...], e_ref, et_ref, 1)
    bonus_ref[...] = (bsum * v).astype(bonus_ref.dtype)


def _rwkv_in_call(z, g, mod, mu, wrkv, w1, a1, g1, w2, a2, g2, w0, a0, k_k, k_a, r_k, e, et, nb, lc):
    bsz, t, d = z.shape
    tm = TOKEN_TILE
    ntile = t // tm
    nctx = lc // tm
    r8 = tm // 8
    tok = pl.BlockSpec((None, tm, d), lambda bi, i: (bi, i, 0))
    return pl.pallas_call(
        functools.partial(_rwkv_in_kernel, nctx=nctx, ntile=ntile),
        grid=(bsz, ntile),
        in_specs=[tok,
                  pl.BlockSpec((None, 8, d), lambda bi, i: (bi, jnp.maximum(i * r8 - 1, 0), 0)),
                  pl.BlockSpec((None, 8, d), lambda bi, i: (bi, jnp.minimum((i + 1) * r8, t // 8 - 1), 0)),
                  _const_spec((1, d)),
                  pl.BlockSpec((None, N_MOD, d), lambda bi, i: (jnp.where(i < nctx, nb, bi), 0, 0)),
                  _const_spec((12, d)),
                  _const_spec((3, d, d)),
                  _const_spec((d, LANES)),
                  _const_spec((d, LANES)),
                  _const_spec((d, RW_GATE_PAD)),
                  _const_spec((LANES, 2 * d)),
                  _const_spec((LANES, 2 * d)),
                  _const_spec((RW_GATE_PAD, d)),
                  _const_spec((2, d)),
                  _const_spec((2, d)),
                  _const_spec((1, d)),
                  _const_spec((1, d)),
                  _const_spec((1, d)),
                  _const_spec((d, LANES)),
                  _const_spec((LANES, d))],
        out_specs=[pl.BlockSpec((None, 3, RW_PAIRS, tm, LANES), lambda bi, i: (bi, 0, 0, i, 0)),
                   pl.BlockSpec((None, 2, 2, RW_PAIRS, tm, LANES), lambda bi, i: (bi, 0, 0, 0, i, 0)),
                   pl.BlockSpec((None, 2, RW_PAIRS, tm, LANES), lambda bi, i: (bi, 0, 0, i, 0)),
                   tok, tok],
        out_shape=[jax.ShapeDtypeStruct((bsz, 3, RW_PAIRS, t, LANES), BF16),
                   jax.ShapeDtypeStruct((bsz, 2, 2, RW_PAIRS, t, LANES), BF16),
                   jax.ShapeDtypeStruct((bsz, 2, RW_PAIRS, t, LANES), F32),
                   jax.ShapeDtypeStruct((bsz, t, d), BF16),
                   jax.ShapeDtypeStruct((bsz, t, d), BF16)],
        compiler_params=_params(("parallel", "parallel")),
        name="rwkv_in",
    )(z, z, z, g, mod, mu, wrkv, w1, a1, g1, w2, a2, g2, w0, a0, k_k, k_a, r_k, e, et)


def _rwkv_chunk_group(chains):
    ln = RW_CHUNK
    row = lax.broadcasted_iota(jnp.int32, (ln, 2 * ln), 0)
    col = lax.broadcasted_iota(jnp.int32, (ln, 2 * ln), 1)
    sidx = jnp.where(col >= ln, col - ln, col)
    lane_a = lax.broadcasted_iota(jnp.int32, (1, LANES), 1) < RW_HEAD
    tr = lax.broadcasted_iota(jnp.int32, (ln, ln), 0)
    tc = lax.broadcasted_iota(jnp.int32, (ln, ln), 1)
    tri = {False: (tc <= tr).astype(F32), True: (tc >= tr).astype(F32)}
    strict = {False: sidx < row, True: sidx > row}
    incl = {False: sidx <= row, True: sidx >= row}
    eye_w = jnp.where(sidx == row, 1.0, 0.0)
    diff = row ^ sidx
    vr = lax.broadcasted_iota(jnp.int32, (LANES, LANES), 0) < RW_HEAD
    kc = lax.broadcasted_iota(jnp.int32, (LANES, LANES), 1) < RW_HEAD
    same_head = vr == kc

    def bd(x):
        zero = jnp.zeros((), x.dtype)
        return jnp.concatenate([jnp.where(lane_a, x, zero), jnp.where(lane_a, zero, x)], axis=0)

    revs = [c[7] for c in chains]
    vs = [c[1] for c in chains]
    s_olds = [c[6] for c in chains]
    gcum = [sum(_dot(tri[c[7]].astype(BF16), part) for part in _split3(c[5])) for c in chains]
    gtot = [jnp.sum(c[5], axis=0, keepdims=True) for c in chains]

    def scaled(c, g, gt):
        r, v, kk, kd, bv, lw = (x.astype(F32) for x in c[:6])
        e_pos = jnp.exp(g)
        e_neg = jnp.exp(-g)
        e_end = jnp.exp(gt - g)
        ar = jnp.concatenate([(-kk * jnp.exp(g - lw)).astype(RW_MM), (r * e_pos).astype(RW_MM)], axis=0)
        bk_end = jnp.concatenate([(bv * e_end).astype(RW_MM), (kd * e_end).astype(RW_MM)], axis=0)
        return ar, (bv * e_neg).astype(RW_MM), (kd * e_neg).astype(RW_MM), bk_end

    sc = [scaled(c, g, gt) for c, g, gt in zip(chains, gcum, gtot)]
    ars_ = [x[0] for x in sc]
    x_bk = [_dot_tb(x[0], jnp.concatenate([bd(x[1]), bd(x[2])], axis=0)) for x in sc]
    x_b = [x[:, :2 * ln] for x in x_bk]
    x_k = [x[:, 2 * ln:] for x in x_bk]
    n_w =[jnp.where(strict[rv], x[:ln], 0.0) for x, rv in zip(x_b, revs)]
    m_rb = [jnp.where(incl[rv], x[ln:], 0.0).astype(RW_MM) for x, rv in zip(x_b, revs)]
    m_k = [jnp.concatenate([jnp.where(strict[rv], x[:ln], 0.0), jnp.where(incl[rv], x[ln:], 0.0)],
                           axis=0).astype(RW_MM) for x, rv in zip(x_k, revs)]
    x_w = [eye_w + jnp.where(diff == 1, n, 0.0) for n in n_w]

    def take_rows(x, h, odd):
        return jnp.concatenate([x[b * h:(b + 1) * h] for b in range(ln // h) if (b % 2 == 1) == odd], axis=0)

    def put_rows(base, upd, h, odd):
        parts, j = [], 0
        for b in range(ln // h):
            if (b % 2 == 1) == odd:
                blk = upd[j * h:(j + 1) * h]
                parts.append(blk if base is None else base[b * h:(b + 1) * h] + blk)
                j += 1
            else:
                parts.append(jnp.zeros((h, upd.shape[1]), upd.dtype) if base is None else base[b * h:(b + 1) * h])
        return jnp.concatenate(parts, axis=0)

    h = 2
    while h < ln:
        lvl = jnp.logical_and(diff >= h, diff < 2 * h)
        n_l = [jnp.where(lvl, n, 0.0) for n in n_w]
        if h < RW_ROW_TILE:
            tmp = [_dot(n.astype(RW_MM), bd(x.astype(RW_MM))) for n, x in zip(n_l, x_w)]
            x_w = [x + _dot(x.astype(RW_MM), bd(t.astype(RW_MM))) for x, t in zip(x_w, tmp)]
        else:
            odd = [not rv for rv in revs]
            tmp = [_dot(take_rows(n, h, o).astype(RW_MM), bd(x.astype(RW_MM))) for n, x, o in zip(n_l, x_w, odd)]
            tmp = [put_rows(None, t, h, o) for t, o in zip(tmp, odd)]
            cor = [_dot(take_rows(x, h, o).astype(RW_MM), bd(t.astype(RW_MM))) for x, t, o in zip(x_w, tmp, odd)]
            x_w = [put_rows(x, c, h, o) for x, c, o in zip(x_w, cor, odd)]
        h *= 2
    gy = [_dot(jnp.concatenate([m, a], axis=1), jnp.concatenate([bd(v), s.T.astype(RW_MM)], axis=0))
          for m, a, v, s in zip(m_k, ars_, vs, s_olds)]
    u = [_dot(x.astype(RW_MM), bd(g[:ln].astype(RW_MM))) for x, g in zip(x_w, gy)]
    y = [g[ln:] + _dot(rb, bd(uu.astype(RW_MM))) for g, rb, uu in zip(gy, m_rb, u)]
    upd = [_dot_ta(jnp.concatenate([uu.astype(RW_MM), v], axis=0), x[3]) for uu, v, x in zip(u, vs, sc)]
    s_new = [s * jnp.exp(gt) + jnp.where(same_head, up, 0.0) for s, gt, up in zip(s_olds, gtot, upd)]
    return list(zip(y, s_new))


def _rwkv_scan_kernel(shf_ref, shb_ref, drf_ref, drb_ref, lwf_ref, lwb_ref, yf_ref, yb_ref, s_s):
    @pl.when(pl.program_id(1) == 0)
    def _():
        s_s[...] = jnp.zeros_like(s_s)

    chains, where = [], []
    for bb in range(RW_BATCH):
        for p in range(RW_PAIRS):
            chains.append((shf_ref[bb, 0, p], shf_ref[bb, 1, p], shf_ref[bb, 2, p], drf_ref[bb, 0, p],
                           drf_ref[bb, 1, p], lwf_ref[bb, p], s_s[bb, 0, p], False))
            where.append((yf_ref, bb, 0, p))
            chains.append((shb_ref[bb, 0, p], shb_ref[bb, 1, p], shb_ref[bb, 2, p], drb_ref[bb, 0, p],
                           drb_ref[bb, 1, p], lwb_ref[bb, p], s_s[bb, 1, p], True))
            where.append((yb_ref, bb, 1, p))
    for (y, s_new), (y_ref, bb, d, p) in zip(_rwkv_chunk_group(chains), where):
        y_ref[bb, p] = y.astype(y_ref.dtype)
        s_s[bb, d, p] = s_new


def _rwkv_scan_call(sh, dr, lw, lc):
    bsz, _, _, t, _ = sh.shape
    ln = RW_CHUNK
    nb = RW_BATCH
    assert bsz % nb == 0
    n = t // ln
    nctx = lc // ln
    rv = lambda i: _rev_chunk(i, nctx, n)
    return pl.pallas_call(
        _rwkv_scan_kernel,
        grid=(bsz // nb, n),
        in_specs=[pl.BlockSpec((nb, 3, RW_PAIRS, ln, LANES), lambda bi, i: (bi, 0, 0, i, 0)),
                  pl.BlockSpec((nb, 3, RW_PAIRS, ln, LANES), lambda bi, i: (bi, 0, 0, rv(i), 0)),
                  pl.BlockSpec((nb, None, 2, RW_PAIRS, ln, LANES), lambda bi, i: (bi, 0, 0, 0, i, 0)),
                  pl.BlockSpec((nb, None, 2, RW_PAIRS, ln, LANES), lambda bi, i: (bi, 1, 0, 0, rv(i), 0)),
                  pl.BlockSpec((nb, None, RW_PAIRS, ln, LANES), lambda bi, i: (bi, 0, 0, i, 0)),
                  pl.BlockSpec((nb, None, RW_PAIRS, ln, LANES), lambda bi, i: (bi, 1, 0, rv(i), 0))],
        out_specs=[pl.BlockSpec((nb, RW_PAIRS, ln, LANES), lambda bi, i: (bi, 0, i, 0)),
                   pl.BlockSpec((nb, RW_PAIRS, ln, LANES), lambda bi, i: (bi, 0, rv(i), 0))],
        out_shape=[jax.ShapeDtypeStruct((bsz, RW_PAIRS, t, LANES), BF16)] * 2,
        scratch_shapes=[pltpu.VMEM((nb, 2, RW_PAIRS, LANES, LANES), F32)],
        compiler_params=_params(("parallel", "arbitrary")),
        name="rwkv_scan",
    )(sh, sh, dr, dr, lw, lw)


def _rwkv_out_kernel(z_ref, yf_ref, yb_ref, gate_ref, bonus_ref, lng_ref, lnb_ref, e_ref, et_ref, w_ref, mod_ref,
                     g2_ref, w1_ref, w2_ref, o_ref):
    y = jnp.concatenate([yf_ref[p].astype(F32) + yb_ref[p].astype(F32) for p in range(RW_PAIRS)], axis=1)
    yc = y - _head_sum(y, e_ref, et_ref, 2) * (1.0 / RW_HEAD)
    var = _head_sum(yc * yc, e_ref, et_ref, 1) * (1.0 / RW_HEAD)
    yn = yc * lax.rsqrt(var + RW_LN_EPS) * lng_ref[...] + lnb_ref[...] + bonus_ref[...]
    out = _dot((yn * gate_ref[...]).astype(BF16), w_ref[...])
    o_ref[...] = _mlp_tail(z_ref[...] + mod_ref[2:3, :] * out, g2_ref[...], mod_ref, w1_ref, w2_ref)


def _rwkv_out_call(z, yf, yb, gate, bonus, ln_g, ln_b, e, et, w, mod, g2, w1, w2, layer, nb, lc, skip):
    bsz, t, d = z.shape
    tm = TOKEN_TILE
    nctx = lc // tm
    nt = t // tm - skip
    tok = pl.BlockSpec((None, tm, d), lambda bi, i: (bi, i + skip, 0))
    pair = pl.BlockSpec((None, RW_PAIRS, tm, LANES), lambda bi, i: (bi, 0, i + skip, 0))
    return pl.pallas_call(
        _rwkv_out_kernel,
        grid=(bsz, nt),
        in_specs=[tok, pair, pair, tok, tok,
                  _const_spec((1, d)), _const_spec((1, d)), _const_spec((d, LANES)), _const_spec((LANES, d)),
                  _const_spec((d, d)),
                  pl.BlockSpec((None, N_MOD, d), lambda bi, i: (jnp.where(i + skip < nctx, nb, bi), 0, 0)),
                  _const_spec((1, d)),
                  _layer_spec((d, D_FF), layer),
                  _layer_spec((D_FF, d), layer)],
        out_specs=pl.BlockSpec((None, tm, d), lambda bi, i: (bi, i, 0)),
        out_shape=jax.ShapeDtypeStruct((bsz, nt * tm, d), F32),
        compiler_params=_params(("parallel", "parallel")),
        name="rwkv_out_mlp",
    )(z, yf, yb, gate, bonus, ln_g, ln_b, e, et, w, mod, g2, w1, w2)


def _group_mean_matrix(n, width):
    idx = jnp.arange(n) // width
    return ((idx[:, None] == idx[None, :]).astype(F32) / width).astype(BF16)


def _rope_tables(lc, s):
    quarter = AT_DH // 4
    inv = ROPE_BASE ** (-jnp.arange(quarter, dtype=F32) / quarter)
    pos = jnp.arange(s)
    rpos = (pos // GRID_W).astype(F32)
    cpos = (pos % GRID_W).astype(F32)
    ang_r = rpos[:, None] * inv[None, :]
    ang_c = cpos[:, None] * inv[None, :]
    cos64 = jnp.concatenate([jnp.cos(ang_r), jnp.cos(ang_r), jnp.cos(ang_c), jnp.cos(ang_c)], axis=1)
    sin64 = jnp.concatenate([-jnp.sin(ang_r), jnp.sin(ang_r), -jnp.sin(ang_c), jnp.sin(ang_c)], axis=1)
    cos = jnp.concatenate([jnp.ones((lc, AT_DH), F32), cos64], axis=0)
    sin = jnp.concatenate([jnp.zeros((lc, AT_DH), F32), sin64], axis=0)
    return jnp.tile(cos, (1, 2)), jnp.tile(sin, (1, 2))


def _even_weights(w_in, b_in):
    def cols(m):
        sizes = (ML_W, ML_W, ML_W, ML_W, ML_GATE_COLS, AT_W, AT_KV_HEADS * AT_DH, AT_KV_HEADS * AT_DH)
        bounds = [sum(sizes[:i + 1]) for i in range(len(sizes) - 1)]
        mq, mk, mv, mo, mg, aq, ak, av = jnp.split(m, bounds, axis=-1)
        mgp = jnp.pad(mg, [(0, 0)] * (m.ndim - 1) + [(0, LANES - ML_GATE_COLS)])
        return jnp.concatenate([mq, mk, mv, mo, aq, ak, av, mgp], axis=-1)
    return cols(w_in).astype(BF16), cols(b_in[None, :])


def kernel(x, c, ctx, c_ctx, ada_w, ada_b, norm1_g, norm2_g, mlp_w1, mlp_w2, ev_w_in, ev_b_in, ev_w_out, ml_f_bias, ml_out_g, at_q_g, at_k_g, at_sink, rw_mu, rw_w_rkv, rw_w0, rw_w1, rw_w2, rw_a0, rw_a1, rw_a2, rw_g1, rw_g2, rw_k_k, rw_k_a, rw_r_k, rw_ln_g, rw_ln_b, rw_w_out):
    bsz, s, d = x.shape
    lc = ctx.shape[1]
    depth = ada_w.shape[0]
    assert d == D_MODEL and lc == TOKEN_TILE and s % TOKEN_TILE == 0 and bsz < MOD_ROWS
    z = (ctx, x)

    cstack = jnp.zeros((MOD_ROWS, d), F32).at[:bsz].set(c).at[bsz].set(c_ctx)
    mod_all = _ada_call(cstack, ada_w, ada_b).reshape(depth, MOD_ROWS, N_MOD, d)

    grp512 = _group_mean_matrix(AT_W, AT_DH)
    head_e = (jnp.arange(d)[:, None] // RW_HEAD == jnp.arange(LANES)[None, :]).astype(BF16)
    head_et = head_e.T
    cos_t, sin_t = _rope_tables(lc, s)
    gate_is_f = ((jnp.arange(LANES) % 8 >= 4) & (jnp.arange(LANES) < 16)).astype(F32)[None, :]

    mlp_w1_b = mlp_w1.astype(BF16)
    mlp_w2_b = mlp_w2.astype(BF16)
    for layer in range(depth):
        mod = mod_all[layer]
        j = layer // 2
        n1 = norm1_g[layer][None, :]
        mlp_args = (norm2_g[layer][None, :], mlp_w1_b, mlp_w2_b, layer)
        skip = lc // TOKEN_TILE if layer == depth - 1 else 0
        if layer % 2 == 0:
            w_in, b_in = _even_weights(ev_w_in[j], ev_b_in[j])
            fb = jnp.zeros((LANES,), F32)
            for dr in range(2):
                fb = fb.at[8 * dr + 4:8 * dr + 8].set(ml_f_bias[j, dr])
            outs = _even_in_call(
                z, n1, mod, w_in, b_in, jnp.tile(at_q_g[j], AT_HEADS)[None, :],
                jnp.tile(at_k_g[j], AT_KV_HEADS)[None, :], grp512, cos_t, sin_t, fb[None, :], gate_is_f, bsz)
            if isinstance(z, tuple):
                z = outs[-1]
            mqv, kt, mo, aq, akt, av, gates, gates_t = outs[:8]
            hf, hb = _mlstm_call(mqv, kt, gates, gates_t, lc)
            sink = jnp.broadcast_to(at_sink[j][:, None], (AT_HEADS, LANES))
            ya = _attn_call(aq, akt, av, sink, lc)
            z = _even_out_call(z, hf, hb, mo, ya, ml_out_g[j][None, :], ev_w_out[j].astype(BF16), mod, *mlp_args,
                               bsz, skip)
        else:
            pad_g = RW_GATE_PAD - RW_GATE_LORA
            zeros = jnp.zeros((RW_DECAY_LORA, d), F32)
            w2 = jnp.concatenate([jnp.concatenate([rw_w2[j, 0], zeros], axis=1),
                                  jnp.concatenate([zeros, rw_w2[j, 1]], axis=1)], axis=0)
            a2 = jnp.concatenate([jnp.concatenate([rw_a2[j, 0], zeros], axis=1),
                                  jnp.concatenate([zeros, rw_a2[j, 1]], axis=1)], axis=0)
            sh, dr_, lw, gate, bonus = _rwkv_in_call(
                z, n1, mod, rw_mu[j].reshape(12, d), rw_w_rkv[j].astype(BF16),
                jnp.concatenate([rw_w1[j, 0], rw_w1[j, 1]], axis=1).astype(BF16),
                jnp.concatenate([rw_a1[j, 0], rw_a1[j, 1]], axis=1).astype(BF16),
                jnp.pad(rw_g1[j], ((0, 0), (0, pad_g))).astype(BF16),
                w2.astype(BF16), a2.astype(BF16),
                jnp.pad(rw_g2[j], ((0, pad_g), (0, 0))).astype(BF16),
                rw_w0[j], rw_a0[j], rw_k_k[j][None, :], rw_k_a[j][None, :], rw_r_k[j].reshape(1, d),
                head_e, head_et, bsz, lc)
            yf, yb = _rwkv_scan_call(sh, dr_, lw, lc)
            z = _rwkv_out_call(z, yf, yb, gate, bonus, rw_ln_g[j][None, :], rw_ln_b[j][None, :], head_e, head_et,
                               rw_w_out[j].astype(BF16), mod, *mlp_args, bsz, lc, skip)
    return z
```
